```python
import math
import jax, jax.numpy as jnp
from jax import lax
import numpy as np

D_MODEL = 1024
BATCH = 4
SEQ = 8192
DEPTH = 2
DEC_BATCH = 128
DEC_SEQ = 1
PAST_LEN = 16384
PAGE_SIZE = 128

N_MIXERS = 2
N_ATTN = (DEPTH + 1) // 2
N_SSM = DEPTH // 2
HEAD_DIM = 64
N_HEADS = D_MODEL // HEAD_DIM
N_KV_HEADS = max(1, N_HEADS // 8)
GQ = N_HEADS // N_KV_HEADS
WINDOW = 128
BLOCK = WINDOW
N_BUCKETS = 32
MAX_EXACT = N_BUCKETS // 2
MAX_DISTANCE = WINDOW
SSM_GROUP = 16
N_SSM_GROUPS = D_MODEL // SSM_GROUP
SSM_STATE = 64
DT_MIN = 1e-3
DT_MAX = 1e-1
N_EXPERT_GROUPS = 4
EXPERTS_PER_GROUP = 8
N_EXPERTS = N_EXPERT_GROUPS * EXPERTS_PER_GROUP
TOP_K = 2
D_EXPERT = D_MODEL // 4
PLE_DIM = 256
RMS_EPS = 1e-6
NEG_INF = -1e30

kernel_name = 'hybrid_swa_sink_s5_hmoe_decoder_step'


def rmsnorm(x, g):
    xf = x.astype(jnp.float32)
    y = xf * lax.rsqrt(jnp.mean(xf * xf, axis=-1, keepdims=True) + RMS_EPS)
    return (y * g.astype(jnp.float32)).astype(x.dtype)


def t5_bucket(dist):
    n = jnp.maximum(dist, 0)
    nf = jnp.maximum(n, 1).astype(jnp.float32)
    large = MAX_EXACT + (jnp.log(nf / MAX_EXACT) / math.log(MAX_DISTANCE / MAX_EXACT)
                         * (N_BUCKETS - MAX_EXACT)).astype(jnp.int32)
    large = jnp.minimum(large, N_BUCKETS - 1)
    return jnp.where(n < MAX_EXACT, n, large)


def rel_bias(table, dist):
    b = table.astype(jnp.float32)[t5_bucket(dist)]
    b = jnp.moveaxis(b, -1, 0)
    return b.reshape((N_KV_HEADS, GQ) + dist.shape)


def sink_attention(q, k, v, bias, mask, sinks):
    s = jnp.einsum('...qkgd,...skd->...kgqs', q, k, preferred_element_type=jnp.float32) * (HEAD_DIM ** -0.5)
    s = jnp.where(mask, s + bias, NEG_INF)
    sink = jnp.broadcast_to(sinks.astype(jnp.float32).reshape(N_KV_HEADS, GQ, 1, 1), s.shape[:-1] + (1,))
    pr = jax.nn.softmax(jnp.concatenate([s, sink], axis=-1), axis=-1)[..., :-1]
    return jnp.einsum('...kgqs,...skd->...qkgd', pr.astype(v.dtype), v)


def attn_mixer(h, w_qkv, w_o, sinks, bias_table, cache_k, cache_v):
    bn, t, _ = h.shape
    nq, nk = N_HEADS * HEAD_DIM, N_KV_HEADS * HEAD_DIM
    qkv = h @ w_qkv
    q = qkv[..., :nq].reshape(bn, t, N_KV_HEADS, GQ, HEAD_DIM)
    k = qkv[..., nq:nq + nk].reshape(bn, t, N_KV_HEADS, HEAD_DIM)
    v = qkv[..., nq + nk:].reshape(bn, t, N_KV_HEADS, HEAD_DIM)
    if cache_k is None:
        nb = t // BLOCK
        qb = q.reshape(bn, nb, BLOCK, N_KV_HEADS, GQ, HEAD_DIM)

        def band(z):
            zp = jnp.pad(z, ((0, 0), (BLOCK, 0), (0, 0), (0, 0))).reshape(bn, nb + 1, BLOCK, N_KV_HEADS, HEAD_DIM)
            return jnp.concatenate([zp[:, :-1], zp[:, 1:]], axis=2)

        qi = jnp.arange(BLOCK)[:, None]
        sj = jnp.arange(2 * BLOCK)[None, :]
        dist = qi + BLOCK - sj
        kpos = jnp.arange(nb)[:, None, None] * BLOCK + sj[None] - BLOCK
        mask = ((dist >= 0) & (dist < WINDOW))[None] & (kpos >= 0)
        o = sink_attention(qb, band(k), band(v), rel_bias(bias_table, dist), mask[:, None, None], sinks)
        start = max(t - WINDOW, 0)
        new_k, new_v = k[:, start:], v[:, start:]
    else:
        w = cache_k.shape[1]
        kc = jnp.concatenate([cache_k.astype(k.dtype), k], axis=1)
        vc = jnp.concatenate([cache_v.astype(v.dtype), v], axis=1)
        dist = jnp.arange(t)[:, None] + w - jnp.arange(w + t)[None, :]
        mask = ((dist >= 0) & (dist < WINDOW))[None, None]
        o = sink_attention(q, kc, vc, rel_bias(bias_table, dist), mask, sinks)
        new_k, new_v = kc[:, t:], vc[:, t:]
    return o.reshape(bn, t, nq) @ w_o, new_k, new_v


def ssm_combine(e1, e2):
    a1r, a1i, b1r, b1i = e1
    a2r, a2i, b2r, b2i = e2
    return (a1r * a2r - a1i * a2i, a1r * a2i + a1i * a2r,
            a2r * b1r - a2i * b1i + b2r, a2r * b1i + a2i * b1r + b2i)


def ssm_mixer(h, a_re, a_im, log_dt, b_re, b_im, c_re, c_im, d_skip, w_glu_a, w_glu_b, h0_re, h0_im):
    f32 = jnp.float32
    bn, t, _ = h.shape
    u = h.astype(f32).reshape(bn, t, N_SSM_GROUPS, SSM_GROUP)
    ar, ai = a_re.astype(f32), a_im.astype(f32)
    dt = jnp.exp(log_dt.astype(f32))[:, None]
    mag = jnp.exp(ar * dt)
    lr, li = mag * jnp.cos(ai * dt), mag * jnp.sin(ai * dt)
    den = ar * ar + ai * ai
    cr = ((lr - 1.0) * ar + li * ai) / den
    ci = (li * ar - (lr - 1.0) * ai) / den
    bu_r = jnp.einsum('btgh,gph->btgp', u, b_re.astype(f32))
    bu_i = jnp.einsum('btgh,gph->btgp', u, b_im.astype(f32))
    xr = cr * bu_r - ci * bu_i
    xi = cr * bu_i + ci * bu_r
    if h0_re is not None:
        h0r, h0i = h0_re.astype(f32), h0_im.astype(f32)
        xr = xr.at[:, 0].add(lr * h0r - li * h0i)
        xi = xi.at[:, 0].add(lr * h0i + li * h0r)
    lam_r = jnp.broadcast_to(lr, (1, t) + lr.shape)
    lam_i = jnp.broadcast_to(li, (1, t) + li.shape)
    _, _, sr, si = lax.associative_scan(ssm_combine, (lam_r, lam_i, xr, xi), axis=1)
    y = (jnp.einsum('btgp,ghp->btgh', sr, c_re.astype(f32))
         - jnp.einsum('btgp,ghp->btgh', si, c_im.astype(f32))
         + d_skip.astype(f32) * u)
    z = jax.nn.gelu(y.reshape(bn, t, D_MODEL)).astype(h.dtype)
    out = (z @ w_glu_a) * jax.nn.sigmoid(z @ w_glu_b)
    return out, sr[:, -1], si[:, -1]


def moe(h, w_grp, b_grp, w_exp, b_exp, w_gate, w_up, w_down):
    f32 = jnp.float32
    bn, t, d = h.shape
    hf = h.reshape(-1, d)
    gl = (hf @ w_grp).astype(f32) + b_grp.astype(f32)
    gp = jax.nn.softmax(gl, axis=-1)
    g_sel = jnp.argmax(gl, axis=-1)
    p_grp = jnp.take_along_axis(gp, g_sel[:, None], axis=-1)
    el = (hf @ w_exp.reshape(d, -1)).astype(f32).reshape(-1, N_EXPERT_GROUPS, EXPERTS_PER_GROUP) + b_exp.astype(f32)
    el_sel = jnp.take_along_axis(el, g_sel[:, None, None], axis=1)[:, 0]
    top_v, top_i = lax.top_k(el_sel, TOP_K)
    wts = jax.nn.softmax(top_v, axis=-1) * p_grp
    eid = g_sel[:, None] * EXPERTS_PER_GROUP + top_i
    gates = jnp.einsum('nk,nke->ne', wts, jax.nn.one_hot(eid, N_EXPERTS, dtype=f32)).astype(hf.dtype)
    out = jnp.zeros_like(hf)
    for e in range(N_EXPERTS):
        a = jax.nn.silu(hf @ w_gate[e]) * (hf @ w_up[e])
        out = out + gates[:, e:e + 1] * (a @ w_down[e])
    return out.reshape(bn, t, d)


def trunk(x, p, cache_k, cache_v, st_re, st_im, prm):
    new_k, new_v, new_re, new_im = [], [], [], []
    for i in range(DEPTH):
        j = i // N_MIXERS
        h = rmsnorm(x, prm['g_mix'][i])
        if i % N_MIXERS == 0:
            ck = None if cache_k is None else cache_k[j]
            cv = None if cache_v is None else cache_v[j]
            o, nk, nv = attn_mixer(h, prm['w_qkv'][j], prm['w_o'][j], prm['sinks'][j], prm['rel_bias_table'], ck, cv)
            new_k.append(nk)
            new_v.append(nv)
        else:
            hr = None if st_re is None else st_re[j]
            hi = None if st_im is None else st_im[j]
            o, sr, si = ssm_mixer(h, prm['ssm_a_re'][j], prm['ssm_a_im'][j], prm['ssm_log_dt'][j],
                                  prm['ssm_b_re'][j], prm['ssm_b_im'][j], prm['ssm_c_re'][j], prm['ssm_c_im'][j],
                                  prm['ssm_d'][j], prm['w_glu_a'][j], prm['w_glu_b'][j], hr, hi)
            new_re.append(sr)
            new_im.append(si)
        x = x + o
        x = x + moe(rmsnorm(x, prm['g_ffn'][i]), prm['w_router_group'][i], prm['b_router_group'][i],
                    prm['w_router_expert'][i], prm['b_router_expert'][i],
                    prm['w_exp_gate'][i], prm['w_exp_up'][i], prm['w_exp_down'][i])
        gate = jax.nn.sigmoid(rmsnorm(x, prm['g_ple'][i]) @ prm['w_ple_gate'][i])
        x = x + gate * (p[i].astype(x.dtype) @ prm['w_ple_proj'][i])
    return (rmsnorm(x, prm['g_final']), jnp.stack(new_k), jnp.stack(new_v),
            jnp.stack(new_re), jnp.stack(new_im))


def setup_inputs(seed: int = 0) -> dict:
    key = jax.random.key(seed)
    ks = iter(jax.random.split(key, 48))
    nrm = lambda shape, scale: scale * jax.random.normal(next(ks), shape, jnp.float32)
    w_buf = min(WINDOW, PAST_LEN)
    qkv_w = (N_HEADS + 2 * N_KV_HEADS) * HEAD_DIM
    n_idx = jnp.arange(SSM_STATE, dtype=jnp.float32)
    u_dt = jax.random.uniform(next(ks), (N_SSM, N_SSM_GROUPS), jnp.float32)
    return {
        'x_prompt': nrm((BATCH, SEQ, D_MODEL), 1.0),
        'x_sample': nrm((DEC_BATCH, DEC_SEQ, D_MODEL), 1.0),
        'cache_k_win': nrm((N_ATTN, DEC_BATCH, w_buf, N_KV_HEADS, HEAD_DIM), 1.0),
        'cache_v_win': nrm((N_ATTN, DEC_BATCH, w_buf, N_KV_HEADS, HEAD_DIM), 1.0),
        'state_ssm_re': nrm((N_SSM, DEC_BATCH, N_SSM_GROUPS, SSM_STATE), 0.1),
        'state_ssm_im': nrm((N_SSM, DEC_BATCH, N_SSM_GROUPS, SSM_STATE), 0.1),
        'p_prompt': nrm((DEPTH, BATCH, SEQ, PLE_DIM), 1.0),
        'p_sample': nrm((DEPTH, DEC_BATCH, DEC_SEQ, PLE_DIM), 1.0),
        'rel_bias_table': nrm((N_BUCKETS, N_HEADS), 0.5),
        'g_mix': 1.0 + nrm((DEPTH, D_MODEL), 0.02),
        'w_qkv': nrm((N_ATTN, D_MODEL, qkv_w), D_MODEL ** -0.5),
        'w_o': nrm((N_ATTN, N_HEADS * HEAD_DIM, D_MODEL), (N_HEADS * HEAD_DIM) ** -0.5),
        'sinks': nrm((N_ATTN, N_HEADS), 0.5),
        'ssm_a_re': -0.5 + nrm((N_SSM, N_SSM_GROUPS, SSM_STATE), 0.01),
        'ssm_a_im': math.pi * n_idx + nrm((N_SSM, N_SSM_GROUPS, SSM_STATE), 0.01),
        'ssm_log_dt': math.log(DT_MIN) + u_dt * (math.log(DT_MAX) - math.log(DT_MIN)),
        'ssm_b_re': nrm((N_SSM, N_SSM_GROUPS, SSM_STATE, SSM_GROUP), (2.0 * SSM_GROUP) ** -0.5),
        'ssm_b_im': nrm((N_SSM, N_SSM_GROUPS, SSM_STATE, SSM_GROUP), (2.0 * SSM_GROUP) ** -0.5),
        'ssm_c_re': nrm((N_SSM, N_SSM_GROUPS, SSM_GROUP, SSM_STATE), (2.0 * SSM_STATE) ** -0.5),
        'ssm_c_im': nrm((N_SSM, N_SSM_GROUPS, SSM_GROUP, SSM_STATE), (2.0 * SSM_STATE) ** -0.5),
        'ssm_d': nrm((N_SSM, N_SSM_GROUPS, SSM_GROUP), 1.0),
        'w_glu_a': nrm((N_SSM, D_MODEL, D_MODEL), D_MODEL ** -0.5),
        'w_glu_b': nrm((N_SSM, D_MODEL, D_MODEL), D_MODEL ** -0.5),
        'g_ffn': 1.0 + nrm((DEPTH, D_MODEL), 0.02),
        'w_router_group': nrm((DEPTH, D_MODEL, N_EXPERT_GROUPS), D_MODEL ** -0.5),
        'b_router_group': nrm((DEPTH, N_EXPERT_GROUPS), 0.01),
        'w_router_expert': nrm((DEPTH, D_MODEL, N_EXPERT_GROUPS, EXPERTS_PER_GROUP), D_MODEL ** -0.5),
        'b_router_expert': nrm((DEPTH, N_EXPERT_GROUPS, EXPERTS_PER_GROUP), 0.01),
        'w_exp_gate': nrm((DEPTH, N_EXPERTS, D_MODEL, D_EXPERT), D_MODEL ** -0.5),
        'w_exp_up': nrm((DEPTH, N_EXPERTS, D_MODEL, D_EXPERT), D_MODEL ** -0.5),
        'w_exp_down': nrm((DEPTH, N_EXPERTS, D_EXPERT, D_MODEL), D_EXPERT ** -0.5),
        'g_ple': 1.0 + nrm((DEPTH, D_MODEL), 0.02),
        'w_ple_gate': nrm((DEPTH, D_MODEL, D_MODEL), D_MODEL ** -0.5),
        'w_ple_proj': nrm((DEPTH, PLE_DIM, D_MODEL), PLE_DIM ** -0.5),
        'g_final': 1.0 + nrm((D_MODEL,), 0.02),
    }


def reference(x_prompt, x_sample, cache_k_win, cache_v_win, state_ssm_re, state_ssm_im, p_prompt, p_sample,
              rel_bias_table, g_mix, w_qkv, w_o, sinks, ssm_a_re, ssm_a_im, ssm_log_dt, ssm_b_re, ssm_b_im,
              ssm_c_re, ssm_c_im, ssm_d, w_glu_a, w_glu_b, g_ffn, w_router_group, b_router_group,
              w_router_expert, b_router_expert, w_exp_gate, w_exp_up, w_exp_down, g_ple, w_ple_gate,
              w_ple_proj, g_final):
    prm = dict(rel_bias_table=rel_bias_table, g_mix=g_mix, w_qkv=w_qkv, w_o=w_o, sinks=sinks,
               ssm_a_re=ssm_a_re, ssm_a_im=ssm_a_im, ssm_log_dt=ssm_log_dt, ssm_b_re=ssm_b_re,
               ssm_b_im=ssm_b_im, ssm_c_re=ssm_c_re, ssm_c_im=ssm_c_im, ssm_d=ssm_d,
               w_glu_a=w_glu_a, w_glu_b=w_glu_b, g_ffn=g_ffn, w_router_group=w_router_group,
               b_router_group=b_router_group, w_router_expert=w_router_expert,
               b_router_expert=b_router_expert, w_exp_gate=w_exp_gate, w_exp_up=w_exp_up,
               w_exp_down=w_exp_down, g_ple=g_ple, w_ple_gate=w_ple_gate, w_ple_proj=w_ple_proj,
               g_final=g_final)
    y_prompt, k_win_prompt, v_win_prompt, ssm_re_prompt, ssm_im_prompt = trunk(
        x_prompt, p_prompt, None, None, None, None, prm)
    y_sample, k_win_sample, v_win_sample, ssm_re_sample, ssm_im_sample = trunk(
        x_sample, p_sample, cache_k_win, cache_v_win, state_ssm_re, state_ssm_im, prm)
    return (y_prompt, y_sample, k_win_prompt, v_win_prompt, k_win_sample, v_win_sample,
            ssm_re_prompt, ssm_im_prompt, ssm_re_sample, ssm_im_sample)
```

```python
import functools
import math

import numpy as np
import jax
import jax.numpy as jnp
from jax import lax
from jax.experimental import pallas as pl
from jax.experimental.pallas import tpu as pltpu

F32 = jnp.float32
BF16 = jnp.bfloat16

D_MODEL = 1024
HEAD_DIM = 64
N_HEADS = 16
N_KV_HEADS = 2
GQ = N_HEADS // N_KV_HEADS
WINDOW = 128
N_BUCKETS = 32
MAX_EXACT = 16
MAX_DISTANCE = 128
SSM_GROUP = 16
N_SSM_GROUPS = 64
SSM_STATE = 64
N_STATES = N_SSM_GROUPS * SSM_STATE
N_EXPERT_GROUPS = 4
EXPERTS_PER_GROUP = 8
N_EXPERTS = 32
D_EXPERT = 256
PLE_DIM = 256
RMS_EPS = 1e-6
NEG_INF = -1e30

LANES = 128
SUBLANES = 8
KV_W = N_KV_HEADS * HEAD_DIM
QKV_W = D_MODEL + 2 * KV_W
SLAB_STATES = 512
N_SLABS = D_MODEL // LANES
VMEM_LIMIT = 56 * 1024 * 1024


def _cparams(sem):
    return pltpu.CompilerParams(dimension_semantics=sem, vmem_limit_bytes=VMEM_LIMIT)


def _rms(x, g):
    return x * lax.rsqrt(jnp.mean(x * x, axis=-1, keepdims=True) + RMS_EPS) * g


def _gelu_tanh(x):
    c = math.sqrt(2.0 / math.pi)
    return x * (0.5 * (1.0 + jnp.tanh(c * (x + 0.044715 * (x * x * x)))))


def _sigmoid(x):
    return 1.0 / (1.0 + jnp.exp(-x))


def _full(shape):
    n = len(shape)
    return pl.BlockSpec(shape, lambda *_: (0,) * n)


def _t5_bucket_np(dist):
    n = np.maximum(dist, 0)
    nf = np.maximum(n, 1).astype(np.float64)
    large = MAX_EXACT + (np.log(nf / MAX_EXACT) / math.log(MAX_DISTANCE / MAX_EXACT)
                         * (N_BUCKETS - MAX_EXACT)).astype(np.int32)
    large = np.minimum(large, N_BUCKETS - 1)
    return np.where(n < MAX_EXACT, n, large).astype(np.int32)


def _bias_kernel(table_ref, idx_ref, o_ref):
    h = pl.program_id(0)
    idx = idx_ref[...]
    acc = jnp.zeros(idx.shape, F32)
    for b in range(N_BUCKETS):
        acc = jnp.where(idx == b, table_ref[b, h], acc)
    o_ref[0] = acc


def _bias_from_table(table, idx_np):
    q, k = idx_np.shape
    return pl.pallas_call(
        _bias_kernel,
        grid=(N_HEADS,),
        in_specs=[pl.BlockSpec(memory_space=pltpu.SMEM), _full((q, k))],
        out_specs=pl.BlockSpec((1, q, k), lambda h: (h, 0, 0)),
        out_shape=jax.ShapeDtypeStruct((N_HEADS, q, k), F32),
        name="rel_bias",
    )(table, jnp.asarray(idx_np))


def _softmax_sink_pv(sc, sink, vpad):
    m = jnp.maximum(jnp.max(sc, axis=-1, keepdims=True), sink)
    e = jnp.exp(sc - m)
    den = jnp.sum(e, axis=-1, keepdims=True) + jnp.exp(sink - m)
    pr = (e / den).astype(BF16)
    return jnp.dot(pr, vpad, preferred_element_type=F32)


def _attn_prompt_kernel(sink_ref, x_ref, g_ref, wqkv_ref, wo_ref, bias_ref,
                        xo_ref, k_ref, v_ref, kv_scr, q_scr, o_scr, *, tq):
    i = pl.program_id(1)
    nsub = tq // WINDOW

    @pl.when(i == 0)
    def _():
        kv_scr[0:WINDOW, :] = jnp.zeros((WINDOW, 2 * KV_W), F32)

    x = x_ref[0]
    h = _rms(x, g_ref[...]).astype(BF16)
    qkv = jnp.dot(h, wqkv_ref[...], preferred_element_type=F32)
    q_scr[...] = (qkv[:, :D_MODEL] * (HEAD_DIM ** -0.5)).astype(BF16)
    kv_scr[WINDOW:WINDOW + tq, :] = qkv[:, D_MODEL:]
    k_ref[0] = qkv[tq - WINDOW:, D_MODEL:D_MODEL + KV_W]
    v_ref[0] = qkv[tq - WINDOW:, D_MODEL + KV_W:]

    qi = lax.broadcasted_iota(jnp.int32, (WINDOW, 2 * WINDOW), 0)
    sj = lax.broadcasted_iota(jnp.int32, (WINDOW, 2 * WINDOW), 1)
    dist = qi + WINDOW - sj
    band_ok = (dist >= 0) & (dist < WINDOW)
    lo = lax.broadcasted_iota(jnp.int32, (2 * WINDOW, KV_W), 1) < HEAD_DIM

    for s in range(nsub):
        band = kv_scr[WINDOW * s:WINDOW * s + 2 * WINDOW, :]
        kband, vband = band[:, :KV_W], band[:, KV_W:]
        kroll = pltpu.roll(kband, HEAD_DIM, 1)
        vroll = pltpu.roll(vband, HEAD_DIM, 1)
        kpad = [[jnp.where(lo, kband, 0.0).astype(BF16), jnp.where(lo, 0.0, kroll).astype(BF16)],
                [jnp.where(lo, kroll, 0.0).astype(BF16), jnp.where(lo, 0.0, kband).astype(BF16)]]
        vpad = [[jnp.where(lo, vband, 0.0).astype(BF16), jnp.where(lo, 0.0, vroll).astype(BF16)],
                [jnp.where(lo, vroll, 0.0).astype(BF16), jnp.where(lo, 0.0, vband).astype(BF16)]]
        min_j = jnp.where(i * nsub + s > 0, 0, WINDOW)
        valid = band_ok & (sj >= min_j)
        for j in range(N_HEADS // 2):
            q2 = q_scr[WINDOW * s:WINDOW * (s + 1), LANES * j:LANES * (j + 1)]
            g = (2 * j) // GQ
            acc = jnp.zeros((WINDOW, LANES), F32)
            for p in range(2):
                hd = 2 * j + p
                sc = lax.dot_general(q2, kpad[g][p], (((1,), (1,)), ((), ())),
                                     preferred_element_type=F32)
                sc = jnp.where(valid, sc + bias_ref[hd], NEG_INF)
                acc = acc + _softmax_sink_pv(sc, sink_ref[hd], vpad[g][p])
            o_scr[WINDOW * s:WINDOW * (s + 1), LANES * j:LANES * (j + 1)] = acc.astype(BF16)

    kv_scr[0:WINDOW, :] = kv_scr[tq:tq + WINDOW, :]
    xo_ref[0] = x + jnp.dot(o_scr[...], wo_ref[...], preferred_element_type=F32)


def _attn_prompt_layer(x, g, wqkv, wo, bias, sinks):
    b, t, d = x.shape
    tq = min(512, t)
    kern = functools.partial(_attn_prompt_kernel, tq=tq)
    return pl.pallas_call(
        kern,
        grid=(b, t // tq),
        in_specs=[
            pl.BlockSpec(memory_space=pltpu.SMEM),
            pl.BlockSpec((1, tq, d), lambda bi, i: (bi, i, 0)),
            _full((1, d)),
            _full((d, QKV_W)),
            _full((d, d)),
            _full((N_HEADS, WINDOW, 2 * WINDOW)),
        ],
        out_specs=[
            pl.BlockSpec((1, tq, d), lambda bi, i: (bi, i, 0)),
            pl.BlockSpec((1, WINDOW, KV_W), lambda bi, i: (bi, 0, 0)),
            pl.BlockSpec((1, WINDOW, KV_W), lambda bi, i: (bi, 0, 0)),
        ],
        out_shape=[
            jax.ShapeDtypeStruct((b, t, d), F32),
            jax.ShapeDtypeStruct((b, WINDOW, KV_W), F32),
            jax.ShapeDtypeStruct((b, WINDOW, KV_W), F32),
        ],
        scratch_shapes=[
            pltpu.VMEM((WINDOW + tq, 2 * KV_W), F32),
            pltpu.VMEM((tq, d), BF16),
            pltpu.VMEM((tq, d), BF16),
        ],
        compiler_params=_cparams(("arbitrary", "arbitrary")),
        name="attn_prompt",
    )(sinks, x, g, wqkv, wo, bias)


def _norm_linear_kernel(x_ref, g_ref, w_ref, o_ref):
    h = _rms(x_ref[...], g_ref[...]).astype(BF16)
    o_ref[...] = jnp.dot(h, w_ref[...], preferred_element_type=F32)


def _norm_linear(x, g, w):
    n, d = x.shape
    return pl.pallas_call(
        _norm_linear_kernel,
        out_shape=jax.ShapeDtypeStruct((n, w.shape[1]), F32),
        compiler_params=_cparams(None),
        name="norm_linear",
    )(x, g, w)


def _linear_residual_kernel(x_ref, o_ref, w_ref, xo_ref):
    xo_ref[...] = x_ref[...] + jnp.dot(o_ref[...].astype(BF16), w_ref[...], preferred_element_type=F32)


def _linear_residual(x, o, w):
    return pl.pallas_call(
        _linear_residual_kernel,
        out_shape=jax.ShapeDtypeStruct(x.shape, F32),
        compiler_params=_cparams(None),
        name="linear_residual",
    )(x, o, w)


def _attn_sample_kernel(sink_ref, q_ref, ck_ref, cv_ref, kn_ref, vn_ref, bias_ref, o_ref, kw_ref, vw_ref):
    kw = jnp.concatenate([ck_ref[:, 1:, :], kn_ref[...]], axis=1)
    vw = jnp.concatenate([cv_ref[:, 1:, :], vn_ref[...]], axis=1)
    kw_ref[...] = kw
    vw_ref[...] = vw
    q = (q_ref[...] * (HEAD_DIM ** -0.5)).astype(BF16)
    sc = jnp.einsum('bhc,bjc->bhj', q, kw.astype(BF16), preferred_element_type=F32)
    sc = sc + bias_ref[...][None]
    sink = sink_ref[...][None]
    m = jnp.maximum(jnp.max(sc, axis=-1, keepdims=True), sink)
    e = jnp.exp(sc - m)
    den = jnp.sum(e, axis=-1, keepdims=True) + jnp.exp(sink - m)
    pr = (e / den).astype(BF16)
    o_ref[...] = jnp.einsum('bhj,bjc->bhc', pr, vw.astype(BF16), preferred_element_type=F32)


def _attn_sample(q3, ck, cv, kn, vn, bias_s, sinks_col):
    n = q3.shape[0]
    bb = 8
    blk = lambda s: pl.BlockSpec((bb,) + s, lambda i: (i, 0, 0))
    return pl.pallas_call(
        _attn_sample_kernel,
        grid=(n // bb,),
        in_specs=[_full((N_HEADS, 1)), blk((N_HEADS, KV_W)), blk((WINDOW, KV_W)), blk((WINDOW, KV_W)),
                  blk((1, KV_W)), blk((1, KV_W)), _full((N_HEADS, WINDOW))],
        out_specs=[blk((N_HEADS, KV_W)), blk((WINDOW, KV_W)), blk((WINDOW, KV_W))],
        out_shape=[jax.ShapeDtypeStruct((n, N_HEADS, KV_W), F32),
                   jax.ShapeDtypeStruct((n, WINDOW, KV_W), F32),
                   jax.ShapeDtypeStruct((n, WINDOW, KV_W), F32)],
        compiler_params=_cparams(("arbitrary",)),
        name="attn_sample",
    )(sinks_col, q3, ck, cv, kn, vn, bias_s)


def _attn_sample_layer(x, g, wqkv, wo, bias_s, sinks, cache_k, cache_v):
    n = x.shape[0]
    qkv = _norm_linear(x, g, wqkv)
    q = qkv[:, :D_MODEL].reshape(n, N_KV_HEADS, GQ, 1, HEAD_DIM)
    place = jnp.eye(N_KV_HEADS, dtype=F32).reshape(1, N_KV_HEADS, 1, N_KV_HEADS, 1)
    q3 = (q * place).reshape(n, N_HEADS, KV_W)
    kn = qkv[:, D_MODEL:D_MODEL + KV_W].reshape(n, 1, KV_W)
    vn = qkv[:, D_MODEL + KV_W:].reshape(n, 1, KV_W)
    o3, kw, vw = _attn_sample(q3, cache_k.reshape(n, WINDOW, KV_W), cache_v.reshape(n, WINDOW, KV_W),
                              kn, vn, bias_s, sinks.reshape(N_HEADS, 1))
    o5 = o3.reshape(n, N_KV_HEADS, GQ, N_KV_HEADS, HEAD_DIM)
    o = jnp.stack([o5[:, gi, :, gi, :] for gi in range(N_KV_HEADS)], axis=1).reshape(n, D_MODEL)
    return _linear_residual(x, o, wo), kw, vw


def _ssm_prep_kernel(ar_ref, ai_ref, ldt_ref, br_ref, bi_ref, c_ref, a_out, bc_out, wb_out, wc_out):
    ar, ai = ar_ref[...], ai_ref[...]
    dt = jnp.exp(ldt_ref[...])
    mag = jnp.exp(ar * dt)
    lr = mag * jnp.cos(ai * dt)
    li = mag * jnp.sin(ai * dt)
    den = ar * ar + ai * ai
    cr = ((lr - 1.0) * ar + li * ai) / den
    ci = (li * ar - (lr - 1.0) * ai) / den
    row = lax.broadcasted_iota(jnp.int32, (SUBLANES, N_STATES), 0)
    a_out[...] = jnp.broadcast_to(lr, (SUBLANES, N_STATES))
    bc_out[...] = jnp.where(row < SUBLANES // 2, -li, li)
    for j in range(N_SLABS):
        crj = cr[:, SLAB_STATES * j:SLAB_STATES * (j + 1)]
        cij = ci[:, SLAB_STATES * j:SLAB_STATES * (j + 1)]
        br, bi = br_ref[j], bi_ref[j]
        wb_out[j, :, :SLAB_STATES] = (crj * br - cij * bi).astype(BF16)
        wb_out[j, :, SLAB_STATES:] = (crj * bi + cij * br).astype(BF16)
        wc_out[j, :SLAB_STATES, :] = c_ref[j, :SLAB_STATES, :].astype(BF16)
        wc_out[j, SLAB_STATES:, :] = (-c_ref[j, SLAB_STATES:, :]).astype(BF16)


def _ssm_prep(a_re, a_im, log_dt, b_re, b_im, c_re, c_im):
    eye = jnp.eye(SUBLANES, dtype=F32)

    def blk_b(b):
        t = b.reshape(N_SLABS, 8, SSM_STATE, SSM_GROUP).transpose(0, 1, 3, 2)
        return jnp.einsum('jghp,gk->jghkp', t, eye).reshape(N_SLABS, LANES, SLAB_STATES)

    def blk_c(c):
        t = c.reshape(N_SLABS, 8, SSM_GROUP, SSM_STATE).transpose(0, 1, 3, 2)
        return jnp.einsum('jgph,gk->jgpkh', t, eye).reshape(N_SLABS, SLAB_STATES, LANES)

    row = lambda a: a.reshape(1, N_STATES)
    ldt = jnp.broadcast_to(log_dt[:, None], (N_SSM_GROUPS, SSM_STATE))
    c_all = jnp.concatenate([blk_c(c_re), blk_c(c_im)], axis=1)
    return pl.pallas_call(
        _ssm_prep_kernel,
        out_shape=[jax.ShapeDtypeStruct((SUBLANES, N_STATES), F32),
                   jax.ShapeDtypeStruct((SUBLANES, N_STATES), F32),
                   jax.ShapeDtypeStruct((N_SLABS, LANES, 2 * SLAB_STATES), BF16),
                   jax.ShapeDtypeStruct((N_SLABS, 2 * SLAB_STATES, LANES), BF16)],
        compiler_params=_cparams(None),
        name="ssm_prep",
    )(row(a_re), row(a_im), row(ldt), blk_b(b_re), blk_b(b_im), c_all)


def _glu_residual(x, y, u, d, wa, wg):
    z = _gelu_tanh(y + d * u).astype(BF16)
    out = jnp.dot(z, wa, preferred_element_type=F32) * _sigmoid(jnp.dot(z, wg, preferred_element_type=F32))
    return x + out


def _ssm_prompt_kernel(x_ref, g_ref, wb_ref, wc_ref, a_ref, bc_ref, d_ref, wa_ref, wg_ref,
                       xo_ref, st_ref, z_scr, u_scr, xs_scr, *, nb, lc):
    i = pl.program_id(0)
    blocks_per_slab = SLAB_STATES // LANES
    half = SUBLANES // 2

    @pl.when(i == 0)
    def _():
        xs_scr[...] = jnp.zeros((SUBLANES, N_STATES), F32)

    for b in range(nb):
        u_scr[b * lc:(b + 1) * lc, :] = _rms(x_ref[b], g_ref[...])
    ub = u_scr[...].astype(BF16)
    for j in range(N_SLABS):
        res = jnp.dot(ub[:, LANES * j:LANES * (j + 1)], wb_ref[j], preferred_element_type=F32)
        for b in range(nb):
            rb = res[b * lc:(b + 1) * lc]
            for c in range(blocks_per_slab):
                blk = blocks_per_slab * j + c
                z_scr[blk, pl.ds(b, lc, stride=SUBLANES), :] = rb[:, LANES * c:LANES * (c + 1)]
                z_scr[blk, pl.ds(half + b, lc, stride=SUBLANES), :] = (
                    rb[:, SLAB_STATES + LANES * c:SLAB_STATES + LANES * (c + 1)])
    if nb < half:
        for blk in range(N_STATES // LANES):
            for r in list(range(nb, half)) + list(range(half + nb, SUBLANES)):
                z_scr[blk, pl.ds(r, lc, stride=SUBLANES), :] = jnp.zeros((lc, LANES), F32)

    group = 8
    for cb in range(N_STATES // LANES // group):
        blks = [group * cb + k for k in range(group)]
        a = [a_ref[:, LANES * q:LANES * (q + 1)] for q in blks]
        bc = [bc_ref[:, LANES * q:LANES * (q + 1)] for q in blks]
        x0 = tuple(xs_scr[:, LANES * q:LANES * (q + 1)] for q in blks)

        def step(t, xs, blks=blks, a=a, bc=bc):
            r = pl.multiple_of(t * SUBLANES, SUBLANES)
            out = []
            for k, q in enumerate(blks):
                xn = a[k] * xs[k] + bc[k] * pltpu.roll(xs[k], half, 0) + z_scr[q, pl.ds(r, SUBLANES), :]
                z_scr[q, pl.ds(r, SUBLANES), :] = xn
                out.append(xn)
            return tuple(out)

        xf = lax.fori_loop(0, lc, step, x0)
        for k, q in enumerate(blks):
            xs_scr[:, LANES * q:LANES * (q + 1)] = xf[k]
    st_ref[...] = xs_scr[...]

    for b in range(nb):
        ys = []
        for j in range(N_SLABS):
            parts = [z_scr[blocks_per_slab * j + c, pl.ds(b, lc, stride=SUBLANES), :] for c in range(blocks_per_slab)]
            parts += [z_scr[blocks_per_slab * j + c, pl.ds(half + b, lc, stride=SUBLANES), :]
                      for c in range(blocks_per_slab)]
            s = jnp.concatenate(parts, axis=1).astype(BF16)
            ys.append(jnp.dot(s, wc_ref[j], preferred_element_type=F32))
        y = jnp.concatenate(ys, axis=1)
        xo_ref[b] = _glu_residual(x_ref[b], y, u_scr[b * lc:(b + 1) * lc, :], d_ref[...], wa_ref[...], wg_ref[...])


def _ssm_prompt_layer(x, g, prep, d, wa, wg):
    a_rows, bc_rows, wb, wc = prep
    nb, t, dm = x.shape
    assert nb <= SUBLANES // 2
    lc = min(128, t)
    kern = functools.partial(_ssm_prompt_kernel, nb=nb, lc=lc)
    return pl.pallas_call(
        kern,
        grid=(t // lc,),
        in_specs=[
            pl.BlockSpec((nb, lc, dm), lambda i: (0, i, 0)),
            _full((1, dm)),
            _full(wb.shape), _full(wc.shape),
            _full((SUBLANES, N_STATES)), _full((SUBLANES, N_STATES)),
            _full((1, dm)), _full((dm, dm)), _full((dm, dm)),
        ],
        out_specs=[pl.BlockSpec((nb, lc, dm), lambda i: (0, i, 0)), _full((SUBLANES, N_STATES))],
        out_shape=[jax.ShapeDtypeStruct((nb, t, dm), F32), jax.ShapeDtypeStruct((SUBLANES, N_STATES), F32)],
        scratch_shapes=[
            pltpu.VMEM((N_STATES // LANES, SUBLANES * lc, LANES), F32),
            pltpu.VMEM((nb * lc, dm), F32),
            pltpu.VMEM((SUBLANES, N_STATES), F32),
        ],
        compiler_params=_cparams(("arbitrary",)),
        name="ssm_prompt",
    )(x, g, wb, wc, a_rows, bc_rows, d, wa, wg)


def _ssm_sample_kernel(x_ref, g_ref, wb_ref, wc_ref, a_ref, bc_ref, h0r_ref, h0i_ref, d_ref, wa_ref, wg_ref,
                       xo_ref, sr_ref, si_ref):
    x = x_ref[...]
    u = _rms(x, g_ref[...])
    ub = u.astype(BF16)
    ys = []
    for j in range(N_SLABS):
        sl = slice(SLAB_STATES * j, SLAB_STATES * (j + 1))
        res = jnp.dot(ub[:, LANES * j:LANES * (j + 1)], wb_ref[j], preferred_element_type=F32)
        lr = a_ref[0:1, sl]
        li = bc_ref[SUBLANES - 1:SUBLANES, sl]
        h0r, h0i = h0r_ref[:, sl], h0i_ref[:, sl]
        xr = res[:, :SLAB_STATES] + (lr * h0r - li * h0i)
        xi = res[:, SLAB_STATES:] + (lr * h0i + li * h0r)
        sr_ref[:, sl] = xr
        si_ref[:, sl] = xi
        s = jnp.concatenate([xr, xi], axis=1).astype(BF16)
        ys.append(jnp.dot(s, wc_ref[j], preferred_element_type=F32))
    y = jnp.concatenate(ys, axis=1)
    xo_ref[...] = _glu_residual(x, y, u, d_ref[...], wa_ref[...], wg_ref[...])


def _ssm_sample_layer(x, g, prep, h0r, h0i, d, wa, wg):
    a_rows, bc_rows, wb, wc = prep
    n = x.shape[0]
    return pl.pallas_call(
        _ssm_sample_kernel,
        out_shape=[jax.ShapeDtypeStruct(x.shape, F32),
                   jax.ShapeDtypeStruct((n, N_STATES), F32),
                   jax.ShapeDtypeStruct((n, N_STATES), F32)],
        compiler_params=_cparams(None),
        name="ssm_sample",
    )(x, g, wb, wc, a_rows, bc_rows, h0r, h0i, d, wa, wg)


ROUTER_W = LANES
EXPERT_LANE0 = N_EXPERT_GROUPS


def _route(h, wr, br):
    h_hi = h.astype(BF16)
    h_lo = (h - h_hi.astype(F32)).astype(BF16)
    w_hi = wr.astype(BF16)
    w_lo = (wr - w_hi.astype(F32)).astype(BF16)
    dot = lambda a, b: jnp.dot(a, b, preferred_element_type=F32)
    logits = dot(h_hi, w_hi) + (dot(h_lo, w_hi) + dot(h_hi, w_lo)) + br
    lane = lax.broadcasted_iota(jnp.int32, logits.shape, 1)
    neg = -jnp.inf
    gl = jnp.where(lane < N_EXPERT_GROUPS, logits, neg)
    gmax = jnp.max(gl, axis=-1, keepdims=True)
    g_sel = jnp.min(jnp.where(gl == gmax, lane, ROUTER_W), axis=-1, keepdims=True)
    p_grp = 1.0 / jnp.sum(jnp.exp(gl - gmax), axis=-1, keepdims=True)
    lane_grp = (lane - EXPERT_LANE0) // EXPERTS_PER_GROUP
    in_grp = (lane >= EXPERT_LANE0) & (lane < EXPERT_LANE0 + N_EXPERTS) & (lane_grp == g_sel)
    el = jnp.where(in_grp, logits, neg)
    v1 = jnp.max(el, axis=-1, keepdims=True)
    i1 = jnp.min(jnp.where(el == v1, lane, ROUTER_W), axis=-1, keepdims=True)
    el2 = jnp.where(lane == i1, neg, el)
    v2 = jnp.max(el2, axis=-1, keepdims=True)
    i2 = jnp.min(jnp.where(el2 == v2, lane, ROUTER_W), axis=-1, keepdims=True)
    e2 = jnp.exp(v2 - v1)
    w1 = p_grp / (1.0 + e2)
    w2 = p_grp * e2 / (1.0 + e2)
    return jnp.where(lane == i1, w1, 0.0) + jnp.where(lane == i2, w2, 0.0)


def _moe_kernel(x_ref, p_ref, gffn_ref, wr_ref, br_ref, wg_ref, wu_ref, wd_ref,
                gple_ref, wpg_ref, wpp_ref, gfin_ref, o_ref, hn_scr, gates_scr, acc_scr, *, final):
    e = pl.program_id(1)

    @pl.when(e == 0)
    def _():
        x = x_ref[...]
        h = _rms(x, gffn_ref[...])
        hn_scr[...] = h.astype(BF16)
        gates_scr[...] = _route(h, wr_ref[...], br_ref[...])
        acc_scr[...] = x

    hb = hn_scr[...]
    gt = jnp.dot(hb, wg_ref[0], preferred_element_type=F32)
    up = jnp.dot(hb, wu_ref[0], preferred_element_type=F32)
    a = (gt * _sigmoid(gt) * up).astype(BF16)
    gates = gates_scr[...]
    lane = lax.broadcasted_iota(jnp.int32, gates.shape, 1)
    ge = jnp.sum(jnp.where(lane == e + EXPERT_LANE0, gates, 0.0), axis=-1, keepdims=True)
    acc_scr[...] += ge * jnp.dot(a, wd_ref[0], preferred_element_type=F32)

    @pl.when(e == pl.num_programs(1) - 1)
    def _():
        x1 = acc_scr[...]
        gate = _sigmoid(jnp.dot(_rms(x1, gple_ref[...]).astype(BF16), wpg_ref[...], preferred_element_type=F32))
        x2 = x1 + gate * jnp.dot(p_ref[...].astype(BF16), wpp_ref[...], preferred_element_type=F32)
        o_ref[...] = _rms(x2, gfin_ref[...]) if final else x2


def _moe_ple_layer(x, p, gffn, wr, br, wg, wu, wd, gple, wpg, wpp, gfin, final):
    n, d = x.shape
    tm = min(1024, n)
    kern = functools.partial(_moe_kernel, final=final)
    tok = lambda w: pl.BlockSpec((tm, w), lambda t, e: (t, 0))
    const = lambda s: pl.BlockSpec(s, lambda t, e: (0,) * len(s))
    return pl.pallas_call(
        kern,
        grid=(n // tm, N_EXPERTS),
        in_specs=[
            tok(d), tok(PLE_DIM), const((1, d)), const((d, ROUTER_W)), const((1, ROUTER_W)),
            pl.BlockSpec((1, d, D_EXPERT), lambda t, e: (e, 0, 0)),
            pl.BlockSpec((1, d, D_EXPERT), lambda t, e: (e, 0, 0)),
            pl.BlockSpec((1, D_EXPERT, d), lambda t, e: (e, 0, 0)),
            const((1, d)), const((d, d)), const((PLE_DIM, d)), const((1, d)),
        ],
        out_specs=tok(d),
        out_shape=jax.ShapeDtypeStruct((n, d), F32),
        scratch_shapes=[pltpu.VMEM((tm, d), BF16), pltpu.VMEM((tm, ROUTER_W), F32), pltpu.VMEM((tm, d), F32)],
        compiler_params=_cparams(("arbitrary", "arbitrary")),
        name="moe_ple",
    )(x, p, gffn, wr, br, wg, wu, wd, gple, wpg, wpp, gfin)


def kernel(x_prompt, x_sample, cache_k_win, cache_v_win, state_ssm_re, state_ssm_im, p_prompt, p_sample,
           rel_bias_table, g_mix, w_qkv, w_o, sinks, ssm_a_re, ssm_a_im, ssm_log_dt, ssm_b_re, ssm_b_im,
           ssm_c_re, ssm_c_im, ssm_d, w_glu_a, w_glu_b, g_ffn, w_router_group, b_router_group,
           w_router_expert, b_router_expert, w_exp_gate, w_exp_up, w_exp_down, g_ple, w_ple_gate,
           w_ple_proj, g_final):
    bsz, seq, d = x_prompt.shape
    ns = x_sample.shape[0]
    depth = g_mix.shape[0]
    row = lambda v: v.reshape(1, -1).astype(F32)

    qi = np.arange(WINDOW)[:, None]
    sj = np.arange(2 * WINDOW)[None, :]
    idx_p = _t5_bucket_np(qi + WINDOW - sj)
    idx_s = np.broadcast_to(_t5_bucket_np(WINDOW - 1 - np.arange(WINDOW))[None, :], (SUBLANES, WINDOW))
    bias_p = _bias_from_table(rel_bias_table, idx_p)
    bias_s = _bias_from_table(rel_bias_table, idx_s)[:, 0, :]

    wqkv_b = w_qkv[0].astype(BF16)
    wo_b = w_o[0].astype(BF16)
    wg_b, wu_b, wd_b = w_exp_gate.astype(BF16), w_exp_up.astype(BF16), w_exp_down.astype(BF16)
    wpg_b, wpp_b = w_ple_gate.astype(BF16), w_ple_proj.astype(BF16)
    wa_b, wgl_b = w_glu_a[0].astype(BF16), w_glu_b[0].astype(BF16)
    pad = ROUTER_W - N_EXPERT_GROUPS - N_EXPERTS
    wr = jnp.concatenate([w_router_group, w_router_expert.reshape(depth, d, N_EXPERTS),
                          jnp.zeros((depth, d, pad), F32)], axis=-1)
    br = jnp.concatenate([b_router_group, b_router_expert.reshape(depth, N_EXPERTS),
                          jnp.zeros((depth, pad), F32)], axis=-1)

    prep = _ssm_prep(ssm_a_re[0], ssm_a_im[0], ssm_log_dt[0], ssm_b_re[0], ssm_b_im[0], ssm_c_re[0], ssm_c_im[0])
    d_row = row(ssm_d[0])

    def moe(x2d, p2d, i):
        return _moe_ple_layer(x2d, p2d, row(g_ffn[i]), wr[i], br[i:i + 1], wg_b[i], wu_b[i], wd_b[i],
                              row(g_ple[i]), wpg_b[i], wpp_b[i], row(g_final), final=(i == depth - 1))

    xp, kp, vp = _attn_prompt_layer(x_prompt, row(g_mix[0]), wqkv_b, wo_b, bias_p, sinks[0])
    xp = moe(xp.reshape(bsz * seq, d), p_prompt[0].reshape(bsz * seq, PLE_DIM), 0).reshape(bsz, seq, d)
    xp, st_p = _ssm_prompt_layer(xp, row(g_mix[1]), prep, d_row, wa_b, wgl_b)
    y_prompt = moe(xp.reshape(bsz * seq, d), p_prompt[1].reshape(bsz * seq, PLE_DIM), 1).reshape(bsz, seq, d)

    xs = x_sample.reshape(ns, d)
    xs, kws, vws = _attn_sample_layer(xs, row(g_mix[0]), wqkv_b, wo_b, bias_s, sinks[0],
                                      cache_k_win[0], cache_v_win[0])
    xs = moe(xs, p_sample[0].reshape(ns, PLE_DIM), 0)
    xs, sr_s, si_s = _ssm_sample_layer(xs, row(g_mix[1]), prep, state_ssm_re[0].reshape(ns, N_STATES),
                                       state_ssm_im[0].reshape(ns, N_STATES), d_row, wa_b, wgl_b)
    y_sample = moe(xs, p_sample[1].reshape(ns, PLE_DIM), 1).reshape(ns, 1, d)

    half = SUBLANES // 2
    win = lambda a, n: a.reshape(1, n, WINDOW, N_KV_HEADS, HEAD_DIM)
    st = lambda a, n: a.reshape(1, n, N_SSM_GROUPS, SSM_STATE)
    return (y_prompt, y_sample, win(kp, bsz), win(vp, bsz), win(kws, ns), win(vws, ns),
            st(st_p[:bsz], bsz), st(st_p[half:half + bsz], bsz), st(sr_s, ns), st(si_s, ns))
```

```python
import functools
import math

import numpy as np
import jax
import jax.numpy as jnp
from jax import lax
from jax.experimental import pallas as pl
from jax.experimental.pallas import tpu as pltpu

F32 = jnp.float32
BF16 = jnp.bfloat16

D_MODEL = 1024
HEAD_DIM = 64
N_HEADS = 16
N_KV_HEADS = 2
GQ = N_HEADS // N_KV_HEADS
WINDOW = 128
N_BUCKETS = 32
MAX_EXACT = 16
MAX_DISTANCE = 128
SSM_GROUP = 16
N_SSM_GROUPS = 64
SSM_STATE = 64
N_STATES = N_SSM_GROUPS * SSM_STATE
N_EXPERT_GROUPS = 4
EXPERTS_PER_GROUP = 8
N_EXPERTS = 32
D_EXPERT = 256
PLE_DIM = 256
RMS_EPS = 1e-6
NEG_INF = -1e30

LANES = 128
SUBLANES = 8
KV_W = N_KV_HEADS * HEAD_DIM
QKV_W = D_MODEL + 2 * KV_W
SLAB_STATES = 512
N_SLABS = D_MODEL // LANES
VMEM_LIMIT = 56 * 1024 * 1024


def _cparams(sem):
    return pltpu.CompilerParams(dimension_semantics=sem, vmem_limit_bytes=VMEM_LIMIT)


def _rms(x, g):
    return x * lax.rsqrt(jnp.mean(x * x, axis=-1, keepdims=True) + RMS_EPS) * g


def _gelu_tanh(x):
    c = math.sqrt(2.0 / math.pi)
    return x * (0.5 * (1.0 + jnp.tanh(c * (x + 0.044715 * (x * x * x)))))


def _sigmoid(x):
    return 1.0 / (1.0 + jnp.exp(-x))


def _full(shape):
    n = len(shape)
    return pl.BlockSpec(shape, lambda *_: (0,) * n)


def _t5_bucket_np(dist):
    n = np.maximum(dist, 0)
    nf = np.maximum(n, 1).astype(np.float64)
    large = MAX_EXACT + (np.log(nf / MAX_EXACT) / math.log(MAX_DISTANCE / MAX_EXACT)
                         * (N_BUCKETS - MAX_EXACT)).astype(np.int32)
    large = np.minimum(large, N_BUCKETS - 1)
    return np.where(n < MAX_EXACT, n, large).astype(np.int32)


def _bias_kernel(table_ref, idx_ref, o_ref):
    h = pl.program_id(0)
    idx = idx_ref[...]
    acc = jnp.zeros(idx.shape, F32)
    for b in range(N_BUCKETS):
        acc = jnp.where(idx == b, table_ref[b, h], acc)
    o_ref[0] = acc


def _bias_from_table(table, idx_np):
    q, k = idx_np.shape
    return pl.pallas_call(
        _bias_kernel,
        grid=(N_HEADS,),
        in_specs=[pl.BlockSpec(memory_space=pltpu.SMEM), _full((q, k))],
        out_specs=pl.BlockSpec((1, q, k), lambda h: (h, 0, 0)),
        out_shape=jax.ShapeDtypeStruct((N_HEADS, q, k), F32),
        name="rel_bias",
    )(table, jnp.asarray(idx_np))


def _softmax_sink_pv(sc, sink, vpad):
    m = jnp.maximum(jnp.max(sc, axis=-1, keepdims=True), sink)
    e = jnp.exp(sc - m)
    den = jnp.sum(e, axis=-1, keepdims=True) + jnp.exp(sink - m)
    pr = (e / den).astype(BF16)
    return jnp.dot(pr, vpad, preferred_element_type=F32)


def _attn_prompt_kernel(sink_ref, x_ref, g_ref, wqkv_ref, wo_ref, bias_ref,
                        xo_ref, k_ref, v_ref, kv_scr, q_scr, o_scr, *, tq):
    i = pl.program_id(1)
    nsub = tq // WINDOW

    @pl.when(i == 0)
    def _():
        kv_scr[0:WINDOW, :] = jnp.zeros((WINDOW, 2 * KV_W), F32)

    x = x_ref[0]
    h = _rms(x, g_ref[...]).astype(BF16)
    qkv = jnp.dot(h, wqkv_ref[...], preferred_element_type=F32)
    q_scr[...] = (qkv[:, :D_MODEL] * (HEAD_DIM ** -0.5)).astype(BF16)
    kv_scr[WINDOW:WINDOW + tq, :] = qkv[:, D_MODEL:]
    k_ref[0] = qkv[tq - WINDOW:, D_MODEL:D_MODEL + KV_W]
    v_ref[0] = qkv[tq - WINDOW:, D_MODEL + KV_W:]

    qi = lax.broadcasted_iota(jnp.int32, (WINDOW, 2 * WINDOW), 0)
    sj = lax.broadcasted_iota(jnp.int32, (WINDOW, 2 * WINDOW), 1)
    dist = qi + WINDOW - sj
    band_ok = (dist >= 0) & (dist < WINDOW)
    lo = lax.broadcasted_iota(jnp.int32, (2 * WINDOW, KV_W), 1) < HEAD_DIM

    for s in range(nsub):
        band = kv_scr[WINDOW * s:WINDOW * s + 2 * WINDOW, :]
        kband, vband = band[:, :KV_W], band[:, KV_W:]
        kroll = pltpu.roll(kband, HEAD_DIM, 1)
        vroll = pltpu.roll(vband, HEAD_DIM, 1)
        kpad = [[jnp.where(lo, kband, 0.0).astype(BF16), jnp.where(lo, 0.0, kroll).astype(BF16)],
                [jnp.where(lo, kroll, 0.0).astype(BF16), jnp.where(lo, 0.0, kband).astype(BF16)]]
        vpad = [[jnp.where(lo, vband, 0.0).astype(BF16), jnp.where(lo, 0.0, vroll).astype(BF16)],
                [jnp.where(lo, vroll, 0.0).astype(BF16), jnp.where(lo, 0.0, vband).astype(BF16)]]
        min_j = jnp.where(i * nsub + s > 0, 0, WINDOW)
        valid = band_ok & (sj >= min_j)
        for j in range(N_HEADS // 2):
            q2 = q_scr[WINDOW * s:WINDOW * (s + 1), LANES * j:LANES * (j + 1)]
            g = (2 * j) // GQ
            acc = jnp.zeros((WINDOW, LANES), F32)
            for p in range(2):
                hd = 2 * j + p
                sc = lax.dot_general(q2, kpad[g][p], (((1,), (1,)), ((), ())),
                                     preferred_element_type=F32)
                sc = jnp.where(valid, sc + bias_ref[hd], NEG_INF)
                acc = acc + _softmax_sink_pv(sc, sink_ref[hd], vpad[g][p])
            o_scr[WINDOW * s:WINDOW * (s + 1), LANES * j:LANES * (j + 1)] = acc.astype(BF16)

    kv_scr[0:WINDOW, :] = kv_scr[tq:tq + WINDOW, :]
    xo_ref[0] = x + jnp.dot(o_scr[...], wo_ref[...], preferred_element_type=F32)


def _attn_prompt_layer(x, g, wqkv, wo, bias, sinks):
    b, t, d = x.shape
    tq = min(512, t)
    kern = functools.partial(_attn_prompt_kernel, tq=tq)
    return pl.pallas_call(
        kern,
        grid=(b, t // tq),
        in_specs=[
            pl.BlockSpec(memory_space=pltpu.SMEM),
            pl.BlockSpec((1, tq, d), lambda bi, i: (bi, i, 0)),
            _full((1, d)),
            _full((d, QKV_W)),
            _full((d, d)),
            _full((N_HEADS, WINDOW, 2 * WINDOW)),
        ],
        out_specs=[
            pl.BlockSpec((1, tq, d), lambda bi, i: (bi, i, 0)),
            pl.BlockSpec((1, WINDOW, KV_W), lambda bi, i: (bi, 0, 0)),
            pl.BlockSpec((1, WINDOW, KV_W), lambda bi, i: (bi, 0, 0)),
        ],
        out_shape=[
            jax.ShapeDtypeStruct((b, t, d), F32),
            jax.ShapeDtypeStruct((b, WINDOW, KV_W), F32),
            jax.ShapeDtypeStruct((b, WINDOW, KV_W), F32),
        ],
        scratch_shapes=[
            pltpu.VMEM((WINDOW + tq, 2 * KV_W), F32),
            pltpu.VMEM((tq, d), BF16),
            pltpu.VMEM((tq, d), BF16),
        ],
        compiler_params=_cparams(("arbitrary", "arbitrary")),
        name="attn_prompt",
    )(sinks, x, g, wqkv, wo, bias)


def _norm_linear_kernel(x_ref, g_ref, w_ref, o_ref):
    h = _rms(x_ref[...], g_ref[...]).astype(BF16)
    o_ref[...] = jnp.dot(h, w_ref[...], preferred_element_type=F32)


def _norm_linear(x, g, w):
    n, d = x.shape
    return pl.pallas_call(
        _norm_linear_kernel,
        out_shape=jax.ShapeDtypeStruct((n, w.shape[1]), F32),
        compiler_params=_cparams(None),
        name="norm_linear",
    )(x, g, w)


def _linear_residual_kernel(x_ref, o_ref, w_ref, xo_ref):
    xo_ref[...] = x_ref[...] + jnp.dot(o_ref[...].astype(BF16), w_ref[...], preferred_element_type=F32)


def _linear_residual(x, o, w):
    return pl.pallas_call(
        _linear_residual_kernel,
        out_shape=jax.ShapeDtypeStruct(x.shape, F32),
        compiler_params=_cparams(None),
        name="linear_residual",
    )(x, o, w)


def _attn_sample_kernel(sink_ref, q_ref, ck_ref, cv_ref, kn_ref, vn_ref, bias_ref, o_ref, kw_ref, vw_ref):
    kw = jnp.concatenate([ck_ref[:, 1:, :], kn_ref[...]], axis=1)
    vw = jnp.concatenate([cv_ref[:, 1:, :], vn_ref[...]], axis=1)
    kw_ref[...] = kw
    vw_ref[...] = vw
    q = (q_ref[...] * (HEAD_DIM ** -0.5)).astype(BF16)
    sc = jnp.einsum('bhc,bjc->bhj', q, kw.astype(BF16), preferred_element_type=F32)
    sc = sc + bias_ref[...][None]
    sink = sink_ref[...][None]
    m = jnp.maximum(jnp.max(sc, axis=-1, keepdims=True), sink)
    e = jnp.exp(sc - m)
    den = jnp.sum(e, axis=-1, keepdims=True) + jnp.exp(sink - m)
    pr = (e / den).astype(BF16)
    o_ref[...] = jnp.einsum('bhj,bjc->bhc', pr, vw.astype(BF16), preferred_element_type=F32)


def _attn_sample(q3, ck, cv, kn, vn, bias_s, sinks_col):
    n = q3.shape[0]
    bb = 8
    blk = lambda s: pl.BlockSpec((bb,) + s, lambda i: (i, 0, 0))
    return pl.pallas_call(
        _attn_sample_kernel,
        grid=(n // bb,),
        in_specs=[_full((N_HEADS, 1)), blk((N_HEADS, KV_W)), blk((WINDOW, KV_W)), blk((WINDOW, KV_W)),
                  blk((1, KV_W)), blk((1, KV_W)), _full((N_HEADS, WINDOW))],
        out_specs=[blk((N_HEADS, KV_W)), blk((WINDOW, KV_W)), blk((WINDOW, KV_W))],
        out_shape=[jax.ShapeDtypeStruct((n, N_HEADS, KV_W), F32),
                   jax.ShapeDtypeStruct((n, WINDOW, KV_W), F32),
                   jax.ShapeDtypeStruct((n, WINDOW, KV_W), F32)],
        compiler_params=_cparams(("arbitrary",)),
        name="attn_sample",
    )(sinks_col, q3, ck, cv, kn, vn, bias_s)


def _attn_sample_layer(x, g, wqkv, wo, bias_s, sinks, cache_k, cache_v):
    n = x.shape[0]
    qkv = _norm_linear(x, g, wqkv)
    q = qkv[:, :D_MODEL].reshape(n, N_KV_HEADS, GQ, 1, HEAD_DIM)
    place = jnp.eye(N_KV_HEADS, dtype=F32).reshape(1, N_KV_HEADS, 1, N_KV_HEADS, 1)
    q3 = (q * place).reshape(n, N_HEADS, KV_W)
    kn = qkv[:, D_MODEL:D_MODEL + KV_W].reshape(n, 1, KV_W)
    vn = qkv[:, D_MODEL + KV_W:].reshape(n, 1, KV_W)
    o3, kw, vw = _attn_sample(q3, cache_k.reshape(n, WINDOW, KV_W), cache_v.reshape(n, WINDOW, KV_W),
                              kn, vn, bias_s, sinks.reshape(N_HEADS, 1))
    o5 = o3.reshape(n, N_KV_HEADS, GQ, N_KV_HEADS, HEAD_DIM)
    o = jnp.stack([o5[:, gi, :, gi, :] for gi in range(N_KV_HEADS)], axis=1).reshape(n, D_MODEL)
    return _linear_residual(x, o, wo), kw, vw


def _ssm_prep_kernel(ar_ref, ai_ref, ldt_ref, br_ref, bi_ref, c_ref, a_out, bc_out, wb_out, wc_out):
    ar, ai = ar_ref[...], ai_ref[...]
    dt = jnp.exp(ldt_ref[...])
    mag = jnp.exp(ar * dt)
    lr = mag * jnp.cos(ai * dt)
    li = mag * jnp.sin(ai * dt)
    den = ar * ar + ai * ai
    cr = ((lr - 1.0) * ar + li * ai) / den
    ci = (li * ar - (lr - 1.0) * ai) / den
    row = lax.broadcasted_iota(jnp.int32, (SUBLANES, N_STATES), 0)
    a_out[...] = jnp.broadcast_to(lr, (SUBLANES, N_STATES))
    bc_out[...] = jnp.where(row < SUBLANES // 2, -li, li)
    for j in range(N_SLABS):
        crj = cr[:, SLAB_STATES * j:SLAB_STATES * (j + 1)]
        cij = ci[:, SLAB_STATES * j:SLAB_STATES * (j + 1)]
        br, bi = br_ref[j], bi_ref[j]
        wb_out[j, :, :SLAB_STATES] = (crj * br - cij * bi).astype(BF16)
        wb_out[j, :, SLAB_STATES:] = (crj * bi + cij * br).astype(BF16)
        wc_out[j, :SLAB_STATES, :] = c_ref[j, :SLAB_STATES, :].astype(BF16)
        wc_out[j, SLAB_STATES:, :] = (-c_ref[j, SLAB_STATES:, :]).astype(BF16)


def _ssm_prep(a_re, a_im, log_dt, b_re, b_im, c_re, c_im):
    eye = jnp.eye(SUBLANES, dtype=F32)

    def blk_b(b):
        t = b.reshape(N_SLABS, 8, SSM_STATE, SSM_GROUP).transpose(0, 1, 3, 2)
        return jnp.einsum('jghp,gk->jghkp', t, eye).reshape(N_SLABS, LANES, SLAB_STATES)

    def blk_c(c):
        t = c.reshape(N_SLABS, 8, SSM_GROUP, SSM_STATE).transpose(0, 1, 3, 2)
        return jnp.einsum('jgph,gk->jgpkh', t, eye).reshape(N_SLABS, SLAB_STATES, LANES)

    row = lambda a: a.reshape(1, N_STATES)
    ldt = jnp.broadcast_to(log_dt[:, None], (N_SSM_GROUPS, SSM_STATE))
    c_all = jnp.concatenate([blk_c(c_re), blk_c(c_im)], axis=1)
    return pl.pallas_call(
        _ssm_prep_kernel,
        out_shape=[jax.ShapeDtypeStruct((SUBLANES, N_STATES), F32),
                   jax.ShapeDtypeStruct((SUBLANES, N_STATES), F32),
                   jax.ShapeDtypeStruct((N_SLABS, LANES, 2 * SLAB_STATES), BF16),
                   jax.ShapeDtypeStruct((N_SLABS, 2 * SLAB_STATES, LANES), BF16)],
        compiler_params=_cparams(None),
        name="ssm_prep",
    )(row(a_re), row(a_im), row(ldt), blk_b(b_re), blk_b(b_im), c_all)


def _glu_residual(x, y, u, d, wa, wg):
    z = _gelu_tanh(y + d * u).astype(BF16)
    out = jnp.dot(z, wa, preferred_element_type=F32) * _sigmoid(jnp.dot(z, wg, preferred_element_type=F32))
    return x + out


def _ssm_prompt_kernel(x_ref, g_ref, wb_ref, wc_ref, a_ref, bc_ref, d_ref, wa_ref, wg_ref,
                       xo_ref, st_ref, z_scr, u_scr, xs_scr, *, nb, lc):
    i = pl.program_id(0)
    blocks_per_slab = SLAB_STATES // LANES
    half = SUBLANES // 2

    @pl.when(i == 0)
    def _():
        xs_scr[...] = jnp.zeros((SUBLANES, N_STATES), F32)

    for b in range(nb):
        u_scr[b * lc:(b + 1) * lc, :] = _rms(x_ref[b], g_ref[...])
    ub = u_scr[...].astype(BF16)
    for j in range(N_SLABS):
        res = jnp.dot(ub[:, LANES * j:LANES * (j + 1)], wb_ref[j], preferred_element_type=F32)
        for b in range(nb):
            rb = res[b * lc:(b + 1) * lc]
            for c in range(blocks_per_slab):
                blk = blocks_per_slab * j + c
                z_scr[blk, pl.ds(b, lc, stride=SUBLANES), :] = rb[:, LANES * c:LANES * (c + 1)]
                z_scr[blk, pl.ds(half + b, lc, stride=SUBLANES), :] = (
                    rb[:, SLAB_STATES + LANES * c:SLAB_STATES + LANES * (c + 1)])
    if nb < half:
        for blk in range(N_STATES // LANES):
            for r in list(range(nb, half)) + list(range(half + nb, SUBLANES)):
                z_scr[blk, pl.ds(r, lc, stride=SUBLANES), :] = jnp.zeros((lc, LANES), F32)

    group = 8
    for cb in range(N_STATES // LANES // group):
        blks = [group * cb + k for k in range(group)]
        a = [a_ref[:, LANES * q:LANES * (q + 1)] for q in blks]
        bc = [bc_ref[:, LANES * q:LANES * (q + 1)] for q in blks]
        x0 = tuple(xs_scr[:, LANES * q:LANES * (q + 1)] for q in blks)

        def step(t, xs, blks=blks, a=a, bc=bc):
            r = pl.multiple_of(t * SUBLANES, SUBLANES)
            out = []
            for k, q in enumerate(blks):
                xn = a[k] * xs[k] + bc[k] * pltpu.roll(xs[k], half, 0) + z_scr[q, pl.ds(r, SUBLANES), :]
                z_scr[q, pl.ds(r, SUBLANES), :] = xn
                out.append(xn)
            return tuple(out)

        xf = lax.fori_loop(0, lc, step, x0)
        for k, q in enumerate(blks):
            xs_scr[:, LANES * q:LANES * (q + 1)] = xf[k]
    st_ref[...] = xs_scr[...]

    for b in range(nb):
        ys = []
        for j in range(N_SLABS):
            parts = [z_scr[blocks_per_slab * j + c, pl.ds(b, lc, stride=SUBLANES), :] for c in range(blocks_per_slab)]
            parts += [z_scr[blocks_per_slab * j + c, pl.ds(half + b, lc, stride=SUBLANES), :]
                      for c in range(blocks_per_slab)]
            s = jnp.concatenate(parts, axis=1).astype(BF16)
            ys.append(jnp.dot(s, wc_ref[j], preferred_element_type=F32))
        y = jnp.concatenate(ys, axis=1)
        xo_ref[b] = _glu_residual(x_ref[b], y, u_scr[b * lc:(b + 1) * lc, :], d_ref[...], wa_ref[...], wg_ref[...])


def _ssm_prompt_layer(x, g, prep, d, wa, wg):
    a_rows, bc_rows, wb, wc = prep
    nb, t, dm = x.shape
    assert nb <= SUBLANES // 2
    lc = min(128, t)
    kern = functools.partial(_ssm_prompt_kernel, nb=nb, lc=lc)
    return pl.pallas_call(
        kern,
        grid=(t // lc,),
        in_specs=[
            pl.BlockSpec((nb, lc, dm), lambda i: (0, i, 0)),
            _full((1, dm)),
            _full(wb.shape), _full(wc.shape),
            _full((SUBLANES, N_STATES)), _full((SUBLANES, N_STATES)),
            _full((1, dm)), _full((dm, dm)), _full((dm, dm)),
        ],
        out_specs=[pl.BlockSpec((nb, lc, dm), lambda i: (0, i, 0)), _full((SUBLANES, N_STATES))],
        out_shape=[jax.ShapeDtypeStruct((nb, t, dm), F32), jax.ShapeDtypeStruct((SUBLANES, N_STATES), F32)],
        scratch_shapes=[
            pltpu.VMEM((N_STATES // LANES, SUBLANES * lc, LANES), F32),
            pltpu.VMEM((nb * lc, dm), F32),
            pltpu.VMEM((SUBLANES, N_STATES), F32),
        ],
        compiler_params=_cparams(("arbitrary",)),
        name="ssm_prompt",
    )(x, g, wb, wc, a_rows, bc_rows, d, wa, wg)


def _ssm_sample_kernel(x_ref, g_ref, wb_ref, wc_ref, a_ref, bc_ref, h0r_ref, h0i_ref, d_ref, wa_ref, wg_ref,
                       xo_ref, sr_ref, si_ref):
    x = x_ref[...]
    u = _rms(x, g_ref[...])
    ub = u.astype(BF16)
    ys = []
    for j in range(N_SLABS):
        sl = slice(SLAB_STATES * j, SLAB_STATES * (j + 1))
        res = jnp.dot(ub[:, LANES * j:LANES * (j + 1)], wb_ref[j], preferred_element_type=F32)
        lr = a_ref[0:1, sl]
        li = bc_ref[SUBLANES - 1:SUBLANES, sl]
        h0r, h0i = h0r_ref[:, sl], h0i_ref[:, sl]
        xr = res[:, :SLAB_STATES] + (lr * h0r - li * h0i)
        xi = res[:, SLAB_STATES:] + (lr * h0i + li * h0r)
        sr_ref[:, sl] = xr
        si_ref[:, sl] = xi
        s = jnp.concatenate([xr, xi], axis=1).astype(BF16)
        ys.append(jnp.dot(s, wc_ref[j], preferred_element_type=F32))
    y = jnp.concatenate(ys, axis=1)
    xo_ref[...] = _glu_residual(x, y, u, d_ref[...], wa_ref[...], wg_ref[...])


def _ssm_sample_layer(x, g, prep, h0r, h0i, d, wa, wg):
    a_rows, bc_rows, wb, wc = prep
    n = x.shape[0]
    return pl.pallas_call(
        _ssm_sample_kernel,
        out_shape=[jax.ShapeDtypeStruct(x.shape, F32),
                   jax.ShapeDtypeStruct((n, N_STATES), F32),
                   jax.ShapeDtypeStruct((n, N_STATES), F32)],
        compiler_params=_cparams(None),
        name="ssm_sample",
    )(x, g, wb, wc, a_rows, bc_rows, h0r, h0i, d, wa, wg)


ROUTER_W = LANES
EXPERT_LANE0 = N_EXPERT_GROUPS


def _route_topk(h, wr, br):
    h_hi = h.astype(BF16)
    h_lo = (h - h_hi.astype(F32)).astype(BF16)
    w_hi = wr.astype(BF16)
    w_lo = (wr - w_hi.astype(F32)).astype(BF16)
    dot = lambda a, b: jnp.dot(a, b, preferred_element_type=F32)
    logits = dot(h_hi, w_hi) + (dot(h_lo, w_hi) + dot(h_hi, w_lo)) + br
    lane = lax.broadcasted_iota(jnp.int32, logits.shape, 1)
    neg = -jnp.inf
    gl = jnp.where(lane < N_EXPERT_GROUPS, logits, neg)
    gmax = jnp.max(gl, axis=-1, keepdims=True)
    g_sel = jnp.min(jnp.where(gl == gmax, lane, ROUTER_W), axis=-1, keepdims=True)
    p_grp = 1.0 / jnp.sum(jnp.exp(gl - gmax), axis=-1, keepdims=True)
    lane_grp = (lane - EXPERT_LANE0) // EXPERTS_PER_GROUP
    in_grp = (lane >= EXPERT_LANE0) & (lane < EXPERT_LANE0 + N_EXPERTS) & (lane_grp == g_sel)
    el = jnp.where(in_grp, logits, neg)
    v1 = jnp.max(el, axis=-1, keepdims=True)
    i1 = jnp.min(jnp.where(el == v1, lane, ROUTER_W), axis=-1, keepdims=True)
    el2 = jnp.where(lane == i1, neg, el)
    v2 = jnp.max(el2, axis=-1, keepdims=True)
    i2 = jnp.min(jnp.where(el2 == v2, lane, ROUTER_W), axis=-1, keepdims=True)
    e2 = jnp.exp(v2 - v1)
    w1 = p_grp / (1.0 + e2)
    w2 = p_grp * e2 / (1.0 + e2)
    return lane, i1, i2, w1, w2


def _route(h, wr, br):
    lane, i1, i2, w1, w2 = _route_topk(h, wr, br)
    return jnp.where(lane == i1, w1, 0.0) + jnp.where(lane == i2, w2, 0.0)


def _moe_kernel(x_ref, p_ref, gffn_ref, wr_ref, br_ref, wg_ref, wu_ref, wd_ref,
                gple_ref, wpg_ref, wpp_ref, gfin_ref, o_ref, hn_scr, gates_scr, acc_scr, *, final):
    e = pl.program_id(1)

    @pl.when(e == 0)
    def _():
        x = x_ref[...]
        h = _rms(x, gffn_ref[...])
        hn_scr[...] = h.astype(BF16)
        gates_scr[...] = _route(h, wr_ref[...], br_ref[...])
        acc_scr[...] = x

    hb = hn_scr[...]
    gt = jnp.dot(hb, wg_ref[0], preferred_element_type=F32)
    up = jnp.dot(hb, wu_ref[0], preferred_element_type=F32)
    a = (gt * _sigmoid(gt) * up).astype(BF16)
    gates = gates_scr[...]
    lane = lax.broadcasted_iota(jnp.int32, gates.shape, 1)
    ge = jnp.sum(jnp.where(lane == e + EXPERT_LANE0, gates, 0.0), axis=-1, keepdims=True)
    acc_scr[...] += ge * jnp.dot(a, wd_ref[0], preferred_element_type=F32)

    @pl.when(e == pl.num_programs(1) - 1)
    def _():
        x1 = acc_scr[...]
        gate = _sigmoid(jnp.dot(_rms(x1, gple_ref[...]).astype(BF16), wpg_ref[...], preferred_element_type=F32))
        x2 = x1 + gate * jnp.dot(p_ref[...].astype(BF16), wpp_ref[...], preferred_element_type=F32)
        o_ref[...] = _rms(x2, gfin_ref[...]) if final else x2


def _moe_ple_layer(x, p, gffn, wr, br, wg, wu, wd, gple, wpg, wpp, gfin, final):
    n, d = x.shape
    tm = min(1024, n)
    kern = functools.partial(_moe_kernel, final=final)
    tok = lambda w: pl.BlockSpec((tm, w), lambda t, e: (t, 0))
    const = lambda s: pl.BlockSpec(s, lambda t, e: (0,) * len(s))
    return pl.pallas_call(
        kern,
        grid=(n // tm, N_EXPERTS),
        in_specs=[
            tok(d), tok(PLE_DIM), const((1, d)), const((d, ROUTER_W)), const((1, ROUTER_W)),
            pl.BlockSpec((1, d, D_EXPERT), lambda t, e: (e, 0, 0)),
            pl.BlockSpec((1, d, D_EXPERT), lambda t, e: (e, 0, 0)),
            pl.BlockSpec((1, D_EXPERT, d), lambda t, e: (e, 0, 0)),
            const((1, d)), const((d, d)), const((PLE_DIM, d)), const((1, d)),
        ],
        out_specs=tok(d),
        out_shape=jax.ShapeDtypeStruct((n, d), F32),
        scratch_shapes=[pltpu.VMEM((tm, d), BF16), pltpu.VMEM((tm, ROUTER_W), F32), pltpu.VMEM((tm, d), F32)],
        compiler_params=_cparams(("arbitrary", "arbitrary")),
        name="moe_ple",
    )(x, p, gffn, wr, br, wg, wu, wd, gple, wpg, wpp, gfin)


TOKEN_TILE_ROWS = D_MODEL // LANES
INFO_W1, INFO_W2, INFO_E1, INFO_E2, INFO_R1, INFO_R2 = range(6)
EXPERT_TILE = 256


def _to_token_tiles(ref_2d, val, n):
    for s in range(TOKEN_TILE_ROWS):
        ref_2d[pl.ds(s, n, stride=TOKEN_TILE_ROWS), :] = val[:, LANES * s:LANES * (s + 1)]


def _from_token_tiles(ref_2d, n):
    return jnp.concatenate([ref_2d[pl.ds(s, n, stride=TOKEN_TILE_ROWS), :] for s in range(TOKEN_TILE_ROWS)],
                           axis=1)


def _tile_rows(i):
    return pl.ds(pl.multiple_of(i * TOKEN_TILE_ROWS, TOKEN_TILE_ROWS), TOKEN_TILE_ROWS)


def _route_kernel(x_ref, g_ref, wr_ref, br_ref, info_ref, cnt_ref, base_scr):
    i = pl.program_id(0)

    @pl.when(i == 0)
    def _():
        base_scr[...] = jnp.zeros(base_scr.shape, F32)

    h = _rms(x_ref[...], g_ref[...])
    lane, i1, i2, w1, w2 = _route_topk(h, wr_ref[...], br_ref[...])
    tm = h.shape[0]
    chosen = jnp.where((lane == i1) | (lane == i2), 1.0, 0.0)
    r = lax.broadcasted_iota(jnp.int32, (tm, tm), 0)
    c = lax.broadcasted_iota(jnp.int32, (tm, tm), 1)
    before = jnp.where(c < r, 1.0, 0.0).astype(BF16)
    prefix = jnp.dot(before, chosen.astype(BF16), preferred_element_type=F32) + base_scr[...]
    rank1 = jnp.sum(jnp.where(lane == i1, prefix, 0.0), axis=-1, keepdims=True)
    rank2 = jnp.sum(jnp.where(lane == i2, prefix, 0.0), axis=-1, keepdims=True)
    base_scr[...] += jnp.sum(chosen, axis=0, keepdims=True)
    cnt_ref[...] = base_scr[...]
    e1 = (i1 - EXPERT_LANE0).astype(F32)
    e2 = (i2 - EXPERT_LANE0).astype(F32)
    info = jnp.zeros(h.shape[:1] + (ROUTER_W,), F32)
    for ln, v in ((INFO_W1, w1), (INFO_W2, w2), (INFO_E1, e1), (INFO_E2, e2), (INFO_R1, rank1), (INFO_R2, rank2)):
        info = jnp.where(lane == ln, v, info)
    info_ref[...] = info


def _route_call(x, gffn, wr, br):
    n, d = x.shape
    tm = 512
    return pl.pallas_call(
        _route_kernel,
        grid=(n // tm,),
        in_specs=[pl.BlockSpec((tm, d), lambda i: (i, 0)), _full((1, d)), _full((d, ROUTER_W)), _full((1, ROUTER_W))],
        out_specs=[pl.BlockSpec((tm, ROUTER_W), lambda i: (i, 0)), _full((1, ROUTER_W))],
        out_shape=[jax.ShapeDtypeStruct((n, ROUTER_W), F32), jax.ShapeDtypeStruct((1, ROUTER_W), F32)],
        scratch_shapes=[pltpu.VMEM((1, ROUTER_W), F32)],
        compiler_params=_cparams(("arbitrary",)),
        name="moe_route",
    )(x, gffn, wr, br)


def _dispatch_kernel(pos1_ref, pos2_ref, cnt_ref, start_ref, x_ref, g_ref, xs_hbm, hbuf, zbuf, sem, zsem,
                     *, td, n_steps, n_pad_rows):
    i = pl.program_id(0)
    slot = i % 2

    def wait_slot(sl):
        for _ in range(2):
            pltpu.make_async_copy(hbuf.at[sl], hbuf.at[sl], sem.at[sl]).wait()

    @pl.when(i >= 2)
    def _():
        wait_slot(slot)

    _to_token_tiles(hbuf.at[slot], _rms(x_ref[...], g_ref[...]), td)

    def issue(n8, carry):
        for u in range(SUBLANES):
            n = n8 * SUBLANES + u
            t = i * td + n
            src = hbuf.at[slot, _tile_rows(n), :]
            pltpu.make_async_copy(src, xs_hbm.at[_tile_rows(pos1_ref[t]), :], sem.at[slot]).start()
            pltpu.make_async_copy(src, xs_hbm.at[_tile_rows(pos2_ref[t]), :], sem.at[slot]).start()
        return carry

    lax.fori_loop(0, td // SUBLANES, issue, 0)

    @pl.when(i == 0)
    def _():
        zbuf[...] = jnp.zeros(zbuf.shape, F32)
        for e in range(N_EXPERTS + 1):
            lo = start_ref[e] + cnt_ref[e]
            hi = start_ref[e + 1]

            def zero_row(r, carry):
                pltpu.make_async_copy(zbuf, xs_hbm.at[_tile_rows(r), :], zsem).start()
                return carry

            lax.fori_loop(lo, hi, zero_row, 0)

    @pl.when(i == n_steps - 1)
    def _():
        wait_slot(slot)
        if n_steps > 1:
            wait_slot(1 - slot)
        pad = xs_hbm.at[pl.ds(0, n_pad_rows * TOKEN_TILE_ROWS), :]
        pltpu.make_async_copy(pad, pad, zsem).wait()


def _dispatch_call(pos1, pos2, cnt, start, x, gffn, a_pad, n_pad_rows):
    n, d = x.shape
    td = 512
    n_steps = n // td
    kern = functools.partial(_dispatch_kernel, td=td, n_steps=n_steps, n_pad_rows=n_pad_rows)
    return pl.pallas_call(
        kern,
        grid_spec=pltpu.PrefetchScalarGridSpec(
            num_scalar_prefetch=4,
            grid=(n_steps,),
            in_specs=[pl.BlockSpec((td, d), lambda i, *_: (i, 0)), pl.BlockSpec((1, d), lambda i, *_: (0, 0))],
            out_specs=pl.BlockSpec(memory_space=pl.ANY),
            scratch_shapes=[pltpu.VMEM((2, td * TOKEN_TILE_ROWS, LANES), F32),
                            pltpu.VMEM((TOKEN_TILE_ROWS, LANES), F32),
                            pltpu.SemaphoreType.DMA((2,)), pltpu.SemaphoreType.DMA(())],
        ),
        out_shape=jax.ShapeDtypeStruct((a_pad * TOKEN_TILE_ROWS, LANES), F32),
        compiler_params=_cparams(("arbitrary",)),
        name="moe_dispatch",
    )(pos1, pos2, cnt, start, x, gffn)


def _experts_kernel(te_ref, x_ref, wg_ref, wu_ref, wd_ref, y_ref, *, te):
    x = _from_token_tiles(x_ref, te).astype(BF16)
    gt = jnp.dot(x, wg_ref[0], preferred_element_type=F32)
    up = jnp.dot(x, wu_ref[0], preferred_element_type=F32)
    a = (gt * _sigmoid(gt) * up).astype(BF16)
    _to_token_tiles(y_ref, jnp.dot(a, wd_ref[0], preferred_element_type=F32), te)


def _experts_call(tile_expert, xs, wg, wu, wd, te):
    rows = te * TOKEN_TILE_ROWS
    n_tiles = xs.shape[0] // rows
    d = wg.shape[1]
    kern = functools.partial(_experts_kernel, te=te)
    return pl.pallas_call(
        kern,
        grid_spec=pltpu.PrefetchScalarGridSpec(
            num_scalar_prefetch=1,
            grid=(n_tiles,),
            in_specs=[pl.BlockSpec((rows, LANES), lambda t, ex: (t, 0)),
                      pl.BlockSpec((1, d, D_EXPERT), lambda t, ex: (ex[t], 0, 0)),
                      pl.BlockSpec((1, d, D_EXPERT), lambda t, ex: (ex[t], 0, 0)),
                      pl.BlockSpec((1, D_EXPERT, d), lambda t, ex: (ex[t], 0, 0))],
            out_specs=pl.BlockSpec((rows, LANES), lambda t, ex: (t, 0)),
        ),
        out_shape=jax.ShapeDtypeStruct(xs.shape, F32),
        compiler_params=_cparams(("arbitrary",)),
        name="moe_experts",
    )(tile_expert, xs, wg, wu, wd)


def _combine_kernel(pos1_ref, pos2_ref, x_ref, info_ref, p_ref, gple_ref, wpg_ref, wpp_ref, gfin_ref, ys_hbm,
                    o_ref, ybuf, sem, *, tm, n_steps, final):
    i = pl.program_id(0)
    slot = i % 2

    def issue(step, sl):
        def body(n8, carry):
            for u in range(SUBLANES):
                n = n8 * SUBLANES + u
                t = step * tm + n
                for k, pos_ref in enumerate((pos1_ref, pos2_ref)):
                    pltpu.make_async_copy(ys_hbm.at[_tile_rows(pos_ref[t]), :],
                                          ybuf.at[sl, k, _tile_rows(n), :], sem.at[sl]).start()
            return carry

        lax.fori_loop(0, tm // SUBLANES, body, 0)

    @pl.when(i == 0)
    def _():
        issue(0, 0)

    @pl.when(i + 1 < n_steps)
    def _():
        issue(i + 1, 1 - slot)

    for k in range(2):
        pltpu.make_async_copy(ybuf.at[slot, k], ybuf.at[slot, k], sem.at[slot]).wait()
    y1 = _from_token_tiles(ybuf.at[slot, 0], tm)
    y2 = _from_token_tiles(ybuf.at[slot, 1], tm)
    info = info_ref[...]
    x1 = x_ref[...] + (info[:, INFO_W1:INFO_W1 + 1] * y1 + info[:, INFO_W2:INFO_W2 + 1] * y2)
    gate = _sigmoid(jnp.dot(_rms(x1, gple_ref[...]).astype(BF16), wpg_ref[...], preferred_element_type=F32))
    x2 = x1 + gate * jnp.dot(p_ref[...].astype(BF16), wpp_ref[...], preferred_element_type=F32)
    o_ref[...] = _rms(x2, gfin_ref[...]) if final else x2


def _combine_call(pos1, pos2, x, info, p, gple, wpg, wpp, gfin, ys, final):
    n, d = x.shape
    tm = 512
    n_steps = n // tm
    kern = functools.partial(_combine_kernel, tm=tm, n_steps=n_steps, final=final)
    tok = lambda w: pl.BlockSpec((tm, w), lambda i, *_: (i, 0))
    const = lambda s: pl.BlockSpec(s, lambda i, *_: (0,) * len(s))
    return pl.pallas_call(
        kern,
        grid_spec=pltpu.PrefetchScalarGridSpec(
            num_scalar_prefetch=2,
            grid=(n_steps,),
            in_specs=[tok(d), tok(ROUTER_W), tok(PLE_DIM), const((1, d)), const((d, d)), const((PLE_DIM, d)),
                      const((1, d)), pl.BlockSpec(memory_space=pl.ANY)],
            out_specs=tok(d),
            scratch_shapes=[pltpu.VMEM((2, 2, tm * TOKEN_TILE_ROWS, LANES), F32), pltpu.SemaphoreType.DMA((2,))],
        ),
        out_shape=jax.ShapeDtypeStruct((n, d), F32),
        compiler_params=_cparams(("arbitrary",)),
        name="moe_combine",
    )(pos1, pos2, x, info, p, gple, wpg, wpp, gfin, ys)


def _moe_routed_layer(x, p, gffn, wr, br, wg, wu, wd, gple, wpg, wpp, gfin, final):
    n = x.shape[0]
    te = EXPERT_TILE
    n_assign = 2 * n
    assert n_assign % te == 0
    a_pad = n_assign + N_EXPERTS * te
    info, cnt = _route_call(x, gffn, wr, br)
    col = lambda c: info[:, c].astype(jnp.int32)
    counts = cnt[0, EXPERT_LANE0:EXPERT_LANE0 + N_EXPERTS].astype(jnp.int32)
    padded = (counts + te - 1) // te * te
    seg_end = jnp.cumsum(padded)
    seg_start = seg_end - padded
    pos1 = seg_start[col(INFO_E1)] + col(INFO_R1)
    pos2 = seg_start[col(INFO_E2)] + col(INFO_R2)
    start = jnp.concatenate([seg_start, seg_end[-1:], jnp.full((1,), a_pad, jnp.int32)])
    cnt_pad = jnp.concatenate([counts, jnp.zeros((1,), jnp.int32)])
    tile_row0 = jnp.arange(a_pad // te, dtype=jnp.int32) * te
    tile_expert = jnp.minimum(jnp.sum(seg_end[None, :] <= tile_row0[:, None], axis=1), N_EXPERTS - 1).astype(jnp.int32)
    xs = _dispatch_call(pos1, pos2, cnt_pad, start, x, gffn, a_pad, N_EXPERTS * te)
    ys = _experts_call(tile_expert, xs, wg, wu, wd, te)
    return _combine_call(pos1, pos2, x, info, p, gple, wpg, wpp, gfin, ys, final)


def kernel(x_prompt, x_sample, cache_k_win, cache_v_win, state_ssm_re, state_ssm_im, p_prompt, p_sample,
           rel_bias_table, g_mix, w_qkv, w_o, sinks, ssm_a_re, ssm_a_im, ssm_log_dt, ssm_b_re, ssm_b_im,
           ssm_c_re, ssm_c_im, ssm_d, w_glu_a, w_glu_b, g_ffn, w_router_group, b_router_group,
           w_router_expert, b_router_expert, w_exp_gate, w_exp_up, w_exp_down, g_ple, w_ple_gate,
           w_ple_proj, g_final):
    bsz, seq, d = x_prompt.shape
    ns = x_sample.shape[0]
    depth = g_mix.shape[0]
    row = lambda v: v.reshape(1, -1).astype(F32)

    qi = np.arange(WINDOW)[:, None]
    sj = np.arange(2 * WINDOW)[None, :]
    idx_p = _t5_bucket_np(qi + WINDOW - sj)
    idx_s = np.broadcast_to(_t5_bucket_np(WINDOW - 1 - np.arange(WINDOW))[None, :], (SUBLANES, WINDOW))
    bias_p = _bias_from_table(rel_bias_table, idx_p)
    bias_s = _bias_from_table(rel_bias_table, idx_s)[:, 0, :]

    wqkv_b = w_qkv[0].astype(BF16)
    wo_b = w_o[0].astype(BF16)
    wg_b, wu_b, wd_b = w_exp_gate.astype(BF16), w_exp_up.astype(BF16), w_exp_down.astype(BF16)
    wpg_b, wpp_b = w_ple_gate.astype(BF16), w_ple_proj.astype(BF16)
    wa_b, wgl_b = w_glu_a[0].astype(BF16), w_glu_b[0].astype(BF16)
    pad = ROUTER_W - N_EXPERT_GROUPS - N_EXPERTS
    wr = jnp.concatenate([w_router_group, w_router_expert.reshape(depth, d, N_EXPERTS),
                          jnp.zeros((depth, d, pad), F32)], axis=-1)
    br = jnp.concatenate([b_router_group, b_router_expert.reshape(depth, N_EXPERTS),
                          jnp.zeros((depth, pad), F32)], axis=-1)

    prep = _ssm_prep(ssm_a_re[0], ssm_a_im[0], ssm_log_dt[0], ssm_b_re[0], ssm_b_im[0], ssm_c_re[0], ssm_c_im[0])
    d_row = row(ssm_d[0])

    def moe(x2d, p2d, i):
        layer = _moe_routed_layer if 2 * x2d.shape[0] >= N_EXPERTS * EXPERT_TILE else _moe_ple_layer
        return layer(x2d, p2d, row(g_ffn[i]), wr[i], br[i:i + 1], wg_b[i], wu_b[i], wd_b[i],
                     row(g_ple[i]), wpg_b[i], wpp_b[i], row(g_final), final=(i == depth - 1))

    xp, kp, vp = _attn_prompt_layer(x_prompt, row(g_mix[0]), wqkv_b, wo_b, bias_p, sinks[0])
    xp = moe(xp.reshape(bsz * seq, d), p_prompt[0].reshape(bsz * seq, PLE_DIM), 0).reshape(bsz, seq, d)
    xp, st_p = _ssm_prompt_layer(xp, row(g_mix[1]), prep, d_row, wa_b, wgl_b)
    y_prompt = moe(xp.reshape(bsz * seq, d), p_prompt[1].reshape(bsz * seq, PLE_DIM), 1).reshape(bsz, seq, d)

    xs = x_sample.reshape(ns, d)
    xs, kws, vws = _attn_sample_layer(xs, row(g_mix[0]), wqkv_b, wo_b, bias_s, sinks[0],
                                      cache_k_win[0], cache_v_win[0])
    xs = moe(xs, p_sample[0].reshape(ns, PLE_DIM), 0)
    xs, sr_s, si_s = _ssm_sample_layer(xs, row(g_mix[1]), prep, state_ssm_re[0].reshape(ns, N_STATES),
                                       state_ssm_im[0].reshape(ns, N_STATES), d_row, wa_b, wgl_b)
    y_sample = moe(xs, p_sample[1].reshape(ns, PLE_DIM), 1).reshape(ns, 1, d)

    half = SUBLANES // 2
    win = lambda a, n: a.reshape(1, n, WINDOW, N_KV_HEADS, HEAD_DIM)
    st = lambda a, n: a.reshape(1, n, N_SSM_GROUPS, SSM_STATE)
    return (y_prompt, y_sample, win(kp, bsz), win(vp, bsz), win(kws, ns), win(vws, ns),
            st(st_p[:bsz], bsz), st(st_p[half:half + bsz], bsz), st(sr_s, ns), st(si_s, ns))
```

```python
import functools
import math

import numpy as np
import jax
import jax.numpy as jnp
from jax import lax
from jax.experimental import pallas as pl
from jax.experimental.pallas import tpu as pltpu

F32 = jnp.float32
BF16 = jnp.bfloat16

D_MODEL = 1024
HEAD_DIM = 64
N_HEADS = 16
N_KV_HEADS = 2
GQ = N_HEADS // N_KV_HEADS
WINDOW = 128
N_BUCKETS = 32
MAX_EXACT = 16
MAX_DISTANCE = 128
SSM_GROUP = 16
N_SSM_GROUPS = 64
SSM_STATE = 64
N_STATES = N_SSM_GROUPS * SSM_STATE
N_EXPERT_GROUPS = 4
EXPERTS_PER_GROUP = 8
N_EXPERTS = 32
D_EXPERT = 256
PLE_DIM = 256
RMS_EPS = 1e-6
NEG_INF = -1e30

LANES = 128
SUBLANES = 8
KV_W = N_KV_HEADS * HEAD_DIM
QKV_W = D_MODEL + 2 * KV_W
SLAB_STATES = 512
N_SLABS = D_MODEL // LANES
VMEM_LIMIT = 56 * 1024 * 1024


def _cparams(sem):
    return pltpu.CompilerParams(dimension_semantics=sem, vmem_limit_bytes=VMEM_LIMIT)


def _rms(x, g):
    return x * lax.rsqrt(jnp.mean(x * x, axis=-1, keepdims=True) + RMS_EPS) * g


def _gelu_tanh(x):
    c = math.sqrt(2.0 / math.pi)
    return x * (0.5 * (1.0 + jnp.tanh(c * (x + 0.044715 * (x * x * x)))))


def _sigmoid(x):
    return 1.0 / (1.0 + jnp.exp(-x))


def _full(shape):
    n = len(shape)
    return pl.BlockSpec(shape, lambda *_: (0,) * n)


def _t5_bucket_np(dist):
    n = np.maximum(dist, 0)
    nf = np.maximum(n, 1).astype(np.float64)
    large = MAX_EXACT + (np.log(nf / MAX_EXACT) / math.log(MAX_DISTANCE / MAX_EXACT)
                         * (N_BUCKETS - MAX_EXACT)).astype(np.int32)
    large = np.minimum(large, N_BUCKETS - 1)
    return np.where(n < MAX_EXACT, n, large).astype(np.int32)


def _bias_kernel(table_ref, idx_ref, o_ref):
    h = pl.program_id(0)
    idx = idx_ref[...]
    acc = jnp.zeros(idx.shape, F32)
    for b in range(N_BUCKETS):
        acc = jnp.where(idx == b, table_ref[b, h], acc)
    o_ref[0] = acc


def _bias_from_table(table, idx_np):
    q, k = idx_np.shape
    return pl.pallas_call(
        _bias_kernel,
        grid=(N_HEADS,),
        in_specs=[pl.BlockSpec(memory_space=pltpu.SMEM), _full((q, k))],
        out_specs=pl.BlockSpec((1, q, k), lambda h: (h, 0, 0)),
        out_shape=jax.ShapeDtypeStruct((N_HEADS, q, k), F32),
        name="rel_bias",
    )(table, jnp.asarray(idx_np))


def _softmax_sink_pv(sc, sink, vpad):
    m = jnp.maximum(jnp.max(sc, axis=-1, keepdims=True), sink)
    e = jnp.exp(sc - m)
    den = jnp.sum(e, axis=-1, keepdims=True) + jnp.exp(sink - m)
    pr = (e / den).astype(BF16)
    return jnp.dot(pr, vpad, preferred_element_type=F32)


def _attn_prompt_kernel(sink_ref, x_ref, g_ref, wqkv_ref, wo_ref, bias_ref,
                        xo_ref, k_ref, v_ref, kv_scr, q_scr, o_scr, *, tq):
    i = pl.program_id(1)
    nsub = tq // WINDOW

    @pl.when(i == 0)
    def _():
        kv_scr[0:WINDOW, :] = jnp.zeros((WINDOW, 2 * KV_W), F32)

    x = x_ref[0]
    h = _rms(x, g_ref[...]).astype(BF16)
    qkv = jnp.dot(h, wqkv_ref[...], preferred_element_type=F32)
    q_scr[...] = (qkv[:, :D_MODEL] * (HEAD_DIM ** -0.5)).astype(BF16)
    kv_scr[WINDOW:WINDOW + tq, :] = qkv[:, D_MODEL:]
    k_ref[0] = qkv[tq - WINDOW:, D_MODEL:D_MODEL + KV_W]
    v_ref[0] = qkv[tq - WINDOW:, D_MODEL + KV_W:]

    qi = lax.broadcasted_iota(jnp.int32, (WINDOW, 2 * WINDOW), 0)
    sj = lax.broadcasted_iota(jnp.int32, (WINDOW, 2 * WINDOW), 1)
    dist = qi + WINDOW - sj
    band_ok = (dist >= 0) & (dist < WINDOW)
    lo = lax.broadcasted_iota(jnp.int32, (2 * WINDOW, KV_W), 1) < HEAD_DIM

    for s in range(nsub):
        band = kv_scr[WINDOW * s:WINDOW * s + 2 * WINDOW, :]
        kband, vband = band[:, :KV_W], band[:, KV_W:]
        kroll = pltpu.roll(kband, HEAD_DIM, 1)
        vroll = pltpu.roll(vband, HEAD_DIM, 1)
        kpad = [[jnp.where(lo, kband, 0.0).astype(BF16), jnp.where(lo, 0.0, kroll).astype(BF16)],
                [jnp.where(lo, kroll, 0.0).astype(BF16), jnp.where(lo, 0.0, kband).astype(BF16)]]
        vpad = [[jnp.where(lo, vband, 0.0).astype(BF16), jnp.where(lo, 0.0, vroll).astype(BF16)],
                [jnp.where(lo, vroll, 0.0).astype(BF16), jnp.where(lo, 0.0, vband).astype(BF16)]]
        min_j = jnp.where(i * nsub + s > 0, 0, WINDOW)
        valid = band_ok & (sj >= min_j)
        for j in range(N_HEADS // 2):
            q2 = q_scr[WINDOW * s:WINDOW * (s + 1), LANES * j:LANES * (j + 1)]
            g = (2 * j) // GQ
            acc = jnp.zeros((WINDOW, LANES), F32)
            for p in range(2):
                hd = 2 * j + p
                sc = lax.dot_general(q2, kpad[g][p], (((1,), (1,)), ((), ())),
                                     preferred_element_type=F32)
                sc = jnp.where(valid, sc + bias_ref[hd], NEG_INF)
                acc = acc + _softmax_sink_pv(sc, sink_ref[hd], vpad[g][p])
            o_scr[WINDOW * s:WINDOW * (s + 1), LANES * j:LANES * (j + 1)] = acc.astype(BF16)

    kv_scr[0:WINDOW, :] = kv_scr[tq:tq + WINDOW, :]
    xo_ref[0] = x + jnp.dot(o_scr[...], wo_ref[...], preferred_element_type=F32)


def _attn_prompt_layer(x, g, wqkv, wo, bias, sinks):
    b, t, d = x.shape
    tq = min(512, t)
    kern = functools.partial(_attn_prompt_kernel, tq=tq)
    return pl.pallas_call(
        kern,
        grid=(b, t // tq),
        in_specs=[
            pl.BlockSpec(memory_space=pltpu.SMEM),
            pl.BlockSpec((1, tq, d), lambda bi, i: (bi, i, 0)),
            _full((1, d)),
            _full((d, QKV_W)),
            _full((d, d)),
            _full((N_HEADS, WINDOW, 2 * WINDOW)),
        ],
        out_specs=[
            pl.BlockSpec((1, tq, d), lambda bi, i: (bi, i, 0)),
            pl.BlockSpec((1, WINDOW, KV_W), lambda bi, i: (bi, 0, 0)),
            pl.BlockSpec((1, WINDOW, KV_W), lambda bi, i: (bi, 0, 0)),
        ],
        out_shape=[
            jax.ShapeDtypeStruct((b, t, d), F32),
            jax.ShapeDtypeStruct((b, WINDOW, KV_W), F32),
            jax.ShapeDtypeStruct((b, WINDOW, KV_W), F32),
        ],
        scratch_shapes=[
            pltpu.VMEM((WINDOW + tq, 2 * KV_W), F32),
            pltpu.VMEM((tq, d), BF16),
            pltpu.VMEM((tq, d), BF16),
        ],
        compiler_params=_cparams(("arbitrary", "arbitrary")),
        name="attn_prompt",
    )(sinks, x, g, wqkv, wo, bias)


def _norm_linear_kernel(x_ref, g_ref, w_ref, o_ref):
    h = _rms(x_ref[...], g_ref[...]).astype(BF16)
    o_ref[...] = jnp.dot(h, w_ref[...], preferred_element_type=F32)


def _norm_linear(x, g, w):
    n, d = x.shape
    return pl.pallas_call(
        _norm_linear_kernel,
        out_shape=jax.ShapeDtypeStruct((n, w.shape[1]), F32),
        compiler_params=_cparams(None),
        name="norm_linear",
    )(x, g, w)


def _linear_residual_kernel(x_ref, o_ref, w_ref, xo_ref):
    xo_ref[...] = x_ref[...] + jnp.dot(o_ref[...].astype(BF16), w_ref[...], preferred_element_type=F32)


def _linear_residual(x, o, w):
    return pl.pallas_call(
        _linear_residual_kernel,
        out_shape=jax.ShapeDtypeStruct(x.shape, F32),
        compiler_params=_cparams(None),
        name="linear_residual",
    )(x, o, w)


def _attn_sample_kernel(sink_ref, q_ref, ck_ref, cv_ref, kn_ref, vn_ref, bias_ref, o_ref, kw_ref, vw_ref):
    kw = jnp.concatenate([ck_ref[:, 1:, :], kn_ref[...]], axis=1)
    vw = jnp.concatenate([cv_ref[:, 1:, :], vn_ref[...]], axis=1)
    kw_ref[...] = kw
    vw_ref[...] = vw
    q = (q_ref[...] * (HEAD_DIM ** -0.5)).astype(BF16)
    sc = jnp.einsum('bhc,bjc->bhj', q, kw.astype(BF16), preferred_element_type=F32)
    sc = sc + bias_ref[...][None]
    sink = sink_ref[...][None]
    m = jnp.maximum(jnp.max(sc, axis=-1, keepdims=True), sink)
    e = jnp.exp(sc - m)
    den = jnp.sum(e, axis=-1, keepdims=True) + jnp.exp(sink - m)
    pr = (e / den).astype(BF16)
    o_ref[...] = jnp.einsum('bhj,bjc->bhc', pr, vw.astype(BF16), preferred_element_type=F32)


def _attn_sample(q3, ck, cv, kn, vn, bias_s, sinks_col):
    n = q3.shape[0]
    bb = 8
    blk = lambda s: pl.BlockSpec((bb,) + s, lambda i: (i, 0, 0))
    return pl.pallas_call(
        _attn_sample_kernel,
        grid=(n // bb,),
        in_specs=[_full((N_HEADS, 1)), blk((N_HEADS, KV_W)), blk((WINDOW, KV_W)), blk((WINDOW, KV_W)),
                  blk((1, KV_W)), blk((1, KV_W)), _full((N_HEADS, WINDOW))],
        out_specs=[blk((N_HEADS, KV_W)), blk((WINDOW, KV_W)), blk((WINDOW, KV_W))],
        out_shape=[jax.ShapeDtypeStruct((n, N_HEADS, KV_W), F32),
                   jax.ShapeDtypeStruct((n, WINDOW, KV_W), F32),
                   jax.ShapeDtypeStruct((n, WINDOW, KV_W), F32)],
        compiler_params=_cparams(("arbitrary",)),
        name="attn_sample",
    )(sinks_col, q3, ck, cv, kn, vn, bias_s)


def _attn_sample_layer(x, g, wqkv, wo, bias_s, sinks, cache_k, cache_v):
    n = x.shape[0]
    qkv = _norm_linear(x, g, wqkv)
    q = qkv[:, :D_MODEL].reshape(n, N_KV_HEADS, GQ, 1, HEAD_DIM)
    place = jnp.eye(N_KV_HEADS, dtype=F32).reshape(1, N_KV_HEADS, 1, N_KV_HEADS, 1)
    q3 = (q * place).reshape(n, N_HEADS, KV_W)
    kn = qkv[:, D_MODEL:D_MODEL + KV_W].reshape(n, 1, KV_W)
    vn = qkv[:, D_MODEL + KV_W:].reshape(n, 1, KV_W)
    o3, kw, vw = _attn_sample(q3, cache_k.reshape(n, WINDOW, KV_W), cache_v.reshape(n, WINDOW, KV_W),
                              kn, vn, bias_s, sinks.reshape(N_HEADS, 1))
    o5 = o3.reshape(n, N_KV_HEADS, GQ, N_KV_HEADS, HEAD_DIM)
    o = jnp.stack([o5[:, gi, :, gi, :] for gi in range(N_KV_HEADS)], axis=1).reshape(n, D_MODEL)
    return _linear_residual(x, o, wo), kw, vw


def _ssm_prep_kernel(ar_ref, ai_ref, ldt_ref, br_ref, bi_ref, c_ref, a_out, bc_out, wb_out, wc_out):
    ar, ai = ar_ref[...], ai_ref[...]
    dt = jnp.exp(ldt_ref[...])
    mag = jnp.exp(ar * dt)
    lr = mag * jnp.cos(ai * dt)
    li = mag * jnp.sin(ai * dt)
    den = ar * ar + ai * ai
    cr = ((lr - 1.0) * ar + li * ai) / den
    ci = (li * ar - (lr - 1.0) * ai) / den
    row = lax.broadcasted_iota(jnp.int32, (SUBLANES, N_STATES), 0)
    a_out[...] = jnp.broadcast_to(lr, (SUBLANES, N_STATES))
    bc_out[...] = jnp.where(row < SUBLANES // 2, -li, li)
    for j in range(N_SLABS):
        crj = cr[:, SLAB_STATES * j:SLAB_STATES * (j + 1)]
        cij = ci[:, SLAB_STATES * j:SLAB_STATES * (j + 1)]
        br, bi = br_ref[j], bi_ref[j]
        wb_out[j, :, :SLAB_STATES] = (crj * br - cij * bi).astype(BF16)
        wb_out[j, :, SLAB_STATES:] = (crj * bi + cij * br).astype(BF16)
        wc_out[j, :SLAB_STATES, :] = c_ref[j, :SLAB_STATES, :].astype(BF16)
        wc_out[j, SLAB_STATES:, :] = (-c_ref[j, SLAB_STATES:, :]).astype(BF16)


def _ssm_prep(a_re, a_im, log_dt, b_re, b_im, c_re, c_im):
    eye = jnp.eye(SUBLANES, dtype=F32)

    def blk_b(b):
        t = b.reshape(N_SLABS, 8, SSM_STATE, SSM_GROUP).transpose(0, 1, 3, 2)
        return jnp.einsum('jghp,gk->jghkp', t, eye).reshape(N_SLABS, LANES, SLAB_STATES)

    def blk_c(c):
        t = c.reshape(N_SLABS, 8, SSM_GROUP, SSM_STATE).transpose(0, 1, 3, 2)
        return jnp.einsum('jgph,gk->jgpkh', t, eye).reshape(N_SLABS, SLAB_STATES, LANES)

    row = lambda a: a.reshape(1, N_STATES)
    ldt = jnp.broadcast_to(log_dt[:, None], (N_SSM_GROUPS, SSM_STATE))
    c_all = jnp.concatenate([blk_c(c_re), blk_c(c_im)], axis=1)
    return pl.pallas_call(
        _ssm_prep_kernel,
        out_shape=[jax.ShapeDtypeStruct((SUBLANES, N_STATES), F32),
                   jax.ShapeDtypeStruct((SUBLANES, N_STATES), F32),
                   jax.ShapeDtypeStruct((N_SLABS, LANES, 2 * SLAB_STATES), BF16),
                   jax.ShapeDtypeStruct((N_SLABS, 2 * SLAB_STATES, LANES), BF16)],
        compiler_params=_cparams(None),
        name="ssm_prep",
    )(row(a_re), row(a_im), row(ldt), blk_b(b_re), blk_b(b_im), c_all)


def _glu_residual(x, y, u, d, wa, wg):
    z = _gelu_tanh(y + d * u).astype(BF16)
    out = jnp.dot(z, wa, preferred_element_type=F32) * _sigmoid(jnp.dot(z, wg, preferred_element_type=F32))
    return x + out


def _ssm_prompt_kernel(x_ref, g_ref, wb_ref, wc_ref, a_ref, bc_ref, d_ref, wa_ref, wg_ref,
                       xo_ref, st_ref, z_scr, u_scr, xs_scr, *, nb, lc):
    i = pl.program_id(0)
    blocks_per_slab = SLAB_STATES // LANES
    half = SUBLANES // 2

    @pl.when(i == 0)
    def _():
        xs_scr[...] = jnp.zeros((SUBLANES, N_STATES), F32)

    for b in range(nb):
        u_scr[b * lc:(b + 1) * lc, :] = _rms(x_ref[b], g_ref[...])
    ub = u_scr[...].astype(BF16)
    for j in range(N_SLABS):
        res = jnp.dot(ub[:, LANES * j:LANES * (j + 1)], wb_ref[j], preferred_element_type=F32)
        for b in range(nb):
            rb = res[b * lc:(b + 1) * lc]
            for c in range(blocks_per_slab):
                blk = blocks_per_slab * j + c
                z_scr[blk, pl.ds(b, lc, stride=SUBLANES), :] = rb[:, LANES * c:LANES * (c + 1)]
                z_scr[blk, pl.ds(half + b, lc, stride=SUBLANES), :] = (
                    rb[:, SLAB_STATES + LANES * c:SLAB_STATES + LANES * (c + 1)])
    if nb < half:
        for blk in range(N_STATES // LANES):
            for r in list(range(nb, half)) + list(range(half + nb, SUBLANES)):
                z_scr[blk, pl.ds(r, lc, stride=SUBLANES), :] = jnp.zeros((lc, LANES), F32)

    group = 8
    for cb in range(N_STATES // LANES // group):
        blks = [group * cb + k for k in range(group)]
        a = [a_ref[:, LANES * q:LANES * (q + 1)] for q in blks]
        bc = [bc_ref[:, LANES * q:LANES * (q + 1)] for q in blks]
        x0 = tuple(xs_scr[:, LANES * q:LANES * (q + 1)] for q in blks)

        def step(t, xs, blks=blks, a=a, bc=bc):
            r = pl.multiple_of(t * SUBLANES, SUBLANES)
            out = []
            for k, q in enumerate(blks):
                xn = a[k] * xs[k] + bc[k] * pltpu.roll(xs[k], half, 0) + z_scr[q, pl.ds(r, SUBLANES), :]
                z_scr[q, pl.ds(r, SUBLANES), :] = xn
                out.append(xn)
            return tuple(out)

        xf = lax.fori_loop(0, lc, step, x0)
        for k, q in enumerate(blks):
            xs_scr[:, LANES * q:LANES * (q + 1)] = xf[k]
    st_ref[...] = xs_scr[...]

    for b in range(nb):
        ys = []
        for j in range(N_SLABS):
            parts = [z_scr[blocks_per_slab * j + c, pl.ds(b, lc, stride=SUBLANES), :] for c in range(blocks_per_slab)]
            parts += [z_scr[blocks_per_slab * j + c, pl.ds(half + b, lc, stride=SUBLANES), :]
                      for c in range(blocks_per_slab)]
            s = jnp.concatenate(parts, axis=1).astype(BF16)
            ys.append(jnp.dot(s, wc_ref[j], preferred_element_type=F32))
        y = jnp.concatenate(ys, axis=1)
        xo_ref[b] = _glu_residual(x_ref[b], y, u_scr[b * lc:(b + 1) * lc, :], d_ref[...], wa_ref[...], wg_ref[...])


def _ssm_prompt_layer(x, g, prep, d, wa, wg):
    a_rows, bc_rows, wb, wc = prep
    nb, t, dm = x.shape
    assert nb <= SUBLANES // 2
    lc = min(128, t)
    kern = functools.partial(_ssm_prompt_kernel, nb=nb, lc=lc)
    return pl.pallas_call(
        kern,
        grid=(t // lc,),
        in_specs=[
            pl.BlockSpec((nb, lc, dm), lambda i: (0, i, 0)),
            _full((1, dm)),
            _full(wb.shape), _full(wc.shape),
            _full((SUBLANES, N_STATES)), _full((SUBLANES, N_STATES)),
            _full((1, dm)), _full((dm, dm)), _full((dm, dm)),
        ],
        out_specs=[pl.BlockSpec((nb, lc, dm), lambda i: (0, i, 0)), _full((SUBLANES, N_STATES))],
        out_shape=[jax.ShapeDtypeStruct((nb, t, dm), F32), jax.ShapeDtypeStruct((SUBLANES, N_STATES), F32)],
        scratch_shapes=[
            pltpu.VMEM((N_STATES // LANES, SUBLANES * lc, LANES), F32),
            pltpu.VMEM((nb * lc, dm), F32),
            pltpu.VMEM((SUBLANES, N_STATES), F32),
        ],
        compiler_params=_cparams(("arbitrary",)),
        name="ssm_prompt",
    )(x, g, wb, wc, a_rows, bc_rows, d, wa, wg)


def _ssm_sample_kernel(x_ref, g_ref, wb_ref, wc_ref, a_ref, bc_ref, h0r_ref, h0i_ref, d_ref, wa_ref, wg_ref,
                       xo_ref, sr_ref, si_ref):
    x = x_ref[...]
    u = _rms(x, g_ref[...])
    ub = u.astype(BF16)
    ys = []
    for j in range(N_SLABS):
        sl = slice(SLAB_STATES * j, SLAB_STATES * (j + 1))
        res = jnp.dot(ub[:, LANES * j:LANES * (j + 1)], wb_ref[j], preferred_element_type=F32)
        lr = a_ref[0:1, sl]
        li = bc_ref[SUBLANES - 1:SUBLANES, sl]
        h0r, h0i = h0r_ref[:, sl], h0i_ref[:, sl]
        xr = res[:, :SLAB_STATES] + (lr * h0r - li * h0i)
        xi = res[:, SLAB_STATES:] + (lr * h0i + li * h0r)
        sr_ref[:, sl] = xr
        si_ref[:, sl] = xi
        s = jnp.concatenate([xr, xi], axis=1).astype(BF16)
        ys.append(jnp.dot(s, wc_ref[j], preferred_element_type=F32))
    y = jnp.concatenate(ys, axis=1)
    xo_ref[...] = _glu_residual(x, y, u, d_ref[...], wa_ref[...], wg_ref[...])


def _ssm_sample_layer(x, g, prep, h0r, h0i, d, wa, wg):
    a_rows, bc_rows, wb, wc = prep
    n = x.shape[0]
    return pl.pallas_call(
        _ssm_sample_kernel,
        out_shape=[jax.ShapeDtypeStruct(x.shape, F32),
                   jax.ShapeDtypeStruct((n, N_STATES), F32),
                   jax.ShapeDtypeStruct((n, N_STATES), F32)],
        compiler_params=_cparams(None),
        name="ssm_sample",
    )(x, g, wb, wc, a_rows, bc_rows, h0r, h0i, d, wa, wg)


ROUTER_W = LANES
EXPERT_LANE0 = N_EXPERT_GROUPS


def _route_topk(h, wr, br):
    h_hi = h.astype(BF16)
    h_lo = (h - h_hi.astype(F32)).astype(BF16)
    w_hi = wr.astype(BF16)
    w_lo = (wr - w_hi.astype(F32)).astype(BF16)
    dot = lambda a, b: jnp.dot(a, b, preferred_element_type=F32)
    logits = dot(h_hi, w_hi) + (dot(h_lo, w_hi) + dot(h_hi, w_lo)) + br
    lane = lax.broadcasted_iota(jnp.int32, logits.shape, 1)
    neg = -jnp.inf
    gl = jnp.where(lane < N_EXPERT_GROUPS, logits, neg)
    gmax = jnp.max(gl, axis=-1, keepdims=True)
    g_sel = jnp.min(jnp.where(gl == gmax, lane, ROUTER_W), axis=-1, keepdims=True)
    p_grp = 1.0 / jnp.sum(jnp.exp(gl - gmax), axis=-1, keepdims=True)
    lane_grp = (lane - EXPERT_LANE0) // EXPERTS_PER_GROUP
    in_grp = (lane >= EXPERT_LANE0) & (lane < EXPERT_LANE0 + N_EXPERTS) & (lane_grp == g_sel)
    el = jnp.where(in_grp, logits, neg)
    v1 = jnp.max(el, axis=-1, keepdims=True)
    i1 = jnp.min(jnp.where(el == v1, lane, ROUTER_W), axis=-1, keepdims=True)
    el2 = jnp.where(lane == i1, neg, el)
    v2 = jnp.max(el2, axis=-1, keepdims=True)
    i2 = jnp.min(jnp.where(el2 == v2, lane, ROUTER_W), axis=-1, keepdims=True)
    e2 = jnp.exp(v2 - v1)
    w1 = p_grp / (1.0 + e2)
    w2 = p_grp * e2 / (1.0 + e2)
    return lane, i1, i2, w1, w2


def _route(h, wr, br):
    lane, i1, i2, w1, w2 = _route_topk(h, wr, br)
    return jnp.where(lane == i1, w1, 0.0) + jnp.where(lane == i2, w2, 0.0)


def _moe_kernel(x_ref, p_ref, gffn_ref, wr_ref, br_ref, wg_ref, wu_ref, wd_ref,
                gple_ref, wpg_ref, wpp_ref, gfin_ref, o_ref, hn_scr, gates_scr, acc_scr, *, final):
    e = pl.program_id(1)

    @pl.when(e == 0)
    def _():
        x = x_ref[...]
        h = _rms(x, gffn_ref[...])
        hn_scr[...] = h.astype(BF16)
        gates_scr[...] = _route(h, wr_ref[...], br_ref[...])
        acc_scr[...] = x

    hb = hn_scr[...]
    gt = jnp.dot(hb, wg_ref[0].astype(BF16), preferred_element_type=F32)
    up = jnp.dot(hb, wu_ref[0].astype(BF16), preferred_element_type=F32)
    a = (gt * _sigmoid(gt) * up).astype(BF16)
    gates = gates_scr[...]
    lane = lax.broadcasted_iota(jnp.int32, gates.shape, 1)
    ge = jnp.sum(jnp.where(lane == e + EXPERT_LANE0, gates, 0.0), axis=-1, keepdims=True)
    acc_scr[...] += ge * jnp.dot(a, wd_ref[0].astype(BF16), preferred_element_type=F32)

    @pl.when(e == pl.num_programs(1) - 1)
    def _():
        x1 = acc_scr[...]
        gate = _sigmoid(jnp.dot(_rms(x1, gple_ref[...]).astype(BF16), wpg_ref[...], preferred_element_type=F32))
        x2 = x1 + gate * jnp.dot(p_ref[...].astype(BF16), wpp_ref[...], preferred_element_type=F32)
        o_ref[...] = _rms(x2, gfin_ref[...]) if final else x2


def _moe_ple_layer(li, x, p, gffn, wr, br, wg, wu, wd, gple, wpg, wpp, gfin, final):
    n, d = x.shape
    tm = min(1024, n)
    kern = functools.partial(_moe_kernel, final=final)
    tok = lambda w: pl.BlockSpec((tm, w), lambda t, e: (t, 0))
    const = lambda s: pl.BlockSpec(s, lambda t, e: (0,) * len(s))
    return pl.pallas_call(
        kern,
        grid=(n // tm, N_EXPERTS),
        in_specs=[
            tok(d), pl.BlockSpec((None, tm, PLE_DIM), lambda t, e: (li, t, 0)),
            const((1, d)), const((d, ROUTER_W)), const((1, ROUTER_W)),
            pl.BlockSpec((None, 1, d, D_EXPERT), lambda t, e: (li, e, 0, 0)),
            pl.BlockSpec((None, 1, d, D_EXPERT), lambda t, e: (li, e, 0, 0)),
            pl.BlockSpec((None, 1, D_EXPERT, d), lambda t, e: (li, e, 0, 0)),
            const((1, d)), const((d, d)), const((PLE_DIM, d)), const((1, d)),
        ],
        out_specs=tok(d),
        out_shape=jax.ShapeDtypeStruct((n, d), F32),
        scratch_shapes=[pltpu.VMEM((tm, d), BF16), pltpu.VMEM((tm, ROUTER_W), F32), pltpu.VMEM((tm, d), F32)],
        compiler_params=_cparams(("arbitrary", "arbitrary")),
        name="moe_ple",
    )(x, p, gffn, wr, br, wg, wu, wd, gple, wpg, wpp, gfin)


TOKEN_TILE_ROWS = D_MODEL // LANES
INFO_W1, INFO_W2, INFO_E1, INFO_E2, INFO_R1, INFO_R2 = range(6)
EXPERT_TILE = 256


def _to_token_tiles(ref_2d, val, n):
    r = val.shape[1] // LANES
    for s in range(r):
        ref_2d[pl.ds(s, n, stride=r), :] = val[:, LANES * s:LANES * (s + 1)]


def _from_token_tiles(ref_2d, n, r):
    return jnp.concatenate([ref_2d[pl.ds(s, n, stride=r), :] for s in range(r)], axis=1)


def _tile_rows(i, r=TOKEN_TILE_ROWS):
    return pl.ds(pl.multiple_of(i * r, r), r)


def _route_kernel(x_ref, g_ref, wr_ref, br_ref, info_ref, cnt_ref, base_scr):
    i = pl.program_id(0)

    @pl.when(i == 0)
    def _():
        base_scr[...] = jnp.zeros(base_scr.shape, F32)

    h = _rms(x_ref[...], g_ref[...])
    lane, i1, i2, w1, w2 = _route_topk(h, wr_ref[...], br_ref[...])
    tm = h.shape[0]
    chosen = jnp.where((lane == i1) | (lane == i2), 1.0, 0.0)
    r = lax.broadcasted_iota(jnp.int32, (tm, tm), 0)
    c = lax.broadcasted_iota(jnp.int32, (tm, tm), 1)
    before = jnp.where(c < r, 1.0, 0.0).astype(BF16)
    prefix = jnp.dot(before, chosen.astype(BF16), preferred_element_type=F32) + base_scr[...]
    rank1 = jnp.sum(jnp.where(lane == i1, prefix, 0.0), axis=-1, keepdims=True)
    rank2 = jnp.sum(jnp.where(lane == i2, prefix, 0.0), axis=-1, keepdims=True)
    base_scr[...] += jnp.sum(chosen, axis=0, keepdims=True)
    cnt_ref[...] = base_scr[...]
    e1 = (i1 - EXPERT_LANE0).astype(F32)
    e2 = (i2 - EXPERT_LANE0).astype(F32)
    info = jnp.zeros(h.shape[:1] + (ROUTER_W,), F32)
    for ln, v in ((INFO_W1, w1), (INFO_W2, w2), (INFO_E1, e1), (INFO_E2, e2), (INFO_R1, rank1), (INFO_R2, rank2)):
        info = jnp.where(lane == ln, v, info)
    info_ref[...] = info


def _route_call(x, gffn, wr, br):
    n, d = x.shape
    tm = 512
    return pl.pallas_call(
        _route_kernel,
        grid=(n // tm,),
        in_specs=[pl.BlockSpec((tm, d), lambda i: (i, 0)), _full((1, d)), _full((d, ROUTER_W)), _full((1, ROUTER_W))],
        out_specs=[pl.BlockSpec((tm, ROUTER_W), lambda i: (i, 0)), _full((1, ROUTER_W))],
        out_shape=[jax.ShapeDtypeStruct((n, ROUTER_W), F32), jax.ShapeDtypeStruct((1, ROUTER_W), F32)],
        scratch_shapes=[pltpu.VMEM((1, ROUTER_W), F32)],
        compiler_params=_cparams(("arbitrary",)),
        name="moe_route",
    )(x, gffn, wr, br)


def _dispatch_kernel(pos_ref, cnt_ref, start_ref, x_ref, g_ref, xs_hbm, hbuf, zbuf, sem, zsem,
                     *, td, n_steps, n_pad_rows):
    i = pl.program_id(0)
    slot = i % 2

    def wait_slot(sl):
        for _ in range(2):
            pltpu.make_async_copy(hbuf.at[sl], hbuf.at[sl], sem.at[sl]).wait()

    @pl.when(i >= 2)
    def _():
        wait_slot(slot)

    _to_token_tiles(hbuf.at[slot], _rms(x_ref[...], g_ref[...]), td)
    rows = _tile_rows

    def issue(n8, carry):
        for u in range(SUBLANES):
            n = n8 * SUBLANES + u
            t = i * td + n
            src = hbuf.at[slot, rows(n), :]
            pltpu.make_async_copy(src, xs_hbm.at[rows(pos_ref[0, t]), :], sem.at[slot]).start(priority=0)
            pltpu.make_async_copy(src, xs_hbm.at[rows(pos_ref[1, t]), :], sem.at[slot]).start(priority=1)
        return carry

    lax.fori_loop(0, td // SUBLANES, issue, 0)

    @pl.when(i == 0)
    def _():
        zbuf[...] = jnp.zeros(zbuf.shape, zbuf.dtype)
        for e in range(N_EXPERTS + 1):
            lo = start_ref[e] + cnt_ref[e]
            hi = start_ref[e + 1]

            def zero_row(r, carry):
                pltpu.make_async_copy(zbuf, xs_hbm.at[rows(r), :], zsem).start()
                return carry

            lax.fori_loop(lo, hi, zero_row, 0)

    @pl.when(i == n_steps - 1)
    def _():
        wait_slot(slot)
        if n_steps > 1:
            wait_slot(1 - slot)
        pad = xs_hbm.at[pl.ds(0, n_pad_rows * TOKEN_TILE_ROWS), :]
        pltpu.make_async_copy(pad, pad, zsem).wait()


def _dispatch_call(pos, cnt, start, x, gffn, a_pad, n_pad_rows):
    n, d = x.shape
    td = 512
    n_steps = n // td
    kern = functools.partial(_dispatch_kernel, td=td, n_steps=n_steps, n_pad_rows=n_pad_rows)
    return pl.pallas_call(
        kern,
        grid_spec=pltpu.PrefetchScalarGridSpec(
            num_scalar_prefetch=3,
            grid=(n_steps,),
            in_specs=[pl.BlockSpec((td, d), lambda i, *_: (i, 0)), pl.BlockSpec((1, d), lambda i, *_: (0, 0))],
            out_specs=pl.BlockSpec(memory_space=pl.ANY),
            scratch_shapes=[pltpu.VMEM((2, td * TOKEN_TILE_ROWS, LANES), F32),
                            pltpu.VMEM((TOKEN_TILE_ROWS, LANES), F32),
                            pltpu.SemaphoreType.DMA((2,)), pltpu.SemaphoreType.DMA(())],
        ),
        out_shape=jax.ShapeDtypeStruct((a_pad * TOKEN_TILE_ROWS, LANES), F32),
        compiler_params=_cparams(("arbitrary",)),
        name="moe_dispatch",
    )(pos, cnt, start, x, gffn)


def _experts_kernel(te_ref, x_ref, wg_ref, wu_ref, wd_ref, y_ref, wg_scr, wu_scr, wd_scr, *, te):
    t = pl.program_id(0)

    @pl.when((t == 0) | (te_ref[t] != te_ref[jnp.maximum(t - 1, 0)]))
    def _():
        wg_scr[...] = wg_ref[0].astype(BF16)
        wu_scr[...] = wu_ref[0].astype(BF16)
        wd_scr[...] = wd_ref[0].astype(BF16)

    x = _from_token_tiles(x_ref, te, TOKEN_TILE_ROWS).astype(BF16)
    gt = jnp.dot(x, wg_scr[...], preferred_element_type=F32)
    up = jnp.dot(x, wu_scr[...], preferred_element_type=F32)
    a = (gt * _sigmoid(gt) * up).astype(BF16)
    _to_token_tiles(y_ref, jnp.dot(a, wd_scr[...], preferred_element_type=F32), te)


def _experts_call(li, tile_expert, xs, wg, wu, wd, te):
    rows = te * TOKEN_TILE_ROWS
    n_tiles = xs.shape[0] // rows
    d = wg.shape[2]
    kern = functools.partial(_experts_kernel, te=te)
    return pl.pallas_call(
        kern,
        grid_spec=pltpu.PrefetchScalarGridSpec(
            num_scalar_prefetch=1,
            grid=(n_tiles,),
            in_specs=[pl.BlockSpec((rows, LANES), lambda t, ex: (t, 0)),
                      pl.BlockSpec((None, 1, d, D_EXPERT), lambda t, ex: (li, ex[t], 0, 0)),
                      pl.BlockSpec((None, 1, d, D_EXPERT), lambda t, ex: (li, ex[t], 0, 0)),
                      pl.BlockSpec((None, 1, D_EXPERT, d), lambda t, ex: (li, ex[t], 0, 0))],
            out_specs=pl.BlockSpec((rows, LANES), lambda t, ex: (t, 0)),
            scratch_shapes=[pltpu.VMEM((d, D_EXPERT), BF16), pltpu.VMEM((d, D_EXPERT), BF16),
                            pltpu.VMEM((D_EXPERT, d), BF16)],
        ),
        out_shape=jax.ShapeDtypeStruct((n_tiles * rows, LANES), F32),
        compiler_params=_cparams(("arbitrary",)),
        name="moe_experts",
    )(tile_expert, xs, wg, wu, wd)


def _combine_kernel(pos_ref, x_ref, info_ref, p_ref, gple_ref, wpg_ref, wpp_ref, gfin_ref, ys_hbm,
                    o_ref, ybuf, sem, *, tm, n_steps, final):
    i = pl.program_id(0)
    slot = i % 2

    def issue(step, sl):
        def body(n8, carry):
            for u in range(SUBLANES):
                n = n8 * SUBLANES + u
                t = step * tm + n
                for k in range(2):
                    pltpu.make_async_copy(ys_hbm.at[_tile_rows(pos_ref[k, t]), :],
                                          ybuf.at[sl, k, _tile_rows(n), :], sem.at[sl]).start(priority=k)
            return carry

        lax.fori_loop(0, tm // SUBLANES, body, 0)

    @pl.when(i == 0)
    def _():
        issue(0, 0)

    @pl.when(i + 1 < n_steps)
    def _():
        issue(i + 1, 1 - slot)

    for k in range(2):
        pltpu.make_async_copy(ybuf.at[slot, k], ybuf.at[slot, k], sem.at[slot]).wait()
    y1 = _from_token_tiles(ybuf.at[slot, 0], tm, TOKEN_TILE_ROWS)
    y2 = _from_token_tiles(ybuf.at[slot, 1], tm, TOKEN_TILE_ROWS)
    info = info_ref[...]
    x1 = x_ref[...] + (info[:, INFO_W1:INFO_W1 + 1] * y1 + info[:, INFO_W2:INFO_W2 + 1] * y2)
    gate = _sigmoid(jnp.dot(_rms(x1, gple_ref[...]).astype(BF16), wpg_ref[...], preferred_element_type=F32))
    x2 = x1 + gate * jnp.dot(p_ref[...].astype(BF16), wpp_ref[...], preferred_element_type=F32)
    o_ref[...] = _rms(x2, gfin_ref[...]) if final else x2


def _combine_call(li, pos, x, info, p, gple, wpg, wpp, gfin, ys, final):
    n, d = x.shape
    tm = 512
    n_steps = n // tm
    kern = functools.partial(_combine_kernel, tm=tm, n_steps=n_steps, final=final)
    tok = lambda w: pl.BlockSpec((tm, w), lambda i, *_: (i, 0))
    const = lambda s: pl.BlockSpec(s, lambda i, *_: (0,) * len(s))
    return pl.pallas_call(
        kern,
        grid_spec=pltpu.PrefetchScalarGridSpec(
            num_scalar_prefetch=1,
            grid=(n_steps,),
            in_specs=[tok(d), tok(ROUTER_W), pl.BlockSpec((None, tm, PLE_DIM), lambda i, *_: (li, i, 0)),
                      const((1, d)), const((d, d)), const((PLE_DIM, d)),
                      const((1, d)), pl.BlockSpec(memory_space=pl.ANY)],
            out_specs=tok(d),
            scratch_shapes=[pltpu.VMEM((2, 2, tm * TOKEN_TILE_ROWS, LANES), F32), pltpu.SemaphoreType.DMA((2,))],
        ),
        out_shape=jax.ShapeDtypeStruct((n, d), F32),
        compiler_params=_cparams(("arbitrary",)),
        name="moe_combine",
    )(pos, x, info, p, gple, wpg, wpp, gfin, ys)


def _positions_kernel(info_ref, start_ref, pos_ref):
    info = info_ref[...]
    lane = lax.broadcasted_iota(jnp.int32, info.shape, 1)
    lane_expert = (lane - EXPERT_LANE0).astype(F32)
    start = start_ref[...]
    cols = []
    for e_lane, r_lane in ((INFO_E1, INFO_R1), (INFO_E2, INFO_R2)):
        seg = jnp.sum(jnp.where(lane_expert == info[:, e_lane:e_lane + 1], start, 0.0), axis=-1, keepdims=True)
        cols.append(seg + info[:, r_lane:r_lane + 1])
    both = jnp.where(lane == 0, cols[0], jnp.where(lane == 1, cols[1], 0.0))
    pos_ref[...] = both.T[:SUBLANES, :].astype(jnp.int32)


def _positions_call(info, start_row):
    n = info.shape[0]
    tm = 512
    return pl.pallas_call(
        _positions_kernel,
        grid=(n // tm,),
        in_specs=[pl.BlockSpec((tm, ROUTER_W), lambda i: (i, 0)), _full((1, ROUTER_W))],
        out_specs=pl.BlockSpec((SUBLANES, tm), lambda i: (0, i)),
        out_shape=jax.ShapeDtypeStruct((SUBLANES, n), jnp.int32),
        compiler_params=_cparams(("arbitrary",)),
        name="moe_positions",
    )(info, start_row)


def _moe_routed_layer(li, x, p, gffn, wr, br, wg, wu, wd, gple, wpg, wpp, gfin, final):
    n = x.shape[0]
    te = EXPERT_TILE
    n_assign = 2 * n
    assert n_assign % te == 0
    a_pad = n_assign + N_EXPERTS * te
    info, cnt = _route_call(x, gffn, wr, br)
    counts = cnt[0, EXPERT_LANE0:EXPERT_LANE0 + N_EXPERTS].astype(jnp.int32)
    padded = (counts + te - 1) // te * te
    seg_end = jnp.cumsum(padded)
    seg_start = seg_end - padded
    start_row = jnp.zeros((1, ROUTER_W), F32).at[0, EXPERT_LANE0:EXPERT_LANE0 + N_EXPERTS].set(seg_start.astype(F32))
    pos = _positions_call(info, start_row)[:2]
    start = jnp.concatenate([seg_start, seg_end[-1:], jnp.full((1,), a_pad, jnp.int32)])
    cnt_pad = jnp.concatenate([counts, jnp.zeros((1,), jnp.int32)])
    tile_row0 = jnp.arange(a_pad // te, dtype=jnp.int32) * te
    tile_expert = jnp.minimum(jnp.sum(seg_end[None, :] <= tile_row0[:, None], axis=1), N_EXPERTS - 1).astype(jnp.int32)
    xs = _dispatch_call(pos, cnt_pad, start, x, gffn, a_pad, N_EXPERTS * te)
    ys = _experts_call(li, tile_expert, xs, wg, wu, wd, te)
    return _combine_call(li, pos, x, info, p, gple, wpg, wpp, gfin, ys, final)


def kernel(x_prompt, x_sample, cache_k_win, cache_v_win, state_ssm_re, state_ssm_im, p_prompt, p_sample,
           rel_bias_table, g_mix, w_qkv, w_o, sinks, ssm_a_re, ssm_a_im, ssm_log_dt, ssm_b_re, ssm_b_im,
           ssm_c_re, ssm_c_im, ssm_d, w_glu_a, w_glu_b, g_ffn, w_router_group, b_router_group,
           w_router_expert, b_router_expert, w_exp_gate, w_exp_up, w_exp_down, g_ple, w_ple_gate,
           w_ple_proj, g_final):
    bsz, seq, d = x_prompt.shape
    ns = x_sample.shape[0]
    depth = g_mix.shape[0]
    row = lambda v: v.reshape(1, -1).astype(F32)

    qi = np.arange(WINDOW)[:, None]
    sj = np.arange(2 * WINDOW)[None, :]
    idx_p = _t5_bucket_np(qi + WINDOW - sj)
    idx_s = np.broadcast_to(_t5_bucket_np(WINDOW - 1 - np.arange(WINDOW))[None, :], (SUBLANES, WINDOW))
    bias_p = _bias_from_table(rel_bias_table, idx_p)
    bias_s = _bias_from_table(rel_bias_table, idx_s)[:, 0, :]

    wqkv_b = w_qkv[0].astype(BF16)
    wo_b = w_o[0].astype(BF16)
    wpg_b, wpp_b = w_ple_gate.astype(BF16), w_ple_proj.astype(BF16)
    wa_b, wgl_b = w_glu_a[0].astype(BF16), w_glu_b[0].astype(BF16)
    pad = ROUTER_W - N_EXPERT_GROUPS - N_EXPERTS
    wr = jnp.concatenate([w_router_group, w_router_expert.reshape(depth, d, N_EXPERTS),
                          jnp.zeros((depth, d, pad), F32)], axis=-1)
    br = jnp.concatenate([b_router_group, b_router_expert.reshape(depth, N_EXPERTS),
                          jnp.zeros((depth, pad), F32)], axis=-1)

    prep = _ssm_prep(ssm_a_re[0], ssm_a_im[0], ssm_log_dt[0], ssm_b_re[0], ssm_b_im[0], ssm_c_re[0], ssm_c_im[0])
    d_row = row(ssm_d[0])

    def moe(x2d, p_all, i):
        layer = _moe_routed_layer if 2 * x2d.shape[0] >= N_EXPERTS * EXPERT_TILE else _moe_ple_layer
        return layer(i, x2d, p_all, row(g_ffn[i]), wr[i], br[i:i + 1], w_exp_gate, w_exp_up, w_exp_down,
                     row(g_ple[i]), wpg_b[i], wpp_b[i], row(g_final), final=(i == depth - 1))

    pp = p_prompt.reshape(depth, bsz * seq, PLE_DIM)
    xp, kp, vp = _attn_prompt_layer(x_prompt, row(g_mix[0]), wqkv_b, wo_b, bias_p, sinks[0])
    xp = moe(xp.reshape(bsz * seq, d), pp, 0).reshape(bsz, seq, d)
    xp, st_p = _ssm_prompt_layer(xp, row(g_mix[1]), prep, d_row, wa_b, wgl_b)
    y_prompt = moe(xp.reshape(bsz * seq, d), pp, 1).reshape(bsz, seq, d)

    ps = p_sample.reshape(depth, ns, PLE_DIM)
    xs = x_sample.reshape(ns, d)
    xs, kws, vws = _attn_sample_layer(xs, row(g_mix[0]), wqkv_b, wo_b, bias_s, sinks[0],
                                      cache_k_win[0], cache_v_win[0])
    xs = moe(xs, ps, 0)
    xs, sr_s, si_s = _ssm_sample_layer(xs, row(g_mix[1]), prep, state_ssm_re[0].reshape(ns, N_STATES),
                                       state_ssm_im[0].reshape(ns, N_STATES), d_row, wa_b, wgl_b)
    y_sample = moe(xs, ps, 1).reshape(ns, 1, d)

    half = SUBLANES // 2
    win = lambda a, n: a.reshape(1, n, WINDOW, N_KV_HEADS, HEAD_DIM)
    st = lambda a, n: a.reshape(1, n, N_SSM_GROUPS, SSM_STATE)
    return (y_prompt, y_sample, win(kp, bsz), win(vp, bsz), win(kws, ns), win(vws, ns),
            st(st_p[:bsz], bsz), st(st_p[half:half + bsz], bsz), st(sr_s, ns), st(si_s, ns))
```

```python
import functools
import math

import numpy as np
import jax
import jax.numpy as jnp
from jax import lax
from jax.experimental import pallas as pl
from jax.experimental.pallas import tpu as pltpu

F32 = jnp.float32
BF16 = jnp.bfloat16

D_MODEL = 1024
HEAD_DIM = 64
N_HEADS = 16
N_KV_HEADS = 2
GQ = N_HEADS // N_KV_HEADS
WINDOW = 128
N_BUCKETS = 32
MAX_EXACT = 16
MAX_DISTANCE = 128
SSM_GROUP = 16
N_SSM_GROUPS = 64
SSM_STATE = 64
N_STATES = N_SSM_GROUPS * SSM_STATE
N_EXPERT_GROUPS = 4
EXPERTS_PER_GROUP = 8
N_EXPERTS = 32
D_EXPERT = 256
PLE_DIM = 256
RMS_EPS = 1e-6
NEG_INF = -1e30

LANES = 128
SUBLANES = 8
KV_W = N_KV_HEADS * HEAD_DIM
QKV_W = D_MODEL + 2 * KV_W
SLAB_STATES = 512
N_SLABS = D_MODEL // LANES
VMEM_LIMIT = 56 * 1024 * 1024


def _cparams(sem):
    return pltpu.CompilerParams(dimension_semantics=sem, vmem_limit_bytes=VMEM_LIMIT)


def _rms(x, g):
    return x * lax.rsqrt(jnp.mean(x * x, axis=-1, keepdims=True) + RMS_EPS) * g


def _gelu_tanh(x):
    c = math.sqrt(2.0 / math.pi)
    return x * (0.5 * (1.0 + jnp.tanh(c * (x + 0.044715 * (x * x * x)))))


def _sigmoid(x):
    return 1.0 / (1.0 + jnp.exp(-x))


def _full(shape):
    n = len(shape)
    return pl.BlockSpec(shape, lambda *_: (0,) * n)


def _t5_bucket_np(dist):
    n = np.maximum(dist, 0)
    nf = np.maximum(n, 1).astype(np.float64)
    large = MAX_EXACT + (np.log(nf / MAX_EXACT) / math.log(MAX_DISTANCE / MAX_EXACT)
                         * (N_BUCKETS - MAX_EXACT)).astype(np.int32)
    large = np.minimum(large, N_BUCKETS - 1)
    return np.where(n < MAX_EXACT, n, large).astype(np.int32)


def _bias_kernel(table_ref, idx_ref, o_ref):
    h = pl.program_id(0)
    idx = idx_ref[...]
    acc = jnp.zeros(idx.shape, F32)
    for b in range(N_BUCKETS):
        acc = jnp.where(idx == b, table_ref[b, h], acc)
    o_ref[0] = acc


def _bias_from_table(table, idx_np):
    q, k = idx_np.shape
    return pl.pallas_call(
        _bias_kernel,
        grid=(N_HEADS,),
        in_specs=[pl.BlockSpec(memory_space=pltpu.SMEM), _full((q, k))],
        out_specs=pl.BlockSpec((1, q, k), lambda h: (h, 0, 0)),
        out_shape=jax.ShapeDtypeStruct((N_HEADS, q, k), F32),
        name="rel_bias",
    )(table, jnp.asarray(idx_np))


HEADS_PER_UNIT = GQ // 2
N_UNITS = N_HEADS // HEADS_PER_UNIT
UNIT_ROWS = HEADS_PER_UNIT * WINDOW


def _band_bias_kernel(table_ref, idx_ref, ok_ref, o_ref):
    h = pl.program_id(0)
    idx = idx_ref[...]
    acc = jnp.zeros(idx.shape, F32)
    for b in range(N_BUCKETS):
        acc = jnp.where(idx == b, table_ref[b, h], acc)
    base = jnp.where(ok_ref[...] != 0, acc, NEG_INF)
    sj = lax.broadcasted_iota(jnp.int32, idx.shape, 1)
    o_ref[0, 0] = base
    o_ref[1, 0] = jnp.where(sj >= WINDOW, base, NEG_INF)


def _band_bias(table, idx_np, ok_np):
    def out_map(h):
        r = h % GQ
        return (0, (h // GQ) * 2 + r % 2, r // 2, 0)

    return pl.pallas_call(
        _band_bias_kernel,
        grid=(N_HEADS,),
        in_specs=[pl.BlockSpec(memory_space=pltpu.SMEM), _full(idx_np.shape), _full(ok_np.shape)],
        out_specs=pl.BlockSpec((2, 1, WINDOW, 2 * WINDOW), out_map),
        out_shape=jax.ShapeDtypeStruct((2, N_UNITS, UNIT_ROWS, 2 * WINDOW), F32),
        name="band_bias",
    )(table, jnp.asarray(idx_np), jnp.asarray(ok_np))


def _attn_prompt_kernel(sink_ref, x_ref, g_ref, wqkv_ref, wo_ref, bias_ref,
                        xo_ref, k_ref, v_ref, kv_scr, q_scr, o_scr, *, tq):
    i = pl.program_id(1)
    nsub = tq // WINDOW

    @pl.when(i == 0)
    def _():
        kv_scr[0:WINDOW, :] = jnp.zeros((WINDOW, 2 * KV_W), F32)

    x = x_ref[0]
    h = _rms(x, g_ref[...]).astype(BF16)
    qkv = jnp.dot(h, wqkv_ref[...], preferred_element_type=F32)
    q_scr[...] = (qkv[:, :D_MODEL] * (HEAD_DIM ** -0.5)).astype(BF16)
    kv_scr[WINDOW:WINDOW + tq, :] = qkv[:, D_MODEL:]
    k_ref[0] = qkv[tq - WINDOW:, D_MODEL:D_MODEL + KV_W]
    v_ref[0] = qkv[tq - WINDOW:, D_MODEL + KV_W:]

    lo = lax.broadcasted_iota(jnp.int32, (2 * WINDOW, KV_W), 1) < HEAD_DIM
    row_head = lax.broadcasted_iota(jnp.int32, (UNIT_ROWS, 1), 0) // WINDOW
    sinks = []
    for u in range(N_UNITS):
        g, p = divmod(u, 2)
        col = jnp.zeros((UNIT_ROWS, 1), F32)
        for jj in range(HEADS_PER_UNIT):
            col = jnp.where(row_head == jj, sink_ref[GQ * g + 2 * jj + p], col)
        sinks.append(col)

    for s in range(nsub):
        band = kv_scr[WINDOW * s:WINDOW * s + 2 * WINDOW, :]
        kband, vband = band[:, :KV_W], band[:, KV_W:]
        kroll = pltpu.roll(kband, HEAD_DIM, 1)
        vroll = pltpu.roll(vband, HEAD_DIM, 1)
        kpad = [[jnp.where(lo, kband, 0.0).astype(BF16), jnp.where(lo, 0.0, kroll).astype(BF16)],
                [jnp.where(lo, kroll, 0.0).astype(BF16), jnp.where(lo, 0.0, kband).astype(BF16)]]
        vpad = [[jnp.where(lo, vband, 0.0).astype(BF16), jnp.where(lo, 0.0, vroll).astype(BF16)],
                [jnp.where(lo, vroll, 0.0).astype(BF16), jnp.where(lo, 0.0, vband).astype(BF16)]]
        variant = jnp.where(i * nsub + s > 0, 0, 1)
        for g in range(N_KV_HEADS):
            slabs = range(HEADS_PER_UNIT * g, HEADS_PER_UNIT * (g + 1))
            qs = jnp.concatenate([q_scr[WINDOW * s:WINDOW * (s + 1), LANES * j:LANES * (j + 1)] for j in slabs],
                                 axis=0)
            acc = None
            for p in range(2):
                u = 2 * g + p
                sc = lax.dot_general(qs, kpad[g][p], (((1,), (1,)), ((), ())), preferred_element_type=F32)
                sc = sc + bias_ref[variant, u]
                m = jnp.maximum(jnp.max(sc, axis=-1, keepdims=True), sinks[u])
                e = jnp.exp(sc - m)
                den = jnp.sum(e, axis=-1, keepdims=True) + jnp.exp(sinks[u] - m)
                pv = jnp.dot(e.astype(BF16), vpad[g][p], preferred_element_type=F32)
                term = pv * (1.0 / den)
                acc = term if acc is None else acc + term
            for jj, j in enumerate(slabs):
                o_scr[WINDOW * s:WINDOW * (s + 1), LANES * j:LANES * (j + 1)] = (
                    acc[WINDOW * jj:WINDOW * (jj + 1)].astype(BF16))

    kv_scr[0:WINDOW, :] = kv_scr[tq:tq + WINDOW, :]
    xo_ref[0] = x + jnp.dot(o_scr[...], wo_ref[...], preferred_element_type=F32)


def _attn_prompt_layer(x, g, wqkv, wo, bias, sinks):
    b, t, d = x.shape
    tq = min(512, t)
    kern = functools.partial(_attn_prompt_kernel, tq=tq)
    return pl.pallas_call(
        kern,
        grid=(b, t // tq),
        in_specs=[
            pl.BlockSpec(memory_space=pltpu.SMEM),
            pl.BlockSpec((1, tq, d), lambda bi, i: (bi, i, 0)),
            _full((1, d)),
            _full((d, QKV_W)),
            _full((d, d)),
            _full((2, N_UNITS, UNIT_ROWS, 2 * WINDOW)),
        ],
        out_specs=[
            pl.BlockSpec((1, tq, d), lambda bi, i: (bi, i, 0)),
            pl.BlockSpec((1, WINDOW, KV_W), lambda bi, i: (bi, 0, 0)),
            pl.BlockSpec((1, WINDOW, KV_W), lambda bi, i: (bi, 0, 0)),
        ],
        out_shape=[
            jax.ShapeDtypeStruct((b, t, d), F32),
            jax.ShapeDtypeStruct((b, WINDOW, KV_W), F32),
            jax.ShapeDtypeStruct((b, WINDOW, KV_W), F32),
        ],
        scratch_shapes=[
            pltpu.VMEM((WINDOW + tq, 2 * KV_W), F32),
            pltpu.VMEM((tq, d), BF16),
            pltpu.VMEM((tq, d), BF16),
        ],
        compiler_params=_cparams(("arbitrary", "arbitrary")),
        name="attn_prompt",
    )(sinks, x, g, wqkv, wo, bias)


def _norm_linear_kernel(x_ref, g_ref, w_ref, o_ref):
    h = _rms(x_ref[...], g_ref[...]).astype(BF16)
    o_ref[...] = jnp.dot(h, w_ref[...], preferred_element_type=F32)


def _norm_linear(x, g, w):
    n, d = x.shape
    return pl.pallas_call(
        _norm_linear_kernel,
        out_shape=jax.ShapeDtypeStruct((n, w.shape[1]), F32),
        compiler_params=_cparams(None),
        name="norm_linear",
    )(x, g, w)


def _linear_residual_kernel(x_ref, o_ref, w_ref, xo_ref):
    xo_ref[...] = x_ref[...] + jnp.dot(o_ref[...].astype(BF16), w_ref[...], preferred_element_type=F32)


def _linear_residual(x, o, w):
    return pl.pallas_call(
        _linear_residual_kernel,
        out_shape=jax.ShapeDtypeStruct(x.shape, F32),
        compiler_params=_cparams(None),
        name="linear_residual",
    )(x, o, w)


def _attn_sample_kernel(sink_ref, q_ref, ck_ref, cv_ref, kn_ref, vn_ref, bias_ref, o_ref, kw_ref, vw_ref):
    kw = jnp.concatenate([ck_ref[:, 1:, :], kn_ref[...]], axis=1)
    vw = jnp.concatenate([cv_ref[:, 1:, :], vn_ref[...]], axis=1)
    kw_ref[...] = kw
    vw_ref[...] = vw
    q = (q_ref[...] * (HEAD_DIM ** -0.5)).astype(BF16)
    sc = jnp.einsum('bhc,bjc->bhj', q, kw.astype(BF16), preferred_element_type=F32)
    sc = sc + bias_ref[...][None]
    sink = sink_ref[...][None]
    m = jnp.maximum(jnp.max(sc, axis=-1, keepdims=True), sink)
    e = jnp.exp(sc - m)
    den = jnp.sum(e, axis=-1, keepdims=True) + jnp.exp(sink - m)
    pr = (e / den).astype(BF16)
    o_ref[...] = jnp.einsum('bhj,bjc->bhc', pr, vw.astype(BF16), preferred_element_type=F32)


def _attn_sample(q3, ck, cv, kn, vn, bias_s, sinks_col):
    n = q3.shape[0]
    bb = 8
    blk = lambda s: pl.BlockSpec((bb,) + s, lambda i: (i, 0, 0))
    return pl.pallas_call(
        _attn_sample_kernel,
        grid=(n // bb,),
        in_specs=[_full((N_HEADS, 1)), blk((N_HEADS, KV_W)), blk((WINDOW, KV_W)), blk((WINDOW, KV_W)),
                  blk((1, KV_W)), blk((1, KV_W)), _full((N_HEADS, WINDOW))],
        out_specs=[blk((N_HEADS, KV_W)), blk((WINDOW, KV_W)), blk((WINDOW, KV_W))],
        out_shape=[jax.ShapeDtypeStruct((n, N_HEADS, KV_W), F32),
                   jax.ShapeDtypeStruct((n, WINDOW, KV_W), F32),
                   jax.ShapeDtypeStruct((n, WINDOW, KV_W), F32)],
        compiler_params=_cparams(("arbitrary",)),
        name="attn_sample",
    )(sinks_col, q3, ck, cv, kn, vn, bias_s)


def _attn_sample_layer(x, g, wqkv, wo, bias_s, sinks, cache_k, cache_v):
    n = x.shape[0]
    qkv = _norm_linear(x, g, wqkv)
    q = qkv[:, :D_MODEL].reshape(n, N_KV_HEADS, GQ, 1, HEAD_DIM)
    place = jnp.eye(N_KV_HEADS, dtype=F32).reshape(1, N_KV_HEADS, 1, N_KV_HEADS, 1)
    q3 = (q * place).reshape(n, N_HEADS, KV_W)
    kn = qkv[:, D_MODEL:D_MODEL + KV_W].reshape(n, 1, KV_W)
    vn = qkv[:, D_MODEL + KV_W:].reshape(n, 1, KV_W)
    o3, kw, vw = _attn_sample(q3, cache_k.reshape(n, WINDOW, KV_W), cache_v.reshape(n, WINDOW, KV_W),
                              kn, vn, bias_s, sinks.reshape(N_HEADS, 1))
    o5 = o3.reshape(n, N_KV_HEADS, GQ, N_KV_HEADS, HEAD_DIM)
    o = jnp.stack([o5[:, gi, :, gi, :] for gi in range(N_KV_HEADS)], axis=1).reshape(n, D_MODEL)
    return _linear_residual(x, o, wo), kw, vw


def _ssm_prep_kernel(ar_ref, ai_ref, ldt_ref, br_ref, bi_ref, c_ref, a_out, bc_out, wb_out, wc_out):
    ar, ai = ar_ref[...], ai_ref[...]
    dt = jnp.exp(ldt_ref[...])
    mag = jnp.exp(ar * dt)
    lr = mag * jnp.cos(ai * dt)
    li = mag * jnp.sin(ai * dt)
    den = ar * ar + ai * ai
    cr = ((lr - 1.0) * ar + li * ai) / den
    ci = (li * ar - (lr - 1.0) * ai) / den
    row = lax.broadcasted_iota(jnp.int32, (SUBLANES, N_STATES), 0)
    a_out[...] = jnp.broadcast_to(lr, (SUBLANES, N_STATES))
    bc_out[...] = jnp.where(row < SUBLANES // 2, -li, li)
    for j in range(N_SLABS):
        crj = cr[:, SLAB_STATES * j:SLAB_STATES * (j + 1)]
        cij = ci[:, SLAB_STATES * j:SLAB_STATES * (j + 1)]
        br, bi = br_ref[j], bi_ref[j]
        wb_out[j, :, :SLAB_STATES] = (crj * br - cij * bi).astype(BF16)
        wb_out[j, :, SLAB_STATES:] = (crj * bi + cij * br).astype(BF16)
        wc_out[j, :SLAB_STATES, :] = c_ref[j, :SLAB_STATES, :].astype(BF16)
        wc_out[j, SLAB_STATES:, :] = (-c_ref[j, SLAB_STATES:, :]).astype(BF16)


def _ssm_prep(a_re, a_im, log_dt, b_re, b_im, c_re, c_im):
    eye = jnp.eye(SUBLANES, dtype=F32)

    def blk_b(b):
        t = b.reshape(N_SLABS, 8, SSM_STATE, SSM_GROUP).transpose(0, 1, 3, 2)
        return jnp.einsum('jghp,gk->jghkp', t, eye).reshape(N_SLABS, LANES, SLAB_STATES)

    def blk_c(c):
        t = c.reshape(N_SLABS, 8, SSM_GROUP, SSM_STATE).transpose(0, 1, 3, 2)
        return jnp.einsum('jgph,gk->jgpkh', t, eye).reshape(N_SLABS, SLAB_STATES, LANES)

    row = lambda a: a.reshape(1, N_STATES)
    ldt = jnp.broadcast_to(log_dt[:, None], (N_SSM_GROUPS, SSM_STATE))
    c_all = jnp.concatenate([blk_c(c_re), blk_c(c_im)], axis=1)
    return pl.pallas_call(
        _ssm_prep_kernel,
        out_shape=[jax.ShapeDtypeStruct((SUBLANES, N_STATES), F32),
                   jax.ShapeDtypeStruct((SUBLANES, N_STATES), F32),
                   jax.ShapeDtypeStruct((N_SLABS, LANES, 2 * SLAB_STATES), BF16),
                   jax.ShapeDtypeStruct((N_SLABS, 2 * SLAB_STATES, LANES), BF16)],
        compiler_params=_cparams(None),
        name="ssm_prep",
    )(row(a_re), row(a_im), row(ldt), blk_b(b_re), blk_b(b_im), c_all)


def _glu_residual(x, y, u, d, wa, wg):
    z = _gelu_tanh(y + d * u).astype(BF16)
    out = jnp.dot(z, wa, preferred_element_type=F32) * _sigmoid(jnp.dot(z, wg, preferred_element_type=F32))
    return x + out


def _ssm_prompt_kernel(x_ref, g_ref, wb_ref, wc_ref, a_ref, bc_ref, d_ref, wa_ref, wg_ref,
                       xo_ref, st_ref, z_scr, u_scr, xs_scr, *, nb, lc):
    i = pl.program_id(0)
    blocks_per_slab = SLAB_STATES // LANES
    half = SUBLANES // 2

    @pl.when(i == 0)
    def _():
        xs_scr[...] = jnp.zeros((SUBLANES, N_STATES), F32)

    for b in range(nb):
        u_scr[b * lc:(b + 1) * lc, :] = _rms(x_ref[b], g_ref[...])
    ub = u_scr[...].astype(BF16)
    for j in range(N_SLABS):
        res = jnp.dot(ub[:, LANES * j:LANES * (j + 1)], wb_ref[j], preferred_element_type=F32)
        for b in range(nb):
            rb = res[b * lc:(b + 1) * lc]
            for c in range(blocks_per_slab):
                blk = blocks_per_slab * j + c
                z_scr[blk, pl.ds(b, lc, stride=SUBLANES), :] = rb[:, LANES * c:LANES * (c + 1)]
                z_scr[blk, pl.ds(half + b, lc, stride=SUBLANES), :] = (
                    rb[:, SLAB_STATES + LANES * c:SLAB_STATES + LANES * (c + 1)])
    if nb < half:
        for blk in range(N_STATES // LANES):
            for r in list(range(nb, half)) + list(range(half + nb, SUBLANES)):
                z_scr[blk, pl.ds(r, lc, stride=SUBLANES), :] = jnp.zeros((lc, LANES), F32)

    group = 8
    for cb in range(N_STATES // LANES // group):
        blks = [group * cb + k for k in range(group)]
        a = [a_ref[:, LANES * q:LANES * (q + 1)] for q in blks]
        bc = [bc_ref[:, LANES * q:LANES * (q + 1)] for q in blks]
        x0 = tuple(xs_scr[:, LANES * q:LANES * (q + 1)] for q in blks)

        def step(t, xs, blks=blks, a=a, bc=bc):
            r = pl.multiple_of(t * SUBLANES, SUBLANES)
            out = []
            for k, q in enumerate(blks):
                xn = a[k] * xs[k] + bc[k] * pltpu.roll(xs[k], half, 0) + z_scr[q, pl.ds(r, SUBLANES), :]
                z_scr[q, pl.ds(r, SUBLANES), :] = xn
                out.append(xn)
            return tuple(out)

        xf = lax.fori_loop(0, lc, step, x0)
        for k, q in enumerate(blks):
            xs_scr[:, LANES * q:LANES * (q + 1)] = xf[k]
    st_ref[...] = xs_scr[...]

    for b in range(nb):
        ys = []
        for j in range(N_SLABS):
            parts = [z_scr[blocks_per_slab * j + c, pl.ds(b, lc, stride=SUBLANES), :] for c in range(blocks_per_slab)]
            parts += [z_scr[blocks_per_slab * j + c, pl.ds(half + b, lc, stride=SUBLANES), :]
                      for c in range(blocks_per_slab)]
            s = jnp.concatenate(parts, axis=1).astype(BF16)
            ys.append(jnp.dot(s, wc_ref[j], preferred_element_type=F32))
        y = jnp.concatenate(ys, axis=1)
        xo_ref[b] = _glu_residual(x_ref[b], y, u_scr[b * lc:(b + 1) * lc, :], d_ref[...], wa_ref[...], wg_ref[...])


def _ssm_prompt_layer(x, g, prep, d, wa, wg):
    a_rows, bc_rows, wb, wc = prep
    nb, t, dm = x.shape
    assert nb <= SUBLANES // 2
    lc = min(128, t)
    kern = functools.partial(_ssm_prompt_kernel, nb=nb, lc=lc)
    return pl.pallas_call(
        kern,
        grid=(t // lc,),
        in_specs=[
            pl.BlockSpec((nb, lc, dm), lambda i: (0, i, 0)),
            _full((1, dm)),
            _full(wb.shape), _full(wc.shape),
            _full((SUBLANES, N_STATES)), _full((SUBLANES, N_STATES)),
            _full((1, dm)), _full((dm, dm)), _full((dm, dm)),
        ],
        out_specs=[pl.BlockSpec((nb, lc, dm), lambda i: (0, i, 0)), _full((SUBLANES, N_STATES))],
        out_shape=[jax.ShapeDtypeStruct((nb, t, dm), F32), jax.ShapeDtypeStruct((SUBLANES, N_STATES), F32)],
        scratch_shapes=[
            pltpu.VMEM((N_STATES // LANES, SUBLANES * lc, LANES), F32),
            pltpu.VMEM((nb * lc, dm), F32),
            pltpu.VMEM((SUBLANES, N_STATES), F32),
        ],
        compiler_params=_cparams(("arbitrary",)),
        name="ssm_prompt",
    )(x, g, wb, wc, a_rows, bc_rows, d, wa, wg)


def _ssm_sample_kernel(x_ref, g_ref, wb_ref, wc_ref, a_ref, bc_ref, h0r_ref, h0i_ref, d_ref, wa_ref, wg_ref,
                       xo_ref, sr_ref, si_ref):
    x = x_ref[...]
    u = _rms(x, g_ref[...])
    ub = u.astype(BF16)
    ys = []
    for j in range(N_SLABS):
        sl = slice(SLAB_STATES * j, SLAB_STATES * (j + 1))
        res = jnp.dot(ub[:, LANES * j:LANES * (j + 1)], wb_ref[j], preferred_element_type=F32)
        lr = a_ref[0:1, sl]
        li = bc_ref[SUBLANES - 1:SUBLANES, sl]
        h0r, h0i = h0r_ref[:, sl], h0i_ref[:, sl]
        xr = res[:, :SLAB_STATES] + (lr * h0r - li * h0i)
        xi = res[:, SLAB_STATES:] + (lr * h0i + li * h0r)
        sr_ref[:, sl] = xr
        si_ref[:, sl] = xi
        s = jnp.concatenate([xr, xi], axis=1).astype(BF16)
        ys.append(jnp.dot(s, wc_ref[j], preferred_element_type=F32))
    y = jnp.concatenate(ys, axis=1)
    xo_ref[...] = _glu_residual(x, y, u, d_ref[...], wa_ref[...], wg_ref[...])


def _ssm_sample_layer(x, g, prep, h0r, h0i, d, wa, wg):
    a_rows, bc_rows, wb, wc = prep
    n = x.shape[0]
    return pl.pallas_call(
        _ssm_sample_kernel,
        out_shape=[jax.ShapeDtypeStruct(x.shape, F32),
                   jax.ShapeDtypeStruct((n, N_STATES), F32),
                   jax.ShapeDtypeStruct((n, N_STATES), F32)],
        compiler_params=_cparams(None),
        name="ssm_sample",
    )(x, g, wb, wc, a_rows, bc_rows, h0r, h0i, d, wa, wg)


ROUTER_W = LANES
EXPERT_LANE0 = N_EXPERT_GROUPS


def _route_topk(h, wr, br):
    h_hi = h.astype(BF16)
    h_lo = (h - h_hi.astype(F32)).astype(BF16)
    w_hi = wr.astype(BF16)
    w_lo = (wr - w_hi.astype(F32)).astype(BF16)
    dot = lambda a, b: jnp.dot(a, b, preferred_element_type=F32)
    logits = dot(h_hi, w_hi) + (dot(h_lo, w_hi) + dot(h_hi, w_lo)) + br
    lane = lax.broadcasted_iota(jnp.int32, logits.shape, 1)
    neg = -jnp.inf
    gl = jnp.where(lane < N_EXPERT_GROUPS, logits, neg)
    gmax = jnp.max(gl, axis=-1, keepdims=True)
    g_sel = jnp.min(jnp.where(gl == gmax, lane, ROUTER_W), axis=-1, keepdims=True)
    p_grp = 1.0 / jnp.sum(jnp.exp(gl - gmax), axis=-1, keepdims=True)
    lane_grp = (lane - EXPERT_LANE0) // EXPERTS_PER_GROUP
    in_grp = (lane >= EXPERT_LANE0) & (lane < EXPERT_LANE0 + N_EXPERTS) & (lane_grp == g_sel)
    el = jnp.where(in_grp, logits, neg)
    v1 = jnp.max(el, axis=-1, keepdims=True)
    i1 = jnp.min(jnp.where(el == v1, lane, ROUTER_W), axis=-1, keepdims=True)
    el2 = jnp.where(lane == i1, neg, el)
    v2 = jnp.max(el2, axis=-1, keepdims=True)
    i2 = jnp.min(jnp.where(el2 == v2, lane, ROUTER_W), axis=-1, keepdims=True)
    e2 = jnp.exp(v2 - v1)
    w1 = p_grp / (1.0 + e2)
    w2 = p_grp * e2 / (1.0 + e2)
    return lane, i1, i2, w1, w2


def _route(h, wr, br):
    lane, i1, i2, w1, w2 = _route_topk(h, wr, br)
    return jnp.where(lane == i1, w1, 0.0) + jnp.where(lane == i2, w2, 0.0)


def _moe_kernel(x_ref, p_ref, gffn_ref, wr_ref, br_ref, wg_ref, wu_ref, wd_ref,
                gple_ref, wpg_ref, wpp_ref, gfin_ref, o_ref, hn_scr, gates_scr, acc_scr, *, final):
    e = pl.program_id(1)

    @pl.when(e == 0)
    def _():
        x = x_ref[...]
        h = _rms(x, gffn_ref[...])
        hn_scr[...] = h.astype(BF16)
        gates_scr[...] = _route(h, wr_ref[...], br_ref[...])
        acc_scr[...] = x

    hb = hn_scr[...]
    gt = jnp.dot(hb, wg_ref[0].astype(BF16), preferred_element_type=F32)
    up = jnp.dot(hb, wu_ref[0].astype(BF16), preferred_element_type=F32)
    a = (gt * _sigmoid(gt) * up).astype(BF16)
    gates = gates_scr[...]
    lane = lax.broadcasted_iota(jnp.int32, gates.shape, 1)
    ge = jnp.sum(jnp.where(lane == e + EXPERT_LANE0, gates, 0.0), axis=-1, keepdims=True)
    acc_scr[...] += ge * jnp.dot(a, wd_ref[0].astype(BF16), preferred_element_type=F32)

    @pl.when(e == pl.num_programs(1) - 1)
    def _():
        x1 = acc_scr[...]
        gate = _sigmoid(jnp.dot(_rms(x1, gple_ref[...]).astype(BF16), wpg_ref[...], preferred_element_type=F32))
        x2 = x1 + gate * jnp.dot(p_ref[...].astype(BF16), wpp_ref[...], preferred_element_type=F32)
        o_ref[...] = _rms(x2, gfin_ref[...]) if final else x2


def _moe_ple_layer(li, x, p, gffn, wr, br, wg, wu, wd, gple, wpg, wpp, gfin, final):
    n, d = x.shape
    tm = min(1024, n)
    kern = functools.partial(_moe_kernel, final=final)
    tok = lambda w: pl.BlockSpec((tm, w), lambda t, e: (t, 0))
    const = lambda s: pl.BlockSpec(s, lambda t, e: (0,) * len(s))
    return pl.pallas_call(
        kern,
        grid=(n // tm, N_EXPERTS),
        in_specs=[
            tok(d), pl.BlockSpec((None, tm, PLE_DIM), lambda t, e: (li, t, 0)),
            const((1, d)), const((d, ROUTER_W)), const((1, ROUTER_W)),
            pl.BlockSpec((None, 1, d, D_EXPERT), lambda t, e: (li, e, 0, 0)),
            pl.BlockSpec((None, 1, d, D_EXPERT), lambda t, e: (li, e, 0, 0)),
            pl.BlockSpec((None, 1, D_EXPERT, d), lambda t, e: (li, e, 0, 0)),
            const((1, d)), const((d, d)), const((PLE_DIM, d)), const((1, d)),
        ],
        out_specs=tok(d),
        out_shape=jax.ShapeDtypeStruct((n, d), F32),
        scratch_shapes=[pltpu.VMEM((tm, d), BF16), pltpu.VMEM((tm, ROUTER_W), F32), pltpu.VMEM((tm, d), F32)],
        compiler_params=_cparams(("arbitrary", "arbitrary")),
        name="moe_ple",
    )(x, p, gffn, wr, br, wg, wu, wd, gple, wpg, wpp, gfin)


TOKEN_TILE_ROWS = D_MODEL // LANES
INFO_W1, INFO_W2, INFO_E1, INFO_E2, INFO_R1, INFO_R2 = range(6)
EXPERT_TILE = 256


def _to_token_tiles(ref_2d, val, n):
    r = val.shape[1] // LANES
    for s in range(r):
        ref_2d[pl.ds(s, n, stride=r), :] = val[:, LANES * s:LANES * (s + 1)]


def _from_token_tiles(ref_2d, n, r):
    return jnp.concatenate([ref_2d[pl.ds(s, n, stride=r), :] for s in range(r)], axis=1)


def _tile_rows(i, r=TOKEN_TILE_ROWS):
    return pl.ds(pl.multiple_of(i * r, r), r)


def _route_kernel(x_ref, g_ref, wr_ref, br_ref, before_ref, info_ref, cnt_ref, base_scr):
    i = pl.program_id(0)

    @pl.when(i == 0)
    def _():
        base_scr[...] = jnp.zeros(base_scr.shape, F32)

    h = _rms(x_ref[...], g_ref[...])
    lane, i1, i2, w1, w2 = _route_topk(h, wr_ref[...], br_ref[...])
    chosen = jnp.where((lane == i1) | (lane == i2), 1.0, 0.0)
    prefix = jnp.dot(before_ref[...], chosen.astype(BF16), preferred_element_type=F32) + base_scr[...]
    rank1 = jnp.sum(jnp.where(lane == i1, prefix, 0.0), axis=-1, keepdims=True)
    rank2 = jnp.sum(jnp.where(lane == i2, prefix, 0.0), axis=-1, keepdims=True)
    base_scr[...] += jnp.sum(chosen, axis=0, keepdims=True)
    cnt_ref[...] = base_scr[...]
    e1 = (i1 - EXPERT_LANE0).astype(F32)
    e2 = (i2 - EXPERT_LANE0).astype(F32)
    info = jnp.zeros(h.shape[:1] + (ROUTER_W,), F32)
    for ln, v in ((INFO_W1, w1), (INFO_W2, w2), (INFO_E1, e1), (INFO_E2, e2), (INFO_R1, rank1), (INFO_R2, rank2)):
        info = jnp.where(lane == ln, v, info)
    info_ref[...] = info


def _route_call(x, gffn, wr, br):
    n, d = x.shape
    tm = 512
    before = jnp.asarray(np.tril(np.ones((tm, tm), np.float32), -1), BF16)
    return pl.pallas_call(
        _route_kernel,
        grid=(n // tm,),
        in_specs=[pl.BlockSpec((tm, d), lambda i: (i, 0)), _full((1, d)), _full((d, ROUTER_W)), _full((1, ROUTER_W)),
                  _full((tm, tm))],
        out_specs=[pl.BlockSpec((tm, ROUTER_W), lambda i: (i, 0)), _full((1, ROUTER_W))],
        out_shape=[jax.ShapeDtypeStruct((n, ROUTER_W), F32), jax.ShapeDtypeStruct((1, ROUTER_W), F32)],
        scratch_shapes=[pltpu.VMEM((1, ROUTER_W), F32)],
        compiler_params=_cparams(("arbitrary",)),
        name="moe_route",
    )(x, gffn, wr, br, before)


def _dispatch_kernel(pos_ref, cnt_ref, start_ref, x_ref, g_ref, xs_hbm, hbuf, zbuf, sem, zsem,
                     *, td, n_steps, n_pad_rows):
    i = pl.program_id(0)
    slot = i % 2

    def wait_slot(sl):
        for _ in range(2):
            pltpu.make_async_copy(hbuf.at[sl], hbuf.at[sl], sem.at[sl]).wait()

    @pl.when(i >= 2)
    def _():
        wait_slot(slot)

    _to_token_tiles(hbuf.at[slot], _rms(x_ref[...], g_ref[...]), td)
    rows = _tile_rows

    def issue(n8, carry):
        for u in range(SUBLANES):
            n = n8 * SUBLANES + u
            t = i * td + n
            src = hbuf.at[slot, rows(n), :]
            pltpu.make_async_copy(src, xs_hbm.at[rows(pos_ref[0, t]), :], sem.at[slot]).start(priority=0)
            pltpu.make_async_copy(src, xs_hbm.at[rows(pos_ref[1, t]), :], sem.at[slot]).start(priority=1)
        return carry

    lax.fori_loop(0, td // SUBLANES, issue, 0)

    @pl.when(i == 0)
    def _():
        zbuf[...] = jnp.zeros(zbuf.shape, zbuf.dtype)
        for e in range(N_EXPERTS + 1):
            lo = start_ref[e] + cnt_ref[e]
            hi = start_ref[e + 1]

            def zero_row(r, carry):
                pltpu.make_async_copy(zbuf, xs_hbm.at[rows(r), :], zsem).start()
                return carry

            lax.fori_loop(lo, hi, zero_row, 0)

    @pl.when(i == n_steps - 1)
    def _():
        wait_slot(slot)
        if n_steps > 1:
            wait_slot(1 - slot)
        pad = xs_hbm.at[pl.ds(0, n_pad_rows * TOKEN_TILE_ROWS), :]
        pltpu.make_async_copy(pad, pad, zsem).wait()


def _dispatch_call(pos, cnt, start, x, gffn, a_pad, n_pad_rows):
    n, d = x.shape
    td = 512
    n_steps = n // td
    kern = functools.partial(_dispatch_kernel, td=td, n_steps=n_steps, n_pad_rows=n_pad_rows)
    return pl.pallas_call(
        kern,
        grid_spec=pltpu.PrefetchScalarGridSpec(
            num_scalar_prefetch=3,
            grid=(n_steps,),
            in_specs=[pl.BlockSpec((td, d), lambda i, *_: (i, 0)), pl.BlockSpec((1, d), lambda i, *_: (0, 0))],
            out_specs=pl.BlockSpec(memory_space=pl.ANY),
            scratch_shapes=[pltpu.VMEM((2, td * TOKEN_TILE_ROWS, LANES), F32),
                            pltpu.VMEM((TOKEN_TILE_ROWS, LANES), F32),
                            pltpu.SemaphoreType.DMA((2,)), pltpu.SemaphoreType.DMA(())],
        ),
        out_shape=jax.ShapeDtypeStruct((a_pad * TOKEN_TILE_ROWS, LANES), F32),
        compiler_params=_cparams(("arbitrary",)),
        name="moe_dispatch",
    )(pos, cnt, start, x, gffn)


EXPERT_IN_SLOTS = 3


def _experts_kernel(te_ref, xs_hbm, wg_ref, wu_ref, wd_ref, y_ref, xbuf, xsem, wg_scr, wu_scr, wd_scr,
                    *, te, n_tiles):
    t = pl.program_id(0)
    rows = te * TOKEN_TILE_ROWS

    def fetch(tile):
        slot = tile % EXPERT_IN_SLOTS
        src = xs_hbm.at[pl.ds(pl.multiple_of(tile * rows, rows), rows), :]
        return pltpu.make_async_copy(src, xbuf.at[slot], xsem.at[slot])

    @pl.when(t == 0)
    def _():
        for ahead in range(min(EXPERT_IN_SLOTS - 1, n_tiles)):
            fetch(ahead).start()

    @pl.when(t + EXPERT_IN_SLOTS - 1 < n_tiles)
    def _():
        fetch(t + EXPERT_IN_SLOTS - 1).start()

    @pl.when((t == 0) | (te_ref[t] != te_ref[jnp.maximum(t - 1, 0)]))
    def _():
        wg_scr[...] = wg_ref[0].astype(BF16)
        wu_scr[...] = wu_ref[0].astype(BF16)
        wd_scr[...] = wd_ref[0].astype(BF16)

    fetch(t).wait()
    x = _from_token_tiles(xbuf.at[t % EXPERT_IN_SLOTS], te, TOKEN_TILE_ROWS).astype(BF16)
    gt = jnp.dot(x, wg_scr[...], preferred_element_type=F32)
    up = jnp.dot(x, wu_scr[...], preferred_element_type=F32)
    a = (gt * _sigmoid(gt) * up).astype(BF16)
    _to_token_tiles(y_ref, jnp.dot(a, wd_scr[...], preferred_element_type=F32), te)


def _experts_call(li, tile_expert, xs, wg, wu, wd, te):
    rows = te * TOKEN_TILE_ROWS
    n_tiles = xs.shape[0] // rows
    d = wg.shape[2]
    kern = functools.partial(_experts_kernel, te=te, n_tiles=n_tiles)
    return pl.pallas_call(
        kern,
        grid_spec=pltpu.PrefetchScalarGridSpec(
            num_scalar_prefetch=1,
            grid=(n_tiles,),
            in_specs=[pl.BlockSpec(memory_space=pl.ANY),
                      pl.BlockSpec((None, 1, d, D_EXPERT), lambda t, ex: (li, ex[t], 0, 0)),
                      pl.BlockSpec((None, 1, d, D_EXPERT), lambda t, ex: (li, ex[t], 0, 0)),
                      pl.BlockSpec((None, 1, D_EXPERT, d), lambda t, ex: (li, ex[t], 0, 0))],
            out_specs=pl.BlockSpec((rows, LANES), lambda t, ex: (t, 0)),
            scratch_shapes=[pltpu.VMEM((EXPERT_IN_SLOTS, rows, LANES), F32),
                            pltpu.SemaphoreType.DMA((EXPERT_IN_SLOTS,)),
                            pltpu.VMEM((d, D_EXPERT), BF16), pltpu.VMEM((d, D_EXPERT), BF16),
                            pltpu.VMEM((D_EXPERT, d), BF16)],
        ),
        out_shape=jax.ShapeDtypeStruct((n_tiles * rows, LANES), F32),
        compiler_params=_cparams(("arbitrary",)),
        name="moe_experts",
    )(tile_expert, xs, wg, wu, wd)


def _combine_kernel(pos_ref, x_ref, info_ref, p_ref, gple_ref, wpg_ref, wpp_ref, gfin_ref, ys_hbm,
                    o_ref, ybuf, sem, *, tm, n_steps, final):
    i = pl.program_id(0)
    slot = i % 2

    def issue(step, sl):
        def body(n8, carry):
            for u in range(SUBLANES):
                n = n8 * SUBLANES + u
                t = step * tm + n
                for k in range(2):
                    pltpu.make_async_copy(ys_hbm.at[_tile_rows(pos_ref[k, t]), :],
                                          ybuf.at[sl, k, _tile_rows(n), :], sem.at[sl]).start(priority=k)
            return carry

        lax.fori_loop(0, tm // SUBLANES, body, 0)

    @pl.when(i == 0)
    def _():
        issue(0, 0)

    @pl.when(i + 1 < n_steps)
    def _():
        issue(i + 1, 1 - slot)

    for k in range(2):
        pltpu.make_async_copy(ybuf.at[slot, k], ybuf.at[slot, k], sem.at[slot]).wait()
    y1 = _from_token_tiles(ybuf.at[slot, 0], tm, TOKEN_TILE_ROWS)
    y2 = _from_token_tiles(ybuf.at[slot, 1], tm, TOKEN_TILE_ROWS)
    info = info_ref[...]
    x1 = x_ref[...] + (info[:, INFO_W1:INFO_W1 + 1] * y1 + info[:, INFO_W2:INFO_W2 + 1] * y2)
    gate = _sigmoid(jnp.dot(_rms(x1, gple_ref[...]).astype(BF16), wpg_ref[...], preferred_element_type=F32))
    x2 = x1 + gate * jnp.dot(p_ref[...].astype(BF16), wpp_ref[...], preferred_element_type=F32)
    o_ref[...] = _rms(x2, gfin_ref[...]) if final else x2


def _combine_call(li, pos, x, info, p, gple, wpg, wpp, gfin, ys, final):
    n, d = x.shape
    tm = 512
    n_steps = n // tm
    kern = functools.partial(_combine_kernel, tm=tm, n_steps=n_steps, final=final)
    tok = lambda w: pl.BlockSpec((tm, w), lambda i, *_: (i, 0))
    const = lambda s: pl.BlockSpec(s, lambda i, *_: (0,) * len(s))
    return pl.pallas_call(
        kern,
        grid_spec=pltpu.PrefetchScalarGridSpec(
            num_scalar_prefetch=1,
            grid=(n_steps,),
            in_specs=[tok(d), tok(ROUTER_W), pl.BlockSpec((None, tm, PLE_DIM), lambda i, *_: (li, i, 0)),
                      const((1, d)), const((d, d)), const((PLE_DIM, d)),
                      const((1, d)), pl.BlockSpec(memory_space=pl.ANY)],
            out_specs=tok(d),
            scratch_shapes=[pltpu.VMEM((2, 2, tm * TOKEN_TILE_ROWS, LANES), F32), pltpu.SemaphoreType.DMA((2,))],
        ),
        out_shape=jax.ShapeDtypeStruct((n, d), F32),
        compiler_params=_cparams(("arbitrary",)),
        name="moe_combine",
    )(pos, x, info, p, gple, wpg, wpp, gfin, ys)


def _positions_kernel(info_ref, start_ref, pos_ref):
    info = info_ref[...]
    lane = lax.broadcasted_iota(jnp.int32, info.shape, 1)
    lane_expert = (lane - EXPERT_LANE0).astype(F32)
    start = start_ref[...]
    cols = []
    for e_lane, r_lane in ((INFO_E1, INFO_R1), (INFO_E2, INFO_R2)):
        seg = jnp.sum(jnp.where(lane_expert == info[:, e_lane:e_lane + 1], start, 0.0), axis=-1, keepdims=True)
        cols.append(seg + info[:, r_lane:r_lane + 1])
    both = jnp.where(lane == 0, cols[0], jnp.where(lane == 1, cols[1], 0.0))
    pos_ref[...] = both.T[:SUBLANES, :].astype(jnp.int32)


def _positions_call(info, start_row):
    n = info.shape[0]
    tm = 512
    return pl.pallas_call(
        _positions_kernel,
        grid=(n // tm,),
        in_specs=[pl.BlockSpec((tm, ROUTER_W), lambda i: (i, 0)), _full((1, ROUTER_W))],
        out_specs=pl.BlockSpec((SUBLANES, tm), lambda i: (0, i)),
        out_shape=jax.ShapeDtypeStruct((SUBLANES, n), jnp.int32),
        compiler_params=_cparams(("arbitrary",)),
        name="moe_positions",
    )(info, start_row)


def _moe_routed_layer(li, x, p, gffn, wr, br, wg, wu, wd, gple, wpg, wpp, gfin, final):
    n = x.shape[0]
    te = EXPERT_TILE
    n_assign = 2 * n
    assert n_assign % te == 0
    a_pad = n_assign + N_EXPERTS * te
    info, cnt = _route_call(x, gffn, wr, br)
    counts = cnt[0, EXPERT_LANE0:EXPERT_LANE0 + N_EXPERTS].astype(jnp.int32)
    padded = (counts + te - 1) // te * te
    seg_end = jnp.cumsum(padded)
    seg_start = seg_end - padded
    start_row = jnp.zeros((1, ROUTER_W), F32).at[0, EXPERT_LANE0:EXPERT_LANE0 + N_EXPERTS].set(seg_start.astype(F32))
    pos = _positions_call(info, start_row)[:2]
    start = jnp.concatenate([seg_start, seg_end[-1:], jnp.full((1,), a_pad, jnp.int32)])
    cnt_pad = jnp.concatenate([counts, jnp.zeros((1,), jnp.int32)])
    tile_row0 = jnp.arange(a_pad // te, dtype=jnp.int32) * te
    tile_expert = jnp.minimum(jnp.sum(seg_end[None, :] <= tile_row0[:, None], axis=1), N_EXPERTS - 1).astype(jnp.int32)
    xs = _dispatch_call(pos, cnt_pad, start, x, gffn, a_pad, N_EXPERTS * te)
    ys = _experts_call(li, tile_expert, xs, wg, wu, wd, te)
    return _combine_call(li, pos, x, info, p, gple, wpg, wpp, gfin, ys, final)


def kernel(x_prompt, x_sample, cache_k_win, cache_v_win, state_ssm_re, state_ssm_im, p_prompt, p_sample,
           rel_bias_table, g_mix, w_qkv, w_o, sinks, ssm_a_re, ssm_a_im, ssm_log_dt, ssm_b_re, ssm_b_im,
           ssm_c_re, ssm_c_im, ssm_d, w_glu_a, w_glu_b, g_ffn, w_router_group, b_router_group,
           w_router_expert, b_router_expert, w_exp_gate, w_exp_up, w_exp_down, g_ple, w_ple_gate,
           w_ple_proj, g_final):
    bsz, seq, d = x_prompt.shape
    ns = x_sample.shape[0]
    depth = g_mix.shape[0]
    row = lambda v: v.reshape(1, -1).astype(F32)

    qi = np.arange(WINDOW)[:, None]
    sj = np.arange(2 * WINDOW)[None, :]
    dist = qi + WINDOW - sj
    idx_p = _t5_bucket_np(dist)
    ok_p = ((dist >= 0) & (dist < WINDOW)).astype(np.int32)
    idx_s = np.broadcast_to(_t5_bucket_np(WINDOW - 1 - np.arange(WINDOW))[None, :], (SUBLANES, WINDOW))
    bias_p = _band_bias(rel_bias_table, idx_p, ok_p)
    bias_s = _bias_from_table(rel_bias_table, idx_s)[:, 0, :]

    wqkv_b = w_qkv[0].astype(BF16)
    wo_b = w_o[0].astype(BF16)
    wpg_b, wpp_b = w_ple_gate.astype(BF16), w_ple_proj.astype(BF16)
    wa_b, wgl_b = w_glu_a[0].astype(BF16), w_glu_b[0].astype(BF16)
    pad = ROUTER_W - N_EXPERT_GROUPS - N_EXPERTS
    wr = jnp.concatenate([w_router_group, w_router_expert.reshape(depth, d, N_EXPERTS),
                          jnp.zeros((depth, d, pad), F32)], axis=-1)
    br = jnp.concatenate([b_router_group, b_router_expert.reshape(depth, N_EXPERTS),
                          jnp.zeros((depth, pad), F32)], axis=-1)

    prep = _ssm_prep(ssm_a_re[0], ssm_a_im[0], ssm_log_dt[0], ssm_b_re[0], ssm_b_im[0], ssm_c_re[0], ssm_c_im[0])
    d_row = row(ssm_d[0])

    def moe(x2d, p_all, i):
        layer = _moe_routed_layer if 2 * x2d.shape[0] >= N_EXPERTS * EXPERT_TILE else _moe_ple_layer
        return layer(i, x2d, p_all, row(g_ffn[i]), wr[i], br[i:i + 1], w_exp_gate, w_exp_up, w_exp_down,
                     row(g_ple[i]), wpg_b[i], wpp_b[i], row(g_final), final=(i == depth - 1))

    pp = p_prompt.reshape(depth, bsz * seq, PLE_DIM)
    xp, kp, vp = _attn_prompt_layer(x_prompt, row(g_mix[0]), wqkv_b, wo_b, bias_p, sinks[0])
    xp = moe(xp.reshape(bsz * seq, d), pp, 0).reshape(bsz, seq, d)
    xp, st_p = _ssm_prompt_layer(xp, row(g_mix[1]), prep, d_row, wa_b, wgl_b)
    y_prompt = moe(xp.reshape(bsz * seq, d), pp, 1).reshape(bsz, seq, d)

    ps = p_sample.reshape(depth, ns, PLE_DIM)
    xs = x_sample.reshape(ns, d)
    xs, kws, vws = _attn_sample_layer(xs, row(g_mix[0]), wqkv_b, wo_b, bias_s, sinks[0],
                                      cache_k_win[0], cache_v_win[0])
    xs = moe(xs, ps, 0)
    xs, sr_s, si_s = _ssm_sample_layer(xs, row(g_mix[1]), prep, state_ssm_re[0].reshape(ns, N_STATES),
                                       state_ssm_im[0].reshape(ns, N_STATES), d_row, wa_b, wgl_b)
    y_sample = moe(xs, ps, 1).reshape(ns, 1, d)

    half = SUBLANES // 2
    win = lambda a, n: a.reshape(1, n, WINDOW, N_KV_HEADS, HEAD_DIM)
    st = lambda a, n: a.reshape(1, n, N_SSM_GROUPS, SSM_STATE)
    return (y_prompt, y_sample, win(kp, bsz), win(vp, bsz), win(kws, ns), win(vws, ns),
            st(st_p[:bsz], bsz), st(st_p[half:half + bsz], bsz), st(sr_s, ns), st(si_s, ns))
```

```python
import functools
import math

import numpy as np
import jax
import jax.numpy as jnp
from jax import lax
from jax.experimental import pallas as pl
from jax.experimental.pallas import tpu as pltpu

F32 = jnp.float32
BF16 = jnp.bfloat16

D_MODEL = 1024
HEAD_DIM = 64
N_HEADS = 16
N_KV_HEADS = 2
GQ = N_HEADS // N_KV_HEADS
WINDOW = 128
N_BUCKETS = 32
MAX_EXACT = 16
MAX_DISTANCE = 128
SSM_GROUP = 16
N_SSM_GROUPS = 64
SSM_STATE = 64
N_STATES = N_SSM_GROUPS * SSM_STATE
N_EXPERT_GROUPS = 4
EXPERTS_PER_GROUP = 8
N_EXPERTS = 32
D_EXPERT = 256
PLE_DIM = 256
RMS_EPS = 1e-6
NEG_INF = -1e30

LANES = 128
SUBLANES = 8
KV_W = N_KV_HEADS * HEAD_DIM
QKV_W = D_MODEL + 2 * KV_W
SLAB_STATES = 512
N_SLABS = D_MODEL // LANES
VMEM_LIMIT = 56 * 1024 * 1024


def _cparams(sem):
    return pltpu.CompilerParams(dimension_semantics=sem, vmem_limit_bytes=VMEM_LIMIT)


def _rms(x, g):
    return x * lax.rsqrt(jnp.mean(x * x, axis=-1, keepdims=True) + RMS_EPS) * g


def _gelu_tanh(x):
    c = math.sqrt(2.0 / math.pi)
    return x * (0.5 * (1.0 + jnp.tanh(c * (x + 0.044715 * (x * x * x)))))


def _sigmoid(x):
    return 1.0 / (1.0 + jnp.exp(-x))


def _full(shape):
    n = len(shape)
    return pl.BlockSpec(shape, lambda *_: (0,) * n)


def _t5_bucket_np(dist):
    n = np.maximum(dist, 0)
    nf = np.maximum(n, 1).astype(np.float64)
    large = MAX_EXACT + (np.log(nf / MAX_EXACT) / math.log(MAX_DISTANCE / MAX_EXACT)
                         * (N_BUCKETS - MAX_EXACT)).astype(np.int32)
    large = np.minimum(large, N_BUCKETS - 1)
    return np.where(n < MAX_EXACT, n, large).astype(np.int32)


def _bias_kernel(table_ref, idx_ref, o_ref):
    h = pl.program_id(0)
    idx = idx_ref[...]
    acc = jnp.zeros(idx.shape, F32)
    for b in range(N_BUCKETS):
        acc = jnp.where(idx == b, table_ref[b, h], acc)
    o_ref[0] = acc


def _bias_from_table(table, idx_np):
    q, k = idx_np.shape
    return pl.pallas_call(
        _bias_kernel,
        grid=(N_HEADS,),
        in_specs=[pl.BlockSpec(memory_space=pltpu.SMEM), _full((q, k))],
        out_specs=pl.BlockSpec((1, q, k), lambda h: (h, 0, 0)),
        out_shape=jax.ShapeDtypeStruct((N_HEADS, q, k), F32),
        name="rel_bias",
    )(table, jnp.asarray(idx_np))


HEADS_PER_UNIT = GQ // 2
N_UNITS = N_HEADS // HEADS_PER_UNIT
UNIT_ROWS = HEADS_PER_UNIT * WINDOW


def _band_bias_kernel(table_ref, idx_ref, ok_ref, o_ref):
    h = pl.program_id(0)
    idx = idx_ref[...]
    acc = jnp.zeros(idx.shape, F32)
    for b in range(N_BUCKETS):
        acc = jnp.where(idx == b, table_ref[b, h], acc)
    base = jnp.where(ok_ref[...] != 0, acc, NEG_INF)
    sj = lax.broadcasted_iota(jnp.int32, idx.shape, 1)
    o_ref[0, 0] = base
    o_ref[1, 0] = jnp.where(sj >= WINDOW, base, NEG_INF)


def _band_bias(table, idx_np, ok_np):
    def out_map(h):
        r = h % GQ
        return (0, (h // GQ) * 2 + r % 2, r // 2, 0)

    return pl.pallas_call(
        _band_bias_kernel,
        grid=(N_HEADS,),
        in_specs=[pl.BlockSpec(memory_space=pltpu.SMEM), _full(idx_np.shape), _full(ok_np.shape)],
        out_specs=pl.BlockSpec((2, 1, WINDOW, 2 * WINDOW), out_map),
        out_shape=jax.ShapeDtypeStruct((2, N_UNITS, UNIT_ROWS, 2 * WINDOW), F32),
        name="band_bias",
    )(table, jnp.asarray(idx_np), jnp.asarray(ok_np))


def _attn_prompt_kernel(sink_ref, x_ref, g_ref, wqkv_ref, wo_ref, bias_ref,
                        xo_ref, k_ref, v_ref, kv_scr, q_scr, o_scr, *, tq):
    i = pl.program_id(1)
    nsub = tq // WINDOW

    @pl.when(i == 0)
    def _():
        kv_scr[0:WINDOW, :] = jnp.zeros((WINDOW, 2 * KV_W), F32)

    x = x_ref[0]
    h = _rms(x, g_ref[...]).astype(BF16)
    qkv = jnp.dot(h, wqkv_ref[...], preferred_element_type=F32)
    q_scr[...] = (qkv[:, :D_MODEL] * (HEAD_DIM ** -0.5)).astype(BF16)
    kv_scr[WINDOW:WINDOW + tq, :] = qkv[:, D_MODEL:]
    k_ref[0] = qkv[tq - WINDOW:, D_MODEL:D_MODEL + KV_W]
    v_ref[0] = qkv[tq - WINDOW:, D_MODEL + KV_W:]

    lo = lax.broadcasted_iota(jnp.int32, (2 * WINDOW, KV_W), 1) < HEAD_DIM
    row_head = lax.broadcasted_iota(jnp.int32, (UNIT_ROWS, 1), 0) // WINDOW
    sinks = []
    for u in range(N_UNITS):
        g, p = divmod(u, 2)
        col = jnp.zeros((UNIT_ROWS, 1), F32)
        for jj in range(HEADS_PER_UNIT):
            col = jnp.where(row_head == jj, sink_ref[GQ * g + 2 * jj + p], col)
        sinks.append(col)

    for s in range(nsub):
        band = kv_scr[WINDOW * s:WINDOW * s + 2 * WINDOW, :]
        kband, vband = band[:, :KV_W], band[:, KV_W:]
        kroll = pltpu.roll(kband, HEAD_DIM, 1)
        vroll = pltpu.roll(vband, HEAD_DIM, 1)
        kpad = [[jnp.where(lo, kband, 0.0).astype(BF16), jnp.where(lo, 0.0, kroll).astype(BF16)],
                [jnp.where(lo, kroll, 0.0).astype(BF16), jnp.where(lo, 0.0, kband).astype(BF16)]]
        vpad = [[jnp.where(lo, vband, 0.0).astype(BF16), jnp.where(lo, 0.0, vroll).astype(BF16)],
                [jnp.where(lo, vroll, 0.0).astype(BF16), jnp.where(lo, 0.0, vband).astype(BF16)]]
        variant = jnp.where(i * nsub + s > 0, 0, 1)
        for g in range(N_KV_HEADS):
            slabs = range(HEADS_PER_UNIT * g, HEADS_PER_UNIT * (g + 1))
            qs = jnp.concatenate([q_scr[WINDOW * s:WINDOW * (s + 1), LANES * j:LANES * (j + 1)] for j in slabs],
                                 axis=0)
            acc = None
            for p in range(2):
                u = 2 * g + p
                sc = lax.dot_general(qs, kpad[g][p], (((1,), (1,)), ((), ())), preferred_element_type=F32)
                sc = sc + bias_ref[variant, u]
                m = jnp.maximum(jnp.max(sc, axis=-1, keepdims=True), sinks[u])
                e = jnp.exp(sc - m)
                den = jnp.sum(e, axis=-1, keepdims=True) + jnp.exp(sinks[u] - m)
                pv = jnp.dot(e.astype(BF16), vpad[g][p], preferred_element_type=F32)
                term = pv * (1.0 / den)
                acc = term if acc is None else acc + term
            for jj, j in enumerate(slabs):
                o_scr[WINDOW * s:WINDOW * (s + 1), LANES * j:LANES * (j + 1)] = (
                    acc[WINDOW * jj:WINDOW * (jj + 1)].astype(BF16))

    kv_scr[0:WINDOW, :] = kv_scr[tq:tq + WINDOW, :]
    xo_ref[0] = x + jnp.dot(o_scr[...], wo_ref[...], preferred_element_type=F32)


def _attn_prompt_layer(x, g, wqkv, wo, bias, sinks):
    b, t, d = x.shape
    tq = min(512, t)
    kern = functools.partial(_attn_prompt_kernel, tq=tq)
    return pl.pallas_call(
        kern,
        grid=(b, t // tq),
        in_specs=[
            pl.BlockSpec(memory_space=pltpu.SMEM),
            pl.BlockSpec((1, tq, d), lambda bi, i: (bi, i, 0)),
            _full((1, d)),
            _full((d, QKV_W)),
            _full((d, d)),
            _full((2, N_UNITS, UNIT_ROWS, 2 * WINDOW)),
        ],
        out_specs=[
            pl.BlockSpec((1, tq, d), lambda bi, i: (bi, i, 0)),
            pl.BlockSpec((1, WINDOW, KV_W), lambda bi, i: (bi, 0, 0)),
            pl.BlockSpec((1, WINDOW, KV_W), lambda bi, i: (bi, 0, 0)),
        ],
        out_shape=[
            jax.ShapeDtypeStruct((b, t, d), F32),
            jax.ShapeDtypeStruct((b, WINDOW, KV_W), F32),
            jax.ShapeDtypeStruct((b, WINDOW, KV_W), F32),
        ],
        scratch_shapes=[
            pltpu.VMEM((WINDOW + tq, 2 * KV_W), F32),
            pltpu.VMEM((tq, d), BF16),
            pltpu.VMEM((tq, d), BF16),
        ],
        compiler_params=_cparams(("arbitrary", "arbitrary")),
        name="attn_prompt",
    )(sinks, x, g, wqkv, wo, bias)


def _norm_linear_kernel(x_ref, g_ref, w_ref, o_ref):
    h = _rms(x_ref[...], g_ref[...]).astype(BF16)
    o_ref[...] = jnp.dot(h, w_ref[...], preferred_element_type=F32)


def _norm_linear(x, g, w):
    n, d = x.shape
    return pl.pallas_call(
        _norm_linear_kernel,
        out_shape=jax.ShapeDtypeStruct((n, w.shape[1]), F32),
        compiler_params=_cparams(None),
        name="norm_linear",
    )(x, g, w)


def _linear_residual_kernel(x_ref, o_ref, w_ref, xo_ref):
    xo_ref[...] = x_ref[...] + jnp.dot(o_ref[...].astype(BF16), w_ref[...], preferred_element_type=F32)


def _linear_residual(x, o, w):
    return pl.pallas_call(
        _linear_residual_kernel,
        out_shape=jax.ShapeDtypeStruct(x.shape, F32),
        compiler_params=_cparams(None),
        name="linear_residual",
    )(x, o, w)


def _attn_sample_kernel(sink_ref, q_ref, ck_ref, cv_ref, kn_ref, vn_ref, bias_ref, o_ref, kw_ref, vw_ref):
    kw = jnp.concatenate([ck_ref[:, 1:, :], kn_ref[...]], axis=1)
    vw = jnp.concatenate([cv_ref[:, 1:, :], vn_ref[...]], axis=1)
    kw_ref[...] = kw
    vw_ref[...] = vw
    q = (q_ref[...] * (HEAD_DIM ** -0.5)).astype(BF16)
    sc = jnp.einsum('bhc,bjc->bhj', q, kw.astype(BF16), preferred_element_type=F32)
    sc = sc + bias_ref[...][None]
    sink = sink_ref[...][None]
    m = jnp.maximum(jnp.max(sc, axis=-1, keepdims=True), sink)
    e = jnp.exp(sc - m)
    den = jnp.sum(e, axis=-1, keepdims=True) + jnp.exp(sink - m)
    pr = (e / den).astype(BF16)
    o_ref[...] = jnp.einsum('bhj,bjc->bhc', pr, vw.astype(BF16), preferred_element_type=F32)


def _attn_sample(q3, ck, cv, kn, vn, bias_s, sinks_col):
    n = q3.shape[0]
    bb = 8
    blk = lambda s: pl.BlockSpec((bb,) + s, lambda i: (i, 0, 0))
    return pl.pallas_call(
        _attn_sample_kernel,
        grid=(n // bb,),
        in_specs=[_full((N_HEADS, 1)), blk((N_HEADS, KV_W)), blk((WINDOW, KV_W)), blk((WINDOW, KV_W)),
                  blk((1, KV_W)), blk((1, KV_W)), _full((N_HEADS, WINDOW))],
        out_specs=[blk((N_HEADS, KV_W)), blk((WINDOW, KV_W)), blk((WINDOW, KV_W))],
        out_shape=[jax.ShapeDtypeStruct((n, N_HEADS, KV_W), F32),
                   jax.ShapeDtypeStruct((n, WINDOW, KV_W), F32),
                   jax.ShapeDtypeStruct((n, WINDOW, KV_W), F32)],
        compiler_params=_cparams(("arbitrary",)),
        name="attn_sample",
    )(sinks_col, q3, ck, cv, kn, vn, bias_s)


def _attn_sample_layer(x, g, wqkv, wo, bias_s, sinks, cache_k, cache_v):
    n = x.shape[0]
    qkv = _norm_linear(x, g, wqkv)
    q = qkv[:, :D_MODEL].reshape(n, N_KV_HEADS, GQ, 1, HEAD_DIM)
    place = jnp.eye(N_KV_HEADS, dtype=F32).reshape(1, N_KV_HEADS, 1, N_KV_HEADS, 1)
    q3 = (q * place).reshape(n, N_HEADS, KV_W)
    kn = qkv[:, D_MODEL:D_MODEL + KV_W].reshape(n, 1, KV_W)
    vn = qkv[:, D_MODEL + KV_W:].reshape(n, 1, KV_W)
    o3, kw, vw = _attn_sample(q3, cache_k.reshape(n, WINDOW, KV_W), cache_v.reshape(n, WINDOW, KV_W),
                              kn, vn, bias_s, sinks.reshape(N_HEADS, 1))
    o5 = o3.reshape(n, N_KV_HEADS, GQ, N_KV_HEADS, HEAD_DIM)
    o = jnp.stack([o5[:, gi, :, gi, :] for gi in range(N_KV_HEADS)], axis=1).reshape(n, D_MODEL)
    return _linear_residual(x, o, wo), kw, vw


def _ssm_prep_kernel(ar_ref, ai_ref, ldt_ref, br_ref, bi_ref, c_ref, a_out, bc_out, wb_out, wc_out):
    ar, ai = ar_ref[...], ai_ref[...]
    dt = jnp.exp(ldt_ref[...])
    mag = jnp.exp(ar * dt)
    lr = mag * jnp.cos(ai * dt)
    li = mag * jnp.sin(ai * dt)
    den = ar * ar + ai * ai
    cr = ((lr - 1.0) * ar + li * ai) / den
    ci = (li * ar - (lr - 1.0) * ai) / den
    row = lax.broadcasted_iota(jnp.int32, (SUBLANES, N_STATES), 0)
    a_out[...] = jnp.broadcast_to(lr, (SUBLANES, N_STATES))
    bc_out[...] = jnp.where(row < SUBLANES // 2, -li, li)
    for j in range(N_SLABS):
        crj = cr[:, SLAB_STATES * j:SLAB_STATES * (j + 1)]
        cij = ci[:, SLAB_STATES * j:SLAB_STATES * (j + 1)]
        br, bi = br_ref[j], bi_ref[j]
        wb_out[j, :, :SLAB_STATES] = (crj * br - cij * bi).astype(BF16)
        wb_out[j, :, SLAB_STATES:] = (crj * bi + cij * br).astype(BF16)
        wc_out[j, :SLAB_STATES, :] = c_ref[j, :SLAB_STATES, :].astype(BF16)
        wc_out[j, SLAB_STATES:, :] = (-c_ref[j, SLAB_STATES:, :]).astype(BF16)


def _ssm_prep(a_re, a_im, log_dt, b_re, b_im, c_re, c_im):
    eye = jnp.eye(SUBLANES, dtype=F32)

    def blk_b(b):
        t = b.reshape(N_SLABS, 8, SSM_STATE, SSM_GROUP).transpose(0, 1, 3, 2)
        return jnp.einsum('jghp,gk->jghkp', t, eye).reshape(N_SLABS, LANES, SLAB_STATES)

    def blk_c(c):
        t = c.reshape(N_SLABS, 8, SSM_GROUP, SSM_STATE).transpose(0, 1, 3, 2)
        return jnp.einsum('jgph,gk->jgpkh', t, eye).reshape(N_SLABS, SLAB_STATES, LANES)

    row = lambda a: a.reshape(1, N_STATES)
    ldt = jnp.broadcast_to(log_dt[:, None], (N_SSM_GROUPS, SSM_STATE))
    c_all = jnp.concatenate([blk_c(c_re), blk_c(c_im)], axis=1)
    return pl.pallas_call(
        _ssm_prep_kernel,
        out_shape=[jax.ShapeDtypeStruct((SUBLANES, N_STATES), F32),
                   jax.ShapeDtypeStruct((SUBLANES, N_STATES), F32),
                   jax.ShapeDtypeStruct((N_SLABS, LANES, 2 * SLAB_STATES), BF16),
                   jax.ShapeDtypeStruct((N_SLABS, 2 * SLAB_STATES, LANES), BF16)],
        compiler_params=_cparams(None),
        name="ssm_prep",
    )(row(a_re), row(a_im), row(ldt), blk_b(b_re), blk_b(b_im), c_all)


def _glu_residual(x, y, u, d, wa, wg):
    z = _gelu_tanh(y + d * u).astype(BF16)
    out = jnp.dot(z, wa, preferred_element_type=F32) * _sigmoid(jnp.dot(z, wg, preferred_element_type=F32))
    return x + out


def _ssm_prompt_kernel(x_ref, g_ref, wb_ref, wc_ref, a_ref, bc_ref, d_ref, wa_ref, wg_ref,
                       xo_ref, st_ref, z_scr, u_scr, xs_scr, *, nb, lc):
    i = pl.program_id(0)
    blocks_per_slab = SLAB_STATES // LANES
    half = SUBLANES // 2

    @pl.when(i == 0)
    def _():
        xs_scr[...] = jnp.zeros((SUBLANES, N_STATES), F32)

    for b in range(nb):
        u_scr[b * lc:(b + 1) * lc, :] = _rms(x_ref[b], g_ref[...])
    ub = u_scr[...].astype(BF16)
    for j in range(N_SLABS):
        res = jnp.dot(ub[:, LANES * j:LANES * (j + 1)], wb_ref[j], preferred_element_type=F32)
        for b in range(nb):
            rb = res[b * lc:(b + 1) * lc]
            for c in range(blocks_per_slab):
                blk = blocks_per_slab * j + c
                z_scr[blk, pl.ds(b, lc, stride=SUBLANES), :] = rb[:, LANES * c:LANES * (c + 1)]
                z_scr[blk, pl.ds(half + b, lc, stride=SUBLANES), :] = (
                    rb[:, SLAB_STATES + LANES * c:SLAB_STATES + LANES * (c + 1)])
    if nb < half:
        for blk in range(N_STATES // LANES):
            for r in list(range(nb, half)) + list(range(half + nb, SUBLANES)):
                z_scr[blk, pl.ds(r, lc, stride=SUBLANES), :] = jnp.zeros((lc, LANES), F32)

    group = 8
    for cb in range(N_STATES // LANES // group):
        blks = [group * cb + k for k in range(group)]
        a = [a_ref[:, LANES * q:LANES * (q + 1)] for q in blks]
        bc = [bc_ref[:, LANES * q:LANES * (q + 1)] for q in blks]
        x0 = tuple(xs_scr[:, LANES * q:LANES * (q + 1)] for q in blks)

        def step(t, xs, blks=blks, a=a, bc=bc):
            r = pl.multiple_of(t * SUBLANES, SUBLANES)
            out = []
            for k, q in enumerate(blks):
                xn = a[k] * xs[k] + bc[k] * pltpu.roll(xs[k], half, 0) + z_scr[q, pl.ds(r, SUBLANES), :]
                z_scr[q, pl.ds(r, SUBLANES), :] = xn
                out.append(xn)
            return tuple(out)

        xf = lax.fori_loop(0, lc, step, x0)
        for k, q in enumerate(blks):
            xs_scr[:, LANES * q:LANES * (q + 1)] = xf[k]
    st_ref[...] = xs_scr[...]

    for b in range(nb):
        ys = []
        for j in range(N_SLABS):
            parts = [z_scr[blocks_per_slab * j + c, pl.ds(b, lc, stride=SUBLANES), :] for c in range(blocks_per_slab)]
            parts += [z_scr[blocks_per_slab * j + c, pl.ds(half + b, lc, stride=SUBLANES), :]
                      for c in range(blocks_per_slab)]
            s = jnp.concatenate(parts, axis=1).astype(BF16)
            ys.append(jnp.dot(s, wc_ref[j], preferred_element_type=F32))
        y = jnp.concatenate(ys, axis=1)
        xo_ref[b] = _glu_residual(x_ref[b], y, u_scr[b * lc:(b + 1) * lc, :], d_ref[...], wa_ref[...], wg_ref[...])


def _ssm_prompt_layer(x, g, prep, d, wa, wg):
    a_rows, bc_rows, wb, wc = prep
    nb, t, dm = x.shape
    assert nb <= SUBLANES // 2
    lc = min(128, t)
    kern = functools.partial(_ssm_prompt_kernel, nb=nb, lc=lc)
    return pl.pallas_call(
        kern,
        grid=(t // lc,),
        in_specs=[
            pl.BlockSpec((nb, lc, dm), lambda i: (0, i, 0)),
            _full((1, dm)),
            _full(wb.shape), _full(wc.shape),
            _full((SUBLANES, N_STATES)), _full((SUBLANES, N_STATES)),
            _full((1, dm)), _full((dm, dm)), _full((dm, dm)),
        ],
        out_specs=[pl.BlockSpec((nb, lc, dm), lambda i: (0, i, 0)), _full((SUBLANES, N_STATES))],
        out_shape=[jax.ShapeDtypeStruct((nb, t, dm), F32), jax.ShapeDtypeStruct((SUBLANES, N_STATES), F32)],
        scratch_shapes=[
            pltpu.VMEM((N_STATES // LANES, SUBLANES * lc, LANES), F32),
            pltpu.VMEM((nb * lc, dm), F32),
            pltpu.VMEM((SUBLANES, N_STATES), F32),
        ],
        compiler_params=_cparams(("arbitrary",)),
        name="ssm_prompt",
    )(x, g, wb, wc, a_rows, bc_rows, d, wa, wg)


def _ssm_sample_kernel(x_ref, g_ref, wb_ref, wc_ref, a_ref, bc_ref, h0r_ref, h0i_ref, d_ref, wa_ref, wg_ref,
                       xo_ref, sr_ref, si_ref):
    x = x_ref[...]
    u = _rms(x, g_ref[...])
    ub = u.astype(BF16)
    ys = []
    for j in range(N_SLABS):
        sl = slice(SLAB_STATES * j, SLAB_STATES * (j + 1))
        res = jnp.dot(ub[:, LANES * j:LANES * (j + 1)], wb_ref[j], preferred_element_type=F32)
        lr = a_ref[0:1, sl]
        li = bc_ref[SUBLANES - 1:SUBLANES, sl]
        h0r, h0i = h0r_ref[:, sl], h0i_ref[:, sl]
        xr = res[:, :SLAB_STATES] + (lr * h0r - li * h0i)
        xi = res[:, SLAB_STATES:] + (lr * h0i + li * h0r)
        sr_ref[:, sl] = xr
        si_ref[:, sl] = xi
        s = jnp.concatenate([xr, xi], axis=1).astype(BF16)
        ys.append(jnp.dot(s, wc_ref[j], preferred_element_type=F32))
    y = jnp.concatenate(ys, axis=1)
    xo_ref[...] = _glu_residual(x, y, u, d_ref[...], wa_ref[...], wg_ref[...])


def _ssm_sample_layer(x, g, prep, h0r, h0i, d, wa, wg):
    a_rows, bc_rows, wb, wc = prep
    n = x.shape[0]
    return pl.pallas_call(
        _ssm_sample_kernel,
        out_shape=[jax.ShapeDtypeStruct(x.shape, F32),
                   jax.ShapeDtypeStruct((n, N_STATES), F32),
                   jax.ShapeDtypeStruct((n, N_STATES), F32)],
        compiler_params=_cparams(None),
        name="ssm_sample",
    )(x, g, wb, wc, a_rows, bc_rows, h0r, h0i, d, wa, wg)


ROUTER_W = LANES
EXPERT_LANE0 = N_EXPERT_GROUPS


def _route_topk(h, wr, br):
    h_hi = h.astype(BF16)
    h_lo = (h - h_hi.astype(F32)).astype(BF16)
    w_hi = wr.astype(BF16)
    w_lo = (wr - w_hi.astype(F32)).astype(BF16)
    dot = lambda a, b: jnp.dot(a, b, preferred_element_type=F32)
    logits = dot(h_hi, w_hi) + (dot(h_lo, w_hi) + dot(h_hi, w_lo)) + br
    lane = lax.broadcasted_iota(jnp.int32, logits.shape, 1)
    neg = -jnp.inf
    gl = jnp.where(lane < N_EXPERT_GROUPS, logits, neg)
    gmax = jnp.max(gl, axis=-1, keepdims=True)
    g_sel = jnp.min(jnp.where(gl == gmax, lane, ROUTER_W), axis=-1, keepdims=True)
    p_grp = 1.0 / jnp.sum(jnp.exp(gl - gmax), axis=-1, keepdims=True)
    lane_grp = (lane - EXPERT_LANE0) // EXPERTS_PER_GROUP
    in_grp = (lane >= EXPERT_LANE0) & (lane < EXPERT_LANE0 + N_EXPERTS) & (lane_grp == g_sel)
    el = jnp.where(in_grp, logits, neg)
    v1 = jnp.max(el, axis=-1, keepdims=True)
    i1 = jnp.min(jnp.where(el == v1, lane, ROUTER_W), axis=-1, keepdims=True)
    el2 = jnp.where(lane == i1, neg, el)
    v2 = jnp.max(el2, axis=-1, keepdims=True)
    i2 = jnp.min(jnp.where(el2 == v2, lane, ROUTER_W), axis=-1, keepdims=True)
    e2 = jnp.exp(v2 - v1)
    w1 = p_grp / (1.0 + e2)
    w2 = p_grp * e2 / (1.0 + e2)
    return lane, i1, i2, w1, w2


def _route(h, wr, br):
    lane, i1, i2, w1, w2 = _route_topk(h, wr, br)
    return jnp.where(lane == i1, w1, 0.0) + jnp.where(lane == i2, w2, 0.0)


def _moe_kernel(x_ref, p_ref, gffn_ref, wr_ref, br_ref, wg_ref, wu_ref, wd_ref,
                gple_ref, wpg_ref, wpp_ref, gfin_ref, o_ref, hn_scr, gates_scr, acc_scr, *, final):
    e = pl.program_id(1)

    @pl.when(e == 0)
    def _():
        x = x_ref[...]
        h = _rms(x, gffn_ref[...])
        hn_scr[...] = h.astype(BF16)
        gates_scr[...] = _route(h, wr_ref[...], br_ref[...])
        acc_scr[...] = x

    hb = hn_scr[...]
    gt = jnp.dot(hb, wg_ref[0].astype(BF16), preferred_element_type=F32)
    up = jnp.dot(hb, wu_ref[0].astype(BF16), preferred_element_type=F32)
    a = (gt * _sigmoid(gt) * up).astype(BF16)
    gates = gates_scr[...]
    lane = lax.broadcasted_iota(jnp.int32, gates.shape, 1)
    ge = jnp.sum(jnp.where(lane == e + EXPERT_LANE0, gates, 0.0), axis=-1, keepdims=True)
    acc_scr[...] += ge * jnp.dot(a, wd_ref[0].astype(BF16), preferred_element_type=F32)

    @pl.when(e == pl.num_programs(1) - 1)
    def _():
        x1 = acc_scr[...]
        gate = _sigmoid(jnp.dot(_rms(x1, gple_ref[...]).astype(BF16), wpg_ref[...], preferred_element_type=F32))
        x2 = x1 + gate * jnp.dot(p_ref[...].astype(BF16), wpp_ref[...], preferred_element_type=F32)
        o_ref[...] = _rms(x2, gfin_ref[...]) if final else x2


def _moe_ple_layer(li, x, p, gffn, wr, br, wg, wu, wd, gple, wpg, wpp, gfin, final):
    n, d = x.shape
    tm = min(1024, n)
    kern = functools.partial(_moe_kernel, final=final)
    tok = lambda w: pl.BlockSpec((tm, w), lambda t, e: (t, 0))
    const = lambda s: pl.BlockSpec(s, lambda t, e: (0,) * len(s))
    return pl.pallas_call(
        kern,
        grid=(n // tm, N_EXPERTS),
        in_specs=[
            tok(d), pl.BlockSpec((None, tm, PLE_DIM), lambda t, e: (li, t, 0)),
            const((1, d)), const((d, ROUTER_W)), const((1, ROUTER_W)),
            pl.BlockSpec((None, 1, d, D_EXPERT), lambda t, e: (li, e, 0, 0)),
            pl.BlockSpec((None, 1, d, D_EXPERT), lambda t, e: (li, e, 0, 0)),
            pl.BlockSpec((None, 1, D_EXPERT, d), lambda t, e: (li, e, 0, 0)),
            const((1, d)), const((d, d)), const((PLE_DIM, d)), const((1, d)),
        ],
        out_specs=tok(d),
        out_shape=jax.ShapeDtypeStruct((n, d), F32),
        scratch_shapes=[pltpu.VMEM((tm, d), BF16), pltpu.VMEM((tm, ROUTER_W), F32), pltpu.VMEM((tm, d), F32)],
        compiler_params=_cparams(("arbitrary", "arbitrary")),
        name="moe_ple",
    )(x, p, gffn, wr, br, wg, wu, wd, gple, wpg, wpp, gfin)


TOKEN_TILE_ROWS = D_MODEL // LANES
INFO_W1, INFO_W2, INFO_E1, INFO_E2, INFO_R1, INFO_R2 = range(6)
EXPERT_TILE = 256


def _to_token_tiles(ref_2d, val, n):
    r = val.shape[1] // LANES
    for s in range(r):
        ref_2d[pl.ds(s, n, stride=r), :] = val[:, LANES * s:LANES * (s + 1)]


def _from_token_tiles(ref_2d, n, r):
    return jnp.concatenate([ref_2d[pl.ds(s, n, stride=r), :] for s in range(r)], axis=1)


def _tile_rows(i, r=TOKEN_TILE_ROWS):
    return pl.ds(pl.multiple_of(i * r, r), r)


def _route_kernel(x_ref, g_ref, wr_ref, br_ref, before_ref, info_ref, cnt_ref, base_scr):
    i = pl.program_id(0)

    @pl.when(i == 0)
    def _():
        base_scr[...] = jnp.zeros(base_scr.shape, F32)

    h = _rms(x_ref[...], g_ref[...])
    lane, i1, i2, w1, w2 = _route_topk(h, wr_ref[...], br_ref[...])
    chosen = jnp.where((lane == i1) | (lane == i2), 1.0, 0.0)
    prefix = jnp.dot(before_ref[...], chosen.astype(BF16), preferred_element_type=F32) + base_scr[...]
    rank1 = jnp.sum(jnp.where(lane == i1, prefix, 0.0), axis=-1, keepdims=True)
    rank2 = jnp.sum(jnp.where(lane == i2, prefix, 0.0), axis=-1, keepdims=True)
    base_scr[...] += jnp.sum(chosen, axis=0, keepdims=True)
    cnt_ref[...] = base_scr[...]
    e1 = (i1 - EXPERT_LANE0).astype(F32)
    e2 = (i2 - EXPERT_LANE0).astype(F32)
    info = jnp.zeros(h.shape[:1] + (ROUTER_W,), F32)
    for ln, v in ((INFO_W1, w1), (INFO_W2, w2), (INFO_E1, e1), (INFO_E2, e2), (INFO_R1, rank1), (INFO_R2, rank2)):
        info = jnp.where(lane == ln, v, info)
    info_ref[...] = info


def _route_call(x, gffn, wr, br):
    n, d = x.shape
    tm = 512
    before = jnp.asarray(np.tril(np.ones((tm, tm), np.float32), -1), BF16)
    return pl.pallas_call(
        _route_kernel,
        grid=(n // tm,),
        in_specs=[pl.BlockSpec((tm, d), lambda i: (i, 0)), _full((1, d)), _full((d, ROUTER_W)), _full((1, ROUTER_W)),
                  _full((tm, tm))],
        out_specs=[pl.BlockSpec((tm, ROUTER_W), lambda i: (i, 0)), _full((1, ROUTER_W))],
        out_shape=[jax.ShapeDtypeStruct((n, ROUTER_W), F32), jax.ShapeDtypeStruct((1, ROUTER_W), F32)],
        scratch_shapes=[pltpu.VMEM((1, ROUTER_W), F32)],
        compiler_params=_cparams(("arbitrary",)),
        name="moe_route",
    )(x, gffn, wr, br, before)


def _dispatch_kernel(pos_ref, cnt_ref, start_ref, x_ref, g_ref, xs_hbm, dest_ref, hbuf, zbuf, sem, zsem,
                     *, td, n_steps, n_tokens, n_pad_rows):
    i = pl.program_id(0)
    slot = i % 2

    def wait_slot(sl):
        for _ in range(2):
            pltpu.make_async_copy(hbuf.at[sl], hbuf.at[sl], sem.at[sl]).wait()

    @pl.when(i >= 2)
    def _():
        wait_slot(slot)

    _to_token_tiles(hbuf.at[slot], _rms(x_ref[...], g_ref[...]), td)
    rows = _tile_rows

    def issue(n8, carry):
        for u in range(SUBLANES):
            n = n8 * SUBLANES + u
            t = i * td + n
            src = hbuf.at[slot, rows(n), :]
            p1, p2 = pos_ref[0, t], pos_ref[1, t]
            pltpu.make_async_copy(src, xs_hbm.at[rows(p1), :], sem.at[slot]).start(priority=0)
            pltpu.make_async_copy(src, xs_hbm.at[rows(p2), :], sem.at[slot]).start(priority=1)
            dest_ref[p1] = t
            dest_ref[p2] = n_tokens + t
        return carry

    lax.fori_loop(0, td // SUBLANES, issue, 0)

    @pl.when(i == 0)
    def _():
        zbuf[...] = jnp.zeros(zbuf.shape, zbuf.dtype)
        spare = 2 * n_tokens
        for e in range(N_EXPERTS + 1):
            lo = start_ref[e] + cnt_ref[e]
            hi = start_ref[e + 1]

            def zero_row(r, spare_row):
                pltpu.make_async_copy(zbuf, xs_hbm.at[rows(r), :], zsem).start()
                dest_ref[r] = spare_row
                return spare_row + 1

            spare = lax.fori_loop(lo, hi, zero_row, spare)

    @pl.when(i == n_steps - 1)
    def _():
        wait_slot(slot)
        if n_steps > 1:
            wait_slot(1 - slot)
        pad = xs_hbm.at[pl.ds(0, n_pad_rows * TOKEN_TILE_ROWS), :]
        pltpu.make_async_copy(pad, pad, zsem).wait()


def _dispatch_call(pos, cnt, start, x, gffn, a_pad, n_pad_rows):
    n, d = x.shape
    td = 512
    n_steps = n // td
    kern = functools.partial(_dispatch_kernel, td=td, n_steps=n_steps, n_tokens=n, n_pad_rows=n_pad_rows)
    return pl.pallas_call(
        kern,
        grid_spec=pltpu.PrefetchScalarGridSpec(
            num_scalar_prefetch=3,
            grid=(n_steps,),
            in_specs=[pl.BlockSpec((td, d), lambda i, *_: (i, 0)), pl.BlockSpec((1, d), lambda i, *_: (0, 0))],
            out_specs=[pl.BlockSpec(memory_space=pl.ANY), pl.BlockSpec(memory_space=pltpu.SMEM)],
            scratch_shapes=[pltpu.VMEM((2, td * TOKEN_TILE_ROWS, LANES), F32),
                            pltpu.VMEM((TOKEN_TILE_ROWS, LANES), F32),
                            pltpu.SemaphoreType.DMA((2,)), pltpu.SemaphoreType.DMA(())],
        ),
        out_shape=[jax.ShapeDtypeStruct((a_pad * TOKEN_TILE_ROWS, LANES), F32),
                   jax.ShapeDtypeStruct((a_pad,), jnp.int32)],
        compiler_params=_cparams(("arbitrary",)),
        name="moe_dispatch",
    )(pos, cnt, start, x, gffn)


EXPERT_IN_SLOTS = 3


def _experts_kernel(te_ref, dest_ref, xs_hbm, wg_ref, wu_ref, wd_ref, y_hbm, xbuf, xsem, ybuf, ysem,
                    wg_scr, wu_scr, wd_scr, *, te, n_tiles):
    t = pl.program_id(0)
    rows = te * TOKEN_TILE_ROWS
    yslot = t % 2

    def wait_rows_out(sl):
        pltpu.make_async_copy(ybuf.at[sl], ybuf.at[sl], ysem.at[sl]).wait()

    @pl.when(t >= 2)
    def _():
        wait_rows_out(yslot)

    def fetch(tile):
        slot = tile % EXPERT_IN_SLOTS
        src = xs_hbm.at[pl.ds(pl.multiple_of(tile * rows, rows), rows), :]
        return pltpu.make_async_copy(src, xbuf.at[slot], xsem.at[slot])

    @pl.when(t == 0)
    def _():
        for ahead in range(min(EXPERT_IN_SLOTS - 1, n_tiles)):
            fetch(ahead).start()

    @pl.when(t + EXPERT_IN_SLOTS - 1 < n_tiles)
    def _():
        fetch(t + EXPERT_IN_SLOTS - 1).start()

    @pl.when((t == 0) | (te_ref[t] != te_ref[jnp.maximum(t - 1, 0)]))
    def _():
        wg_scr[...] = wg_ref[0].astype(BF16)
        wu_scr[...] = wu_ref[0].astype(BF16)
        wd_scr[...] = wd_ref[0].astype(BF16)

    fetch(t).wait()
    x = _from_token_tiles(xbuf.at[t % EXPERT_IN_SLOTS], te, TOKEN_TILE_ROWS).astype(BF16)
    gt = jnp.dot(x, wg_scr[...], preferred_element_type=F32)
    up = jnp.dot(x, wu_scr[...], preferred_element_type=F32)
    a = (gt * _sigmoid(gt) * up).astype(BF16)
    _to_token_tiles(ybuf.at[yslot], jnp.dot(a, wd_scr[...], preferred_element_type=F32), te)

    def send(r8, carry):
        for u in range(SUBLANES):
            r = r8 * SUBLANES + u
            dst = y_hbm.at[_tile_rows(dest_ref[t * te + r]), :]
            pltpu.make_async_copy(ybuf.at[yslot, _tile_rows(r), :], dst, ysem.at[yslot]).start(priority=u % 2)
        return carry

    lax.fori_loop(0, te // SUBLANES, send, 0)

    @pl.when(t == n_tiles - 1)
    def _():
        wait_rows_out(yslot)
        if n_tiles > 1:
            wait_rows_out(1 - yslot)


def _experts_call(li, tile_expert, dest, xs, wg, wu, wd, te):
    rows = te * TOKEN_TILE_ROWS
    n_tiles = xs.shape[0] // rows
    d = wg.shape[2]
    kern = functools.partial(_experts_kernel, te=te, n_tiles=n_tiles)
    return pl.pallas_call(
        kern,
        grid_spec=pltpu.PrefetchScalarGridSpec(
            num_scalar_prefetch=2,
            grid=(n_tiles,),
            in_specs=[pl.BlockSpec(memory_space=pl.ANY),
                      pl.BlockSpec((None, 1, d, D_EXPERT), lambda t, ex, ds: (li, ex[t], 0, 0)),
                      pl.BlockSpec((None, 1, d, D_EXPERT), lambda t, ex, ds: (li, ex[t], 0, 0)),
                      pl.BlockSpec((None, 1, D_EXPERT, d), lambda t, ex, ds: (li, ex[t], 0, 0))],
            out_specs=pl.BlockSpec(memory_space=pl.ANY),
            scratch_shapes=[pltpu.VMEM((EXPERT_IN_SLOTS, rows, LANES), F32),
                            pltpu.SemaphoreType.DMA((EXPERT_IN_SLOTS,)),
                            pltpu.VMEM((2, rows, LANES), F32), pltpu.SemaphoreType.DMA((2,)),
                            pltpu.VMEM((d, D_EXPERT), BF16), pltpu.VMEM((d, D_EXPERT), BF16),
                            pltpu.VMEM((D_EXPERT, d), BF16)],
        ),
        out_shape=jax.ShapeDtypeStruct((n_tiles * rows, LANES), F32),
        compiler_params=_cparams(("arbitrary",)),
        name="moe_experts",
    )(tile_expert, dest, xs, wg, wu, wd)


def _combine_kernel(x_ref, info_ref, p_ref, gple_ref, wpg_ref, wpp_ref, gfin_ref, y1_ref, y2_ref, o_ref,
                    *, tm, final):
    y1 = _from_token_tiles(y1_ref, tm, TOKEN_TILE_ROWS)
    y2 = _from_token_tiles(y2_ref, tm, TOKEN_TILE_ROWS)
    info = info_ref[...]
    x1 = x_ref[...] + (info[:, INFO_W1:INFO_W1 + 1] * y1 + info[:, INFO_W2:INFO_W2 + 1] * y2)
    gate = _sigmoid(jnp.dot(_rms(x1, gple_ref[...]).astype(BF16), wpg_ref[...], preferred_element_type=F32))
    x2 = x1 + gate * jnp.dot(p_ref[...].astype(BF16), wpp_ref[...], preferred_element_type=F32)
    o_ref[...] = _rms(x2, gfin_ref[...]) if final else x2


def _combine_call(li, x, info, p, gple, wpg, wpp, gfin, ys, final):
    n, d = x.shape
    tm = 512
    n_steps = n // tm
    kern = functools.partial(_combine_kernel, tm=tm, final=final)
    tok = lambda w: pl.BlockSpec((tm, w), lambda i: (i, 0))
    const = lambda s: pl.BlockSpec(s, lambda i: (0,) * len(s))
    yrows = tm * TOKEN_TILE_ROWS
    return pl.pallas_call(
        kern,
        grid=(n_steps,),
        in_specs=[tok(d), tok(ROUTER_W), pl.BlockSpec((None, tm, PLE_DIM), lambda i: (li, i, 0)),
                  const((1, d)), const((d, d)), const((PLE_DIM, d)), const((1, d)),
                  pl.BlockSpec((yrows, LANES), lambda i: (i, 0)),
                  pl.BlockSpec((yrows, LANES), lambda i: (n_steps + i, 0))],
        out_specs=tok(d),
        out_shape=jax.ShapeDtypeStruct((n, d), F32),
        compiler_params=_cparams(("arbitrary",)),
        name="moe_combine",
    )(x, info, p, gple, wpg, wpp, gfin, ys, ys)


def _positions_kernel(info_ref, start_ref, pos_ref):
    info = info_ref[...]
    lane = lax.broadcasted_iota(jnp.int32, info.shape, 1)
    lane_expert = (lane - EXPERT_LANE0).astype(F32)
    start = start_ref[...]
    cols = []
    for e_lane, r_lane in ((INFO_E1, INFO_R1), (INFO_E2, INFO_R2)):
        seg = jnp.sum(jnp.where(lane_expert == info[:, e_lane:e_lane + 1], start, 0.0), axis=-1, keepdims=True)
        cols.append(seg + info[:, r_lane:r_lane + 1])
    both = jnp.where(lane == 0, cols[0], jnp.where(lane == 1, cols[1], 0.0))
    pos_ref[...] = both.T[:SUBLANES, :].astype(jnp.int32)


def _positions_call(info, start_row):
    n = info.shape[0]
    tm = 2048 if n % 2048 == 0 else 512
    return pl.pallas_call(
        _positions_kernel,
        grid=(n // tm,),
        in_specs=[pl.BlockSpec((tm, ROUTER_W), lambda i: (i, 0)), _full((1, ROUTER_W))],
        out_specs=pl.BlockSpec((SUBLANES, tm), lambda i: (0, i)),
        out_shape=jax.ShapeDtypeStruct((SUBLANES, n), jnp.int32),
        compiler_params=_cparams(("arbitrary",)),
        name="moe_positions",
    )(info, start_row)


def _moe_routed_layer(li, x, p, gffn, wr, br, wg, wu, wd, gple, wpg, wpp, gfin, final):
    n = x.shape[0]
    te = EXPERT_TILE
    n_assign = 2 * n
    assert n_assign % te == 0
    a_pad = n_assign + N_EXPERTS * te
    info, cnt = _route_call(x, gffn, wr, br)
    counts = cnt[0, EXPERT_LANE0:EXPERT_LANE0 + N_EXPERTS].astype(jnp.int32)
    padded = (counts + te - 1) // te * te
    seg_end = jnp.cumsum(padded)
    seg_start = seg_end - padded
    start_row = jnp.zeros((1, ROUTER_W), F32).at[0, EXPERT_LANE0:EXPERT_LANE0 + N_EXPERTS].set(seg_start.astype(F32))
    pos = _positions_call(info, start_row)[:2]
    start = jnp.concatenate([seg_start, seg_end[-1:], jnp.full((1,), a_pad, jnp.int32)])
    cnt_pad = jnp.concatenate([counts, jnp.zeros((1,), jnp.int32)])
    tile_row0 = jnp.arange(a_pad // te, dtype=jnp.int32) * te
    tile_expert = jnp.minimum(jnp.sum(seg_end[None, :] <= tile_row0[:, None], axis=1), N_EXPERTS - 1).astype(jnp.int32)
    xs, dest = _dispatch_call(pos, cnt_pad, start, x, gffn, a_pad, N_EXPERTS * te)
    ys = _experts_call(li, tile_expert, dest, xs, wg, wu, wd, te)
    return _combine_call(li, x, info, p, gple, wpg, wpp, gfin, ys, final)


def kernel(x_prompt, x_sample, cache_k_win, cache_v_win, state_ssm_re, state_ssm_im, p_prompt, p_sample,
           rel_bias_table, g_mix, w_qkv, w_o, sinks, ssm_a_re, ssm_a_im, ssm_log_dt, ssm_b_re, ssm_b_im,
           ssm_c_re, ssm_c_im, ssm_d, w_glu_a, w_glu_b, g_ffn, w_router_group, b_router_group,
           w_router_expert, b_router_expert, w_exp_gate, w_exp_up, w_exp_down, g_ple, w_ple_gate,
           w_ple_proj, g_final):
    bsz, seq, d = x_prompt.shape
    ns = x_sample.shape[0]
    depth = g_mix.shape[0]
    row = lambda v: v.reshape(1, -1).astype(F32)

    qi = np.arange(WINDOW)[:, None]
    sj = np.arange(2 * WINDOW)[None, :]
    dist = qi + WINDOW - sj
    idx_p = _t5_bucket_np(dist)
    ok_p = ((dist >= 0) & (dist < WINDOW)).astype(np.int32)
    idx_s = np.broadcast_to(_t5_bucket_np(WINDOW - 1 - np.arange(WINDOW))[None, :], (SUBLANES, WINDOW))
    bias_p = _band_bias(rel_bias_table, idx_p, ok_p)
    bias_s = _bias_from_table(rel_bias_table, idx_s)[:, 0, :]

    wqkv_b = w_qkv[0].astype(BF16)
    wo_b = w_o[0].astype(BF16)
    wpg_b, wpp_b = w_ple_gate.astype(BF16), w_ple_proj.astype(BF16)
    wa_b, wgl_b = w_glu_a[0].astype(BF16), w_glu_b[0].astype(BF16)
    pad = ROUTER_W - N_EXPERT_GROUPS - N_EXPERTS
    wr = jnp.concatenate([w_router_group, w_router_expert.reshape(depth, d, N_EXPERTS),
                          jnp.zeros((depth, d, pad), F32)], axis=-1)
    br = jnp.concatenate([b_router_group, b_router_expert.reshape(depth, N_EXPERTS),
                          jnp.zeros((depth, pad), F32)], axis=-1)

    prep = _ssm_prep(ssm_a_re[0], ssm_a_im[0], ssm_log_dt[0], ssm_b_re[0], ssm_b_im[0], ssm_c_re[0], ssm_c_im[0])
    d_row = row(ssm_d[0])

    def moe(x2d, p_all, i):
        layer = _moe_routed_layer if 2 * x2d.shape[0] >= N_EXPERTS * EXPERT_TILE else _moe_ple_layer
        return layer(i, x2d, p_all, row(g_ffn[i]), wr[i], br[i:i + 1], w_exp_gate, w_exp_up, w_exp_down,
                     row(g_ple[i]), wpg_b[i], wpp_b[i], row(g_final), final=(i == depth - 1))

    pp = p_prompt.reshape(depth, bsz * seq, PLE_DIM)
    xp, kp, vp = _attn_prompt_layer(x_prompt, row(g_mix[0]), wqkv_b, wo_b, bias_p, sinks[0])
    xp = moe(xp.reshape(bsz * seq, d), pp, 0).reshape(bsz, seq, d)
    xp, st_p = _ssm_prompt_layer(xp, row(g_mix[1]), prep, d_row, wa_b, wgl_b)
    y_prompt = moe(xp.reshape(bsz * seq, d), pp, 1).reshape(bsz, seq, d)

    ps = p_sample.reshape(depth, ns, PLE_DIM)
    xs = x_sample.reshape(ns, d)
    xs, kws, vws = _attn_sample_layer(xs, row(g_mix[0]), wqkv_b, wo_b, bias_s, sinks[0],
                                      cache_k_win[0], cache_v_win[0])
    xs = moe(xs, ps, 0)
    xs, sr_s, si_s = _ssm_sample_layer(xs, row(g_mix[1]), prep, state_ssm_re[0].reshape(ns, N_STATES),
                                       state_ssm_im[0].reshape(ns, N_STATES), d_row, wa_b, wgl_b)
    y_sample = moe(xs, ps, 1).reshape(ns, 1, d)

    half = SUBLANES // 2
    win = lambda a, n: a.reshape(1, n, WINDOW, N_KV_HEADS, HEAD_DIM)
    st = lambda a, n: a.reshape(1, n, N_SSM_GROUPS, SSM_STATE)
    return (y_prompt, y_sample, win(kp, bsz), win(vp, bsz), win(kws, ns), win(vws, ns),
            st(st_p[:bsz], bsz), st(st_p[half:half + bsz], bsz), st(sr_s, ns), st(si_s, ns))
```

```python
import functools
import math

import numpy as np
import jax
import jax.numpy as jnp
from jax import lax
from jax.experimental import pallas as pl
from jax.experimental.pallas import tpu as pltpu

F32 = jnp.float32
BF16 = jnp.bfloat16

D_MODEL = 1024
HEAD_DIM = 64
N_HEADS = 16
N_KV_HEADS = 2
GQ = N_HEADS // N_KV_HEADS
WINDOW = 128
N_BUCKETS = 32
MAX_EXACT = 16
MAX_DISTANCE = 128
SSM_GROUP = 16
N_SSM_GROUPS = 64
SSM_STATE = 64
N_STATES = N_SSM_GROUPS * SSM_STATE
N_EXPERT_GROUPS = 4
EXPERTS_PER_GROUP = 8
N_EXPERTS = 32
D_EXPERT = 256
PLE_DIM = 256
RMS_EPS = 1e-6
NEG_INF = -1e30

LANES = 128
SUBLANES = 8
KV_W = N_KV_HEADS * HEAD_DIM
QKV_W = D_MODEL + 2 * KV_W
SLAB_STATES = 512
N_SLABS = D_MODEL // LANES
VMEM_LIMIT = 56 * 1024 * 1024


def _cparams(sem):
    return pltpu.CompilerParams(dimension_semantics=sem, vmem_limit_bytes=VMEM_LIMIT)


def _rms(x, g):
    return x * lax.rsqrt(jnp.mean(x * x, axis=-1, keepdims=True) + RMS_EPS) * g


def _gelu_tanh(x):
    c = math.sqrt(2.0 / math.pi)
    return x * (0.5 * (1.0 + jnp.tanh(c * (x + 0.044715 * (x * x * x)))))


def _sigmoid(x):
    return 1.0 / (1.0 + jnp.exp(-x))


def _full(shape):
    n = len(shape)
    return pl.BlockSpec(shape, lambda *_: (0,) * n)


def _t5_bucket_np(dist):
    n = np.maximum(dist, 0)
    nf = np.maximum(n, 1).astype(np.float64)
    large = MAX_EXACT + (np.log(nf / MAX_EXACT) / math.log(MAX_DISTANCE / MAX_EXACT)
                         * (N_BUCKETS - MAX_EXACT)).astype(np.int32)
    large = np.minimum(large, N_BUCKETS - 1)
    return np.where(n < MAX_EXACT, n, large).astype(np.int32)


def _bias_kernel(table_ref, idx_ref, o_ref):
    h = pl.program_id(0)
    idx = idx_ref[...]
    acc = jnp.zeros(idx.shape, F32)
    for b in range(N_BUCKETS):
        acc = jnp.where(idx == b, table_ref[b, h], acc)
    o_ref[0] = acc


def _bias_from_table(table, idx_np):
    q, k = idx_np.shape
    return pl.pallas_call(
        _bias_kernel,
        grid=(N_HEADS,),
        in_specs=[pl.BlockSpec(memory_space=pltpu.SMEM), _full((q, k))],
        out_specs=pl.BlockSpec((1, q, k), lambda h: (h, 0, 0)),
        out_shape=jax.ShapeDtypeStruct((N_HEADS, q, k), F32),
        name="rel_bias",
    )(table, jnp.asarray(idx_np))


HEADS_PER_UNIT = GQ // 2
N_UNITS = N_HEADS // HEADS_PER_UNIT
UNIT_ROWS = HEADS_PER_UNIT * WINDOW


def _band_bias_kernel(table_ref, idx_ref, ok_ref, o_ref):
    h = pl.program_id(0)
    idx = idx_ref[...]
    acc = jnp.zeros(idx.shape, F32)
    for b in range(N_BUCKETS):
        acc = jnp.where(idx == b, table_ref[b, h], acc)
    base = jnp.where(ok_ref[...] != 0, acc, NEG_INF)
    sj = lax.broadcasted_iota(jnp.int32, idx.shape, 1)
    o_ref[0, 0] = base
    o_ref[1, 0] = jnp.where(sj >= WINDOW, base, NEG_INF)


def _band_bias(table, idx_np, ok_np):
    def out_map(h):
        r = h % GQ
        return (0, (h // GQ) * 2 + r % 2, r // 2, 0)

    return pl.pallas_call(
        _band_bias_kernel,
        grid=(N_HEADS,),
        in_specs=[pl.BlockSpec(memory_space=pltpu.SMEM), _full(idx_np.shape), _full(ok_np.shape)],
        out_specs=pl.BlockSpec((2, 1, WINDOW, 2 * WINDOW), out_map),
        out_shape=jax.ShapeDtypeStruct((2, N_UNITS, UNIT_ROWS, 2 * WINDOW), F32),
        name="band_bias",
    )(table, jnp.asarray(idx_np), jnp.asarray(ok_np))


def _attn_prompt_kernel(sink_ref, x_ref, g_ref, wqkv_ref, wo_ref, bias_ref,
                        xo_ref, k_ref, v_ref, kv_scr, q_scr, o_scr, *, tq):
    i = pl.program_id(1)
    nsub = tq // WINDOW

    @pl.when(i == 0)
    def _():
        kv_scr[0:WINDOW, :] = jnp.zeros((WINDOW, 2 * KV_W), F32)

    x = x_ref[0]
    h = _rms(x, g_ref[...]).astype(BF16)
    qkv = jnp.dot(h, wqkv_ref[...], preferred_element_type=F32)
    q_scr[...] = (qkv[:, :D_MODEL] * (HEAD_DIM ** -0.5)).astype(BF16)
    kv_scr[WINDOW:WINDOW + tq, :] = qkv[:, D_MODEL:]
    k_ref[0] = qkv[tq - WINDOW:, D_MODEL:D_MODEL + KV_W]
    v_ref[0] = qkv[tq - WINDOW:, D_MODEL + KV_W:]

    lo = lax.broadcasted_iota(jnp.int32, (2 * WINDOW, KV_W), 1) < HEAD_DIM
    row_head = lax.broadcasted_iota(jnp.int32, (UNIT_ROWS, 1), 0) // WINDOW
    sinks = []
    for u in range(N_UNITS):
        g, p = divmod(u, 2)
        col = jnp.zeros((UNIT_ROWS, 1), F32)
        for jj in range(HEADS_PER_UNIT):
            col = jnp.where(row_head == jj, sink_ref[GQ * g + 2 * jj + p], col)
        sinks.append(col)

    for s in range(nsub):
        band = kv_scr[WINDOW * s:WINDOW * s + 2 * WINDOW, :]
        kband, vband = band[:, :KV_W], band[:, KV_W:]
        kroll = pltpu.roll(kband, HEAD_DIM, 1)
        vroll = pltpu.roll(vband, HEAD_DIM, 1)
        kpad = [[jnp.where(lo, kband, 0.0).astype(BF16), jnp.where(lo, 0.0, kroll).astype(BF16)],
                [jnp.where(lo, kroll, 0.0).astype(BF16), jnp.where(lo, 0.0, kband).astype(BF16)]]
        vpad = [[jnp.where(lo, vband, 0.0).astype(BF16), jnp.where(lo, 0.0, vroll).astype(BF16)],
                [jnp.where(lo, vroll, 0.0).astype(BF16), jnp.where(lo, 0.0, vband).astype(BF16)]]
        variant = jnp.where(i * nsub + s > 0, 0, 1)
        for g in range(N_KV_HEADS):
            slabs = range(HEADS_PER_UNIT * g, HEADS_PER_UNIT * (g + 1))
            qs = jnp.concatenate([q_scr[WINDOW * s:WINDOW * (s + 1), LANES * j:LANES * (j + 1)] for j in slabs],
                                 axis=0)
            acc = None
            for p in range(2):
                u = 2 * g + p
                sc = lax.dot_general(qs, kpad[g][p], (((1,), (1,)), ((), ())), preferred_element_type=F32)
                sc = sc + bias_ref[variant, u]
                m = jnp.maximum(jnp.max(sc, axis=-1, keepdims=True), sinks[u])
                e = jnp.exp(sc - m)
                den = jnp.sum(e, axis=-1, keepdims=True) + jnp.exp(sinks[u] - m)
                pv = jnp.dot(e.astype(BF16), vpad[g][p], preferred_element_type=F32)
                term = pv * (1.0 / den)
                acc = term if acc is None else acc + term
            for jj, j in enumerate(slabs):
                o_scr[WINDOW * s:WINDOW * (s + 1), LANES * j:LANES * (j + 1)] = (
                    acc[WINDOW * jj:WINDOW * (jj + 1)].astype(BF16))

    kv_scr[0:WINDOW, :] = kv_scr[tq:tq + WINDOW, :]
    xo_ref[0] = x + jnp.dot(o_scr[...], wo_ref[...], preferred_element_type=F32)


def _attn_prompt_layer(x, g, wqkv, wo, bias, sinks):
    b, t, d = x.shape
    tq = min(512, t)
    kern = functools.partial(_attn_prompt_kernel, tq=tq)
    return pl.pallas_call(
        kern,
        grid=(b, t // tq),
        in_specs=[
            pl.BlockSpec(memory_space=pltpu.SMEM),
            pl.BlockSpec((1, tq, d), lambda bi, i: (bi, i, 0)),
            _full((1, d)),
            _full((d, QKV_W)),
            _full((d, d)),
            _full((2, N_UNITS, UNIT_ROWS, 2 * WINDOW)),
        ],
        out_specs=[
            pl.BlockSpec((1, tq, d), lambda bi, i: (bi, i, 0)),
            pl.BlockSpec((1, WINDOW, KV_W), lambda bi, i: (bi, 0, 0)),
            pl.BlockSpec((1, WINDOW, KV_W), lambda bi, i: (bi, 0, 0)),
        ],
        out_shape=[
            jax.ShapeDtypeStruct((b, t, d), F32),
            jax.ShapeDtypeStruct((b, WINDOW, KV_W), F32),
            jax.ShapeDtypeStruct((b, WINDOW, KV_W), F32),
        ],
        scratch_shapes=[
            pltpu.VMEM((WINDOW + tq, 2 * KV_W), F32),
            pltpu.VMEM((tq, d), BF16),
            pltpu.VMEM((tq, d), BF16),
        ],
        compiler_params=_cparams(("arbitrary", "arbitrary")),
        name="attn_prompt",
    )(sinks, x, g, wqkv, wo, bias)


def _norm_linear_kernel(x_ref, g_ref, w_ref, o_ref):
    h = _rms(x_ref[...], g_ref[...]).astype(BF16)
    o_ref[...] = jnp.dot(h, w_ref[...], preferred_element_type=F32)


def _norm_linear(x, g, w):
    n, d = x.shape
    return pl.pallas_call(
        _norm_linear_kernel,
        out_shape=jax.ShapeDtypeStruct((n, w.shape[1]), F32),
        compiler_params=_cparams(None),
        name="norm_linear",
    )(x, g, w)


def _linear_residual_kernel(x_ref, o_ref, w_ref, xo_ref):
    xo_ref[...] = x_ref[...] + jnp.dot(o_ref[...].astype(BF16), w_ref[...], preferred_element_type=F32)


def _linear_residual(x, o, w):
    return pl.pallas_call(
        _linear_residual_kernel,
        out_shape=jax.ShapeDtypeStruct(x.shape, F32),
        compiler_params=_cparams(None),
        name="linear_residual",
    )(x, o, w)


def _attn_sample_kernel(sink_ref, q_ref, ck_ref, cv_ref, kn_ref, vn_ref, bias_ref, o_ref, kw_ref, vw_ref):
    kw = jnp.concatenate([ck_ref[:, 1:, :], kn_ref[...]], axis=1)
    vw = jnp.concatenate([cv_ref[:, 1:, :], vn_ref[...]], axis=1)
    kw_ref[...] = kw
    vw_ref[...] = vw
    q = (q_ref[...] * (HEAD_DIM ** -0.5)).astype(BF16)
    sc = jnp.einsum('bhc,bjc->bhj', q, kw.astype(BF16), preferred_element_type=F32)
    sc = sc + bias_ref[...][None]
    sink = sink_ref[...][None]
    m = jnp.maximum(jnp.max(sc, axis=-1, keepdims=True), sink)
    e = jnp.exp(sc - m)
    den = jnp.sum(e, axis=-1, keepdims=True) + jnp.exp(sink - m)
    pr = (e / den).astype(BF16)
    o_ref[...] = jnp.einsum('bhj,bjc->bhc', pr, vw.astype(BF16), preferred_element_type=F32)


def _attn_sample(q3, ck, cv, kn, vn, bias_s, sinks_col):
    n = q3.shape[0]
    bb = 8
    blk = lambda s: pl.BlockSpec((bb,) + s, lambda i: (i, 0, 0))
    return pl.pallas_call(
        _attn_sample_kernel,
        grid=(n // bb,),
        in_specs=[_full((N_HEADS, 1)), blk((N_HEADS, KV_W)), blk((WINDOW, KV_W)), blk((WINDOW, KV_W)),
                  blk((1, KV_W)), blk((1, KV_W)), _full((N_HEADS, WINDOW))],
        out_specs=[blk((N_HEADS, KV_W)), blk((WINDOW, KV_W)), blk((WINDOW, KV_W))],
        out_shape=[jax.ShapeDtypeStruct((n, N_HEADS, KV_W), F32),
                   jax.ShapeDtypeStruct((n, WINDOW, KV_W), F32),
                   jax.ShapeDtypeStruct((n, WINDOW, KV_W), F32)],
        compiler_params=_cparams(("arbitrary",)),
        name="attn_sample",
    )(sinks_col, q3, ck, cv, kn, vn, bias_s)


def _attn_sample_layer(x, g, wqkv, wo, bias_s, sinks, cache_k, cache_v):
    n = x.shape[0]
    qkv = _norm_linear(x, g, wqkv)
    q = qkv[:, :D_MODEL].reshape(n, N_KV_HEADS, GQ, 1, HEAD_DIM)
    place = jnp.eye(N_KV_HEADS, dtype=F32).reshape(1, N_KV_HEADS, 1, N_KV_HEADS, 1)
    q3 = (q * place).reshape(n, N_HEADS, KV_W)
    kn = qkv[:, D_MODEL:D_MODEL + KV_W].reshape(n, 1, KV_W)
    vn = qkv[:, D_MODEL + KV_W:].reshape(n, 1, KV_W)
    o3, kw, vw = _attn_sample(q3, cache_k.reshape(n, WINDOW, KV_W), cache_v.reshape(n, WINDOW, KV_W),
                              kn, vn, bias_s, sinks.reshape(N_HEADS, 1))
    o5 = o3.reshape(n, N_KV_HEADS, GQ, N_KV_HEADS, HEAD_DIM)
    o = jnp.stack([o5[:, gi, :, gi, :] for gi in range(N_KV_HEADS)], axis=1).reshape(n, D_MODEL)
    return _linear_residual(x, o, wo), kw, vw


def _ssm_prep_kernel(ar_ref, ai_ref, ldt_ref, br_ref, bi_ref, c_ref, a_out, bc_out, wb_out, wc_out, wc2_out):
    ar, ai = ar_ref[...], ai_ref[...]
    dt = jnp.exp(ldt_ref[...])
    mag = jnp.exp(ar * dt)
    lr = mag * jnp.cos(ai * dt)
    li = mag * jnp.sin(ai * dt)
    den = ar * ar + ai * ai
    cr = ((lr - 1.0) * ar + li * ai) / den
    ci = (li * ar - (lr - 1.0) * ai) / den
    row = lax.broadcasted_iota(jnp.int32, (SUBLANES, N_STATES), 0)
    a_out[...] = jnp.broadcast_to(lr, (SUBLANES, N_STATES))
    bc_out[...] = jnp.where(row < SUBLANES // 2, -li, li)
    for j in range(N_SLABS):
        crj = cr[:, SLAB_STATES * j:SLAB_STATES * (j + 1)]
        cij = ci[:, SLAB_STATES * j:SLAB_STATES * (j + 1)]
        br, bi = br_ref[j], bi_ref[j]
        wb_out[j, :, :SLAB_STATES] = (crj * br - cij * bi).astype(BF16)
        wb_out[j, :, SLAB_STATES:] = (crj * bi + cij * br).astype(BF16)
        wc_out[j, :SLAB_STATES, :] = c_ref[j, :SLAB_STATES, :].astype(BF16)
        wc_out[j, SLAB_STATES:, :] = (-c_ref[j, SLAB_STATES:, :]).astype(BF16)
        wc2_out[j, :, :LANES] = c_ref[j, :SLAB_STATES, :].astype(BF16)
        wc2_out[j, :, LANES:] = (-c_ref[j, SLAB_STATES:, :]).astype(BF16)


def _ssm_prep(a_re, a_im, log_dt, b_re, b_im, c_re, c_im):
    eye = jnp.eye(SUBLANES, dtype=F32)

    def blk_b(b):
        t = b.reshape(N_SLABS, 8, SSM_STATE, SSM_GROUP).transpose(0, 1, 3, 2)
        return jnp.einsum('jghp,gk->jghkp', t, eye).reshape(N_SLABS, LANES, SLAB_STATES)

    def blk_c(c):
        t = c.reshape(N_SLABS, 8, SSM_GROUP, SSM_STATE).transpose(0, 1, 3, 2)
        return jnp.einsum('jgph,gk->jgpkh', t, eye).reshape(N_SLABS, SLAB_STATES, LANES)

    row = lambda a: a.reshape(1, N_STATES)
    ldt = jnp.broadcast_to(log_dt[:, None], (N_SSM_GROUPS, SSM_STATE))
    c_all = jnp.concatenate([blk_c(c_re), blk_c(c_im)], axis=1)
    return pl.pallas_call(
        _ssm_prep_kernel,
        out_shape=[jax.ShapeDtypeStruct((SUBLANES, N_STATES), F32),
                   jax.ShapeDtypeStruct((SUBLANES, N_STATES), F32),
                   jax.ShapeDtypeStruct((N_SLABS, LANES, 2 * SLAB_STATES), BF16),
                   jax.ShapeDtypeStruct((N_SLABS, 2 * SLAB_STATES, LANES), BF16),
                   jax.ShapeDtypeStruct((N_SLABS, SLAB_STATES, 2 * LANES), BF16)],
        compiler_params=_cparams(None),
        name="ssm_prep",
    )(row(a_re), row(a_im), row(ldt), blk_b(b_re), blk_b(b_im), c_all)


def _glu_residual(x, y, u, d, wa, wg):
    z = _gelu_tanh(y + d * u).astype(BF16)
    out = jnp.dot(z, wa, preferred_element_type=F32) * _sigmoid(jnp.dot(z, wg, preferred_element_type=F32))
    return x + out


def _ssm_prompt_kernel(x_ref, g_ref, wb_ref, wc_ref, a_ref, bc_ref, d_ref, wa_ref, wg_ref,
                       xo_ref, st_ref, z_scr, u_scr, o_scr, xs_scr, *, nb, lc):
    i = pl.program_id(0)
    blocks_per_slab = SLAB_STATES // LANES
    half = SUBLANES // 2
    pairs = lc // 2
    rows = half * lc
    roll_half = lambda v: pltpu.roll(v, half, 1)

    @pl.when(i == 0)
    def _():
        xs_scr[...] = jnp.zeros((SUBLANES, N_STATES), F32)
        if nb < half:
            u_scr[...] = jnp.zeros(u_scr.shape, F32)

    for b in range(nb):
        u = _rms(x_ref[b], g_ref[...])
        for c in range(N_SLABS):
            u_scr[c, pl.ds(b, lc, stride=half), :] = u[:, LANES * c:LANES * (c + 1)]

    lower = lax.broadcasted_iota(jnp.int32, (pairs, SUBLANES, LANES), 1) < half
    for j in range(N_SLABS):
        res = jnp.dot(u_scr[j].astype(BF16), wb_ref[j], preferred_element_type=F32)
        for c in range(blocks_per_slab):
            re = res[:, LANES * c:LANES * (c + 1)].reshape(pairs, SUBLANES, LANES)
            im = res[:, SLAB_STATES + LANES * c:SLAB_STATES + LANES * (c + 1)].reshape(pairs, SUBLANES, LANES)
            blk = blocks_per_slab * j + c
            z_scr[blk, :, 0] = jnp.where(lower, re, roll_half(im))
            z_scr[blk, :, 1] = jnp.where(lower, roll_half(re), im)

    group = 8
    for cb in range(N_STATES // LANES // group):
        blks = [group * cb + k for k in range(group)]
        a = [a_ref[:, LANES * q:LANES * (q + 1)] for q in blks]
        bc = [bc_ref[:, LANES * q:LANES * (q + 1)] for q in blks]
        x0 = tuple(xs_scr[:, LANES * q:LANES * (q + 1)] for q in blks)

        def step2(tp, xs, blks=blks, a=a, bc=bc):
            for par in range(2):
                out = []
                for k, q in enumerate(blks):
                    xn = a[k] * xs[k] + bc[k] * pltpu.roll(xs[k], half, 0) + z_scr[q, tp, par]
                    z_scr[q, tp, par] = xn
                    out.append(xn)
                xs = tuple(out)
            return xs

        xf = lax.fori_loop(0, pairs, step2, x0)
        for k, q in enumerate(blks):
            xs_scr[:, LANES * q:LANES * (q + 1)] = xf[k]
    st_ref[...] = xs_scr[...]

    ys = []
    for j in range(N_SLABS):
        s = jnp.concatenate([z_scr[blocks_per_slab * j + c].reshape(SUBLANES * lc, LANES)
                             for c in range(blocks_per_slab)], axis=1).astype(BF16)
        p3 = jnp.dot(s, wc_ref[j], preferred_element_type=F32).reshape(lc, SUBLANES, 2 * LANES)
        y4 = (p3[:, :, :LANES] + roll_half(p3[:, :, LANES:])).reshape(pairs, 2, SUBLANES, LANES)
        ys.append(jnp.where(lower, y4[:, 0], roll_half(y4[:, 1])).reshape(rows, LANES))
    y = jnp.concatenate(ys, axis=1)
    u_all = jnp.concatenate([u_scr[c] for c in range(N_SLABS)], axis=1)
    z = _gelu_tanh(y + d_ref[...] * u_all).astype(BF16)
    out = (jnp.dot(z, wa_ref[...], preferred_element_type=F32)
           * _sigmoid(jnp.dot(z, wg_ref[...], preferred_element_type=F32)))
    for c in range(N_SLABS):
        o_scr[c] = out[:, LANES * c:LANES * (c + 1)]
    for b in range(nb):
        xo_ref[b] = x_ref[b] + jnp.concatenate(
            [o_scr[c, pl.ds(b, lc, stride=half), :] for c in range(N_SLABS)], axis=1)


def _ssm_prompt_layer(x, g, prep, d, wa, wg):
    a_rows, bc_rows, wb, _, wc2 = prep
    nb, t, dm = x.shape
    half = SUBLANES // 2
    assert nb <= half
    lc = min(128, t)
    kern = functools.partial(_ssm_prompt_kernel, nb=nb, lc=lc)
    return pl.pallas_call(
        kern,
        grid=(t // lc,),
        in_specs=[
            pl.BlockSpec((nb, lc, dm), lambda i: (0, i, 0)),
            _full((1, dm)),
            _full(wb.shape), _full(wc2.shape),
            _full((SUBLANES, N_STATES)), _full((SUBLANES, N_STATES)),
            _full((1, dm)), _full((dm, dm)), _full((dm, dm)),
        ],
        out_specs=[pl.BlockSpec((nb, lc, dm), lambda i: (0, i, 0)), _full((SUBLANES, N_STATES))],
        out_shape=[jax.ShapeDtypeStruct((nb, t, dm), F32), jax.ShapeDtypeStruct((SUBLANES, N_STATES), F32)],
        scratch_shapes=[
            pltpu.VMEM((N_STATES // LANES, lc // 2, 2, SUBLANES, LANES), F32),
            pltpu.VMEM((N_SLABS, half * lc, LANES), F32),
            pltpu.VMEM((N_SLABS, half * lc, LANES), F32),
            pltpu.VMEM((SUBLANES, N_STATES), F32),
        ],
        compiler_params=_cparams(("arbitrary",)),
        name="ssm_prompt",
    )(x, g, wb, wc2, a_rows, bc_rows, d, wa, wg)


def _ssm_sample_kernel(x_ref, g_ref, wb_ref, wc_ref, a_ref, bc_ref, h0r_ref, h0i_ref, d_ref, wa_ref, wg_ref,
                       xo_ref, sr_ref, si_ref):
    x = x_ref[...]
    u = _rms(x, g_ref[...])
    ub = u.astype(BF16)
    ys = []
    for j in range(N_SLABS):
        sl = slice(SLAB_STATES * j, SLAB_STATES * (j + 1))
        res = jnp.dot(ub[:, LANES * j:LANES * (j + 1)], wb_ref[j], preferred_element_type=F32)
        lr = a_ref[0:1, sl]
        li = bc_ref[SUBLANES - 1:SUBLANES, sl]
        h0r, h0i = h0r_ref[:, sl], h0i_ref[:, sl]
        xr = res[:, :SLAB_STATES] + (lr * h0r - li * h0i)
        xi = res[:, SLAB_STATES:] + (lr * h0i + li * h0r)
        sr_ref[:, sl] = xr
        si_ref[:, sl] = xi
        s = jnp.concatenate([xr, xi], axis=1).astype(BF16)
        ys.append(jnp.dot(s, wc_ref[j], preferred_element_type=F32))
    y = jnp.concatenate(ys, axis=1)
    xo_ref[...] = _glu_residual(x, y, u, d_ref[...], wa_ref[...], wg_ref[...])


def _ssm_sample_layer(x, g, prep, h0r, h0i, d, wa, wg):
    a_rows, bc_rows, wb, wc, _ = prep
    n = x.shape[0]
    return pl.pallas_call(
        _ssm_sample_kernel,
        out_shape=[jax.ShapeDtypeStruct(x.shape, F32),
                   jax.ShapeDtypeStruct((n, N_STATES), F32),
                   jax.ShapeDtypeStruct((n, N_STATES), F32)],
        compiler_params=_cparams(None),
        name="ssm_sample",
    )(x, g, wb, wc, a_rows, bc_rows, h0r, h0i, d, wa, wg)


ROUTER_W = LANES
EXPERT_LANE0 = N_EXPERT_GROUPS


def _route_topk(h, wr, br):
    h_hi = h.astype(BF16)
    h_lo = (h - h_hi.astype(F32)).astype(BF16)
    w_hi = wr.astype(BF16)
    w_lo = (wr - w_hi.astype(F32)).astype(BF16)
    dot = lambda a, b: jnp.dot(a, b, preferred_element_type=F32)
    both = dot(h_hi, jnp.concatenate([w_hi, w_lo], axis=1))
    logits = both[:, :ROUTER_W] + (dot(h_lo, w_hi) + both[:, ROUTER_W:]) + br
    lane_i = lax.broadcasted_iota(jnp.int32, logits.shape, 1)
    lane = lane_i.astype(F32)
    neg = -jnp.inf
    past_end = float(ROUTER_W)
    gl = jnp.where(lane_i < N_EXPERT_GROUPS, logits, neg)
    gmax = jnp.max(gl, axis=-1, keepdims=True)
    g_sel = jnp.min(jnp.where(gl == gmax, lane, past_end), axis=-1, keepdims=True)
    p_grp = 1.0 / jnp.sum(jnp.exp(gl - gmax), axis=-1, keepdims=True)
    lane_grp = ((lane_i - EXPERT_LANE0) // EXPERTS_PER_GROUP).astype(F32)
    in_grp = (lane_i >= EXPERT_LANE0) & (lane_i < EXPERT_LANE0 + N_EXPERTS) & (lane_grp == g_sel)
    el = jnp.where(in_grp, logits, neg)
    v1 = jnp.max(el, axis=-1, keepdims=True)
    i1 = jnp.min(jnp.where(el == v1, lane, past_end), axis=-1, keepdims=True)
    el2 = jnp.where(lane == i1, neg, el)
    v2 = jnp.max(el2, axis=-1, keepdims=True)
    i2 = jnp.min(jnp.where(el2 == v2, lane, past_end), axis=-1, keepdims=True)
    e2 = jnp.exp(v2 - v1)
    w1 = p_grp / (1.0 + e2)
    w2 = p_grp * e2 / (1.0 + e2)
    return lane, i1, i2, w1, w2


def _route(h, wr, br):
    lane, i1, i2, w1, w2 = _route_topk(h, wr, br)
    return jnp.where(lane == i1, w1, 0.0) + jnp.where(lane == i2, w2, 0.0)


def _moe_kernel(x_ref, p_ref, gffn_ref, wr_ref, br_ref, wg_ref, wu_ref, wd_ref,
                gple_ref, wpg_ref, wpp_ref, gfin_ref, o_ref, hn_scr, gates_scr, acc_scr, *, final):
    e = pl.program_id(1)

    @pl.when(e == 0)
    def _():
        x = x_ref[...]
        h = _rms(x, gffn_ref[...])
        hn_scr[...] = h.astype(BF16)
        gates_scr[...] = _route(h, wr_ref[...], br_ref[...])
        acc_scr[...] = x

    hb = hn_scr[...]
    gt = jnp.dot(hb, wg_ref[0].astype(BF16), preferred_element_type=F32)
    up = jnp.dot(hb, wu_ref[0].astype(BF16), preferred_element_type=F32)
    a = (gt * _sigmoid(gt) * up).astype(BF16)
    gates = gates_scr[...]
    lane = lax.broadcasted_iota(jnp.int32, gates.shape, 1)
    ge = jnp.sum(jnp.where(lane == e + EXPERT_LANE0, gates, 0.0), axis=-1, keepdims=True)
    acc_scr[...] += ge * jnp.dot(a, wd_ref[0].astype(BF16), preferred_element_type=F32)

    @pl.when(e == pl.num_programs(1) - 1)
    def _():
        x1 = acc_scr[...]
        gate = _sigmoid(jnp.dot(_rms(x1, gple_ref[...]).astype(BF16), wpg_ref[...], preferred_element_type=F32))
        x2 = x1 + gate * jnp.dot(p_ref[...].astype(BF16), wpp_ref[...], preferred_element_type=F32)
        o_ref[...] = _rms(x2, gfin_ref[...]) if final else x2


def _moe_ple_layer(li, x, p, gffn, wr, br, wg, wu, wd, gple, wpg, wpp, gfin, final):
    n, d = x.shape
    tm = min(1024, n)
    kern = functools.partial(_moe_kernel, final=final)
    tok = lambda w: pl.BlockSpec((tm, w), lambda t, e: (t, 0))
    const = lambda s: pl.BlockSpec(s, lambda t, e: (0,) * len(s))
    return pl.pallas_call(
        kern,
        grid=(n // tm, N_EXPERTS),
        in_specs=[
            tok(d), pl.BlockSpec((None, tm, PLE_DIM), lambda t, e: (li, t, 0)),
            const((1, d)), const((d, ROUTER_W)), const((1, ROUTER_W)),
            pl.BlockSpec((None, 1, d, D_EXPERT), lambda t, e: (li, e, 0, 0)),
            pl.BlockSpec((None, 1, d, D_EXPERT), lambda t, e: (li, e, 0, 0)),
            pl.BlockSpec((None, 1, D_EXPERT, d), lambda t, e: (li, e, 0, 0)),
            const((1, d)), const((d, d)), const((PLE_DIM, d)), const((1, d)),
        ],
        out_specs=tok(d),
        out_shape=jax.ShapeDtypeStruct((n, d), F32),
        scratch_shapes=[pltpu.VMEM((tm, d), BF16), pltpu.VMEM((tm, ROUTER_W), F32), pltpu.VMEM((tm, d), F32)],
        compiler_params=_cparams(("arbitrary", "arbitrary")),
        name="moe_ple",
    )(x, p, gffn, wr, br, wg, wu, wd, gple, wpg, wpp, gfin)


TOKEN_TILE_ROWS = D_MODEL // LANES
INFO_W1, INFO_W2, INFO_E1, INFO_E2, INFO_R1, INFO_R2 = range(6)
EXPERT_TILE = 256


def _to_token_tiles(ref_2d, val, n):
    r = val.shape[1] // LANES
    for s in range(r):
        ref_2d[pl.ds(s, n, stride=r), :] = val[:, LANES * s:LANES * (s + 1)]


def _from_token_tiles(ref_2d, n, r):
    return jnp.concatenate([ref_2d[pl.ds(s, n, stride=r), :] for s in range(r)], axis=1)


def _tile_rows(i, r=TOKEN_TILE_ROWS):
    return pl.ds(pl.multiple_of(i * r, r), r)


def _route_kernel(x_ref, g_ref, wr_ref, br_ref, before_ref, info_ref, cnt_ref, base_scr):
    i = pl.program_id(0)

    @pl.when(i == 0)
    def _():
        base_scr[...] = jnp.zeros(base_scr.shape, F32)

    h = _rms(x_ref[...], g_ref[...])
    lane, i1, i2, w1, w2 = _route_topk(h, wr_ref[...], br_ref[...])
    chosen = jnp.where((lane == i1) | (lane == i2), 1.0, 0.0)
    prefix = jnp.dot(before_ref[...], chosen.astype(BF16), preferred_element_type=F32) + base_scr[...]
    rank1 = jnp.sum(jnp.where(lane == i1, prefix, 0.0), axis=-1, keepdims=True)
    rank2 = jnp.sum(jnp.where(lane == i2, prefix, 0.0), axis=-1, keepdims=True)
    base_scr[...] += jnp.sum(chosen, axis=0, keepdims=True)
    cnt_ref[...] = base_scr[...]
    e1 = (i1 - EXPERT_LANE0).astype(F32)
    e2 = (i2 - EXPERT_LANE0).astype(F32)
    info = jnp.zeros(h.shape[:1] + (ROUTER_W,), F32)
    for ln, v in ((INFO_W1, w1), (INFO_W2, w2), (INFO_E1, e1), (INFO_E2, e2), (INFO_R1, rank1), (INFO_R2, rank2)):
        info = jnp.where(lane == ln, v, info)
    info_ref[...] = info


def _route_call(x, gffn, wr, br):
    n, d = x.shape
    tm = 512
    before = jnp.asarray(np.tril(np.ones((tm, tm), np.float32), -1), BF16)
    return pl.pallas_call(
        _route_kernel,
        grid=(n // tm,),
        in_specs=[pl.BlockSpec((tm, d), lambda i: (i, 0)), _full((1, d)), _full((d, ROUTER_W)), _full((1, ROUTER_W)),
                  _full((tm, tm))],
        out_specs=[pl.BlockSpec((tm, ROUTER_W), lambda i: (i, 0)), _full((1, ROUTER_W))],
        out_shape=[jax.ShapeDtypeStruct((n, ROUTER_W), F32), jax.ShapeDtypeStruct((1, ROUTER_W), F32)],
        scratch_shapes=[pltpu.VMEM((1, ROUTER_W), F32)],
        compiler_params=_cparams(("arbitrary",)),
        name="moe_route",
    )(x, gffn, wr, br, before)


def _dispatch_kernel(pos_ref, cnt_ref, start_ref, x_ref, g_ref, xs_hbm, hbuf, zbuf, sem, zsem,
                     *, td, n_steps, n_pad_rows):
    i = pl.program_id(0)
    slot = i % 2

    def wait_slot(sl):
        for _ in range(2):
            pltpu.make_async_copy(hbuf.at[sl], hbuf.at[sl], sem.at[sl]).wait()

    @pl.when(i >= 2)
    def _():
        wait_slot(slot)

    _to_token_tiles(hbuf.at[slot], _rms(x_ref[...], g_ref[...]), td)
    rows = _tile_rows

    def issue(n8, carry):
        for u in range(SUBLANES):
            n = n8 * SUBLANES + u
            t = i * td + n
            src = hbuf.at[slot, rows(n), :]
            pltpu.make_async_copy(src, xs_hbm.at[rows(pos_ref[0, t]), :], sem.at[slot]).start(priority=0)
            pltpu.make_async_copy(src, xs_hbm.at[rows(pos_ref[1, t]), :], sem.at[slot]).start(priority=1)
        return carry

    lax.fori_loop(0, td // SUBLANES, issue, 0)

    @pl.when(i == 0)
    def _():
        zbuf[...] = jnp.zeros(zbuf.shape, zbuf.dtype)
        for e in range(N_EXPERTS + 1):
            lo = start_ref[e] + cnt_ref[e]
            hi = start_ref[e + 1]

            def zero_row(r, carry):
                pltpu.make_async_copy(zbuf, xs_hbm.at[rows(r), :], zsem).start()
                return carry

            lax.fori_loop(lo, hi, zero_row, 0)

    @pl.when(i == n_steps - 1)
    def _():
        wait_slot(slot)
        if n_steps > 1:
            wait_slot(1 - slot)
        pad = xs_hbm.at[pl.ds(0, n_pad_rows * TOKEN_TILE_ROWS), :]
        pltpu.make_async_copy(pad, pad, zsem).wait()


def _dispatch_call(pos, cnt, start, x, gffn, a_pad, n_pad_rows):
    n, d = x.shape
    td = 512
    n_steps = n // td
    kern = functools.partial(_dispatch_kernel, td=td, n_steps=n_steps, n_pad_rows=n_pad_rows)
    return pl.pallas_call(
        kern,
        grid_spec=pltpu.PrefetchScalarGridSpec(
            num_scalar_prefetch=3,
            grid=(n_steps,),
            in_specs=[pl.BlockSpec((td, d), lambda i, *_: (i, 0)), pl.BlockSpec((1, d), lambda i, *_: (0, 0))],
            out_specs=pl.BlockSpec(memory_space=pl.ANY),
            scratch_shapes=[pltpu.VMEM((2, td * TOKEN_TILE_ROWS, LANES), F32),
                            pltpu.VMEM((TOKEN_TILE_ROWS, LANES), F32),
                            pltpu.SemaphoreType.DMA((2,)), pltpu.SemaphoreType.DMA(())],
        ),
        out_shape=jax.ShapeDtypeStruct((a_pad * TOKEN_TILE_ROWS, LANES), F32),
        compiler_params=_cparams(("arbitrary",)),
        name="moe_dispatch",
    )(pos, cnt, start, x, gffn)


EXPERT_IN_SLOTS = 3


def _experts_kernel(te_ref, xs_hbm, wg_ref, wu_ref, wd_ref, y_ref, xbuf, xsem, wg_scr, wu_scr, wd_scr,
                    *, te, n_tiles):
    t = pl.program_id(0)
    rows = te * TOKEN_TILE_ROWS

    def fetch(tile):
        slot = tile % EXPERT_IN_SLOTS
        src = xs_hbm.at[pl.ds(pl.multiple_of(tile * rows, rows), rows), :]
        return pltpu.make_async_copy(src, xbuf.at[slot], xsem.at[slot])

    @pl.when(t == 0)
    def _():
        for ahead in range(min(EXPERT_IN_SLOTS - 1, n_tiles)):
            fetch(ahead).start()

    @pl.when(t + EXPERT_IN_SLOTS - 1 < n_tiles)
    def _():
        fetch(t + EXPERT_IN_SLOTS - 1).start()

    @pl.when((t == 0) | (te_ref[t] != te_ref[jnp.maximum(t - 1, 0)]))
    def _():
        wg_scr[...] = wg_ref[0].astype(BF16)
        wu_scr[...] = wu_ref[0].astype(BF16)
        wd_scr[...] = wd_ref[0].astype(BF16)

    fetch(t).wait()
    x = _from_token_tiles(xbuf.at[t % EXPERT_IN_SLOTS], te, TOKEN_TILE_ROWS).astype(BF16)
    gt = jnp.dot(x, wg_scr[...], preferred_element_type=F32)
    up = jnp.dot(x, wu_scr[...], preferred_element_type=F32)
    a = (gt * _sigmoid(gt) * up).astype(BF16)
    _to_token_tiles(y_ref, jnp.dot(a, wd_scr[...], preferred_element_type=F32), te)


def _experts_call(li, tile_expert, xs, wg, wu, wd, te):
    rows = te * TOKEN_TILE_ROWS
    n_tiles = xs.shape[0] // rows
    d = wg.shape[2]
    kern = functools.partial(_experts_kernel, te=te, n_tiles=n_tiles)
    return pl.pallas_call(
        kern,
        grid_spec=pltpu.PrefetchScalarGridSpec(
            num_scalar_prefetch=1,
            grid=(n_tiles,),
            in_specs=[pl.BlockSpec(memory_space=pl.ANY),
                      pl.BlockSpec((None, 1, d, D_EXPERT), lambda t, ex: (li, ex[t], 0, 0)),
                      pl.BlockSpec((None, 1, d, D_EXPERT), lambda t, ex: (li, ex[t], 0, 0)),
                      pl.BlockSpec((None, 1, D_EXPERT, d), lambda t, ex: (li, ex[t], 0, 0))],
            out_specs=pl.BlockSpec((rows, LANES), lambda t, ex: (t, 0)),
            scratch_shapes=[pltpu.VMEM((EXPERT_IN_SLOTS, rows, LANES), F32),
                            pltpu.SemaphoreType.DMA((EXPERT_IN_SLOTS,)),
                            pltpu.VMEM((d, D_EXPERT), BF16), pltpu.VMEM((d, D_EXPERT), BF16),
                            pltpu.VMEM((D_EXPERT, d), BF16)],
        ),
        out_shape=jax.ShapeDtypeStruct((n_tiles * rows, LANES), F32),
        compiler_params=_cparams(("arbitrary",)),
        name="moe_experts",
    )(tile_expert, xs, wg, wu, wd)


def _combine_kernel(pos_ref, x_ref, info_ref, p_ref, gple_ref, wpg_ref, wpp_ref, gfin_ref, ys_hbm,
                    o_ref, ybuf, sem, *, tm, n_steps, final):
    i = pl.program_id(0)
    slot = i % 2

    def issue(step, sl):
        def body(n8, carry):
            for u in range(SUBLANES):
                n = n8 * SUBLANES + u
                t = step * tm + n
                for k in range(2):
                    pltpu.make_async_copy(ys_hbm.at[_tile_rows(pos_ref[k, t]), :],
                                          ybuf.at[sl, k, _tile_rows(n), :], sem.at[sl]).start(priority=k)
            return carry

        lax.fori_loop(0, tm // SUBLANES, body, 0)

    @pl.when(i == 0)
    def _():
        issue(0, 0)

    @pl.when(i + 1 < n_steps)
    def _():
        issue(i + 1, 1 - slot)

    for k in range(2):
        pltpu.make_async_copy(ybuf.at[slot, k], ybuf.at[slot, k], sem.at[slot]).wait()
    y1 = _from_token_tiles(ybuf.at[slot, 0], tm, TOKEN_TILE_ROWS)
    y2 = _from_token_tiles(ybuf.at[slot, 1], tm, TOKEN_TILE_ROWS)
    info = info_ref[...]
    x1 = x_ref[...] + (info[:, INFO_W1:INFO_W1 + 1] * y1 + info[:, INFO_W2:INFO_W2 + 1] * y2)
    gate = _sigmoid(jnp.dot(_rms(x1, gple_ref[...]).astype(BF16), wpg_ref[...], preferred_element_type=F32))
    x2 = x1 + gate * jnp.dot(p_ref[...].astype(BF16), wpp_ref[...], preferred_element_type=F32)
    o_ref[...] = _rms(x2, gfin_ref[...]) if final else x2


def _combine_call(li, pos, x, info, p, gple, wpg, wpp, gfin, ys, final):
    n, d = x.shape
    tm = 512
    n_steps = n // tm
    kern = functools.partial(_combine_kernel, tm=tm, n_steps=n_steps, final=final)
    tok = lambda w: pl.BlockSpec((tm, w), lambda i, *_: (i, 0))
    const = lambda s: pl.BlockSpec(s, lambda i, *_: (0,) * len(s))
    return pl.pallas_call(
        kern,
        grid_spec=pltpu.PrefetchScalarGridSpec(
            num_scalar_prefetch=1,
            grid=(n_steps,),
            in_specs=[tok(d), tok(ROUTER_W), pl.BlockSpec((None, tm, PLE_DIM), lambda i, *_: (li, i, 0)),
                      const((1, d)), const((d, d)), const((PLE_DIM, d)),
                      const((1, d)), pl.BlockSpec(memory_space=pl.ANY)],
            out_specs=tok(d),
            scratch_shapes=[pltpu.VMEM((2, 2, tm * TOKEN_TILE_ROWS, LANES), F32), pltpu.SemaphoreType.DMA((2,))],
        ),
        out_shape=jax.ShapeDtypeStruct((n, d), F32),
        compiler_params=_cparams(("arbitrary",)),
        name="moe_combine",
    )(pos, x, info, p, gple, wpg, wpp, gfin, ys)


def _positions_kernel(info_ref, start_ref, pos_ref):
    info = info_ref[...]
    lane = lax.broadcasted_iota(jnp.int32, info.shape, 1)
    lane_expert = (lane - EXPERT_LANE0).astype(F32)
    start = start_ref[...]
    cols = []
    for e_lane, r_lane in ((INFO_E1, INFO_R1), (INFO_E2, INFO_R2)):
        seg = jnp.sum(jnp.where(lane_expert == info[:, e_lane:e_lane + 1], start, 0.0), axis=-1, keepdims=True)
        cols.append(seg + info[:, r_lane:r_lane + 1])
    both = jnp.where(lane == 0, cols[0], jnp.where(lane == 1, cols[1], 0.0))
    pos_ref[...] = both.T[:SUBLANES, :].astype(jnp.int32)


def _positions_call(info, start_row):
    n = info.shape[0]
    tm = 2048 if n % 2048 == 0 else 512
    return pl.pallas_call(
        _positions_kernel,
        grid=(n // tm,),
        in_specs=[pl.BlockSpec((tm, ROUTER_W), lambda i: (i, 0)), _full((1, ROUTER_W))],
        out_specs=pl.BlockSpec((SUBLANES, tm), lambda i: (0, i)),
        out_shape=jax.ShapeDtypeStruct((SUBLANES, n), jnp.int32),
        compiler_params=_cparams(("arbitrary",)),
        name="moe_positions",
    )(info, start_row)


def _moe_routed_layer(li, x, p, gffn, wr, br, wg, wu, wd, gple, wpg, wpp, gfin, final):
    n = x.shape[0]
    te = EXPERT_TILE
    n_assign = 2 * n
    assert n_assign % te == 0
    a_pad = n_assign + N_EXPERTS * te
    info, cnt = _route_call(x, gffn, wr, br)
    counts = cnt[0, EXPERT_LANE0:EXPERT_LANE0 + N_EXPERTS].astype(jnp.int32)
    padded = (counts + te - 1) // te * te
    seg_end = jnp.cumsum(padded)
    seg_start = seg_end - padded
    start_row = jnp.zeros((1, ROUTER_W), F32).at[0, EXPERT_LANE0:EXPERT_LANE0 + N_EXPERTS].set(seg_start.astype(F32))
    pos = _positions_call(info, start_row)[:2]
    start = jnp.concatenate([seg_start, seg_end[-1:], jnp.full((1,), a_pad, jnp.int32)])
    cnt_pad = jnp.concatenate([counts, jnp.zeros((1,), jnp.int32)])
    tile_row0 = jnp.arange(a_pad // te, dtype=jnp.int32) * te
    tile_expert = jnp.minimum(jnp.sum(seg_end[None, :] <= tile_row0[:, None], axis=1), N_EXPERTS - 1).astype(jnp.int32)
    xs = _dispatch_call(pos, cnt_pad, start, x, gffn, a_pad, N_EXPERTS * te)
    ys = _experts_call(li, tile_expert, xs, wg, wu, wd, te)
    return _combine_call(li, pos, x, info, p, gple, wpg, wpp, gfin, ys, final)


def kernel(x_prompt, x_sample, cache_k_win, cache_v_win, state_ssm_re, state_ssm_im, p_prompt, p_sample,
           rel_bias_table, g_mix, w_qkv, w_o, sinks, ssm_a_re, ssm_a_im, ssm_log_dt, ssm_b_re, ssm_b_im,
           ssm_c_re, ssm_c_im, ssm_d, w_glu_a, w_glu_b, g_ffn, w_router_group, b_router_group,
           w_router_expert, b_router_expert, w_exp_gate, w_exp_up, w_exp_down, g_ple, w_ple_gate,
           w_ple_proj, g_final):
    bsz, seq, d = x_prompt.shape
    ns = x_sample.shape[0]
    depth = g_mix.shape[0]
    row = lambda v: v.reshape(1, -1).astype(F32)

    qi = np.arange(WINDOW)[:, None]
    sj = np.arange(2 * WINDOW)[None, :]
    dist = qi + WINDOW - sj
    idx_p = _t5_bucket_np(dist)
    ok_p = ((dist >= 0) & (dist < WINDOW)).astype(np.int32)
    idx_s = np.broadcast_to(_t5_bucket_np(WINDOW - 1 - np.arange(WINDOW))[None, :], (SUBLANES, WINDOW))
    bias_p = _band_bias(rel_bias_table, idx_p, ok_p)
    bias_s = _bias_from_table(rel_bias_table, idx_s)[:, 0, :]

    wqkv_b = w_qkv[0].astype(BF16)
    wo_b = w_o[0].astype(BF16)
    wpg_b, wpp_b = w_ple_gate.astype(BF16), w_ple_proj.astype(BF16)
    wa_b, wgl_b = w_glu_a[0].astype(BF16), w_glu_b[0].astype(BF16)
    pad = ROUTER_W - N_EXPERT_GROUPS - N_EXPERTS
    wr = jnp.concatenate([w_router_group, w_router_expert.reshape(depth, d, N_EXPERTS),
                          jnp.zeros((depth, d, pad), F32)], axis=-1)
    br = jnp.concatenate([b_router_group, b_router_expert.reshape(depth, N_EXPERTS),
                          jnp.zeros((depth, pad), F32)], axis=-1)

    prep = _ssm_prep(ssm_a_re[0], ssm_a_im[0], ssm_log_dt[0], ssm_b_re[0], ssm_b_im[0], ssm_c_re[0], ssm_c_im[0])
    d_row = row(ssm_d[0])

    def moe(x2d, p_all, i):
        layer = _moe_routed_layer if 2 * x2d.shape[0] >= N_EXPERTS * EXPERT_TILE else _moe_ple_layer
        return layer(i, x2d, p_all, row(g_ffn[i]), wr[i], br[i:i + 1], w_exp_gate, w_exp_up, w_exp_down,
                     row(g_ple[i]), wpg_b[i], wpp_b[i], row(g_final), final=(i == depth - 1))

    pp = p_prompt.reshape(depth, bsz * seq, PLE_DIM)
    xp, kp, vp = _attn_prompt_layer(x_prompt, row(g_mix[0]), wqkv_b, wo_b, bias_p, sinks[0])
    xp = moe(xp.reshape(bsz * seq, d), pp, 0).reshape(bsz, seq, d)
    xp, st_p = _ssm_prompt_layer(xp, row(g_mix[1]), prep, d_row, wa_b, wgl_b)
    y_prompt = moe(xp.reshape(bsz * seq, d), pp, 1).reshape(bsz, seq, d)

    ps = p_sample.reshape(depth, ns, PLE_DIM)
    xs = x_sample.reshape(ns, d)
    xs, kws, vws = _attn_sample_layer(xs, row(g_mix[0]), wqkv_b, wo_b, bias_s, sinks[0],
                                      cache_k_win[0], cache_v_win[0])
    xs = moe(xs, ps, 0)
    xs, sr_s, si_s = _ssm_sample_layer(xs, row(g_mix[1]), prep, state_ssm_re[0].reshape(ns, N_STATES),
                                       state_ssm_im[0].reshape(ns, N_STATES), d_row, wa_b, wgl_b)
    y_sample = moe(xs, ps, 1).reshape(ns, 1, d)

    half = SUBLANES // 2
    win = lambda a, n: a.reshape(1, n, WINDOW, N_KV_HEADS, HEAD_DIM)
    st = lambda a, n: a.reshape(1, n, N_SSM_GROUPS, SSM_STATE)
    return (y_prompt, y_sample, win(kp, bsz), win(vp, bsz), win(kws, ns), win(vws, ns),
            st(st_p[:bsz], bsz), st(st_p[half:half + bsz], bsz), st(sr_s, ns), st(si_s, ns))
```

```python
import functools
import math

import numpy as np
import jax
import jax.numpy as jnp
from jax import lax
from jax.experimental import pallas as pl
from jax.experimental.pallas import tpu as pltpu

F32 = jnp.float32
BF16 = jnp.bfloat16

D_MODEL = 1024
HEAD_DIM = 64
N_HEADS = 16
N_KV_HEADS = 2
GQ = N_HEADS // N_KV_HEADS
WINDOW = 128
N_BUCKETS = 32
MAX_EXACT = 16
MAX_DISTANCE = 128
SSM_GROUP = 16
N_SSM_GROUPS = 64
SSM_STATE = 64
N_STATES = N_SSM_GROUPS * SSM_STATE
N_EXPERT_GROUPS = 4
EXPERTS_PER_GROUP = 8
N_EXPERTS = 32
D_EXPERT = 256
PLE_DIM = 256
RMS_EPS = 1e-6
NEG_INF = -1e30

LANES = 128
SUBLANES = 8
KV_W = N_KV_HEADS * HEAD_DIM
QKV_W = D_MODEL + 2 * KV_W
SLAB_STATES = 512
N_SLABS = D_MODEL // LANES
VMEM_LIMIT = 56 * 1024 * 1024


def _cparams(sem):
    return pltpu.CompilerParams(dimension_semantics=sem, vmem_limit_bytes=VMEM_LIMIT)


def _rms(x, g):
    return x * lax.rsqrt(jnp.mean(x * x, axis=-1, keepdims=True) + RMS_EPS) * g


def _gelu_tanh(x):
    c = math.sqrt(2.0 / math.pi)
    return x * (0.5 * (1.0 + jnp.tanh(c * (x + 0.044715 * (x * x * x)))))


def _sigmoid(x):
    return 1.0 / (1.0 + jnp.exp(-x))


def _full(shape):
    n = len(shape)
    return pl.BlockSpec(shape, lambda *_: (0,) * n)


def _t5_bucket_np(dist):
    n = np.maximum(dist, 0)
    nf = np.maximum(n, 1).astype(np.float64)
    large = MAX_EXACT + (np.log(nf / MAX_EXACT) / math.log(MAX_DISTANCE / MAX_EXACT)
                         * (N_BUCKETS - MAX_EXACT)).astype(np.int32)
    large = np.minimum(large, N_BUCKETS - 1)
    return np.where(n < MAX_EXACT, n, large).astype(np.int32)


def _bias_kernel(table_ref, idx_ref, o_ref):
    h = pl.program_id(0)
    idx = idx_ref[...]
    acc = jnp.zeros(idx.shape, F32)
    for b in range(N_BUCKETS):
        acc = jnp.where(idx == b, table_ref[b, h], acc)
    o_ref[0] = acc


def _bias_from_table(table, idx_np):
    q, k = idx_np.shape
    return pl.pallas_call(
        _bias_kernel,
        grid=(N_HEADS,),
        in_specs=[pl.BlockSpec(memory_space=pltpu.SMEM), _full((q, k))],
        out_specs=pl.BlockSpec((1, q, k), lambda h: (h, 0, 0)),
        out_shape=jax.ShapeDtypeStruct((N_HEADS, q, k), F32),
        name="rel_bias",
    )(table, jnp.asarray(idx_np))


HEADS_PER_UNIT = GQ // 2
N_UNITS = N_HEADS // HEADS_PER_UNIT
UNIT_ROWS = HEADS_PER_UNIT * WINDOW


def _band_bias_kernel(table_ref, idx_ref, ok_ref, o_ref):
    h = pl.program_id(0)
    idx = idx_ref[...]
    acc = jnp.zeros(idx.shape, F32)
    for b in range(N_BUCKETS):
        acc = jnp.where(idx == b, table_ref[b, h], acc)
    base = jnp.where(ok_ref[...] != 0, acc, NEG_INF)
    sj = lax.broadcasted_iota(jnp.int32, idx.shape, 1)
    o_ref[0, 0] = base
    o_ref[1, 0] = jnp.where(sj >= WINDOW, base, NEG_INF)


def _band_bias(table, idx_np, ok_np):
    def out_map(h):
        r = h % GQ
        return (0, (h // GQ) * 2 + r % 2, r // 2, 0)

    return pl.pallas_call(
        _band_bias_kernel,
        grid=(N_HEADS,),
        in_specs=[pl.BlockSpec(memory_space=pltpu.SMEM), _full(idx_np.shape), _full(ok_np.shape)],
        out_specs=pl.BlockSpec((2, 1, WINDOW, 2 * WINDOW), out_map),
        out_shape=jax.ShapeDtypeStruct((2, N_UNITS, UNIT_ROWS, 2 * WINDOW), F32),
        name="band_bias",
    )(table, jnp.asarray(idx_np), jnp.asarray(ok_np))


def _attn_prompt_kernel(sink_ref, x_ref, g_ref, wqkv_ref, wo_ref, bias_ref,
                        xo_ref, k_ref, v_ref, kv_scr, q_scr, o_scr, *, tq):
    i = pl.program_id(1)
    nsub = tq // WINDOW

    @pl.when(i == 0)
    def _():
        kv_scr[0:WINDOW, :] = jnp.zeros((WINDOW, 2 * KV_W), F32)

    x = x_ref[0]
    h = _rms(x, g_ref[...]).astype(BF16)
    qkv = jnp.dot(h, wqkv_ref[...], preferred_element_type=F32)
    q_scr[...] = (qkv[:, :D_MODEL] * (HEAD_DIM ** -0.5)).astype(BF16)
    kv_scr[WINDOW:WINDOW + tq, :] = qkv[:, D_MODEL:]
    k_ref[0] = qkv[tq - WINDOW:, D_MODEL:D_MODEL + KV_W]
    v_ref[0] = qkv[tq - WINDOW:, D_MODEL + KV_W:]

    lo = lax.broadcasted_iota(jnp.int32, (2 * WINDOW, KV_W), 1) < HEAD_DIM
    row_head = lax.broadcasted_iota(jnp.int32, (UNIT_ROWS, 1), 0) // WINDOW
    sinks = []
    for u in range(N_UNITS):
        g, p = divmod(u, 2)
        col = jnp.zeros((UNIT_ROWS, 1), F32)
        for jj in range(HEADS_PER_UNIT):
            col = jnp.where(row_head == jj, sink_ref[GQ * g + 2 * jj + p], col)
        sinks.append(col)

    for s in range(nsub):
        band = kv_scr[WINDOW * s:WINDOW * s + 2 * WINDOW, :]
        kband, vband = band[:, :KV_W], band[:, KV_W:]
        kroll = pltpu.roll(kband, HEAD_DIM, 1)
        vroll = pltpu.roll(vband, HEAD_DIM, 1)
        kpad = [[jnp.where(lo, kband, 0.0).astype(BF16), jnp.where(lo, 0.0, kroll).astype(BF16)],
                [jnp.where(lo, kroll, 0.0).astype(BF16), jnp.where(lo, 0.0, kband).astype(BF16)]]
        vpad = [[jnp.where(lo, vband, 0.0).astype(BF16), jnp.where(lo, 0.0, vroll).astype(BF16)],
                [jnp.where(lo, vroll, 0.0).astype(BF16), jnp.where(lo, 0.0, vband).astype(BF16)]]
        variant = jnp.where(i * nsub + s > 0, 0, 1)
        for g in range(N_KV_HEADS):
            slabs = range(HEADS_PER_UNIT * g, HEADS_PER_UNIT * (g + 1))
            qs = jnp.concatenate([q_scr[WINDOW * s:WINDOW * (s + 1), LANES * j:LANES * (j + 1)] for j in slabs],
                                 axis=0)
            acc = None
            for p in range(2):
                u = 2 * g + p
                sc = lax.dot_general(qs, kpad[g][p], (((1,), (1,)), ((), ())), preferred_element_type=F32)
                sc = sc + bias_ref[variant, u]
                m = jnp.maximum(jnp.max(sc, axis=-1, keepdims=True), sinks[u])
                e = jnp.exp(sc - m)
                den = jnp.sum(e, axis=-1, keepdims=True) + jnp.exp(sinks[u] - m)
                pv = jnp.dot(e.astype(BF16), vpad[g][p], preferred_element_type=F32)
                term = pv * (1.0 / den)
                acc = term if acc is None else acc + term
            for jj, j in enumerate(slabs):
                o_scr[WINDOW * s:WINDOW * (s + 1), LANES * j:LANES * (j + 1)] = (
                    acc[WINDOW * jj:WINDOW * (jj + 1)].astype(BF16))

    kv_scr[0:WINDOW, :] = kv_scr[tq:tq + WINDOW, :]
    xo_ref[0] = x + jnp.dot(o_scr[...], wo_ref[...], preferred_element_type=F32)


def _attn_prompt_layer(x, g, wqkv, wo, bias, sinks):
    b, t, d = x.shape
    tq = min(512, t)
    kern = functools.partial(_attn_prompt_kernel, tq=tq)
    return pl.pallas_call(
        kern,
        grid=(b, t // tq),
        in_specs=[
            pl.BlockSpec(memory_space=pltpu.SMEM),
            pl.BlockSpec((1, tq, d), lambda bi, i: (bi, i, 0)),
            _full((1, d)),
            _full((d, QKV_W)),
            _full((d, d)),
            _full((2, N_UNITS, UNIT_ROWS, 2 * WINDOW)),
        ],
        out_specs=[
            pl.BlockSpec((1, tq, d), lambda bi, i: (bi, i, 0)),
            pl.BlockSpec((1, WINDOW, KV_W), lambda bi, i: (bi, 0, 0)),
            pl.BlockSpec((1, WINDOW, KV_W), lambda bi, i: (bi, 0, 0)),
        ],
        out_shape=[
            jax.ShapeDtypeStruct((b, t, d), F32),
            jax.ShapeDtypeStruct((b, WINDOW, KV_W), F32),
            jax.ShapeDtypeStruct((b, WINDOW, KV_W), F32),
        ],
        scratch_shapes=[
            pltpu.VMEM((WINDOW + tq, 2 * KV_W), F32),
            pltpu.VMEM((tq, d), BF16),
            pltpu.VMEM((tq, d), BF16),
        ],
        compiler_params=_cparams(("arbitrary", "arbitrary")),
        name="attn_prompt",
    )(sinks, x, g, wqkv, wo, bias)


def _norm_linear_kernel(x_ref, g_ref, w_ref, o_ref):
    h = _rms(x_ref[...], g_ref[...]).astype(BF16)
    o_ref[...] = jnp.dot(h, w_ref[...], preferred_element_type=F32)


def _norm_linear(x, g, w):
    n, d = x.shape
    return pl.pallas_call(
        _norm_linear_kernel,
        out_shape=jax.ShapeDtypeStruct((n, w.shape[1]), F32),
        compiler_params=_cparams(None),
        name="norm_linear",
    )(x, g, w)


def _linear_residual_kernel(x_ref, o_ref, w_ref, xo_ref):
    xo_ref[...] = x_ref[...] + jnp.dot(o_ref[...].astype(BF16), w_ref[...], preferred_element_type=F32)


def _linear_residual(x, o, w):
    return pl.pallas_call(
        _linear_residual_kernel,
        out_shape=jax.ShapeDtypeStruct(x.shape, F32),
        compiler_params=_cparams(None),
        name="linear_residual",
    )(x, o, w)


def _attn_sample_kernel(sink_ref, q_ref, ck_ref, cv_ref, kn_ref, vn_ref, bias_ref, o_ref, kw_ref, vw_ref):
    kw = jnp.concatenate([ck_ref[:, 1:, :], kn_ref[...]], axis=1)
    vw = jnp.concatenate([cv_ref[:, 1:, :], vn_ref[...]], axis=1)
    kw_ref[...] = kw
    vw_ref[...] = vw
    q = (q_ref[...] * (HEAD_DIM ** -0.5)).astype(BF16)
    sc = jnp.einsum('bhc,bjc->bhj', q, kw.astype(BF16), preferred_element_type=F32)
    sc = sc + bias_ref[...][None]
    sink = sink_ref[...][None]
    m = jnp.maximum(jnp.max(sc, axis=-1, keepdims=True), sink)
    e = jnp.exp(sc - m)
    den = jnp.sum(e, axis=-1, keepdims=True) + jnp.exp(sink - m)
    pr = (e / den).astype(BF16)
    o_ref[...] = jnp.einsum('bhj,bjc->bhc', pr, vw.astype(BF16), preferred_element_type=F32)


def _attn_sample(q3, ck, cv, kn, vn, bias_s, sinks_col):
    n = q3.shape[0]
    bb = 8
    blk = lambda s: pl.BlockSpec((bb,) + s, lambda i: (i, 0, 0))
    return pl.pallas_call(
        _attn_sample_kernel,
        grid=(n // bb,),
        in_specs=[_full((N_HEADS, 1)), blk((N_HEADS, KV_W)), blk((WINDOW, KV_W)), blk((WINDOW, KV_W)),
                  blk((1, KV_W)), blk((1, KV_W)), _full((N_HEADS, WINDOW))],
        out_specs=[blk((N_HEADS, KV_W)), blk((WINDOW, KV_W)), blk((WINDOW, KV_W))],
        out_shape=[jax.ShapeDtypeStruct((n, N_HEADS, KV_W), F32),
                   jax.ShapeDtypeStruct((n, WINDOW, KV_W), F32),
                   jax.ShapeDtypeStruct((n, WINDOW, KV_W), F32)],
        compiler_params=_cparams(("arbitrary",)),
        name="attn_sample",
    )(sinks_col, q3, ck, cv, kn, vn, bias_s)


def _attn_sample_layer(x, g, wqkv, wo, bias_s, sinks, cache_k, cache_v):
    n = x.shape[0]
    qkv = _norm_linear(x, g, wqkv)
    q = qkv[:, :D_MODEL].reshape(n, N_KV_HEADS, GQ, 1, HEAD_DIM)
    place = jnp.eye(N_KV_HEADS, dtype=F32).reshape(1, N_KV_HEADS, 1, N_KV_HEADS, 1)
    q3 = (q * place).reshape(n, N_HEADS, KV_W)
    kn = qkv[:, D_MODEL:D_MODEL + KV_W].reshape(n, 1, KV_W)
    vn = qkv[:, D_MODEL + KV_W:].reshape(n, 1, KV_W)
    o3, kw, vw = _attn_sample(q3, cache_k.reshape(n, WINDOW, KV_W), cache_v.reshape(n, WINDOW, KV_W),
                              kn, vn, bias_s, sinks.reshape(N_HEADS, 1))
    o5 = o3.reshape(n, N_KV_HEADS, GQ, N_KV_HEADS, HEAD_DIM)
    o = jnp.stack([o5[:, gi, :, gi, :] for gi in range(N_KV_HEADS)], axis=1).reshape(n, D_MODEL)
    return _linear_residual(x, o, wo), kw, vw


def _ssm_prep_kernel(ar_ref, ai_ref, ldt_ref, br_ref, bi_ref, c_ref, a_out, bc_out, wb_out, wc_out, wc2_out):
    ar, ai = ar_ref[...], ai_ref[...]
    dt = jnp.exp(ldt_ref[...])
    mag = jnp.exp(ar * dt)
    lr = mag * jnp.cos(ai * dt)
    li = mag * jnp.sin(ai * dt)
    den = ar * ar + ai * ai
    cr = ((lr - 1.0) * ar + li * ai) / den
    ci = (li * ar - (lr - 1.0) * ai) / den
    row = lax.broadcasted_iota(jnp.int32, (SUBLANES, N_STATES), 0)
    a_out[...] = jnp.broadcast_to(lr, (SUBLANES, N_STATES))
    bc_out[...] = jnp.where(row < SUBLANES // 2, -li, li)
    for j in range(N_SLABS):
        crj = cr[:, SLAB_STATES * j:SLAB_STATES * (j + 1)]
        cij = ci[:, SLAB_STATES * j:SLAB_STATES * (j + 1)]
        br, bi = br_ref[j], bi_ref[j]
        wb_out[j, :, :SLAB_STATES] = (crj * br - cij * bi).astype(BF16)
        wb_out[j, :, SLAB_STATES:] = (crj * bi + cij * br).astype(BF16)
        wc_out[j, :SLAB_STATES, :] = c_ref[j, :SLAB_STATES, :].astype(BF16)
        wc_out[j, SLAB_STATES:, :] = (-c_ref[j, SLAB_STATES:, :]).astype(BF16)
        wc2_out[j, :, :LANES] = c_ref[j, :SLAB_STATES, :].astype(BF16)
        wc2_out[j, :, LANES:] = (-c_ref[j, SLAB_STATES:, :]).astype(BF16)


def _ssm_prep(a_re, a_im, log_dt, b_re, b_im, c_re, c_im):
    eye = jnp.eye(SUBLANES, dtype=F32)

    def blk_b(b):
        t = b.reshape(N_SLABS, 8, SSM_STATE, SSM_GROUP).transpose(0, 1, 3, 2)
        return jnp.einsum('jghp,gk->jghkp', t, eye).reshape(N_SLABS, LANES, SLAB_STATES)

    def blk_c(c):
        t = c.reshape(N_SLABS, 8, SSM_GROUP, SSM_STATE).transpose(0, 1, 3, 2)
        return jnp.einsum('jgph,gk->jgpkh', t, eye).reshape(N_SLABS, SLAB_STATES, LANES)

    row = lambda a: a.reshape(1, N_STATES)
    ldt = jnp.broadcast_to(log_dt[:, None], (N_SSM_GROUPS, SSM_STATE))
    c_all = jnp.concatenate([blk_c(c_re), blk_c(c_im)], axis=1)
    return pl.pallas_call(
        _ssm_prep_kernel,
        out_shape=[jax.ShapeDtypeStruct((SUBLANES, N_STATES), F32),
                   jax.ShapeDtypeStruct((SUBLANES, N_STATES), F32),
                   jax.ShapeDtypeStruct((N_SLABS, LANES, 2 * SLAB_STATES), BF16),
                   jax.ShapeDtypeStruct((N_SLABS, 2 * SLAB_STATES, LANES), BF16),
                   jax.ShapeDtypeStruct((N_SLABS, SLAB_STATES, 2 * LANES), BF16)],
        compiler_params=_cparams(None),
        name="ssm_prep",
    )(row(a_re), row(a_im), row(ldt), blk_b(b_re), blk_b(b_im), c_all)


def _glu_residual(x, y, u, d, wa, wg):
    z = _gelu_tanh(y + d * u).astype(BF16)
    out = jnp.dot(z, wa, preferred_element_type=F32) * _sigmoid(jnp.dot(z, wg, preferred_element_type=F32))
    return x + out


def _ssm_prompt_kernel(x_ref, g_ref, wb_ref, wc_ref, a_ref, bc_ref, d_ref, wa_ref, wg_ref,
                       xo_ref, st_ref, z_scr, u_scr, o_scr, xs_scr, *, nb, lc):
    i = pl.program_id(0)
    blocks_per_slab = SLAB_STATES // LANES
    half = SUBLANES // 2
    pairs = lc // 2
    rows = half * lc
    roll_half = lambda v: pltpu.roll(v, half, 1)

    @pl.when(i == 0)
    def _():
        xs_scr[...] = jnp.zeros((SUBLANES, N_STATES), F32)
        if nb < half:
            u_scr[...] = jnp.zeros(u_scr.shape, F32)

    for b in range(nb):
        u = _rms(x_ref[b], g_ref[...])
        for c in range(N_SLABS):
            u_scr[c, pl.ds(b, lc, stride=half), :] = u[:, LANES * c:LANES * (c + 1)]

    lower = lax.broadcasted_iota(jnp.int32, (pairs, SUBLANES, LANES), 1) < half
    for j in range(N_SLABS):
        res = jnp.dot(u_scr[j].astype(BF16), wb_ref[j], preferred_element_type=F32)
        for c in range(blocks_per_slab):
            re = res[:, LANES * c:LANES * (c + 1)].reshape(pairs, SUBLANES, LANES)
            im = res[:, SLAB_STATES + LANES * c:SLAB_STATES + LANES * (c + 1)].reshape(pairs, SUBLANES, LANES)
            blk = blocks_per_slab * j + c
            z_scr[blk, :, 0] = jnp.where(lower, re, roll_half(im))
            z_scr[blk, :, 1] = jnp.where(lower, roll_half(re), im)

    group = 8
    for cb in range(N_STATES // LANES // group):
        blks = [group * cb + k for k in range(group)]
        a = [a_ref[:, LANES * q:LANES * (q + 1)] for q in blks]
        bc = [bc_ref[:, LANES * q:LANES * (q + 1)] for q in blks]
        x0 = tuple(xs_scr[:, LANES * q:LANES * (q + 1)] for q in blks)

        def step2(tp, xs, blks=blks, a=a, bc=bc):
            for par in range(2):
                out = []
                for k, q in enumerate(blks):
                    xn = a[k] * xs[k] + bc[k] * pltpu.roll(xs[k], half, 0) + z_scr[q, tp, par]
                    z_scr[q, tp, par] = xn
                    out.append(xn)
                xs = tuple(out)
            return xs

        xf = lax.fori_loop(0, pairs, step2, x0)
        for k, q in enumerate(blks):
            xs_scr[:, LANES * q:LANES * (q + 1)] = xf[k]
    st_ref[...] = xs_scr[...]

    ys = []
    for j in range(N_SLABS):
        s = jnp.concatenate([z_scr[blocks_per_slab * j + c].reshape(SUBLANES * lc, LANES)
                             for c in range(blocks_per_slab)], axis=1).astype(BF16)
        p3 = jnp.dot(s, wc_ref[j], preferred_element_type=F32).reshape(lc, SUBLANES, 2 * LANES)
        y4 = (p3[:, :, :LANES] + roll_half(p3[:, :, LANES:])).reshape(pairs, 2, SUBLANES, LANES)
        ys.append(jnp.where(lower, y4[:, 0], roll_half(y4[:, 1])).reshape(rows, LANES))
    y = jnp.concatenate(ys, axis=1)
    u_all = jnp.concatenate([u_scr[c] for c in range(N_SLABS)], axis=1)
    z = _gelu_tanh(y + d_ref[...] * u_all).astype(BF16)
    out = (jnp.dot(z, wa_ref[...], preferred_element_type=F32)
           * _sigmoid(jnp.dot(z, wg_ref[...], preferred_element_type=F32)))
    for c in range(N_SLABS):
        o_scr[c] = out[:, LANES * c:LANES * (c + 1)]
    for b in range(nb):
        xo_ref[b] = x_ref[b] + jnp.concatenate(
            [o_scr[c, pl.ds(b, lc, stride=half), :] for c in range(N_SLABS)], axis=1)


def _ssm_prompt_layer(x, g, prep, d, wa, wg):
    a_rows, bc_rows, wb, _, wc2 = prep
    nb, t, dm = x.shape
    half = SUBLANES // 2
    assert nb <= half
    lc = min(128, t)
    kern = functools.partial(_ssm_prompt_kernel, nb=nb, lc=lc)
    return pl.pallas_call(
        kern,
        grid=(t // lc,),
        in_specs=[
            pl.BlockSpec((nb, lc, dm), lambda i: (0, i, 0)),
            _full((1, dm)),
            _full(wb.shape), _full(wc2.shape),
            _full((SUBLANES, N_STATES)), _full((SUBLANES, N_STATES)),
            _full((1, dm)), _full((dm, dm)), _full((dm, dm)),
        ],
        out_specs=[pl.BlockSpec((nb, lc, dm), lambda i: (0, i, 0)), _full((SUBLANES, N_STATES))],
        out_shape=[jax.ShapeDtypeStruct((nb, t, dm), F32), jax.ShapeDtypeStruct((SUBLANES, N_STATES), F32)],
        scratch_shapes=[
            pltpu.VMEM((N_STATES // LANES, lc // 2, 2, SUBLANES, LANES), F32),
            pltpu.VMEM((N_SLABS, half * lc, LANES), F32),
            pltpu.VMEM((N_SLABS, half * lc, LANES), F32),
            pltpu.VMEM((SUBLANES, N_STATES), F32),
        ],
        compiler_params=_cparams(("arbitrary",)),
        name="ssm_prompt",
    )(x, g, wb, wc2, a_rows, bc_rows, d, wa, wg)


def _ssm_sample_kernel(x_ref, g_ref, wb_ref, wc_ref, a_ref, bc_ref, h0r_ref, h0i_ref, d_ref, wa_ref, wg_ref,
                       xo_ref, sr_ref, si_ref):
    x = x_ref[...]
    u = _rms(x, g_ref[...])
    ub = u.astype(BF16)
    ys = []
    for j in range(N_SLABS):
        sl = slice(SLAB_STATES * j, SLAB_STATES * (j + 1))
        res = jnp.dot(ub[:, LANES * j:LANES * (j + 1)], wb_ref[j], preferred_element_type=F32)
        lr = a_ref[0:1, sl]
        li = bc_ref[SUBLANES - 1:SUBLANES, sl]
        h0r, h0i = h0r_ref[:, sl], h0i_ref[:, sl]
        xr = res[:, :SLAB_STATES] + (lr * h0r - li * h0i)
        xi = res[:, SLAB_STATES:] + (lr * h0i + li * h0r)
        sr_ref[:, sl] = xr
        si_ref[:, sl] = xi
        s = jnp.concatenate([xr, xi], axis=1).astype(BF16)
        ys.append(jnp.dot(s, wc_ref[j], preferred_element_type=F32))
    y = jnp.concatenate(ys, axis=1)
    xo_ref[...] = _glu_residual(x, y, u, d_ref[...], wa_ref[...], wg_ref[...])


def _ssm_sample_layer(x, g, prep, h0r, h0i, d, wa, wg):
    a_rows, bc_rows, wb, wc, _ = prep
    n = x.shape[0]
    return pl.pallas_call(
        _ssm_sample_kernel,
        out_shape=[jax.ShapeDtypeStruct(x.shape, F32),
                   jax.ShapeDtypeStruct((n, N_STATES), F32),
                   jax.ShapeDtypeStruct((n, N_STATES), F32)],
        compiler_params=_cparams(None),
        name="ssm_sample",
    )(x, g, wb, wc, a_rows, bc_rows, h0r, h0i, d, wa, wg)


ROUTER_W = LANES
EXPERT_LANE0 = N_EXPERT_GROUPS


def _route_topk(h, wr, br):
    h_hi = h.astype(BF16)
    h_lo = (h - h_hi.astype(F32)).astype(BF16)
    w_hi = wr.astype(BF16)
    w_lo = (wr - w_hi.astype(F32)).astype(BF16)
    dot = lambda a, b: jnp.dot(a, b, preferred_element_type=F32)
    both = dot(h_hi, jnp.concatenate([w_hi, w_lo], axis=1))
    logits = both[:, :ROUTER_W] + (dot(h_lo, w_hi) + both[:, ROUTER_W:]) + br
    lane_i = lax.broadcasted_iota(jnp.int32, logits.shape, 1)
    lane = lane_i.astype(F32)
    neg = -jnp.inf
    past_end = float(ROUTER_W)
    gl = jnp.where(lane_i < N_EXPERT_GROUPS, logits, neg)
    gmax = jnp.max(gl, axis=-1, keepdims=True)
    g_sel = jnp.min(jnp.where(gl == gmax, lane, past_end), axis=-1, keepdims=True)
    p_grp = 1.0 / jnp.sum(jnp.exp(gl - gmax), axis=-1, keepdims=True)
    lane_grp = ((lane_i - EXPERT_LANE0) // EXPERTS_PER_GROUP).astype(F32)
    in_grp = (lane_i >= EXPERT_LANE0) & (lane_i < EXPERT_LANE0 + N_EXPERTS) & (lane_grp == g_sel)
    el = jnp.where(in_grp, logits, neg)
    v1 = jnp.max(el, axis=-1, keepdims=True)
    i1 = jnp.min(jnp.where(el == v1, lane, past_end), axis=-1, keepdims=True)
    el2 = jnp.where(lane == i1, neg, el)
    v2 = jnp.max(el2, axis=-1, keepdims=True)
    i2 = jnp.min(jnp.where(el2 == v2, lane, past_end), axis=-1, keepdims=True)
    e2 = jnp.exp(v2 - v1)
    w1 = p_grp / (1.0 + e2)
    w2 = p_grp * e2 / (1.0 + e2)
    return lane, i1, i2, w1, w2


def _route(h, wr, br):
    lane, i1, i2, w1, w2 = _route_topk(h, wr, br)
    return jnp.where(lane == i1, w1, 0.0) + jnp.where(lane == i2, w2, 0.0)


def _moe_kernel(x_ref, p_ref, gffn_ref, wr_ref, br_ref, wg_ref, wu_ref, wd_ref,
                gple_ref, wpg_ref, wpp_ref, gfin_ref, o_ref, hn_scr, gates_scr, acc_scr, *, final):
    e = pl.program_id(1)

    @pl.when(e == 0)
    def _():
        x = x_ref[...]
        h = _rms(x, gffn_ref[...])
        hn_scr[...] = h.astype(BF16)
        gates_scr[...] = _route(h, wr_ref[...], br_ref[...])
        acc_scr[...] = x

    hb = hn_scr[...]
    gt = jnp.dot(hb, wg_ref[0].astype(BF16), preferred_element_type=F32)
    up = jnp.dot(hb, wu_ref[0].astype(BF16), preferred_element_type=F32)
    a = (gt * _sigmoid(gt) * up).astype(BF16)
    gates = gates_scr[...]
    lane = lax.broadcasted_iota(jnp.int32, gates.shape, 1)
    ge = jnp.sum(jnp.where(lane == e + EXPERT_LANE0, gates, 0.0), axis=-1, keepdims=True)
    acc_scr[...] += ge * jnp.dot(a, wd_ref[0].astype(BF16), preferred_element_type=F32)

    @pl.when(e == pl.num_programs(1) - 1)
    def _():
        x1 = acc_scr[...]
        gate = _sigmoid(jnp.dot(_rms(x1, gple_ref[...]).astype(BF16), wpg_ref[...], preferred_element_type=F32))
        x2 = x1 + gate * jnp.dot(p_ref[...].astype(BF16), wpp_ref[...], preferred_element_type=F32)
        o_ref[...] = _rms(x2, gfin_ref[...]) if final else x2


def _moe_ple_layer(li, x, p, gffn, wr, br, wg, wu, wd, gple, wpg, wpp, gfin, final):
    n, d = x.shape
    tm = min(1024, n)
    kern = functools.partial(_moe_kernel, final=final)
    tok = lambda w: pl.BlockSpec((tm, w), lambda t, e: (t, 0))
    const = lambda s: pl.BlockSpec(s, lambda t, e: (0,) * len(s))
    return pl.pallas_call(
        kern,
        grid=(n // tm, N_EXPERTS),
        in_specs=[
            tok(d), pl.BlockSpec((None, tm, PLE_DIM), lambda t, e: (li, t, 0)),
            const((1, d)), const((d, ROUTER_W)), const((1, ROUTER_W)),
            pl.BlockSpec((None, 1, d, D_EXPERT), lambda t, e: (li, e, 0, 0)),
            pl.BlockSpec((None, 1, d, D_EXPERT), lambda t, e: (li, e, 0, 0)),
            pl.BlockSpec((None, 1, D_EXPERT, d), lambda t, e: (li, e, 0, 0)),
            const((1, d)), const((d, d)), const((PLE_DIM, d)), const((1, d)),
        ],
        out_specs=tok(d),
        out_shape=jax.ShapeDtypeStruct((n, d), F32),
        scratch_shapes=[pltpu.VMEM((tm, d), BF16), pltpu.VMEM((tm, ROUTER_W), F32), pltpu.VMEM((tm, d), F32)],
        compiler_params=_cparams(("arbitrary", "arbitrary")),
        name="moe_ple",
    )(x, p, gffn, wr, br, wg, wu, wd, gple, wpg, wpp, gfin)


TOKEN_DATA_ROWS = D_MODEL // LANES
TOKEN_PITCH = TOKEN_DATA_ROWS + 1
INFO_W1, INFO_W2, INFO_E1, INFO_E2, INFO_R1, INFO_R2 = range(6)
EXPERT_TILE = 256
ZERO_RUN = 256


def _to_token_tiles(ref_2d, val, n, first=0):
    for s in range(TOKEN_DATA_ROWS):
        ref_2d[pl.ds(first * TOKEN_PITCH + s, n, stride=TOKEN_PITCH), :] = val[:, LANES * s:LANES * (s + 1)]


def _zero_spare_rows(ref_2d, n):
    ref_2d[pl.ds(TOKEN_DATA_ROWS, n, stride=TOKEN_PITCH), :] = jnp.zeros((n, LANES), F32)


def _from_token_tiles(ref_2d, n, first=0):
    return jnp.concatenate([ref_2d[pl.ds(first * TOKEN_PITCH + s, n, stride=TOKEN_PITCH), :]
                            for s in range(TOKEN_DATA_ROWS)], axis=1)


def _token_slots(i, n=1):
    return pl.ds(i * TOKEN_PITCH, n * TOKEN_PITCH)


def _route_kernel(x_ref, g_ref, wr_ref, br_ref, before_ref, info_ref, cnt_ref, base_scr):
    i = pl.program_id(0)

    @pl.when(i == 0)
    def _():
        base_scr[...] = jnp.zeros(base_scr.shape, F32)

    h = _rms(x_ref[...], g_ref[...])
    lane, i1, i2, w1, w2 = _route_topk(h, wr_ref[...], br_ref[...])
    chosen = jnp.where((lane == i1) | (lane == i2), 1.0, 0.0)
    prefix = jnp.dot(before_ref[...], chosen.astype(BF16), preferred_element_type=F32) + base_scr[...]
    rank1 = jnp.sum(jnp.where(lane == i1, prefix, 0.0), axis=-1, keepdims=True)
    rank2 = jnp.sum(jnp.where(lane == i2, prefix, 0.0), axis=-1, keepdims=True)
    base_scr[...] += jnp.sum(chosen, axis=0, keepdims=True)
    cnt_ref[...] = base_scr[...]
    e1 = (i1 - EXPERT_LANE0).astype(F32)
    e2 = (i2 - EXPERT_LANE0).astype(F32)
    info = jnp.zeros(h.shape[:1] + (ROUTER_W,), F32)
    for ln, v in ((INFO_W1, w1), (INFO_W2, w2), (INFO_E1, e1), (INFO_E2, e2), (INFO_R1, rank1), (INFO_R2, rank2)):
        info = jnp.where(lane == ln, v, info)
    info_ref[...] = info


def _route_call(x, gffn, wr, br):
    n, d = x.shape
    tm = 512
    before = jnp.asarray(np.tril(np.ones((tm, tm), np.float32), -1), BF16)
    return pl.pallas_call(
        _route_kernel,
        grid=(n // tm,),
        in_specs=[pl.BlockSpec((tm, d), lambda i: (i, 0)), _full((1, d)), _full((d, ROUTER_W)), _full((1, ROUTER_W)),
                  _full((tm, tm))],
        out_specs=[pl.BlockSpec((tm, ROUTER_W), lambda i: (i, 0)), _full((1, ROUTER_W))],
        out_shape=[jax.ShapeDtypeStruct((n, ROUTER_W), F32), jax.ShapeDtypeStruct((1, ROUTER_W), F32)],
        scratch_shapes=[pltpu.VMEM((1, ROUTER_W), F32)],
        compiler_params=_cparams(("arbitrary",)),
        name="moe_route",
    )(x, gffn, wr, br, before)


def _dispatch_kernel(pos_ref, cnt_ref, start_ref, x_ref, g_ref, xs_hbm, hbuf, zbuf, sem, zsem,
                     *, td, n_steps, n_pad_rows):
    i = pl.program_id(0)
    slot = i % 2

    def wait_slot(sl):
        for _ in range(2):
            pltpu.make_async_copy(hbuf.at[sl], hbuf.at[sl], sem.at[sl]).wait()

    @pl.when(i >= 2)
    def _():
        wait_slot(slot)

    @pl.when(i < 2)
    def _():
        _zero_spare_rows(hbuf.at[slot], td)

    _to_token_tiles(hbuf.at[slot], _rms(x_ref[...], g_ref[...]), td)

    def issue(n8, carry):
        for u in range(SUBLANES):
            n = n8 * SUBLANES + u
            t = i * td + n
            src = hbuf.at[slot, _token_slots(n), :]
            pltpu.make_async_copy(src, xs_hbm.at[_token_slots(pos_ref[0, t]), :], sem.at[slot]).start(priority=0)
            pltpu.make_async_copy(src, xs_hbm.at[_token_slots(pos_ref[1, t]), :], sem.at[slot]).start(priority=1)
        return carry

    lax.fori_loop(0, td // SUBLANES, issue, 0)

    @pl.when(i == 0)
    def _():
        zbuf[...] = jnp.zeros(zbuf.shape, zbuf.dtype)

        def zero_run(first, n_slots):
            pltpu.make_async_copy(zbuf.at[_token_slots(0, n_slots), :], xs_hbm.at[_token_slots(first, n_slots), :],
                                  zsem).start()

        for e in range(N_EXPERTS + 1):
            lo = start_ref[e] + cnt_ref[e]
            n = start_ref[e + 1] - lo

            def full_run(k, carry, lo=lo):
                zero_run(lo + k * ZERO_RUN, ZERO_RUN)
                return carry

            lax.fori_loop(0, n // ZERO_RUN, full_run, 0)
            cur = lo + (n // ZERO_RUN) * ZERO_RUN
            p = ZERO_RUN // 2
            while p >= 1:
                @pl.when((n & p) != 0)
                def _(cur=cur, p=p):
                    zero_run(cur, p)

                cur = cur + (n & p)
                p //= 2

    @pl.when(i == n_steps - 1)
    def _():
        wait_slot(slot)
        if n_steps > 1:
            wait_slot(1 - slot)
        pad = xs_hbm.at[_token_slots(0, n_pad_rows), :]
        pltpu.make_async_copy(pad, pad, zsem).wait()


def _dispatch_call(pos, cnt, start, x, gffn, a_pad, n_pad_rows):
    n, d = x.shape
    td = 512
    n_steps = n // td
    kern = functools.partial(_dispatch_kernel, td=td, n_steps=n_steps, n_pad_rows=n_pad_rows)
    return pl.pallas_call(
        kern,
        grid_spec=pltpu.PrefetchScalarGridSpec(
            num_scalar_prefetch=3,
            grid=(n_steps,),
            in_specs=[pl.BlockSpec((td, d), lambda i, *_: (i, 0)), pl.BlockSpec((1, d), lambda i, *_: (0, 0))],
            out_specs=pl.BlockSpec(memory_space=pl.ANY),
            scratch_shapes=[pltpu.VMEM((2, td * TOKEN_PITCH, LANES), F32),
                            pltpu.VMEM((ZERO_RUN * TOKEN_PITCH, LANES), F32),
                            pltpu.SemaphoreType.DMA((2,)), pltpu.SemaphoreType.DMA(())],
        ),
        out_shape=jax.ShapeDtypeStruct((a_pad * TOKEN_PITCH, LANES), F32),
        compiler_params=_cparams(("arbitrary",)),
        name="moe_dispatch",
    )(pos, cnt, start, x, gffn)


EXPERT_IN_SLOTS = 3


def _experts_kernel(te_ref, xs_hbm, wg_ref, wu_ref, wd_ref, y_ref, xbuf, xsem, wg_scr, wu_scr, wd_scr,
                    *, te, n_tiles):
    t = pl.program_id(0)
    rows = te * TOKEN_PITCH

    def fetch(tile):
        slot = tile % EXPERT_IN_SLOTS
        src = xs_hbm.at[pl.ds(pl.multiple_of(tile * rows, rows), rows), :]
        return pltpu.make_async_copy(src, xbuf.at[slot], xsem.at[slot])

    @pl.when(t == 0)
    def _():
        for ahead in range(min(EXPERT_IN_SLOTS - 1, n_tiles)):
            fetch(ahead).start()

    @pl.when(t + EXPERT_IN_SLOTS - 1 < n_tiles)
    def _():
        fetch(t + EXPERT_IN_SLOTS - 1).start()

    @pl.when((t == 0) | (te_ref[t] != te_ref[jnp.maximum(t - 1, 0)]))
    def _():
        wg_scr[...] = wg_ref[0].astype(BF16)
        wu_scr[...] = wu_ref[0].astype(BF16)
        wd_scr[...] = wd_ref[0].astype(BF16)

    fetch(t).wait()
    x = _from_token_tiles(xbuf.at[t % EXPERT_IN_SLOTS], te).astype(BF16)
    gt = jnp.dot(x, wg_scr[...], preferred_element_type=F32)
    up = jnp.dot(x, wu_scr[...], preferred_element_type=F32)
    a = (gt * _sigmoid(gt) * up).astype(BF16)
    _to_token_tiles(y_ref, jnp.dot(a, wd_scr[...], preferred_element_type=F32), te)
    _zero_spare_rows(y_ref, te)


def _experts_call(li, tile_expert, xs, wg, wu, wd, te):
    rows = te * TOKEN_PITCH
    n_tiles = xs.shape[0] // rows
    d = wg.shape[2]
    kern = functools.partial(_experts_kernel, te=te, n_tiles=n_tiles)
    return pl.pallas_call(
        kern,
        grid_spec=pltpu.PrefetchScalarGridSpec(
            num_scalar_prefetch=1,
            grid=(n_tiles,),
            in_specs=[pl.BlockSpec(memory_space=pl.ANY),
                      pl.BlockSpec((None, 1, d, D_EXPERT), lambda t, ex: (li, ex[t], 0, 0)),
                      pl.BlockSpec((None, 1, d, D_EXPERT), lambda t, ex: (li, ex[t], 0, 0)),
                      pl.BlockSpec((None, 1, D_EXPERT, d), lambda t, ex: (li, ex[t], 0, 0))],
            out_specs=pl.BlockSpec((rows, LANES), lambda t, ex: (t, 0)),
            scratch_shapes=[pltpu.VMEM((EXPERT_IN_SLOTS, rows, LANES), F32),
                            pltpu.SemaphoreType.DMA((EXPERT_IN_SLOTS,)),
                            pltpu.VMEM((d, D_EXPERT), BF16), pltpu.VMEM((d, D_EXPERT), BF16),
                            pltpu.VMEM((D_EXPERT, d), BF16)],
        ),
        out_shape=jax.ShapeDtypeStruct((n_tiles * rows, LANES), F32),
        compiler_params=_cparams(("arbitrary",)),
        name="moe_experts",
    )(tile_expert, xs, wg, wu, wd)


def _combine_kernel(pos_ref, x_ref, info_ref, p_ref, gple_ref, wpg_ref, wpp_ref, gfin_ref, ys_hbm,
                    o_ref, ybuf, sem, *, tm, n_steps, final):
    i = pl.program_id(0)
    slot = i % 2

    def issue(step, sl):
        def body(n8, carry):
            for u in range(SUBLANES):
                n = n8 * SUBLANES + u
                t = step * tm + n
                for k in range(2):
                    pltpu.make_async_copy(ys_hbm.at[_token_slots(pos_ref[k, t]), :],
                                          ybuf.at[sl, k, _token_slots(n), :], sem.at[sl]).start(priority=k)
            return carry

        lax.fori_loop(0, tm // SUBLANES, body, 0)

    @pl.when(i == 0)
    def _():
        issue(0, 0)

    @pl.when(i + 1 < n_steps)
    def _():
        issue(i + 1, 1 - slot)

    for k in range(2):
        pltpu.make_async_copy(ybuf.at[slot, k], ybuf.at[slot, k], sem.at[slot]).wait()
    part = tm // 2
    for first in (0, part):
        tok = slice(first, first + part)
        y1 = _from_token_tiles(ybuf.at[slot, 0], part, first)
        y2 = _from_token_tiles(ybuf.at[slot, 1], part, first)
        info = info_ref[tok, :]
        x1 = x_ref[tok, :] + (info[:, INFO_W1:INFO_W1 + 1] * y1 + info[:, INFO_W2:INFO_W2 + 1] * y2)
        gate = _sigmoid(jnp.dot(_rms(x1, gple_ref[...]).astype(BF16), wpg_ref[...], preferred_element_type=F32))
        x2 = x1 + gate * jnp.dot(p_ref[tok, :].astype(BF16), wpp_ref[...], preferred_element_type=F32)
        o_ref[tok, :] = _rms(x2, gfin_ref[...]) if final else x2


def _combine_call(li, pos, x, info, p, gple, wpg, wpp, gfin, ys, final):
    n, d = x.shape
    tm = 512
    n_steps = n // tm
    kern = functools.partial(_combine_kernel, tm=tm, n_steps=n_steps, final=final)
    tok = lambda w: pl.BlockSpec((tm, w), lambda i, *_: (i, 0))
    const = lambda s: pl.BlockSpec(s, lambda i, *_: (0,) * len(s))
    return pl.pallas_call(
        kern,
        grid_spec=pltpu.PrefetchScalarGridSpec(
            num_scalar_prefetch=1,
            grid=(n_steps,),
            in_specs=[tok(d), tok(ROUTER_W), pl.BlockSpec((None, tm, PLE_DIM), lambda i, *_: (li, i, 0)),
                      const((1, d)), const((d, d)), const((PLE_DIM, d)),
                      const((1, d)), pl.BlockSpec(memory_space=pl.ANY)],
            out_specs=tok(d),
            scratch_shapes=[pltpu.VMEM((2, 2, tm * TOKEN_PITCH, LANES), F32), pltpu.SemaphoreType.DMA((2,))],
        ),
        out_shape=jax.ShapeDtypeStruct((n, d), F32),
        compiler_params=_cparams(("arbitrary",)),
        name="moe_combine",
    )(pos, x, info, p, gple, wpg, wpp, gfin, ys)


def _positions_kernel(info_ref, start_ref, pos_ref):
    info = info_ref[...]
    lane = lax.broadcasted_iota(jnp.int32, info.shape, 1)
    lane_expert = (lane - EXPERT_LANE0).astype(F32)
    start = start_ref[...]
    cols = []
    for e_lane, r_lane in ((INFO_E1, INFO_R1), (INFO_E2, INFO_R2)):
        seg = jnp.sum(jnp.where(lane_expert == info[:, e_lane:e_lane + 1], start, 0.0), axis=-1, keepdims=True)
        cols.append(seg + info[:, r_lane:r_lane + 1])
    both = jnp.where(lane == 0, cols[0], jnp.where(lane == 1, cols[1], 0.0))
    pos_ref[...] = both.T[:SUBLANES, :].astype(jnp.int32)


def _positions_call(info, start_row):
    n = info.shape[0]
    tm = 2048 if n % 2048 == 0 else 512
    return pl.pallas_call(
        _positions_kernel,
        grid=(n // tm,),
        in_specs=[pl.BlockSpec((tm, ROUTER_W), lambda i: (i, 0)), _full((1, ROUTER_W))],
        out_specs=pl.BlockSpec((SUBLANES, tm), lambda i: (0, i)),
        out_shape=jax.ShapeDtypeStruct((SUBLANES, n), jnp.int32),
        compiler_params=_cparams(("arbitrary",)),
        name="moe_positions",
    )(info, start_row)


def _moe_routed_layer(li, x, p, gffn, wr, br, wg, wu, wd, gple, wpg, wpp, gfin, final):
    n = x.shape[0]
    te = EXPERT_TILE
    n_assign = 2 * n
    assert n_assign % te == 0
    a_pad = n_assign + N_EXPERTS * te
    info, cnt = _route_call(x, gffn, wr, br)
    counts = cnt[0, EXPERT_LANE0:EXPERT_LANE0 + N_EXPERTS].astype(jnp.int32)
    padded = (counts + te - 1) // te * te
    seg_end = jnp.cumsum(padded)
    seg_start = seg_end - padded
    start_row = jnp.zeros((1, ROUTER_W), F32).at[0, EXPERT_LANE0:EXPERT_LANE0 + N_EXPERTS].set(seg_start.astype(F32))
    pos = _positions_call(info, start_row)[:2]
    start = jnp.concatenate([seg_start, seg_end[-1:], jnp.full((1,), a_pad, jnp.int32)])
    cnt_pad = jnp.concatenate([counts, jnp.zeros((1,), jnp.int32)])
    tile_row0 = jnp.arange(a_pad // te, dtype=jnp.int32) * te
    tile_expert = jnp.minimum(jnp.sum(seg_end[None, :] <= tile_row0[:, None], axis=1), N_EXPERTS - 1).astype(jnp.int32)
    xs = _dispatch_call(pos, cnt_pad, start, x, gffn, a_pad, N_EXPERTS * te)
    ys = _experts_call(li, tile_expert, xs, wg, wu, wd, te)
    return _combine_call(li, pos, x, info, p, gple, wpg, wpp, gfin, ys, final)


def kernel(x_prompt, x_sample, cache_k_win, cache_v_win, state_ssm_re, state_ssm_im, p_prompt, p_sample,
           rel_bias_table, g_mix, w_qkv, w_o, sinks, ssm_a_re, ssm_a_im, ssm_log_dt, ssm_b_re, ssm_b_im,
           ssm_c_re, ssm_c_im, ssm_d, w_glu_a, w_glu_b, g_ffn, w_router_group, b_router_group,
           w_router_expert, b_router_expert, w_exp_gate, w_exp_up, w_exp_down, g_ple, w_ple_gate,
           w_ple_proj, g_final):
    bsz, seq, d = x_prompt.shape
    ns = x_sample.shape[0]
    depth = g_mix.shape[0]
    row = lambda v: v.reshape(1, -1).astype(F32)

    qi = np.arange(WINDOW)[:, None]
    sj = np.arange(2 * WINDOW)[None, :]
    dist = qi + WINDOW - sj
    idx_p = _t5_bucket_np(dist)
    ok_p = ((dist >= 0) & (dist < WINDOW)).astype(np.int32)
    idx_s = np.broadcast_to(_t5_bucket_np(WINDOW - 1 - np.arange(WINDOW))[None, :], (SUBLANES, WINDOW))
    bias_p = _band_bias(rel_bias_table, idx_p, ok_p)
    bias_s = _bias_from_table(rel_bias_table, idx_s)[:, 0, :]

    wqkv_b = w_qkv[0].astype(BF16)
    wo_b = w_o[0].astype(BF16)
    wpg_b, wpp_b = w_ple_gate.astype(BF16), w_ple_proj.astype(BF16)
    wa_b, wgl_b = w_glu_a[0].astype(BF16), w_glu_b[0].astype(BF16)
    pad = ROUTER_W - N_EXPERT_GROUPS - N_EXPERTS
    wr = jnp.concatenate([w_router_group, w_router_expert.reshape(depth, d, N_EXPERTS),
                          jnp.zeros((depth, d, pad), F32)], axis=-1)
    br = jnp.concatenate([b_router_group, b_router_expert.reshape(depth, N_EXPERTS),
                          jnp.zeros((depth, pad), F32)], axis=-1)

    prep = _ssm_prep(ssm_a_re[0], ssm_a_im[0], ssm_log_dt[0], ssm_b_re[0], ssm_b_im[0], ssm_c_re[0], ssm_c_im[0])
    d_row = row(ssm_d[0])

    def moe(x2d, p_all, i):
        layer = _moe_routed_layer if 2 * x2d.shape[0] >= N_EXPERTS * EXPERT_TILE else _moe_ple_layer
        return layer(i, x2d, p_all, row(g_ffn[i]), wr[i], br[i:i + 1], w_exp_gate, w_exp_up, w_exp_down,
                     row(g_ple[i]), wpg_b[i], wpp_b[i], row(g_final), final=(i == depth - 1))

    pp = p_prompt.reshape(depth, bsz * seq, PLE_DIM)
    xp, kp, vp = _attn_prompt_layer(x_prompt, row(g_mix[0]), wqkv_b, wo_b, bias_p, sinks[0])
    xp = moe(xp.reshape(bsz * seq, d), pp, 0).reshape(bsz, seq, d)
    xp, st_p = _ssm_prompt_layer(xp, row(g_mix[1]), prep, d_row, wa_b, wgl_b)
    y_prompt = moe(xp.reshape(bsz * seq, d), pp, 1).reshape(bsz, seq, d)

    ps = p_sample.reshape(depth, ns, PLE_DIM)
    xs = x_sample.reshape(ns, d)
    xs, kws, vws = _attn_sample_layer(xs, row(g_mix[0]), wqkv_b, wo_b, bias_s, sinks[0],
                                      cache_k_win[0], cache_v_win[0])
    xs = moe(xs, ps, 0)
    xs, sr_s, si_s = _ssm_sample_layer(xs, row(g_mix[1]), prep, state_ssm_re[0].reshape(ns, N_STATES),
                                       state_ssm_im[0].reshape(ns, N_STATES), d_row, wa_b, wgl_b)
    y_sample = moe(xs, ps, 1).reshape(ns, 1, d)

    half = SUBLANES // 2
    win = lambda a, n: a.reshape(1, n, WINDOW, N_KV_HEADS, HEAD_DIM)
    st = lambda a, n: a.reshape(1, n, N_SSM_GROUPS, SSM_STATE)
    return (y_prompt, y_sample, win(kp, bsz), win(vp, bsz), win(kws, ns), win(vws, ns),
            st(st_p[:bsz], bsz), st(st_p[half:half + bsz], bsz), st(sr_s, ns), st(si_s, ns))
```

```python
import functools
import math

import numpy as np
import jax
import jax.numpy as jnp
from jax import lax
from jax.experimental import pallas as pl
from jax.experimental.pallas import tpu as pltpu

F32 = jnp.float32
BF16 = jnp.bfloat16

D_MODEL = 1024
HEAD_DIM = 64
N_HEADS = 16
N_KV_HEADS = 2
GQ = N_HEADS // N_KV_HEADS
WINDOW = 128
N_BUCKETS = 32
MAX_EXACT = 16
MAX_DISTANCE = 128
SSM_GROUP = 16
N_SSM_GROUPS = 64
SSM_STATE = 64
N_STATES = N_SSM_GROUPS * SSM_STATE
N_EXPERT_GROUPS = 4
EXPERTS_PER_GROUP = 8
N_EXPERTS = 32
D_EXPERT = 256
PLE_DIM = 256
RMS_EPS = 1e-6
NEG_INF = -1e30

LANES = 128
SUBLANES = 8
KV_W = N_KV_HEADS * HEAD_DIM
QKV_W = D_MODEL + 2 * KV_W
SLAB_STATES = 512
N_SLABS = D_MODEL // LANES
VMEM_LIMIT = 56 * 1024 * 1024


def _cparams(sem):
    return pltpu.CompilerParams(dimension_semantics=sem, vmem_limit_bytes=VMEM_LIMIT)


def _rms(x, g):
    return x * lax.rsqrt(jnp.mean(x * x, axis=-1, keepdims=True) + RMS_EPS) * g


def _gelu_tanh(x):
    c = math.sqrt(2.0 / math.pi)
    return x * (0.5 * (1.0 + jnp.tanh(c * (x + 0.044715 * (x * x * x)))))


def _sigmoid(x):
    return 1.0 / (1.0 + jnp.exp(-x))


def _full(shape):
    n = len(shape)
    return pl.BlockSpec(shape, lambda *_: (0,) * n)


def _t5_bucket_np(dist):
    n = np.maximum(dist, 0)
    nf = np.maximum(n, 1).astype(np.float64)
    large = MAX_EXACT + (np.log(nf / MAX_EXACT) / math.log(MAX_DISTANCE / MAX_EXACT)
                         * (N_BUCKETS - MAX_EXACT)).astype(np.int32)
    large = np.minimum(large, N_BUCKETS - 1)
    return np.where(n < MAX_EXACT, n, large).astype(np.int32)


def _bias_kernel(table_ref, idx_ref, o_ref):
    h = pl.program_id(0)
    idx = idx_ref[...]
    acc = jnp.zeros(idx.shape, F32)
    for b in range(N_BUCKETS):
        acc = jnp.where(idx == b, table_ref[b, h], acc)
    o_ref[0] = acc


def _bias_from_table(table, idx_np):
    q, k = idx_np.shape
    return pl.pallas_call(
        _bias_kernel,
        grid=(N_HEADS,),
        in_specs=[pl.BlockSpec(memory_space=pltpu.SMEM), _full((q, k))],
        out_specs=pl.BlockSpec((1, q, k), lambda h: (h, 0, 0)),
        out_shape=jax.ShapeDtypeStruct((N_HEADS, q, k), F32),
        name="rel_bias",
    )(table, jnp.asarray(idx_np))


HEADS_PER_UNIT = GQ // 2
N_UNITS = N_HEADS // HEADS_PER_UNIT
UNIT_ROWS = HEADS_PER_UNIT * WINDOW


def _band_bias_kernel(table_ref, idx_ref, ok_ref, o_ref):
    h = pl.program_id(0)
    idx = idx_ref[...]
    acc = jnp.zeros(idx.shape, F32)
    for b in range(N_BUCKETS):
        acc = jnp.where(idx == b, table_ref[b, h], acc)
    base = jnp.where(ok_ref[...] != 0, acc, NEG_INF)
    sj = lax.broadcasted_iota(jnp.int32, idx.shape, 1)
    o_ref[0, 0] = base
    o_ref[1, 0] = jnp.where(sj >= WINDOW, base, NEG_INF)


def _band_bias(table, idx_np, ok_np):
    def out_map(h):
        r = h % GQ
        return (0, (h // GQ) * 2 + r % 2, r // 2, 0)

    return pl.pallas_call(
        _band_bias_kernel,
        grid=(N_HEADS,),
        in_specs=[pl.BlockSpec(memory_space=pltpu.SMEM), _full(idx_np.shape), _full(ok_np.shape)],
        out_specs=pl.BlockSpec((2, 1, WINDOW, 2 * WINDOW), out_map),
        out_shape=jax.ShapeDtypeStruct((2, N_UNITS, UNIT_ROWS, 2 * WINDOW), F32),
        name="band_bias",
    )(table, jnp.asarray(idx_np), jnp.asarray(ok_np))


def _attn_prompt_kernel(sink_ref, x_ref, g_ref, wqkv_ref, wo_ref, bias_ref,
                        xo_ref, k_ref, v_ref, kv_scr, q_scr, o_scr, *, tq):
    i = pl.program_id(1)
    nsub = tq // WINDOW

    @pl.when(i == 0)
    def _():
        kv_scr[0:WINDOW, :] = jnp.zeros((WINDOW, 2 * KV_W), F32)

    x = x_ref[0]
    h = _rms(x, g_ref[...]).astype(BF16)
    qkv = jnp.dot(h, wqkv_ref[...], preferred_element_type=F32)
    q_scr[...] = (qkv[:, :D_MODEL] * (HEAD_DIM ** -0.5)).astype(BF16)
    kv_scr[WINDOW:WINDOW + tq, :] = qkv[:, D_MODEL:]
    k_ref[0] = qkv[tq - WINDOW:, D_MODEL:D_MODEL + KV_W]
    v_ref[0] = qkv[tq - WINDOW:, D_MODEL + KV_W:]

    lo = lax.broadcasted_iota(jnp.int32, (2 * WINDOW, KV_W), 1) < HEAD_DIM
    row_head = lax.broadcasted_iota(jnp.int32, (UNIT_ROWS, 1), 0) // WINDOW
    sinks = []
    for u in range(N_UNITS):
        g, p = divmod(u, 2)
        col = jnp.zeros((UNIT_ROWS, 1), F32)
        for jj in range(HEADS_PER_UNIT):
            col = jnp.where(row_head == jj, sink_ref[GQ * g + 2 * jj + p], col)
        sinks.append(col)

    for s in range(nsub):
        band = kv_scr[WINDOW * s:WINDOW * s + 2 * WINDOW, :]
        kband, vband = band[:, :KV_W], band[:, KV_W:]
        kroll = pltpu.roll(kband, HEAD_DIM, 1)
        vroll = pltpu.roll(vband, HEAD_DIM, 1)
        kpad = [[jnp.where(lo, kband, 0.0).astype(BF16), jnp.where(lo, 0.0, kroll).astype(BF16)],
                [jnp.where(lo, kroll, 0.0).astype(BF16), jnp.where(lo, 0.0, kband).astype(BF16)]]
        vpad = [[jnp.where(lo, vband, 0.0).astype(BF16), jnp.where(lo, 0.0, vroll).astype(BF16)],
                [jnp.where(lo, vroll, 0.0).astype(BF16), jnp.where(lo, 0.0, vband).astype(BF16)]]
        variant = jnp.where(i * nsub + s > 0, 0, 1)
        for g in range(N_KV_HEADS):
            slabs = range(HEADS_PER_UNIT * g, HEADS_PER_UNIT * (g + 1))
            qs = jnp.concatenate([q_scr[WINDOW * s:WINDOW * (s + 1), LANES * j:LANES * (j + 1)] for j in slabs],
                                 axis=0)
            acc = None
            for p in range(2):
                u = 2 * g + p
                sc = lax.dot_general(qs, kpad[g][p], (((1,), (1,)), ((), ())), preferred_element_type=F32)
                sc = sc + bias_ref[variant, u]
                m = jnp.maximum(jnp.max(sc, axis=-1, keepdims=True), sinks[u])
                e = jnp.exp(sc - m)
                den = jnp.sum(e, axis=-1, keepdims=True) + jnp.exp(sinks[u] - m)
                pv = jnp.dot(e.astype(BF16), vpad[g][p], preferred_element_type=F32)
                term = pv * (1.0 / den)
                acc = term if acc is None else acc + term
            for jj, j in enumerate(slabs):
                o_scr[WINDOW * s:WINDOW * (s + 1), LANES * j:LANES * (j + 1)] = (
                    acc[WINDOW * jj:WINDOW * (jj + 1)].astype(BF16))

    kv_scr[0:WINDOW, :] = kv_scr[tq:tq + WINDOW, :]
    xo_ref[0] = x + jnp.dot(o_scr[...], wo_ref[...], preferred_element_type=F32)


def _attn_prompt_layer(x, g, wqkv, wo, bias, sinks):
    b, t, d = x.shape
    tq = min(512, t)
    kern = functools.partial(_attn_prompt_kernel, tq=tq)
    return pl.pallas_call(
        kern,
        grid=(b, t // tq),
        in_specs=[
            pl.BlockSpec(memory_space=pltpu.SMEM),
            pl.BlockSpec((1, tq, d), lambda bi, i: (bi, i, 0)),
            _full((1, d)),
            _full((d, QKV_W)),
            _full((d, d)),
            _full((2, N_UNITS, UNIT_ROWS, 2 * WINDOW)),
        ],
        out_specs=[
            pl.BlockSpec((1, tq, d), lambda bi, i: (bi, i, 0)),
            pl.BlockSpec((1, WINDOW, KV_W), lambda bi, i: (bi, 0, 0)),
            pl.BlockSpec((1, WINDOW, KV_W), lambda bi, i: (bi, 0, 0)),
        ],
        out_shape=[
            jax.ShapeDtypeStruct((b, t, d), F32),
            jax.ShapeDtypeStruct((b, WINDOW, KV_W), F32),
            jax.ShapeDtypeStruct((b, WINDOW, KV_W), F32),
        ],
        scratch_shapes=[
            pltpu.VMEM((WINDOW + tq, 2 * KV_W), F32),
            pltpu.VMEM((tq, d), BF16),
            pltpu.VMEM((tq, d), BF16),
        ],
        compiler_params=_cparams(("arbitrary", "arbitrary")),
        name="attn_prompt",
    )(sinks, x, g, wqkv, wo, bias)


def _norm_linear_kernel(x_ref, g_ref, w_ref, o_ref):
    h = _rms(x_ref[...], g_ref[...]).astype(BF16)
    o_ref[...] = jnp.dot(h, w_ref[...], preferred_element_type=F32)


def _norm_linear(x, g, w):
    n, d = x.shape
    return pl.pallas_call(
        _norm_linear_kernel,
        out_shape=jax.ShapeDtypeStruct((n, w.shape[1]), F32),
        compiler_params=_cparams(None),
        name="norm_linear",
    )(x, g, w)


def _linear_residual_kernel(x_ref, o_ref, w_ref, xo_ref):
    xo_ref[...] = x_ref[...] + jnp.dot(o_ref[...].astype(BF16), w_ref[...], preferred_element_type=F32)


def _linear_residual(x, o, w):
    return pl.pallas_call(
        _linear_residual_kernel,
        out_shape=jax.ShapeDtypeStruct(x.shape, F32),
        compiler_params=_cparams(None),
        name="linear_residual",
    )(x, o, w)


def _attn_sample_kernel(sink_ref, q_ref, ck_ref, cv_ref, kn_ref, vn_ref, bias_ref, o_ref, kw_ref, vw_ref):
    kw = jnp.concatenate([ck_ref[:, 1:, :], kn_ref[...]], axis=1)
    vw = jnp.concatenate([cv_ref[:, 1:, :], vn_ref[...]], axis=1)
    kw_ref[...] = kw
    vw_ref[...] = vw
    q = (q_ref[...] * (HEAD_DIM ** -0.5)).astype(BF16)
    sc = jnp.einsum('bhc,bjc->bhj', q, kw.astype(BF16), preferred_element_type=F32)
    sc = sc + bias_ref[...][None]
    sink = sink_ref[...][None]
    m = jnp.maximum(jnp.max(sc, axis=-1, keepdims=True), sink)
    e = jnp.exp(sc - m)
    den = jnp.sum(e, axis=-1, keepdims=True) + jnp.exp(sink - m)
    pr = (e / den).astype(BF16)
    o_ref[...] = jnp.einsum('bhj,bjc->bhc', pr, vw.astype(BF16), preferred_element_type=F32)


def _attn_sample(q3, ck, cv, kn, vn, bias_s, sinks_col):
    n = q3.shape[0]
    bb = 8
    blk = lambda s: pl.BlockSpec((bb,) + s, lambda i: (i, 0, 0))
    return pl.pallas_call(
        _attn_sample_kernel,
        grid=(n // bb,),
        in_specs=[_full((N_HEADS, 1)), blk((N_HEADS, KV_W)), blk((WINDOW, KV_W)), blk((WINDOW, KV_W)),
                  blk((1, KV_W)), blk((1, KV_W)), _full((N_HEADS, WINDOW))],
        out_specs=[blk((N_HEADS, KV_W)), blk((WINDOW, KV_W)), blk((WINDOW, KV_W))],
        out_shape=[jax.ShapeDtypeStruct((n, N_HEADS, KV_W), F32),
                   jax.ShapeDtypeStruct((n, WINDOW, KV_W), F32),
                   jax.ShapeDtypeStruct((n, WINDOW, KV_W), F32)],
        compiler_params=_cparams(("arbitrary",)),
        name="attn_sample",
    )(sinks_col, q3, ck, cv, kn, vn, bias_s)


def _attn_sample_layer(x, g, wqkv, wo, bias_s, sinks, cache_k, cache_v):
    n = x.shape[0]
    qkv = _norm_linear(x, g, wqkv)
    q = qkv[:, :D_MODEL].reshape(n, N_KV_HEADS, GQ, 1, HEAD_DIM)
    place = jnp.eye(N_KV_HEADS, dtype=F32).reshape(1, N_KV_HEADS, 1, N_KV_HEADS, 1)
    q3 = (q * place).reshape(n, N_HEADS, KV_W)
    kn = qkv[:, D_MODEL:D_MODEL + KV_W].reshape(n, 1, KV_W)
    vn = qkv[:, D_MODEL + KV_W:].reshape(n, 1, KV_W)
    o3, kw, vw = _attn_sample(q3, cache_k.reshape(n, WINDOW, KV_W), cache_v.reshape(n, WINDOW, KV_W),
                              kn, vn, bias_s, sinks.reshape(N_HEADS, 1))
    o5 = o3.reshape(n, N_KV_HEADS, GQ, N_KV_HEADS, HEAD_DIM)
    o = jnp.stack([o5[:, gi, :, gi, :] for gi in range(N_KV_HEADS)], axis=1).reshape(n, D_MODEL)
    return _linear_residual(x, o, wo), kw, vw


def _ssm_prep_kernel(ar_ref, ai_ref, ldt_ref, br_ref, bi_ref, c_ref, a_out, bc_out, wb_out, wc_out, wc2_out):
    ar, ai = ar_ref[...], ai_ref[...]
    dt = jnp.exp(ldt_ref[...])
    mag = jnp.exp(ar * dt)
    lr = mag * jnp.cos(ai * dt)
    li = mag * jnp.sin(ai * dt)
    den = ar * ar + ai * ai
    cr = ((lr - 1.0) * ar + li * ai) / den
    ci = (li * ar - (lr - 1.0) * ai) / den
    row = lax.broadcasted_iota(jnp.int32, (SUBLANES, N_STATES), 0)
    a_out[...] = jnp.broadcast_to(lr, (SUBLANES, N_STATES))
    bc_out[...] = jnp.where(row < SUBLANES // 2, -li, li)
    for j in range(N_SLABS):
        crj = cr[:, SLAB_STATES * j:SLAB_STATES * (j + 1)]
        cij = ci[:, SLAB_STATES * j:SLAB_STATES * (j + 1)]
        br, bi = br_ref[j], bi_ref[j]
        wb_out[j, :, :SLAB_STATES] = (crj * br - cij * bi).astype(BF16)
        wb_out[j, :, SLAB_STATES:] = (crj * bi + cij * br).astype(BF16)
        wc_out[j, :SLAB_STATES, :] = c_ref[j, :SLAB_STATES, :].astype(BF16)
        wc_out[j, SLAB_STATES:, :] = (-c_ref[j, SLAB_STATES:, :]).astype(BF16)
        wc2_out[j, :, :LANES] = c_ref[j, :SLAB_STATES, :].astype(BF16)
        wc2_out[j, :, LANES:] = (-c_ref[j, SLAB_STATES:, :]).astype(BF16)


def _ssm_prep(a_re, a_im, log_dt, b_re, b_im, c_re, c_im):
    eye = jnp.eye(SUBLANES, dtype=F32)

    def blk_b(b):
        t = b.reshape(N_SLABS, 8, SSM_STATE, SSM_GROUP).transpose(0, 1, 3, 2)
        return jnp.einsum('jghp,gk->jghkp', t, eye).reshape(N_SLABS, LANES, SLAB_STATES)

    def blk_c(c):
        t = c.reshape(N_SLABS, 8, SSM_GROUP, SSM_STATE).transpose(0, 1, 3, 2)
        return jnp.einsum('jgph,gk->jgpkh', t, eye).reshape(N_SLABS, SLAB_STATES, LANES)

    row = lambda a: a.reshape(1, N_STATES)
    ldt = jnp.broadcast_to(log_dt[:, None], (N_SSM_GROUPS, SSM_STATE))
    c_all = jnp.concatenate([blk_c(c_re), blk_c(c_im)], axis=1)
    return pl.pallas_call(
        _ssm_prep_kernel,
        out_shape=[jax.ShapeDtypeStruct((SUBLANES, N_STATES), F32),
                   jax.ShapeDtypeStruct((SUBLANES, N_STATES), F32),
                   jax.ShapeDtypeStruct((N_SLABS, LANES, 2 * SLAB_STATES), BF16),
                   jax.ShapeDtypeStruct((N_SLABS, 2 * SLAB_STATES, LANES), BF16),
                   jax.ShapeDtypeStruct((N_SLABS, SLAB_STATES, 2 * LANES), BF16)],
        compiler_params=_cparams(None),
        name="ssm_prep",
    )(row(a_re), row(a_im), row(ldt), blk_b(b_re), blk_b(b_im), c_all)


def _glu_residual(x, y, u, d, wa, wg):
    z = _gelu_tanh(y + d * u).astype(BF16)
    out = jnp.dot(z, wa, preferred_element_type=F32) * _sigmoid(jnp.dot(z, wg, preferred_element_type=F32))
    return x + out


def _ssm_prompt_kernel(x_ref, g_ref, wb_ref, wc_ref, a_ref, bc_ref, d_ref, wa_ref, wg_ref,
                       xo_ref, st_ref, z_scr, u_scr, o_scr, xs_scr, *, nb, lc):
    i = pl.program_id(0)
    blocks_per_slab = SLAB_STATES // LANES
    half = SUBLANES // 2
    pairs = lc // 2
    rows = half * lc
    roll_half = lambda v: pltpu.roll(v, half, 1)

    @pl.when(i == 0)
    def _():
        xs_scr[...] = jnp.zeros((SUBLANES, N_STATES), F32)
        if nb < half:
            u_scr[...] = jnp.zeros(u_scr.shape, F32)

    for b in range(nb):
        u = _rms(x_ref[b], g_ref[...])
        for c in range(N_SLABS):
            u_scr[c, pl.ds(b, lc, stride=half), :] = u[:, LANES * c:LANES * (c + 1)]

    lower = lax.broadcasted_iota(jnp.int32, (pairs, SUBLANES, LANES), 1) < half
    ys = [None] * N_SLABS

    def project_in(j):
        res = jnp.dot(u_scr[j].astype(BF16), wb_ref[j], preferred_element_type=F32)
        for c in range(blocks_per_slab):
            re = res[:, LANES * c:LANES * (c + 1)].reshape(pairs, SUBLANES, LANES)
            im = res[:, SLAB_STATES + LANES * c:SLAB_STATES + LANES * (c + 1)].reshape(pairs, SUBLANES, LANES)
            blk = blocks_per_slab * j + c
            z_scr[blk, :, 0] = jnp.where(lower, re, roll_half(im))
            z_scr[blk, :, 1] = jnp.where(lower, roll_half(re), im)

    def scan(blks):
        a = [a_ref[:, LANES * q:LANES * (q + 1)] for q in blks]
        bc = [bc_ref[:, LANES * q:LANES * (q + 1)] for q in blks]
        xs = [xs_scr[:, LANES * q:LANES * (q + 1)] for q in blks]
        for tp in range(pairs):
            for par in range(2):
                for k, q in enumerate(blks):
                    xs[k] = a[k] * xs[k] + bc[k] * pltpu.roll(xs[k], half, 0) + z_scr[q, tp, par]
                    z_scr[q, tp, par] = xs[k]
        for k, q in enumerate(blks):
            xs_scr[:, LANES * q:LANES * (q + 1)] = xs[k]

    def project_out(j):
        s = jnp.concatenate([z_scr[blocks_per_slab * j + c].reshape(SUBLANES * lc, LANES)
                             for c in range(blocks_per_slab)], axis=1).astype(BF16)
        p3 = jnp.dot(s, wc_ref[j], preferred_element_type=F32).reshape(lc, SUBLANES, 2 * LANES)
        y4 = (p3[:, :, :LANES] + roll_half(p3[:, :, LANES:])).reshape(pairs, 2, SUBLANES, LANES)
        ys[j] = jnp.where(lower, y4[:, 0], roll_half(y4[:, 1])).reshape(rows, LANES)

    slabs_per_group = 2
    n_groups = N_SLABS // slabs_per_group
    group_slabs = lambda gi: range(slabs_per_group * gi, slabs_per_group * (gi + 1))
    for gi in range(n_groups + 2):
        if gi < n_groups:
            for j in group_slabs(gi):
                project_in(j)
        if 1 <= gi <= n_groups:
            lo = slabs_per_group * blocks_per_slab * (gi - 1)
            scan(list(range(lo, lo + slabs_per_group * blocks_per_slab)))
        if gi >= 2:
            for j in group_slabs(gi - 2):
                project_out(j)
    st_ref[...] = xs_scr[...]
    y = jnp.concatenate(ys, axis=1)
    u_all = jnp.concatenate([u_scr[c] for c in range(N_SLABS)], axis=1)
    z = _gelu_tanh(y + d_ref[...] * u_all).astype(BF16)
    out = (jnp.dot(z, wa_ref[...], preferred_element_type=F32)
           * _sigmoid(jnp.dot(z, wg_ref[...], preferred_element_type=F32)))
    for c in range(N_SLABS):
        o_scr[c] = out[:, LANES * c:LANES * (c + 1)]
    for b in range(nb):
        xo_ref[b] = x_ref[b] + jnp.concatenate(
            [o_scr[c, pl.ds(b, lc, stride=half), :] for c in range(N_SLABS)], axis=1)


def _ssm_prompt_layer(x, g, prep, d, wa, wg):
    a_rows, bc_rows, wb, _, wc2 = prep
    nb, t, dm = x.shape
    half = SUBLANES // 2
    assert nb <= half
    lc = min(128, t)
    kern = functools.partial(_ssm_prompt_kernel, nb=nb, lc=lc)
    return pl.pallas_call(
        kern,
        grid=(t // lc,),
        in_specs=[
            pl.BlockSpec((nb, lc, dm), lambda i: (0, i, 0)),
            _full((1, dm)),
            _full(wb.shape), _full(wc2.shape),
            _full((SUBLANES, N_STATES)), _full((SUBLANES, N_STATES)),
            _full((1, dm)), _full((dm, dm)), _full((dm, dm)),
        ],
        out_specs=[pl.BlockSpec((nb, lc, dm), lambda i: (0, i, 0)), _full((SUBLANES, N_STATES))],
        out_shape=[jax.ShapeDtypeStruct((nb, t, dm), F32), jax.ShapeDtypeStruct((SUBLANES, N_STATES), F32)],
        scratch_shapes=[
            pltpu.VMEM((N_STATES // LANES, lc // 2, 2, SUBLANES, LANES), F32),
            pltpu.VMEM((N_SLABS, half * lc, LANES), F32),
            pltpu.VMEM((N_SLABS, half * lc, LANES), F32),
            pltpu.VMEM((SUBLANES, N_STATES), F32),
        ],
        compiler_params=_cparams(("arbitrary",)),
        name="ssm_prompt",
    )(x, g, wb, wc2, a_rows, bc_rows, d, wa, wg)


def _ssm_sample_kernel(x_ref, g_ref, wb_ref, wc_ref, a_ref, bc_ref, h0r_ref, h0i_ref, d_ref, wa_ref, wg_ref,
                       xo_ref, sr_ref, si_ref):
    x = x_ref[...]
    u = _rms(x, g_ref[...])
    ub = u.astype(BF16)
    ys = []
    for j in range(N_SLABS):
        sl = slice(SLAB_STATES * j, SLAB_STATES * (j + 1))
        res = jnp.dot(ub[:, LANES * j:LANES * (j + 1)], wb_ref[j], preferred_element_type=F32)
        lr = a_ref[0:1, sl]
        li = bc_ref[SUBLANES - 1:SUBLANES, sl]
        h0r, h0i = h0r_ref[:, sl], h0i_ref[:, sl]
        xr = res[:, :SLAB_STATES] + (lr * h0r - li * h0i)
        xi = res[:, SLAB_STATES:] + (lr * h0i + li * h0r)
        sr_ref[:, sl] = xr
        si_ref[:, sl] = xi
        s = jnp.concatenate([xr, xi], axis=1).astype(BF16)
        ys.append(jnp.dot(s, wc_ref[j], preferred_element_type=F32))
    y = jnp.concatenate(ys, axis=1)
    xo_ref[...] = _glu_residual(x, y, u, d_ref[...], wa_ref[...], wg_ref[...])


def _ssm_sample_layer(x, g, prep, h0r, h0i, d, wa, wg):
    a_rows, bc_rows, wb, wc, _ = prep
    n = x.shape[0]
    return pl.pallas_call(
        _ssm_sample_kernel,
        out_shape=[jax.ShapeDtypeStruct(x.shape, F32),
                   jax.ShapeDtypeStruct((n, N_STATES), F32),
                   jax.ShapeDtypeStruct((n, N_STATES), F32)],
        compiler_params=_cparams(None),
        name="ssm_sample",
    )(x, g, wb, wc, a_rows, bc_rows, h0r, h0i, d, wa, wg)


ROUTER_W = LANES
EXPERT_LANE0 = N_EXPERT_GROUPS


def _route_topk(h, wr, br):
    h_hi = h.astype(BF16)
    h_lo = (h - h_hi.astype(F32)).astype(BF16)
    w_hi = wr.astype(BF16)
    w_lo = (wr - w_hi.astype(F32)).astype(BF16)
    dot = lambda a, b: jnp.dot(a, b, preferred_element_type=F32)
    both = dot(h_hi, jnp.concatenate([w_hi, w_lo], axis=1))
    logits = both[:, :ROUTER_W] + (dot(h_lo, w_hi) + both[:, ROUTER_W:]) + br
    lane_i = lax.broadcasted_iota(jnp.int32, logits.shape, 1)
    lane = lane_i.astype(F32)
    neg = -jnp.inf
    past_end = float(ROUTER_W)
    gl = jnp.where(lane_i < N_EXPERT_GROUPS, logits, neg)
    gmax = jnp.max(gl, axis=-1, keepdims=True)
    g_sel = jnp.min(jnp.where(gl == gmax, lane, past_end), axis=-1, keepdims=True)
    p_grp = 1.0 / jnp.sum(jnp.exp(gl - gmax), axis=-1, keepdims=True)
    lane_grp = ((lane_i - EXPERT_LANE0) // EXPERTS_PER_GROUP).astype(F32)
    in_grp = (lane_i >= EXPERT_LANE0) & (lane_i < EXPERT_LANE0 + N_EXPERTS) & (lane_grp == g_sel)
    el = jnp.where(in_grp, logits, neg)
    v1 = jnp.max(el, axis=-1, keepdims=True)
    i1 = jnp.min(jnp.where(el == v1, lane, past_end), axis=-1, keepdims=True)
    el2 = jnp.where(lane == i1, neg, el)
    v2 = jnp.max(el2, axis=-1, keepdims=True)
    i2 = jnp.min(jnp.where(el2 == v2, lane, past_end), axis=-1, keepdims=True)
    e2 = jnp.exp(v2 - v1)
    w1 = p_grp / (1.0 + e2)
    w2 = p_grp * e2 / (1.0 + e2)
    return lane, i1, i2, w1, w2


def _route(h, wr, br):
    lane, i1, i2, w1, w2 = _route_topk(h, wr, br)
    return jnp.where(lane == i1, w1, 0.0) + jnp.where(lane == i2, w2, 0.0)


def _moe_kernel(x_ref, p_ref, gffn_ref, wr_ref, br_ref, wg_ref, wu_ref, wd_ref,
                gple_ref, wpg_ref, wpp_ref, gfin_ref, o_ref, hn_scr, gates_scr, acc_scr, *, final):
    e = pl.program_id(1)

    @pl.when(e == 0)
    def _():
        x = x_ref[...]
        h = _rms(x, gffn_ref[...])
        hn_scr[...] = h.astype(BF16)
        gates_scr[...] = _route(h, wr_ref[...], br_ref[...])
        acc_scr[...] = x

    hb = hn_scr[...]
    gt = jnp.dot(hb, wg_ref[0].astype(BF16), preferred_element_type=F32)
    up = jnp.dot(hb, wu_ref[0].astype(BF16), preferred_element_type=F32)
    a = (gt * _sigmoid(gt) * up).astype(BF16)
    gates = gates_scr[...]
    lane = lax.broadcasted_iota(jnp.int32, gates.shape, 1)
    ge = jnp.sum(jnp.where(lane == e + EXPERT_LANE0, gates, 0.0), axis=-1, keepdims=True)
    acc_scr[...] += ge * jnp.dot(a, wd_ref[0].astype(BF16), preferred_element_type=F32)

    @pl.when(e == pl.num_programs(1) - 1)
    def _():
        x1 = acc_scr[...]
        gate = _sigmoid(jnp.dot(_rms(x1, gple_ref[...]).astype(BF16), wpg_ref[...], preferred_element_type=F32))
        x2 = x1 + gate * jnp.dot(p_ref[...].astype(BF16), wpp_ref[...], preferred_element_type=F32)
        o_ref[...] = _rms(x2, gfin_ref[...]) if final else x2


def _moe_ple_layer(li, x, p, gffn, wr, br, wg, wu, wd, gple, wpg, wpp, gfin, final):
    n, d = x.shape
    tm = min(1024, n)
    kern = functools.partial(_moe_kernel, final=final)
    tok = lambda w: pl.BlockSpec((tm, w), lambda t, e: (t, 0))
    const = lambda s: pl.BlockSpec(s, lambda t, e: (0,) * len(s))
    return pl.pallas_call(
        kern,
        grid=(n // tm, N_EXPERTS),
        in_specs=[
            tok(d), pl.BlockSpec((None, tm, PLE_DIM), lambda t, e: (li, t, 0)),
            const((1, d)), const((d, ROUTER_W)), const((1, ROUTER_W)),
            pl.BlockSpec((None, 1, d, D_EXPERT), lambda t, e: (li, e, 0, 0)),
            pl.BlockSpec((None, 1, d, D_EXPERT), lambda t, e: (li, e, 0, 0)),
            pl.BlockSpec((None, 1, D_EXPERT, d), lambda t, e: (li, e, 0, 0)),
            const((1, d)), const((d, d)), const((PLE_DIM, d)), const((1, d)),
        ],
        out_specs=tok(d),
        out_shape=jax.ShapeDtypeStruct((n, d), F32),
        scratch_shapes=[pltpu.VMEM((tm, d), BF16), pltpu.VMEM((tm, ROUTER_W), F32), pltpu.VMEM((tm, d), F32)],
        compiler_params=_cparams(("arbitrary", "arbitrary")),
        name="moe_ple",
    )(x, p, gffn, wr, br, wg, wu, wd, gple, wpg, wpp, gfin)


TOKEN_DATA_ROWS = D_MODEL // LANES
TOKEN_PITCH = TOKEN_DATA_ROWS + 1
INFO_W1, INFO_W2, INFO_E1, INFO_E2, INFO_R1, INFO_R2 = range(6)
EXPERT_TILE = 256
ZERO_RUN = 256


def _to_token_tiles(ref_2d, val, n, first=0):
    for s in range(TOKEN_DATA_ROWS):
        ref_2d[pl.ds(first * TOKEN_PITCH + s, n, stride=TOKEN_PITCH), :] = val[:, LANES * s:LANES * (s + 1)]


def _zero_spare_rows(ref_2d, n):
    ref_2d[pl.ds(TOKEN_DATA_ROWS, n, stride=TOKEN_PITCH), :] = jnp.zeros((n, LANES), F32)


def _from_token_tiles(ref_2d, n, first=0):
    return jnp.concatenate([ref_2d[pl.ds(first * TOKEN_PITCH + s, n, stride=TOKEN_PITCH), :]
                            for s in range(TOKEN_DATA_ROWS)], axis=1)


def _token_slots(i, n=1):
    return pl.ds(i * TOKEN_PITCH, n * TOKEN_PITCH)


def _route_kernel(x_ref, g_ref, wr_ref, br_ref, before_ref, info_ref, cnt_ref, base_scr):
    i = pl.program_id(0)

    @pl.when(i == 0)
    def _():
        base_scr[...] = jnp.zeros(base_scr.shape, F32)

    h = _rms(x_ref[...], g_ref[...])
    lane, i1, i2, w1, w2 = _route_topk(h, wr_ref[...], br_ref[...])
    chosen = jnp.where((lane == i1) | (lane == i2), 1.0, 0.0)
    prefix = jnp.dot(before_ref[...], chosen.astype(BF16), preferred_element_type=F32) + base_scr[...]
    rank1 = jnp.sum(jnp.where(lane == i1, prefix, 0.0), axis=-1, keepdims=True)
    rank2 = jnp.sum(jnp.where(lane == i2, prefix, 0.0), axis=-1, keepdims=True)
    base_scr[...] += jnp.sum(chosen, axis=0, keepdims=True)
    cnt_ref[...] = base_scr[...]
    e1 = (i1 - EXPERT_LANE0).astype(F32)
    e2 = (i2 - EXPERT_LANE0).astype(F32)
    info = jnp.zeros(h.shape[:1] + (ROUTER_W,), F32)
    for ln, v in ((INFO_W1, w1), (INFO_W2, w2), (INFO_E1, e1), (INFO_E2, e2), (INFO_R1, rank1), (INFO_R2, rank2)):
        info = jnp.where(lane == ln, v, info)
    info_ref[...] = info


def _route_call(x, gffn, wr, br):
    n, d = x.shape
    tm = 512
    before = jnp.asarray(np.tril(np.ones((tm, tm), np.float32), -1), BF16)
    return pl.pallas_call(
        _route_kernel,
        grid=(n // tm,),
        in_specs=[pl.BlockSpec((tm, d), lambda i: (i, 0)), _full((1, d)), _full((d, ROUTER_W)), _full((1, ROUTER_W)),
                  _full((tm, tm))],
        out_specs=[pl.BlockSpec((tm, ROUTER_W), lambda i: (i, 0)), _full((1, ROUTER_W))],
        out_shape=[jax.ShapeDtypeStruct((n, ROUTER_W), F32), jax.ShapeDtypeStruct((1, ROUTER_W), F32)],
        scratch_shapes=[pltpu.VMEM((1, ROUTER_W), F32)],
        compiler_params=_cparams(("arbitrary",)),
        name="moe_route",
    )(x, gffn, wr, br, before)


def _dispatch_kernel(pos_ref, cnt_ref, start_ref, x_ref, g_ref, xs_hbm, hbuf, zbuf, sem, zsem,
                     *, td, n_steps, n_pad_rows):
    i = pl.program_id(0)
    slot = i % 2

    def wait_slot(sl):
        for _ in range(2):
            pltpu.make_async_copy(hbuf.at[sl], hbuf.at[sl], sem.at[sl]).wait()

    @pl.when(i >= 2)
    def _():
        wait_slot(slot)

    @pl.when(i < 2)
    def _():
        _zero_spare_rows(hbuf.at[slot], td)

    _to_token_tiles(hbuf.at[slot], _rms(x_ref[...], g_ref[...]), td)

    def issue(n8, carry):
        for u in range(SUBLANES):
            n = n8 * SUBLANES + u
            t = i * td + n
            src = hbuf.at[slot, _token_slots(n), :]
            pltpu.make_async_copy(src, xs_hbm.at[_token_slots(pos_ref[0, t]), :], sem.at[slot]).start(priority=0)
            pltpu.make_async_copy(src, xs_hbm.at[_token_slots(pos_ref[1, t]), :], sem.at[slot]).start(priority=1)
        return carry

    lax.fori_loop(0, td // SUBLANES, issue, 0)

    @pl.when(i == 0)
    def _():
        zbuf[...] = jnp.zeros(zbuf.shape, zbuf.dtype)

        def zero_run(first, n_slots):
            pltpu.make_async_copy(zbuf.at[_token_slots(0, n_slots), :], xs_hbm.at[_token_slots(first, n_slots), :],
                                  zsem).start()

        for e in range(N_EXPERTS + 1):
            lo = start_ref[e] + cnt_ref[e]
            n = start_ref[e + 1] - lo

            def full_run(k, carry, lo=lo):
                zero_run(lo + k * ZERO_RUN, ZERO_RUN)
                return carry

            lax.fori_loop(0, n // ZERO_RUN, full_run, 0)
            cur = lo + (n // ZERO_RUN) * ZERO_RUN
            p = ZERO_RUN // 2
            while p >= 1:
                @pl.when((n & p) != 0)
                def _(cur=cur, p=p):
                    zero_run(cur, p)

                cur = cur + (n & p)
                p //= 2

    @pl.when(i == n_steps - 1)
    def _():
        wait_slot(slot)
        if n_steps > 1:
            wait_slot(1 - slot)
        pad = xs_hbm.at[_token_slots(0, n_pad_rows), :]
        pltpu.make_async_copy(pad, pad, zsem).wait()


def _dispatch_call(pos, cnt, start, x, gffn, a_pad, n_pad_rows):
    n, d = x.shape
    td = 512
    n_steps = n // td
    kern = functools.partial(_dispatch_kernel, td=td, n_steps=n_steps, n_pad_rows=n_pad_rows)
    return pl.pallas_call(
        kern,
        grid_spec=pltpu.PrefetchScalarGridSpec(
            num_scalar_prefetch=3,
            grid=(n_steps,),
            in_specs=[pl.BlockSpec((td, d), lambda i, *_: (i, 0)), pl.BlockSpec((1, d), lambda i, *_: (0, 0))],
            out_specs=pl.BlockSpec(memory_space=pl.ANY),
            scratch_shapes=[pltpu.VMEM((2, td * TOKEN_PITCH, LANES), F32),
                            pltpu.VMEM((ZERO_RUN * TOKEN_PITCH, LANES), F32),
                            pltpu.SemaphoreType.DMA((2,)), pltpu.SemaphoreType.DMA(())],
        ),
        out_shape=jax.ShapeDtypeStruct((a_pad * TOKEN_PITCH, LANES), F32),
        compiler_params=_cparams(("arbitrary",)),
        name="moe_dispatch",
    )(pos, cnt, start, x, gffn)


EXPERT_IN_SLOTS = 3


def _experts_kernel(te_ref, xs_hbm, wg_ref, wu_ref, wd_ref, y_ref, xbuf, xsem, wg_scr, wu_scr, wd_scr,
                    *, te, n_tiles):
    t = pl.program_id(0)
    rows = te * TOKEN_PITCH

    def fetch(tile):
        slot = tile % EXPERT_IN_SLOTS
        src = xs_hbm.at[pl.ds(pl.multiple_of(tile * rows, rows), rows), :]
        return pltpu.make_async_copy(src, xbuf.at[slot], xsem.at[slot])

    @pl.when(t == 0)
    def _():
        for ahead in range(min(EXPERT_IN_SLOTS - 1, n_tiles)):
            fetch(ahead).start()

    @pl.when(t + EXPERT_IN_SLOTS - 1 < n_tiles)
    def _():
        fetch(t + EXPERT_IN_SLOTS - 1).start()

    @pl.when((t == 0) | (te_ref[t] != te_ref[jnp.maximum(t - 1, 0)]))
    def _():
        wg_scr[...] = wg_ref[0].astype(BF16)
        wu_scr[...] = wu_ref[0].astype(BF16)
        wd_scr[...] = wd_ref[0].astype(BF16)

    fetch(t).wait()
    x = _from_token_tiles(xbuf.at[t % EXPERT_IN_SLOTS], te).astype(BF16)
    gt = jnp.dot(x, wg_scr[...], preferred_element_type=F32)
    up = jnp.dot(x, wu_scr[...], preferred_element_type=F32)
    a = (gt * _sigmoid(gt) * up).astype(BF16)
    _to_token_tiles(y_ref, jnp.dot(a, wd_scr[...], preferred_element_type=F32), te)
    _zero_spare_rows(y_ref, te)


def _experts_call(li, tile_expert, xs, wg, wu, wd, te):
    rows = te * TOKEN_PITCH
    n_tiles = xs.shape[0] // rows
    d = wg.shape[2]
    kern = functools.partial(_experts_kernel, te=te, n_tiles=n_tiles)
    return pl.pallas_call(
        kern,
        grid_spec=pltpu.PrefetchScalarGridSpec(
            num_scalar_prefetch=1,
            grid=(n_tiles,),
            in_specs=[pl.BlockSpec(memory_space=pl.ANY),
                      pl.BlockSpec((None, 1, d, D_EXPERT), lambda t, ex: (li, ex[t], 0, 0)),
                      pl.BlockSpec((None, 1, d, D_EXPERT), lambda t, ex: (li, ex[t], 0, 0)),
                      pl.BlockSpec((None, 1, D_EXPERT, d), lambda t, ex: (li, ex[t], 0, 0))],
            out_specs=pl.BlockSpec((rows, LANES), lambda t, ex: (t, 0)),
            scratch_shapes=[pltpu.VMEM((EXPERT_IN_SLOTS, rows, LANES), F32),
                            pltpu.SemaphoreType.DMA((EXPERT_IN_SLOTS,)),
                            pltpu.VMEM((d, D_EXPERT), BF16), pltpu.VMEM((d, D_EXPERT), BF16),
                            pltpu.VMEM((D_EXPERT, d), BF16)],
        ),
        out_shape=jax.ShapeDtypeStruct((n_tiles * rows, LANES), F32),
        compiler_params=_cparams(("arbitrary",)),
        name="moe_experts",
    )(tile_expert, xs, wg, wu, wd)


def _combine_kernel(pos_ref, x_ref, info_ref, p_ref, gple_ref, wpg_ref, wpp_ref, gfin_ref, ys_hbm,
                    o_ref, ybuf, sem, *, tm, n_steps, final):
    i = pl.program_id(0)
    slot = i % 2

    def issue(step, sl):
        def body(n8, carry):
            for u in range(SUBLANES):
                n = n8 * SUBLANES + u
                t = step * tm + n
                for k in range(2):
                    pltpu.make_async_copy(ys_hbm.at[_token_slots(pos_ref[k, t]), :],
                                          ybuf.at[sl, k, _token_slots(n), :], sem.at[sl]).start(priority=k)
            return carry

        lax.fori_loop(0, tm // SUBLANES, body, 0)

    @pl.when(i == 0)
    def _():
        issue(0, 0)

    @pl.when(i + 1 < n_steps)
    def _():
        issue(i + 1, 1 - slot)

    for k in range(2):
        pltpu.make_async_copy(ybuf.at[slot, k], ybuf.at[slot, k], sem.at[slot]).wait()
    part = tm // 2
    for first in (0, part):
        tok = slice(first, first + part)
        y1 = _from_token_tiles(ybuf.at[slot, 0], part, first)
        y2 = _from_token_tiles(ybuf.at[slot, 1], part, first)
        info = info_ref[tok, :]
        x1 = x_ref[tok, :] + (info[:, INFO_W1:INFO_W1 + 1] * y1 + info[:, INFO_W2:INFO_W2 + 1] * y2)
        gate = _sigmoid(jnp.dot(_rms(x1, gple_ref[...]).astype(BF16), wpg_ref[...], preferred_element_type=F32))
        x2 = x1 + gate * jnp.dot(p_ref[tok, :].astype(BF16), wpp_ref[...], preferred_element_type=F32)
        o_ref[tok, :] = _rms(x2, gfin_ref[...]) if final else x2


def _combine_call(li, pos, x, info, p, gple, wpg, wpp, gfin, ys, final):
    n, d = x.shape
    tm = 512
    n_steps = n // tm
    kern = functools.partial(_combine_kernel, tm=tm, n_steps=n_steps, final=final)
    tok = lambda w: pl.BlockSpec((tm, w), lambda i, *_: (i, 0))
    const = lambda s: pl.BlockSpec(s, lambda i, *_: (0,) * len(s))
    return pl.pallas_call(
        kern,
        grid_spec=pltpu.PrefetchScalarGridSpec(
            num_scalar_prefetch=1,
            grid=(n_steps,),
            in_specs=[tok(d), tok(ROUTER_W), pl.BlockSpec((None, tm, PLE_DIM), lambda i, *_: (li, i, 0)),
                      const((1, d)), const((d, d)), const((PLE_DIM, d)),
                      const((1, d)), pl.BlockSpec(memory_space=pl.ANY)],
            out_specs=tok(d),
            scratch_shapes=[pltpu.VMEM((2, 2, tm * TOKEN_PITCH, LANES), F32), pltpu.SemaphoreType.DMA((2,))],
        ),
        out_shape=jax.ShapeDtypeStruct((n, d), F32),
        compiler_params=_cparams(("arbitrary",)),
        name="moe_combine",
    )(pos, x, info, p, gple, wpg, wpp, gfin, ys)


def _positions_kernel(info_ref, start_ref, pos_ref):
    info = info_ref[...]
    lane = lax.broadcasted_iota(jnp.int32, info.shape, 1)
    lane_expert = (lane - EXPERT_LANE0).astype(F32)
    start = start_ref[...]
    cols = []
    for e_lane, r_lane in ((INFO_E1, INFO_R1), (INFO_E2, INFO_R2)):
        seg = jnp.sum(jnp.where(lane_expert == info[:, e_lane:e_lane + 1], start, 0.0), axis=-1, keepdims=True)
        cols.append(seg + info[:, r_lane:r_lane + 1])
    both = jnp.where(lane == 0, cols[0], jnp.where(lane == 1, cols[1], 0.0))
    pos_ref[...] = both.T[:SUBLANES, :].astype(jnp.int32)


def _positions_call(info, start_row):
    n = info.shape[0]
    tm = 2048 if n % 2048 == 0 else 512
    return pl.pallas_call(
        _positions_kernel,
        grid=(n // tm,),
        in_specs=[pl.BlockSpec((tm, ROUTER_W), lambda i: (i, 0)), _full((1, ROUTER_W))],
        out_specs=pl.BlockSpec((SUBLANES, tm), lambda i: (0, i)),
        out_shape=jax.ShapeDtypeStruct((SUBLANES, n), jnp.int32),
        compiler_params=_cparams(("arbitrary",)),
        name="moe_positions",
    )(info, start_row)


def _moe_routed_layer(li, x, p, gffn, wr, br, wg, wu, wd, gple, wpg, wpp, gfin, final):
    n = x.shape[0]
    te = EXPERT_TILE
    n_assign = 2 * n
    assert n_assign % te == 0
    a_pad = n_assign + N_EXPERTS * te
    info, cnt = _route_call(x, gffn, wr, br)
    counts = cnt[0, EXPERT_LANE0:EXPERT_LANE0 + N_EXPERTS].astype(jnp.int32)
    padded = (counts + te - 1) // te * te
    seg_end = jnp.cumsum(padded)
    seg_start = seg_end - padded
    start_row = jnp.zeros((1, ROUTER_W), F32).at[0, EXPERT_LANE0:EXPERT_LANE0 + N_EXPERTS].set(seg_start.astype(F32))
    pos = _positions_call(info, start_row)[:2]
    start = jnp.concatenate([seg_start, seg_end[-1:], jnp.full((1,), a_pad, jnp.int32)])
    cnt_pad = jnp.concatenate([counts, jnp.zeros((1,), jnp.int32)])
    tile_row0 = jnp.arange(a_pad // te, dtype=jnp.int32) * te
    tile_expert = jnp.minimum(jnp.sum(seg_end[None, :] <= tile_row0[:, None], axis=1), N_EXPERTS - 1).astype(jnp.int32)
    xs = _dispatch_call(pos, cnt_pad, start, x, gffn, a_pad, N_EXPERTS * te)
    ys = _experts_call(li, tile_expert, xs, wg, wu, wd, te)
    return _combine_call(li, pos, x, info, p, gple, wpg, wpp, gfin, ys, final)


def kernel(x_prompt, x_sample, cache_k_win, cache_v_win, state_ssm_re, state_ssm_im, p_prompt, p_sample,
           rel_bias_table, g_mix, w_qkv, w_o, sinks, ssm_a_re, ssm_a_im, ssm_log_dt, ssm_b_re, ssm_b_im,
           ssm_c_re, ssm_c_im, ssm_d, w_glu_a, w_glu_b, g_ffn, w_router_group, b_router_group,
           w_router_expert, b_router_expert, w_exp_gate, w_exp_up, w_exp_down, g_ple, w_ple_gate,
           w_ple_proj, g_final):
    bsz, seq, d = x_prompt.shape
    ns = x_sample.shape[0]
    depth = g_mix.shape[0]
    row = lambda v: v.reshape(1, -1).astype(F32)

    qi = np.arange(WINDOW)[:, None]
    sj = np.arange(2 * WINDOW)[None, :]
    dist = qi + WINDOW - sj
    idx_p = _t5_bucket_np(dist)
    ok_p = ((dist >= 0) & (dist < WINDOW)).astype(np.int32)
    idx_s = np.broadcast_to(_t5_bucket_np(WINDOW - 1 - np.arange(WINDOW))[None, :], (SUBLANES, WINDOW))
    bias_p = _band_bias(rel_bias_table, idx_p, ok_p)
    bias_s = _bias_from_table(rel_bias_table, idx_s)[:, 0, :]

    wqkv_b = w_qkv[0].astype(BF16)
    wo_b = w_o[0].astype(BF16)
    wpg_b, wpp_b = w_ple_gate.astype(BF16), w_ple_proj.astype(BF16)
    wa_b, wgl_b = w_glu_a[0].astype(BF16), w_glu_b[0].astype(BF16)
    pad = ROUTER_W - N_EXPERT_GROUPS - N_EXPERTS
    wr = jnp.concatenate([w_router_group, w_router_expert.reshape(depth, d, N_EXPERTS),
                          jnp.zeros((depth, d, pad), F32)], axis=-1)
    br = jnp.concatenate([b_router_group, b_router_expert.reshape(depth, N_EXPERTS),
                          jnp.zeros((depth, pad), F32)], axis=-1)

    prep = _ssm_prep(ssm_a_re[0], ssm_a_im[0], ssm_log_dt[0], ssm_b_re[0], ssm_b_im[0], ssm_c_re[0], ssm_c_im[0])
    d_row = row(ssm_d[0])

    def moe(x2d, p_all, i):
        layer = _moe_routed_layer if 2 * x2d.shape[0] >= N_EXPERTS * EXPERT_TILE else _moe_ple_layer
        return layer(i, x2d, p_all, row(g_ffn[i]), wr[i], br[i:i + 1], w_exp_gate, w_exp_up, w_exp_down,
                     row(g_ple[i]), wpg_b[i], wpp_b[i], row(g_final), final=(i == depth - 1))

    pp = p_prompt.reshape(depth, bsz * seq, PLE_DIM)
    xp, kp, vp = _attn_prompt_layer(x_prompt, row(g_mix[0]), wqkv_b, wo_b, bias_p, sinks[0])
    xp = moe(xp.reshape(bsz * seq, d), pp, 0).reshape(bsz, seq, d)
    xp, st_p = _ssm_prompt_layer(xp, row(g_mix[1]), prep, d_row, wa_b, wgl_b)
    y_prompt = moe(xp.reshape(bsz * seq, d), pp, 1).reshape(bsz, seq, d)

    ps = p_sample.reshape(depth, ns, PLE_DIM)
    xs = x_sample.reshape(ns, d)
    xs, kws, vws = _attn_sample_layer(xs, row(g_mix[0]), wqkv_b, wo_b, bias_s, sinks[0],
                                      cache_k_win[0], cache_v_win[0])
    xs = moe(xs, ps, 0)
    xs, sr_s, si_s = _ssm_sample_layer(xs, row(g_mix[1]), prep, state_ssm_re[0].reshape(ns, N_STATES),
                                       state_ssm_im[0].reshape(ns, N_STATES), d_row, wa_b, wgl_b)
    y_sample = moe(xs, ps, 1).reshape(ns, 1, d)

    half = SUBLANES // 2
    win = lambda a, n: a.reshape(1, n, WINDOW, N_KV_HEADS, HEAD_DIM)
    st = lambda a, n: a.reshape(1, n, N_SSM_GROUPS, SSM_STATE)
    return (y_prompt, y_sample, win(kp, bsz), win(vp, bsz), win(kws, ns), win(vws, ns),
            st(st_p[:bsz], bsz), st(st_p[half:half + bsz], bsz), st(sr_s, ns), st(si_s, ns))
```

```python
import functools
import math

import numpy as np
import jax
import jax.numpy as jnp
from jax import lax
from jax.experimental import pallas as pl
from jax.experimental.pallas import tpu as pltpu

F32 = jnp.float32
BF16 = jnp.bfloat16

D_MODEL = 1024
HEAD_DIM = 64
N_HEADS = 16
N_KV_HEADS = 2
GQ = N_HEADS // N_KV_HEADS
WINDOW = 128
N_BUCKETS = 32
MAX_EXACT = 16
MAX_DISTANCE = 128
SSM_GROUP = 16
N_SSM_GROUPS = 64
SSM_STATE = 64
N_STATES = N_SSM_GROUPS * SSM_STATE
N_EXPERT_GROUPS = 4
EXPERTS_PER_GROUP = 8
N_EXPERTS = 32
D_EXPERT = 256
PLE_DIM = 256
RMS_EPS = 1e-6
NEG_INF = -1e30

LANES = 128
SUBLANES = 8
KV_W = N_KV_HEADS * HEAD_DIM
QKV_W = D_MODEL + 2 * KV_W
SLAB_STATES = 512
N_SLABS = D_MODEL // LANES
VMEM_LIMIT = 56 * 1024 * 1024


def _cparams(sem):
    return pltpu.CompilerParams(dimension_semantics=sem, vmem_limit_bytes=VMEM_LIMIT)


def _rms(x, g):
    return x * lax.rsqrt(jnp.mean(x * x, axis=-1, keepdims=True) + RMS_EPS) * g


def _gelu_tanh(x):
    c = math.sqrt(2.0 / math.pi)
    return x * (0.5 * (1.0 + jnp.tanh(c * (x + 0.044715 * (x * x * x)))))


def _sigmoid(x):
    return 1.0 / (1.0 + jnp.exp(-x))


def _full(shape):
    n = len(shape)
    return pl.BlockSpec(shape, lambda *_: (0,) * n)


def _t5_bucket_np(dist):
    n = np.maximum(dist, 0)
    nf = np.maximum(n, 1).astype(np.float64)
    large = MAX_EXACT + (np.log(nf / MAX_EXACT) / math.log(MAX_DISTANCE / MAX_EXACT)
                         * (N_BUCKETS - MAX_EXACT)).astype(np.int32)
    large = np.minimum(large, N_BUCKETS - 1)
    return np.where(n < MAX_EXACT, n, large).astype(np.int32)


def _bias_kernel(table_ref, idx_ref, o_ref):
    h = pl.program_id(0)
    idx = idx_ref[...]
    acc = jnp.zeros(idx.shape, F32)
    for b in range(N_BUCKETS):
        acc = jnp.where(idx == b, table_ref[b, h], acc)
    o_ref[0] = acc


def _bias_from_table(table, idx_np):
    q, k = idx_np.shape
    return pl.pallas_call(
        _bias_kernel,
        grid=(N_HEADS,),
        in_specs=[pl.BlockSpec(memory_space=pltpu.SMEM), _full((q, k))],
        out_specs=pl.BlockSpec((1, q, k), lambda h: (h, 0, 0)),
        out_shape=jax.ShapeDtypeStruct((N_HEADS, q, k), F32),
        name="rel_bias",
    )(table, jnp.asarray(idx_np))


HEADS_PER_UNIT = GQ // 2
N_UNITS = N_HEADS // HEADS_PER_UNIT
UNIT_ROWS = HEADS_PER_UNIT * WINDOW


def _band_bias_kernel(table_ref, idx_ref, ok_ref, o_ref):
    h = pl.program_id(0)
    idx = idx_ref[...]
    acc = jnp.zeros(idx.shape, F32)
    for b in range(N_BUCKETS):
        acc = jnp.where(idx == b, table_ref[b, h], acc)
    base = jnp.where(ok_ref[...] != 0, acc, NEG_INF)
    sj = lax.broadcasted_iota(jnp.int32, idx.shape, 1)
    o_ref[0, 0] = base
    o_ref[1, 0] = jnp.where(sj >= WINDOW, base, NEG_INF)


def _band_bias(table, idx_np, ok_np):
    def out_map(h):
        r = h % GQ
        return (0, (h // GQ) * 2 + r % 2, r // 2, 0)

    return pl.pallas_call(
        _band_bias_kernel,
        grid=(N_HEADS,),
        in_specs=[pl.BlockSpec(memory_space=pltpu.SMEM), _full(idx_np.shape), _full(ok_np.shape)],
        out_specs=pl.BlockSpec((2, 1, WINDOW, 2 * WINDOW), out_map),
        out_shape=jax.ShapeDtypeStruct((2, N_UNITS, UNIT_ROWS, 2 * WINDOW), F32),
        name="band_bias",
    )(table, jnp.asarray(idx_np), jnp.asarray(ok_np))


def _attn_prompt_kernel(sink_ref, x_ref, g_ref, wqkv_ref, wo_ref, bias_ref,
                        xo_ref, k_ref, v_ref, kv_scr, q_scr, o_scr, *, tq):
    i = pl.program_id(1)
    nsub = tq // WINDOW

    @pl.when(i == 0)
    def _():
        kv_scr[0:WINDOW, :] = jnp.zeros((WINDOW, 2 * KV_W), F32)

    x = x_ref[0]
    h = _rms(x, g_ref[...]).astype(BF16)
    qkv = jnp.dot(h, wqkv_ref[...], preferred_element_type=F32)
    q_scr[...] = (qkv[:, :D_MODEL] * (HEAD_DIM ** -0.5)).astype(BF16)
    kv_scr[WINDOW:WINDOW + tq, :] = qkv[:, D_MODEL:]
    k_ref[0] = qkv[tq - WINDOW:, D_MODEL:D_MODEL + KV_W]
    v_ref[0] = qkv[tq - WINDOW:, D_MODEL + KV_W:]

    lo = lax.broadcasted_iota(jnp.int32, (2 * WINDOW, KV_W), 1) < HEAD_DIM
    row_head = lax.broadcasted_iota(jnp.int32, (UNIT_ROWS, 1), 0) // WINDOW
    sinks = []
    for u in range(N_UNITS):
        g, p = divmod(u, 2)
        col = jnp.zeros((UNIT_ROWS, 1), F32)
        for jj in range(HEADS_PER_UNIT):
            col = jnp.where(row_head == jj, sink_ref[GQ * g + 2 * jj + p], col)
        sinks.append(col)

    for s in range(nsub):
        band = kv_scr[WINDOW * s:WINDOW * s + 2 * WINDOW, :]
        kband, vband = band[:, :KV_W], band[:, KV_W:]
        kroll = pltpu.roll(kband, HEAD_DIM, 1)
        vroll = pltpu.roll(vband, HEAD_DIM, 1)
        kpad = [[jnp.where(lo, kband, 0.0).astype(BF16), jnp.where(lo, 0.0, kroll).astype(BF16)],
                [jnp.where(lo, kroll, 0.0).astype(BF16), jnp.where(lo, 0.0, kband).astype(BF16)]]
        vpad = [[jnp.where(lo, vband, 0.0).astype(BF16), jnp.where(lo, 0.0, vroll).astype(BF16)],
                [jnp.where(lo, vroll, 0.0).astype(BF16), jnp.where(lo, 0.0, vband).astype(BF16)]]
        variant = jnp.where(i * nsub + s > 0, 0, 1)
        for g in range(N_KV_HEADS):
            slabs = range(HEADS_PER_UNIT * g, HEADS_PER_UNIT * (g + 1))
            qs = jnp.concatenate([q_scr[WINDOW * s:WINDOW * (s + 1), LANES * j:LANES * (j + 1)] for j in slabs],
                                 axis=0)
            acc = None
            for p in range(2):
                u = 2 * g + p
                sc = lax.dot_general(qs, kpad[g][p], (((1,), (1,)), ((), ())), preferred_element_type=F32)
                sc = sc + bias_ref[variant, u]
                m = jnp.maximum(jnp.max(sc, axis=-1, keepdims=True), sinks[u])
                e = jnp.exp(sc - m)
                den = jnp.sum(e, axis=-1, keepdims=True) + jnp.exp(sinks[u] - m)
                pv = jnp.dot(e.astype(BF16), vpad[g][p], preferred_element_type=F32)
                term = pv * (1.0 / den)
                acc = term if acc is None else acc + term
            for jj, j in enumerate(slabs):
                o_scr[WINDOW * s:WINDOW * (s + 1), LANES * j:LANES * (j + 1)] = (
                    acc[WINDOW * jj:WINDOW * (jj + 1)].astype(BF16))

    kv_scr[0:WINDOW, :] = kv_scr[tq:tq + WINDOW, :]
    xo_ref[0] = x + jnp.dot(o_scr[...], wo_ref[...], preferred_element_type=F32)


def _attn_prompt_layer(x, g, wqkv, wo, bias, sinks):
    b, t, d = x.shape
    tq = min(512, t)
    kern = functools.partial(_attn_prompt_kernel, tq=tq)
    return pl.pallas_call(
        kern,
        grid=(b, t // tq),
        in_specs=[
            pl.BlockSpec(memory_space=pltpu.SMEM),
            pl.BlockSpec((1, tq, d), lambda bi, i: (bi, i, 0)),
            _full((1, d)),
            _full((d, QKV_W)),
            _full((d, d)),
            _full((2, N_UNITS, UNIT_ROWS, 2 * WINDOW)),
        ],
        out_specs=[
            pl.BlockSpec((1, tq, d), lambda bi, i: (bi, i, 0)),
            pl.BlockSpec((1, WINDOW, KV_W), lambda bi, i: (bi, 0, 0)),
            pl.BlockSpec((1, WINDOW, KV_W), lambda bi, i: (bi, 0, 0)),
        ],
        out_shape=[
            jax.ShapeDtypeStruct((b, t, d), F32),
            jax.ShapeDtypeStruct((b, WINDOW, KV_W), F32),
            jax.ShapeDtypeStruct((b, WINDOW, KV_W), F32),
        ],
        scratch_shapes=[
            pltpu.VMEM((WINDOW + tq, 2 * KV_W), F32),
            pltpu.VMEM((tq, d), BF16),
            pltpu.VMEM((tq, d), BF16),
        ],
        compiler_params=_cparams(("arbitrary", "arbitrary")),
        name="attn_prompt",
    )(sinks, x, g, wqkv, wo, bias)


def _norm_linear_kernel(x_ref, g_ref, w_ref, o_ref):
    h = _rms(x_ref[...], g_ref[...]).astype(BF16)
    o_ref[...] = jnp.dot(h, w_ref[...], preferred_element_type=F32)


def _norm_linear(x, g, w):
    n, d = x.shape
    return pl.pallas_call(
        _norm_linear_kernel,
        out_shape=jax.ShapeDtypeStruct((n, w.shape[1]), F32),
        compiler_params=_cparams(None),
        name="norm_linear",
    )(x, g, w)


def _linear_residual_kernel(x_ref, o_ref, w_ref, xo_ref):
    xo_ref[...] = x_ref[...] + jnp.dot(o_ref[...].astype(BF16), w_ref[...], preferred_element_type=F32)


def _linear_residual(x, o, w):
    return pl.pallas_call(
        _linear_residual_kernel,
        out_shape=jax.ShapeDtypeStruct(x.shape, F32),
        compiler_params=_cparams(None),
        name="linear_residual",
    )(x, o, w)


def _attn_sample_kernel(sink_ref, q_ref, ck_ref, cv_ref, kn_ref, vn_ref, bias_ref, o_ref, kw_ref, vw_ref):
    kw = jnp.concatenate([ck_ref[:, 1:, :], kn_ref[...]], axis=1)
    vw = jnp.concatenate([cv_ref[:, 1:, :], vn_ref[...]], axis=1)
    kw_ref[...] = kw
    vw_ref[...] = vw
    q = (q_ref[...] * (HEAD_DIM ** -0.5)).astype(BF16)
    sc = jnp.einsum('bhc,bjc->bhj', q, kw.astype(BF16), preferred_element_type=F32)
    sc = sc + bias_ref[...][None]
    sink = sink_ref[...][None]
    m = jnp.maximum(jnp.max(sc, axis=-1, keepdims=True), sink)
    e = jnp.exp(sc - m)
    den = jnp.sum(e, axis=-1, keepdims=True) + jnp.exp(sink - m)
    pr = (e / den).astype(BF16)
    o_ref[...] = jnp.einsum('bhj,bjc->bhc', pr, vw.astype(BF16), preferred_element_type=F32)


def _attn_sample(q3, ck, cv, kn, vn, bias_s, sinks_col):
    n = q3.shape[0]
    bb = 8
    blk = lambda s: pl.BlockSpec((bb,) + s, lambda i: (i, 0, 0))
    return pl.pallas_call(
        _attn_sample_kernel,
        grid=(n // bb,),
        in_specs=[_full((N_HEADS, 1)), blk((N_HEADS, KV_W)), blk((WINDOW, KV_W)), blk((WINDOW, KV_W)),
                  blk((1, KV_W)), blk((1, KV_W)), _full((N_HEADS, WINDOW))],
        out_specs=[blk((N_HEADS, KV_W)), blk((WINDOW, KV_W)), blk((WINDOW, KV_W))],
        out_shape=[jax.ShapeDtypeStruct((n, N_HEADS, KV_W), F32),
                   jax.ShapeDtypeStruct((n, WINDOW, KV_W), F32),
                   jax.ShapeDtypeStruct((n, WINDOW, KV_W), F32)],
        compiler_params=_cparams(("arbitrary",)),
        name="attn_sample",
    )(sinks_col, q3, ck, cv, kn, vn, bias_s)


def _attn_sample_layer(x, g, wqkv, wo, bias_s, sinks, cache_k, cache_v):
    n = x.shape[0]
    qkv = _norm_linear(x, g, wqkv)
    q = qkv[:, :D_MODEL].reshape(n, N_KV_HEADS, GQ, 1, HEAD_DIM)
    place = jnp.eye(N_KV_HEADS, dtype=F32).reshape(1, N_KV_HEADS, 1, N_KV_HEADS, 1)
    q3 = (q * place).reshape(n, N_HEADS, KV_W)
    kn = qkv[:, D_MODEL:D_MODEL + KV_W].reshape(n, 1, KV_W)
    vn = qkv[:, D_MODEL + KV_W:].reshape(n, 1, KV_W)
    o3, kw, vw = _attn_sample(q3, cache_k.reshape(n, WINDOW, KV_W), cache_v.reshape(n, WINDOW, KV_W),
                              kn, vn, bias_s, sinks.reshape(N_HEADS, 1))
    o5 = o3.reshape(n, N_KV_HEADS, GQ, N_KV_HEADS, HEAD_DIM)
    o = jnp.stack([o5[:, gi, :, gi, :] for gi in range(N_KV_HEADS)], axis=1).reshape(n, D_MODEL)
    return _linear_residual(x, o, wo), kw, vw


def _ssm_prep_kernel(ar_ref, ai_ref, ldt_ref, br_ref, bi_ref, c_ref, a_out, bc_out, wb_out, wc_out, wc2_out):
    ar, ai = ar_ref[...], ai_ref[...]
    dt = jnp.exp(ldt_ref[...])
    mag = jnp.exp(ar * dt)
    lr = mag * jnp.cos(ai * dt)
    li = mag * jnp.sin(ai * dt)
    den = ar * ar + ai * ai
    cr = ((lr - 1.0) * ar + li * ai) / den
    ci = (li * ar - (lr - 1.0) * ai) / den
    row = lax.broadcasted_iota(jnp.int32, (SUBLANES, N_STATES), 0)
    a_out[...] = jnp.broadcast_to(lr, (SUBLANES, N_STATES))
    bc_out[...] = jnp.where(row < SUBLANES // 2, -li, li)
    for j in range(N_SLABS):
        crj = cr[:, SLAB_STATES * j:SLAB_STATES * (j + 1)]
        cij = ci[:, SLAB_STATES * j:SLAB_STATES * (j + 1)]
        br, bi = br_ref[j], bi_ref[j]
        wb_out[j, :, :SLAB_STATES] = (crj * br - cij * bi).astype(BF16)
        wb_out[j, :, SLAB_STATES:] = (crj * bi + cij * br).astype(BF16)
        wc_out[j, :SLAB_STATES, :] = c_ref[j, :SLAB_STATES, :].astype(BF16)
        wc_out[j, SLAB_STATES:, :] = (-c_ref[j, SLAB_STATES:, :]).astype(BF16)
        wc2_out[j, :, :LANES] = c_ref[j, :SLAB_STATES, :].astype(BF16)
        wc2_out[j, :, LANES:] = (-c_ref[j, SLAB_STATES:, :]).astype(BF16)


def _ssm_prep(a_re, a_im, log_dt, b_re, b_im, c_re, c_im):
    eye = jnp.eye(SUBLANES, dtype=F32)

    def blk_b(b):
        t = b.reshape(N_SLABS, 8, SSM_STATE, SSM_GROUP).transpose(0, 1, 3, 2)
        return jnp.einsum('jghp,gk->jghkp', t, eye).reshape(N_SLABS, LANES, SLAB_STATES)

    def blk_c(c):
        t = c.reshape(N_SLABS, 8, SSM_GROUP, SSM_STATE).transpose(0, 1, 3, 2)
        return jnp.einsum('jgph,gk->jgpkh', t, eye).reshape(N_SLABS, SLAB_STATES, LANES)

    row = lambda a: a.reshape(1, N_STATES)
    ldt = jnp.broadcast_to(log_dt[:, None], (N_SSM_GROUPS, SSM_STATE))
    c_all = jnp.concatenate([blk_c(c_re), blk_c(c_im)], axis=1)
    return pl.pallas_call(
        _ssm_prep_kernel,
        out_shape=[jax.ShapeDtypeStruct((SUBLANES, N_STATES), F32),
                   jax.ShapeDtypeStruct((SUBLANES, N_STATES), F32),
                   jax.ShapeDtypeStruct((N_SLABS, LANES, 2 * SLAB_STATES), BF16),
                   jax.ShapeDtypeStruct((N_SLABS, 2 * SLAB_STATES, LANES), BF16),
                   jax.ShapeDtypeStruct((N_SLABS, SLAB_STATES, 2 * LANES), BF16)],
        compiler_params=_cparams(None),
        name="ssm_prep",
    )(row(a_re), row(a_im), row(ldt), blk_b(b_re), blk_b(b_im), c_all)


def _glu_residual(x, y, u, d, wa, wg):
    z = _gelu_tanh(y + d * u).astype(BF16)
    out = jnp.dot(z, wa, preferred_element_type=F32) * _sigmoid(jnp.dot(z, wg, preferred_element_type=F32))
    return x + out


def _ssm_prompt_kernel(x_ref, g_ref, wb_ref, wc_ref, a_ref, bc_ref, d_ref, wa_ref, wg_ref,
                       xo_ref, st_ref, z_scr, u_scr, o_scr, xs_scr, *, nb, lc):
    i = pl.program_id(0)
    blocks_per_slab = SLAB_STATES // LANES
    half = SUBLANES // 2
    pairs = lc // 2
    rows = half * lc
    roll_half = lambda v: pltpu.roll(v, half, 1)

    @pl.when(i == 0)
    def _():
        xs_scr[...] = jnp.zeros((SUBLANES, N_STATES), F32)
        if nb < half:
            u_scr[...] = jnp.zeros(u_scr.shape, F32)

    for b in range(nb):
        u = _rms(x_ref[b], g_ref[...])
        for c in range(N_SLABS):
            u_scr[c, pl.ds(b, lc, stride=half), :] = u[:, LANES * c:LANES * (c + 1)]

    lower = lax.broadcasted_iota(jnp.int32, (pairs, SUBLANES, LANES), 1) < half
    ys = [None] * N_SLABS

    def project_in(j):
        res = jnp.dot(u_scr[j].astype(BF16), wb_ref[j], preferred_element_type=F32)
        for c in range(blocks_per_slab):
            re = res[:, LANES * c:LANES * (c + 1)].reshape(pairs, SUBLANES, LANES)
            im = res[:, SLAB_STATES + LANES * c:SLAB_STATES + LANES * (c + 1)].reshape(pairs, SUBLANES, LANES)
            blk = blocks_per_slab * j + c
            z_scr[blk, :, 0] = jnp.where(lower, re, roll_half(im))
            z_scr[blk, :, 1] = jnp.where(lower, roll_half(re), im)

    def scan(blks):
        a = [a_ref[:, LANES * q:LANES * (q + 1)] for q in blks]
        bc = [bc_ref[:, LANES * q:LANES * (q + 1)] for q in blks]
        xs = [xs_scr[:, LANES * q:LANES * (q + 1)] for q in blks]
        for tp in range(pairs):
            for par in range(2):
                for k, q in enumerate(blks):
                    xs[k] = a[k] * xs[k] + bc[k] * pltpu.roll(xs[k], half, 0) + z_scr[q, tp, par]
                    z_scr[q, tp, par] = xs[k]
        for k, q in enumerate(blks):
            xs_scr[:, LANES * q:LANES * (q + 1)] = xs[k]

    def project_out(j):
        s = jnp.concatenate([z_scr[blocks_per_slab * j + c].reshape(SUBLANES * lc, LANES)
                             for c in range(blocks_per_slab)], axis=1).astype(BF16)
        p3 = jnp.dot(s, wc_ref[j], preferred_element_type=F32).reshape(lc, SUBLANES, 2 * LANES)
        y4 = (p3[:, :, :LANES] + roll_half(p3[:, :, LANES:])).reshape(pairs, 2, SUBLANES, LANES)
        ys[j] = jnp.where(lower, y4[:, 0], roll_half(y4[:, 1])).reshape(rows, LANES)

    slabs_per_group = 2
    n_groups = N_SLABS // slabs_per_group
    group_slabs = lambda gi: range(slabs_per_group * gi, slabs_per_group * (gi + 1))
    for gi in range(n_groups + 2):
        if gi < n_groups:
            for j in group_slabs(gi):
                project_in(j)
        if 1 <= gi <= n_groups:
            lo = slabs_per_group * blocks_per_slab * (gi - 1)
            scan(list(range(lo, lo + slabs_per_group * blocks_per_slab)))
        if gi >= 2:
            for j in group_slabs(gi - 2):
                project_out(j)
    st_ref[...] = xs_scr[...]
    y = jnp.concatenate(ys, axis=1)
    u_all = jnp.concatenate([u_scr[c] for c in range(N_SLABS)], axis=1)
    z = _gelu_tanh(y + d_ref[...] * u_all).astype(BF16)
    out = (jnp.dot(z, wa_ref[...], preferred_element_type=F32)
           * _sigmoid(jnp.dot(z, wg_ref[...], preferred_element_type=F32)))
    for c in range(N_SLABS):
        o_scr[c] = out[:, LANES * c:LANES * (c + 1)]
    for b in range(nb):
        xo_ref[b] = x_ref[b] + jnp.concatenate(
            [o_scr[c, pl.ds(b, lc, stride=half), :] for c in range(N_SLABS)], axis=1)


def _ssm_prompt_layer(x, g, prep, d, wa, wg):
    a_rows, bc_rows, wb, _, wc2 = prep
    nb, t, dm = x.shape
    half = SUBLANES // 2
    assert nb <= half
    lc = min(128, t)
    kern = functools.partial(_ssm_prompt_kernel, nb=nb, lc=lc)
    return pl.pallas_call(
        kern,
        grid=(t // lc,),
        in_specs=[
            pl.BlockSpec((nb, lc, dm), lambda i: (0, i, 0)),
            _full((1, dm)),
            _full(wb.shape), _full(wc2.shape),
            _full((SUBLANES, N_STATES)), _full((SUBLANES, N_STATES)),
            _full((1, dm)), _full((dm, dm)), _full((dm, dm)),
        ],
        out_specs=[pl.BlockSpec((nb, lc, dm), lambda i: (0, i, 0)), _full((SUBLANES, N_STATES))],
        out_shape=[jax.ShapeDtypeStruct((nb, t, dm), F32), jax.ShapeDtypeStruct((SUBLANES, N_STATES), F32)],
        scratch_shapes=[
            pltpu.VMEM((N_STATES // LANES, lc // 2, 2, SUBLANES, LANES), F32),
            pltpu.VMEM((N_SLABS, half * lc, LANES), F32),
            pltpu.VMEM((N_SLABS, half * lc, LANES), F32),
            pltpu.VMEM((SUBLANES, N_STATES), F32),
        ],
        compiler_params=_cparams(("arbitrary",)),
        name="ssm_prompt",
    )(x, g, wb, wc2, a_rows, bc_rows, d, wa, wg)


def _ssm_sample_kernel(x_ref, g_ref, wb_ref, wc_ref, a_ref, bc_ref, h0r_ref, h0i_ref, d_ref, wa_ref, wg_ref,
                       xo_ref, sr_ref, si_ref):
    x = x_ref[...]
    u = _rms(x, g_ref[...])
    ub = u.astype(BF16)
    ys = []
    for j in range(N_SLABS):
        sl = slice(SLAB_STATES * j, SLAB_STATES * (j + 1))
        res = jnp.dot(ub[:, LANES * j:LANES * (j + 1)], wb_ref[j], preferred_element_type=F32)
        lr = a_ref[0:1, sl]
        li = bc_ref[SUBLANES - 1:SUBLANES, sl]
        h0r, h0i = h0r_ref[:, sl], h0i_ref[:, sl]
        xr = res[:, :SLAB_STATES] + (lr * h0r - li * h0i)
        xi = res[:, SLAB_STATES:] + (lr * h0i + li * h0r)
        sr_ref[:, sl] = xr
        si_ref[:, sl] = xi
        s = jnp.concatenate([xr, xi], axis=1).astype(BF16)
        ys.append(jnp.dot(s, wc_ref[j], preferred_element_type=F32))
    y = jnp.concatenate(ys, axis=1)
    xo_ref[...] = _glu_residual(x, y, u, d_ref[...], wa_ref[...], wg_ref[...])


def _ssm_sample_layer(x, g, prep, h0r, h0i, d, wa, wg):
    a_rows, bc_rows, wb, wc, _ = prep
    n = x.shape[0]
    return pl.pallas_call(
        _ssm_sample_kernel,
        out_shape=[jax.ShapeDtypeStruct(x.shape, F32),
                   jax.ShapeDtypeStruct((n, N_STATES), F32),
                   jax.ShapeDtypeStruct((n, N_STATES), F32)],
        compiler_params=_cparams(None),
        name="ssm_sample",
    )(x, g, wb, wc, a_rows, bc_rows, h0r, h0i, d, wa, wg)


ROUTER_W = LANES
EXPERT_LANE0 = N_EXPERT_GROUPS


def _route_topk(h, wr, br):
    h_hi = h.astype(BF16)
    h_lo = (h - h_hi.astype(F32)).astype(BF16)
    w_hi = wr.astype(BF16)
    w_lo = (wr - w_hi.astype(F32)).astype(BF16)
    dot = lambda a, b: jnp.dot(a, b, preferred_element_type=F32)
    both = dot(h_hi, jnp.concatenate([w_hi, w_lo], axis=1))
    logits = both[:, :ROUTER_W] + (dot(h_lo, w_hi) + both[:, ROUTER_W:]) + br
    lane_i = lax.broadcasted_iota(jnp.int32, logits.shape, 1)
    lane = lane_i.astype(F32)
    neg = -jnp.inf
    past_end = float(ROUTER_W)
    gl = jnp.where(lane_i < N_EXPERT_GROUPS, logits, neg)
    gmax = jnp.max(gl, axis=-1, keepdims=True)
    g_sel = jnp.min(jnp.where(gl == gmax, lane, past_end), axis=-1, keepdims=True)
    p_grp = 1.0 / jnp.sum(jnp.exp(gl - gmax), axis=-1, keepdims=True)
    lane_grp = ((lane_i - EXPERT_LANE0) // EXPERTS_PER_GROUP).astype(F32)
    in_grp = (lane_i >= EXPERT_LANE0) & (lane_i < EXPERT_LANE0 + N_EXPERTS) & (lane_grp == g_sel)
    el = jnp.where(in_grp, logits, neg)
    v1 = jnp.max(el, axis=-1, keepdims=True)
    i1 = jnp.min(jnp.where(el == v1, lane, past_end), axis=-1, keepdims=True)
    el2 = jnp.where(lane == i1, neg, el)
    v2 = jnp.max(el2, axis=-1, keepdims=True)
    i2 = jnp.min(jnp.where(el2 == v2, lane, past_end), axis=-1, keepdims=True)
    e2 = jnp.exp(v2 - v1)
    w1 = p_grp / (1.0 + e2)
    w2 = p_grp * e2 / (1.0 + e2)
    return lane, i1, i2, w1, w2


def _route(h, wr, br):
    lane, i1, i2, w1, w2 = _route_topk(h, wr, br)
    return jnp.where(lane == i1, w1, 0.0) + jnp.where(lane == i2, w2, 0.0)


def _moe_kernel(x_ref, p_ref, gffn_ref, wr_ref, br_ref, wg_ref, wu_ref, wd_ref,
                gple_ref, wpg_ref, wpp_ref, gfin_ref, o_ref, hn_scr, gates_scr, acc_scr, *, final):
    e = pl.program_id(1)

    @pl.when(e == 0)
    def _():
        x = x_ref[...]
        h = _rms(x, gffn_ref[...])
        hn_scr[...] = h.astype(BF16)
        gates_scr[...] = _route(h, wr_ref[...], br_ref[...])
        acc_scr[...] = x

    hb = hn_scr[...]
    gt = jnp.dot(hb, wg_ref[0].astype(BF16), preferred_element_type=F32)
    up = jnp.dot(hb, wu_ref[0].astype(BF16), preferred_element_type=F32)
    a = (gt * _sigmoid(gt) * up).astype(BF16)
    gates = gates_scr[...]
    lane = lax.broadcasted_iota(jnp.int32, gates.shape, 1)
    ge = jnp.sum(jnp.where(lane == e + EXPERT_LANE0, gates, 0.0), axis=-1, keepdims=True)
    acc_scr[...] += ge * jnp.dot(a, wd_ref[0].astype(BF16), preferred_element_type=F32)

    @pl.when(e == pl.num_programs(1) - 1)
    def _():
        x1 = acc_scr[...]
        gate = _sigmoid(jnp.dot(_rms(x1, gple_ref[...]).astype(BF16), wpg_ref[...], preferred_element_type=F32))
        x2 = x1 + gate * jnp.dot(p_ref[...].astype(BF16), wpp_ref[...], preferred_element_type=F32)
        o_ref[...] = _rms(x2, gfin_ref[...]) if final else x2


def _moe_ple_layer(li, x, p, gffn, wr, br, wg, wu, wd, gple, wpg, wpp, gfin, final):
    n, d = x.shape
    tm = min(1024, n)
    kern = functools.partial(_moe_kernel, final=final)
    tok = lambda w: pl.BlockSpec((tm, w), lambda t, e: (t, 0))
    const = lambda s: pl.BlockSpec(s, lambda t, e: (0,) * len(s))
    return pl.pallas_call(
        kern,
        grid=(n // tm, N_EXPERTS),
        in_specs=[
            tok(d), pl.BlockSpec((None, tm, PLE_DIM), lambda t, e: (li, t, 0)),
            const((1, d)), const((d, ROUTER_W)), const((1, ROUTER_W)),
            pl.BlockSpec((None, 1, d, D_EXPERT), lambda t, e: (li, e, 0, 0)),
            pl.BlockSpec((None, 1, d, D_EXPERT), lambda t, e: (li, e, 0, 0)),
            pl.BlockSpec((None, 1, D_EXPERT, d), lambda t, e: (li, e, 0, 0)),
            const((1, d)), const((d, d)), const((PLE_DIM, d)), const((1, d)),
        ],
        out_specs=tok(d),
        out_shape=jax.ShapeDtypeStruct((n, d), F32),
        scratch_shapes=[pltpu.VMEM((tm, d), BF16), pltpu.VMEM((tm, ROUTER_W), F32), pltpu.VMEM((tm, d), F32)],
        compiler_params=_cparams(("arbitrary", "arbitrary")),
        name="moe_ple",
    )(x, p, gffn, wr, br, wg, wu, wd, gple, wpg, wpp, gfin)


TOKEN_DATA_ROWS = D_MODEL // LANES
TOKEN_PITCH = TOKEN_DATA_ROWS + 1
INFO_W1, INFO_W2, INFO_E1, INFO_E2, INFO_R1, INFO_R2 = range(6)
EXPERT_TILE = 256
ZERO_RUN = 256


def _to_token_tiles(ref_2d, val, n, first=0):
    for s in range(TOKEN_DATA_ROWS):
        ref_2d[pl.ds(first * TOKEN_PITCH + s, n, stride=TOKEN_PITCH), :] = val[:, LANES * s:LANES * (s + 1)]


def _zero_spare_rows(ref_2d, n):
    ref_2d[pl.ds(TOKEN_DATA_ROWS, n, stride=TOKEN_PITCH), :] = jnp.zeros((n, LANES), F32)


def _from_token_tiles(ref_2d, n, first=0):
    return jnp.concatenate([ref_2d[pl.ds(first * TOKEN_PITCH + s, n, stride=TOKEN_PITCH), :]
                            for s in range(TOKEN_DATA_ROWS)], axis=1)


def _token_slots(i, n=1):
    return pl.ds(i * TOKEN_PITCH, n * TOKEN_PITCH)


def _slot_at(first_row):
    return pl.ds(first_row, TOKEN_PITCH)


def _route_kernel(x_ref, g_ref, wr_ref, br_ref, before_ref, info_ref, cnt_ref, base_scr):
    i = pl.program_id(0)

    @pl.when(i == 0)
    def _():
        base_scr[...] = jnp.zeros(base_scr.shape, F32)

    h = _rms(x_ref[...], g_ref[...])
    lane, i1, i2, w1, w2 = _route_topk(h, wr_ref[...], br_ref[...])
    chosen = jnp.where((lane == i1) | (lane == i2), 1.0, 0.0)
    prefix = jnp.dot(before_ref[...], chosen.astype(BF16), preferred_element_type=F32) + base_scr[...]
    rank1 = jnp.sum(jnp.where(lane == i1, prefix, 0.0), axis=-1, keepdims=True)
    rank2 = jnp.sum(jnp.where(lane == i2, prefix, 0.0), axis=-1, keepdims=True)
    base_scr[...] += jnp.sum(chosen, axis=0, keepdims=True)
    cnt_ref[...] = base_scr[...]
    e1 = (i1 - EXPERT_LANE0).astype(F32)
    e2 = (i2 - EXPERT_LANE0).astype(F32)
    info = jnp.zeros(h.shape[:1] + (ROUTER_W,), F32)
    for ln, v in ((INFO_W1, w1), (INFO_W2, w2), (INFO_E1, e1), (INFO_E2, e2), (INFO_R1, rank1), (INFO_R2, rank2)):
        info = jnp.where(lane == ln, v, info)
    info_ref[...] = info


def _route_call(x, gffn, wr, br):
    n, d = x.shape
    tm = 512
    before = jnp.asarray(np.tril(np.ones((tm, tm), np.float32), -1), BF16)
    return pl.pallas_call(
        _route_kernel,
        grid=(n // tm,),
        in_specs=[pl.BlockSpec((tm, d), lambda i: (i, 0)), _full((1, d)), _full((d, ROUTER_W)), _full((1, ROUTER_W)),
                  _full((tm, tm))],
        out_specs=[pl.BlockSpec((tm, ROUTER_W), lambda i: (i, 0)), _full((1, ROUTER_W))],
        out_shape=[jax.ShapeDtypeStruct((n, ROUTER_W), F32), jax.ShapeDtypeStruct((1, ROUTER_W), F32)],
        scratch_shapes=[pltpu.VMEM((1, ROUTER_W), F32)],
        compiler_params=_cparams(("arbitrary",)),
        name="moe_route",
    )(x, gffn, wr, br, before)


def _dispatch_kernel(pos1_ref, pos2_ref, cnt_ref, start_ref, x_ref, g_ref, xs_hbm, hbuf, zbuf, sem, zsem,
                     *, td, n_steps, n_pad_rows):
    i = pl.program_id(0)
    slot = i % 2

    def wait_slot(sl):
        for _ in range(2):
            pltpu.make_async_copy(hbuf.at[sl], hbuf.at[sl], sem.at[sl]).wait()

    @pl.when(i >= 2)
    def _():
        wait_slot(slot)

    @pl.when(i < 2)
    def _():
        _zero_spare_rows(hbuf.at[slot], td)

    _to_token_tiles(hbuf.at[slot], _rms(x_ref[...], g_ref[...]), td)

    def issue(n8, carry):
        for u in range(SUBLANES):
            n = n8 * SUBLANES + u
            t = i * td + n
            src = hbuf.at[slot, _token_slots(n), :]
            pltpu.make_async_copy(src, xs_hbm.at[_slot_at(pos1_ref[t]), :], sem.at[slot]).start(priority=0)
            pltpu.make_async_copy(src, xs_hbm.at[_slot_at(pos2_ref[t]), :], sem.at[slot]).start(priority=1)
        return carry

    lax.fori_loop(0, td // SUBLANES, issue, 0)

    @pl.when(i == 0)
    def _():
        zbuf[...] = jnp.zeros(zbuf.shape, zbuf.dtype)

        def zero_run(first, n_slots):
            pltpu.make_async_copy(zbuf.at[_token_slots(0, n_slots), :], xs_hbm.at[_token_slots(first, n_slots), :],
                                  zsem).start()

        for e in range(N_EXPERTS + 1):
            lo = start_ref[e] + cnt_ref[e]
            n = start_ref[e + 1] - lo

            def full_run(k, carry, lo=lo):
                zero_run(lo + k * ZERO_RUN, ZERO_RUN)
                return carry

            lax.fori_loop(0, n // ZERO_RUN, full_run, 0)
            cur = lo + (n // ZERO_RUN) * ZERO_RUN
            p = ZERO_RUN // 2
            while p >= 1:
                @pl.when((n & p) != 0)
                def _(cur=cur, p=p):
                    zero_run(cur, p)

                cur = cur + (n & p)
                p //= 2

    @pl.when(i == n_steps - 1)
    def _():
        wait_slot(slot)
        if n_steps > 1:
            wait_slot(1 - slot)
        pad = xs_hbm.at[_token_slots(0, n_pad_rows), :]
        pltpu.make_async_copy(pad, pad, zsem).wait()


def _dispatch_call(pos1, pos2, cnt, start, x, gffn, a_pad, n_pad_rows):
    n, d = x.shape
    td = 512
    n_steps = n // td
    kern = functools.partial(_dispatch_kernel, td=td, n_steps=n_steps, n_pad_rows=n_pad_rows)
    return pl.pallas_call(
        kern,
        grid_spec=pltpu.PrefetchScalarGridSpec(
            num_scalar_prefetch=4,
            grid=(n_steps,),
            in_specs=[pl.BlockSpec((td, d), lambda i, *_: (i, 0)), pl.BlockSpec((1, d), lambda i, *_: (0, 0))],
            out_specs=pl.BlockSpec(memory_space=pl.ANY),
            scratch_shapes=[pltpu.VMEM((2, td * TOKEN_PITCH, LANES), F32),
                            pltpu.VMEM((ZERO_RUN * TOKEN_PITCH, LANES), F32),
                            pltpu.SemaphoreType.DMA((2,)), pltpu.SemaphoreType.DMA(())],
        ),
        out_shape=jax.ShapeDtypeStruct((a_pad * TOKEN_PITCH, LANES), F32),
        compiler_params=_cparams(("arbitrary",)),
        name="moe_dispatch",
    )(pos1, pos2, cnt, start, x, gffn)


EXPERT_IN_SLOTS = 3


def _experts_kernel(te_ref, xs_hbm, wg_ref, wu_ref, wd_ref, y_ref, xbuf, xsem, wg_scr, wu_scr, wd_scr,
                    *, te, n_tiles):
    t = pl.program_id(0)
    rows = te * TOKEN_PITCH

    def fetch(tile):
        slot = tile % EXPERT_IN_SLOTS
        src = xs_hbm.at[pl.ds(pl.multiple_of(tile * rows, rows), rows), :]
        return pltpu.make_async_copy(src, xbuf.at[slot], xsem.at[slot])

    @pl.when(t == 0)
    def _():
        for ahead in range(min(EXPERT_IN_SLOTS - 1, n_tiles)):
            fetch(ahead).start()

    @pl.when(t + EXPERT_IN_SLOTS - 1 < n_tiles)
    def _():
        fetch(t + EXPERT_IN_SLOTS - 1).start()

    @pl.when((t == 0) | (te_ref[t] != te_ref[jnp.maximum(t - 1, 0)]))
    def _():
        wg_scr[...] = wg_ref[0].astype(BF16)
        wu_scr[...] = wu_ref[0].astype(BF16)
        wd_scr[...] = wd_ref[0].astype(BF16)

    fetch(t).wait()
    x = _from_token_tiles(xbuf.at[t % EXPERT_IN_SLOTS], te).astype(BF16)
    gt = jnp.dot(x, wg_scr[...], preferred_element_type=F32)
    up = jnp.dot(x, wu_scr[...], preferred_element_type=F32)
    a = (gt * _sigmoid(gt) * up).astype(BF16)
    _to_token_tiles(y_ref, jnp.dot(a, wd_scr[...], preferred_element_type=F32), te)
    _zero_spare_rows(y_ref, te)


def _experts_call(li, tile_expert, xs, wg, wu, wd, te):
    rows = te * TOKEN_PITCH
    n_tiles = xs.shape[0] // rows
    d = wg.shape[2]
    kern = functools.partial(_experts_kernel, te=te, n_tiles=n_tiles)
    return pl.pallas_call(
        kern,
        grid_spec=pltpu.PrefetchScalarGridSpec(
            num_scalar_prefetch=1,
            grid=(n_tiles,),
            in_specs=[pl.BlockSpec(memory_space=pl.ANY),
                      pl.BlockSpec((None, 1, d, D_EXPERT), lambda t, ex: (li, ex[t], 0, 0)),
                      pl.BlockSpec((None, 1, d, D_EXPERT), lambda t, ex: (li, ex[t], 0, 0)),
                      pl.BlockSpec((None, 1, D_EXPERT, d), lambda t, ex: (li, ex[t], 0, 0))],
            out_specs=pl.BlockSpec((rows, LANES), lambda t, ex: (t, 0)),
            scratch_shapes=[pltpu.VMEM((EXPERT_IN_SLOTS, rows, LANES), F32),
                            pltpu.SemaphoreType.DMA((EXPERT_IN_SLOTS,)),
                            pltpu.VMEM((d, D_EXPERT), BF16), pltpu.VMEM((d, D_EXPERT), BF16),
                            pltpu.VMEM((D_EXPERT, d), BF16)],
        ),
        out_shape=jax.ShapeDtypeStruct((n_tiles * rows, LANES), F32),
        compiler_params=_cparams(("arbitrary",)),
        name="moe_experts",
    )(tile_expert, xs, wg, wu, wd)


def _combine_kernel(pos1_ref, pos2_ref, x_ref, info_ref, p_ref, gple_ref, wpg_ref, wpp_ref, gfin_ref, ys_hbm,
                    o_ref, ybuf, sem, *, tm, n_steps, final):
    i = pl.program_id(0)
    slot = i % 2

    def issue(step, sl):
        def body(n8, carry):
            for u in range(SUBLANES):
                n = n8 * SUBLANES + u
                t = step * tm + n
                for k, pos_ref in enumerate((pos1_ref, pos2_ref)):
                    pltpu.make_async_copy(ys_hbm.at[_slot_at(pos_ref[t]), :],
                                          ybuf.at[sl, k, _token_slots(n), :], sem.at[sl]).start(priority=k)
            return carry

        lax.fori_loop(0, tm // SUBLANES, body, 0)

    def wait_rows(sl):
        for k in range(2):
            pltpu.make_async_copy(ybuf.at[sl, k], ybuf.at[sl, k], sem.at[sl]).wait()

    @pl.when(i == 0)
    def _():
        issue(0, 0)

    @pl.when(i + 1 < n_steps)
    def _():
        issue(i + 1, 1 - slot)

    wait_rows(slot)
    part = tm // 2
    for first in (0, part):
        tok = slice(first, first + part)
        y1 = _from_token_tiles(ybuf.at[slot, 0], part, first)
        y2 = _from_token_tiles(ybuf.at[slot, 1], part, first)
        info = info_ref[tok, :]
        x1 = x_ref[tok, :] + (info[:, INFO_W1:INFO_W1 + 1] * y1 + info[:, INFO_W2:INFO_W2 + 1] * y2)
        gate = _sigmoid(jnp.dot(_rms(x1, gple_ref[...]).astype(BF16), wpg_ref[...], preferred_element_type=F32))
        x2 = x1 + gate * jnp.dot(p_ref[tok, :].astype(BF16), wpp_ref[...], preferred_element_type=F32)
        o_ref[tok, :] = _rms(x2, gfin_ref[...]) if final else x2


def _combine_call(li, pos1, pos2, x, info, p, gple, wpg, wpp, gfin, ys, final):
    n, d = x.shape
    tm = 512
    n_steps = n // tm
    kern = functools.partial(_combine_kernel, tm=tm, n_steps=n_steps, final=final)
    tok = lambda w: pl.BlockSpec((tm, w), lambda i, *_: (i, 0))
    const = lambda s: pl.BlockSpec(s, lambda i, *_: (0,) * len(s))
    return pl.pallas_call(
        kern,
        grid_spec=pltpu.PrefetchScalarGridSpec(
            num_scalar_prefetch=2,
            grid=(n_steps,),
            in_specs=[tok(d), tok(ROUTER_W), pl.BlockSpec((None, tm, PLE_DIM), lambda i, *_: (li, i, 0)),
                      const((1, d)), const((d, d)), const((PLE_DIM, d)),
                      const((1, d)), pl.BlockSpec(memory_space=pl.ANY)],
            out_specs=tok(d),
            scratch_shapes=[pltpu.VMEM((2, 2, tm * TOKEN_PITCH, LANES), F32), pltpu.SemaphoreType.DMA((2,))],
        ),
        out_shape=jax.ShapeDtypeStruct((n, d), F32),
        compiler_params=_cparams(("arbitrary",)),
        name="moe_combine",
    )(pos1, pos2, x, info, p, gple, wpg, wpp, gfin, ys)


def _positions_kernel(info_ref, start_ref, pos_ref):
    info = info_ref[...]
    lane = lax.broadcasted_iota(jnp.int32, info.shape, 1)
    lane_expert = (lane - EXPERT_LANE0).astype(F32)
    start = start_ref[...]
    cols = []
    for e_lane, r_lane in ((INFO_E1, INFO_R1), (INFO_E2, INFO_R2)):
        seg = jnp.sum(jnp.where(lane_expert == info[:, e_lane:e_lane + 1], start, 0.0), axis=-1, keepdims=True)
        cols.append((seg + info[:, r_lane:r_lane + 1]) * float(TOKEN_PITCH))
    both = jnp.where(lane == 0, cols[0], jnp.where(lane == 1, cols[1], 0.0))
    pos_ref[...] = both.T[:SUBLANES, :].astype(jnp.int32)


def _positions_call(info, start_row):
    n = info.shape[0]
    tm = 2048 if n % 2048 == 0 else 512
    return pl.pallas_call(
        _positions_kernel,
        grid=(n // tm,),
        in_specs=[pl.BlockSpec((tm, ROUTER_W), lambda i: (i, 0)), _full((1, ROUTER_W))],
        out_specs=pl.BlockSpec((SUBLANES, tm), lambda i: (0, i)),
        out_shape=jax.ShapeDtypeStruct((SUBLANES, n), jnp.int32),
        compiler_params=_cparams(("arbitrary",)),
        name="moe_positions",
    )(info, start_row)


def _moe_routed_layer(li, x, p, gffn, wr, br, wg, wu, wd, gple, wpg, wpp, gfin, final):
    n = x.shape[0]
    te = EXPERT_TILE
    n_assign = 2 * n
    assert n_assign % te == 0
    a_pad = n_assign + N_EXPERTS * te
    info, cnt = _route_call(x, gffn, wr, br)
    counts = cnt[0, EXPERT_LANE0:EXPERT_LANE0 + N_EXPERTS].astype(jnp.int32)
    padded = (counts + te - 1) // te * te
    seg_end = jnp.cumsum(padded)
    seg_start = seg_end - padded
    start_row = jnp.zeros((1, ROUTER_W), F32).at[0, EXPERT_LANE0:EXPERT_LANE0 + N_EXPERTS].set(seg_start.astype(F32))
    pos = _positions_call(info, start_row)
    pos1, pos2 = pos[0], pos[1]
    start = jnp.concatenate([seg_start, seg_end[-1:], jnp.full((1,), a_pad, jnp.int32)])
    cnt_pad = jnp.concatenate([counts, jnp.zeros((1,), jnp.int32)])
    tile_row0 = jnp.arange(a_pad // te, dtype=jnp.int32) * te
    tile_expert = jnp.minimum(jnp.sum(seg_end[None, :] <= tile_row0[:, None], axis=1), N_EXPERTS - 1).astype(jnp.int32)
    xs = _dispatch_call(pos1, pos2, cnt_pad, start, x, gffn, a_pad, N_EXPERTS * te)
    ys = _experts_call(li, tile_expert, xs, wg, wu, wd, te)
    return _combine_call(li, pos1, pos2, x, info, p, gple, wpg, wpp, gfin, ys, final)


def kernel(x_prompt, x_sample, cache_k_win, cache_v_win, state_ssm_re, state_ssm_im, p_prompt, p_sample,
           rel_bias_table, g_mix, w_qkv, w_o, sinks, ssm_a_re, ssm_a_im, ssm_log_dt, ssm_b_re, ssm_b_im,
           ssm_c_re, ssm_c_im, ssm_d, w_glu_a, w_glu_b, g_ffn, w_router_group, b_router_group,
           w_router_expert, b_router_expert, w_exp_gate, w_exp_up, w_exp_down, g_ple, w_ple_gate,
           w_ple_proj, g_final):
    bsz, seq, d = x_prompt.shape
    ns = x_sample.shape[0]
    depth = g_mix.shape[0]
    row = lambda v: v.reshape(1, -1).astype(F32)

    qi = np.arange(WINDOW)[:, None]
    sj = np.arange(2 * WINDOW)[None, :]
    dist = qi + WINDOW - sj
    idx_p = _t5_bucket_np(dist)
    ok_p = ((dist >= 0) & (dist < WINDOW)).astype(np.int32)
    idx_s = np.broadcast_to(_t5_bucket_np(WINDOW - 1 - np.arange(WINDOW))[None, :], (SUBLANES, WINDOW))
    bias_p = _band_bias(rel_bias_table, idx_p, ok_p)
    bias_s = _bias_from_table(rel_bias_table, idx_s)[:, 0, :]

    wqkv_b = w_qkv[0].astype(BF16)
    wo_b = w_o[0].astype(BF16)
    wpg_b, wpp_b = w_ple_gate.astype(BF16), w_ple_proj.astype(BF16)
    wa_b, wgl_b = w_glu_a[0].astype(BF16), w_glu_b[0].astype(BF16)
    pad = ROUTER_W - N_EXPERT_GROUPS - N_EXPERTS
    wr = jnp.concatenate([w_router_group, w_router_expert.reshape(depth, d, N_EXPERTS),
                          jnp.zeros((depth, d, pad), F32)], axis=-1)
    br = jnp.concatenate([b_router_group, b_router_expert.reshape(depth, N_EXPERTS),
                          jnp.zeros((depth, pad), F32)], axis=-1)

    prep = _ssm_prep(ssm_a_re[0], ssm_a_im[0], ssm_log_dt[0], ssm_b_re[0], ssm_b_im[0], ssm_c_re[0], ssm_c_im[0])
    d_row = row(ssm_d[0])

    def moe(x2d, p_all, i):
        layer = _moe_routed_layer if 2 * x2d.shape[0] >= N_EXPERTS * EXPERT_TILE else _moe_ple_layer
        return layer(i, x2d, p_all, row(g_ffn[i]), wr[i], br[i:i + 1], w_exp_gate, w_exp_up, w_exp_down,
                     row(g_ple[i]), wpg_b[i], wpp_b[i], row(g_final), final=(i == depth - 1))

    pp = p_prompt.reshape(depth, bsz * seq, PLE_DIM)
    xp, kp, vp = _attn_prompt_layer(x_prompt, row(g_mix[0]), wqkv_b, wo_b, bias_p, sinks[0])
    xp = moe(xp.reshape(bsz * seq, d), pp, 0).reshape(bsz, seq, d)
    xp, st_p = _ssm_prompt_layer(xp, row(g_mix[1]), prep, d_row, wa_b, wgl_b)
    y_prompt = moe(xp.reshape(bsz * seq, d), pp, 1).reshape(bsz, seq, d)

    ps = p_sample.reshape(depth, ns, PLE_DIM)
    xs = x_sample.reshape(ns, d)
    xs, kws, vws = _attn_sample_layer(xs, row(g_mix[0]), wqkv_b, wo_b, bias_s, sinks[0],
                                      cache_k_win[0], cache_v_win[0])
    xs = moe(xs, ps, 0)
    xs, sr_s, si_s = _ssm_sample_layer(xs, row(g_mix[1]), prep, state_ssm_re[0].reshape(ns, N_STATES),
                                       state_ssm_im[0].reshape(ns, N_STATES), d_row, wa_b, wgl_b)
    y_sample = moe(xs, ps, 1).reshape(ns, 1, d)

    half = SUBLANES // 2
    win = lambda a, n: a.reshape(1, n, WINDOW, N_KV_HEADS, HEAD_DIM)
    st = lambda a, n: a.reshape(1, n, N_SSM_GROUPS, SSM_STATE)
    return (y_prompt, y_sample, win(kp, bsz), win(vp, bsz), win(kws, ns), win(vws, ns),
            st(st_p[:bsz], bsz), st(st_p[half:half + bsz], bsz), st(sr_s, ns), st(si_s, ns))
```

```python
import functools
import math

import numpy as np
import jax
import jax.numpy as jnp
from jax import lax
from jax.experimental import pallas as pl
from jax.experimental.pallas import tpu as pltpu

F32 = jnp.float32
BF16 = jnp.bfloat16

D_MODEL = 1024
HEAD_DIM = 64
N_HEADS = 16
N_KV_HEADS = 2
GQ = N_HEADS // N_KV_HEADS
WINDOW = 128
N_BUCKETS = 32
MAX_EXACT = 16
MAX_DISTANCE = 128
SSM_GROUP = 16
N_SSM_GROUPS = 64
SSM_STATE = 64
N_STATES = N_SSM_GROUPS * SSM_STATE
N_EXPERT_GROUPS = 4
EXPERTS_PER_GROUP = 8
N_EXPERTS = 32
D_EXPERT = 256
PLE_DIM = 256
RMS_EPS = 1e-6
NEG_INF = -1e30

LANES = 128
SUBLANES = 8
KV_W = N_KV_HEADS * HEAD_DIM
QKV_W = D_MODEL + 2 * KV_W
SLAB_STATES = 512
N_SLABS = D_MODEL // LANES
VMEM_LIMIT = 56 * 1024 * 1024


def _cparams(sem):
    return pltpu.CompilerParams(dimension_semantics=sem, vmem_limit_bytes=VMEM_LIMIT)


def _rms(x, g):
    return x * lax.rsqrt(jnp.mean(x * x, axis=-1, keepdims=True) + RMS_EPS) * g


def _gelu_tanh(x):
    c = math.sqrt(2.0 / math.pi)
    return x * (0.5 * (1.0 + jnp.tanh(c * (x + 0.044715 * (x * x * x)))))


def _sigmoid(x):
    return 1.0 / (1.0 + jnp.exp(-x))


def _full(shape):
    n = len(shape)
    return pl.BlockSpec(shape, lambda *_: (0,) * n)


def _t5_bucket_np(dist):
    n = np.maximum(dist, 0)
    nf = np.maximum(n, 1).astype(np.float64)
    large = MAX_EXACT + (np.log(nf / MAX_EXACT) / math.log(MAX_DISTANCE / MAX_EXACT)
                         * (N_BUCKETS - MAX_EXACT)).astype(np.int32)
    large = np.minimum(large, N_BUCKETS - 1)
    return np.where(n < MAX_EXACT, n, large).astype(np.int32)


def _bias_kernel(table_ref, idx_ref, o_ref):
    h = pl.program_id(0)
    idx = idx_ref[...]
    acc = jnp.zeros(idx.shape, F32)
    for b in range(N_BUCKETS):
        acc = jnp.where(idx == b, table_ref[b, h], acc)
    o_ref[0] = acc


def _bias_from_table(table, idx_np):
    q, k = idx_np.shape
    return pl.pallas_call(
        _bias_kernel,
        grid=(N_HEADS,),
        in_specs=[pl.BlockSpec(memory_space=pltpu.SMEM), _full((q, k))],
        out_specs=pl.BlockSpec((1, q, k), lambda h: (h, 0, 0)),
        out_shape=jax.ShapeDtypeStruct((N_HEADS, q, k), F32),
        name="rel_bias",
    )(table, jnp.asarray(idx_np))


HEADS_PER_UNIT = GQ // 2
N_UNITS = N_HEADS // HEADS_PER_UNIT
UNIT_ROWS = HEADS_PER_UNIT * WINDOW


def _band_bias_kernel(table_ref, idx_ref, ok_ref, o_ref):
    h = pl.program_id(0)
    idx = idx_ref[...]
    acc = jnp.zeros(idx.shape, F32)
    for b in range(N_BUCKETS):
        acc = jnp.where(idx == b, table_ref[b, h], acc)
    base = jnp.where(ok_ref[...] != 0, acc, NEG_INF)
    sj = lax.broadcasted_iota(jnp.int32, idx.shape, 1)
    o_ref[0, 0] = base
    o_ref[1, 0] = jnp.where(sj >= WINDOW, base, NEG_INF)


def _band_bias(table, idx_np, ok_np):
    def out_map(h):
        r = h % GQ
        return (0, (h // GQ) * 2 + r % 2, r // 2, 0)

    return pl.pallas_call(
        _band_bias_kernel,
        grid=(N_HEADS,),
        in_specs=[pl.BlockSpec(memory_space=pltpu.SMEM), _full(idx_np.shape), _full(ok_np.shape)],
        out_specs=pl.BlockSpec((2, 1, WINDOW, 2 * WINDOW), out_map),
        out_shape=jax.ShapeDtypeStruct((2, N_UNITS, UNIT_ROWS, 2 * WINDOW), F32),
        name="band_bias",
    )(table, jnp.asarray(idx_np), jnp.asarray(ok_np))


def _attn_prompt_kernel(sink_ref, x_ref, g_ref, wqkv_ref, wo_ref, bias_ref,
                        xo_ref, k_ref, v_ref, kv_scr, q_scr, o_scr, *, tq):
    i = pl.program_id(1)
    nsub = tq // WINDOW

    @pl.when(i == 0)
    def _():
        kv_scr[0:WINDOW, :] = jnp.zeros((WINDOW, 2 * KV_W), F32)

    x = x_ref[0]
    h = _rms(x, g_ref[...]).astype(BF16)
    qkv = jnp.dot(h, wqkv_ref[...], preferred_element_type=F32)
    q_scr[...] = (qkv[:, :D_MODEL] * (HEAD_DIM ** -0.5)).astype(BF16)
    kv_scr[WINDOW:WINDOW + tq, :] = qkv[:, D_MODEL:]
    k_ref[0] = qkv[tq - WINDOW:, D_MODEL:D_MODEL + KV_W]
    v_ref[0] = qkv[tq - WINDOW:, D_MODEL + KV_W:]

    lo = lax.broadcasted_iota(jnp.int32, (2 * WINDOW, KV_W), 1) < HEAD_DIM
    row_head = lax.broadcasted_iota(jnp.int32, (UNIT_ROWS, 1), 0) // WINDOW
    sinks = []
    for u in range(N_UNITS):
        g, p = divmod(u, 2)
        col = jnp.zeros((UNIT_ROWS, 1), F32)
        for jj in range(HEADS_PER_UNIT):
            col = jnp.where(row_head == jj, sink_ref[GQ * g + 2 * jj + p], col)
        sinks.append(col)

    for s in range(nsub):
        band = kv_scr[WINDOW * s:WINDOW * s + 2 * WINDOW, :]
        kband, vband = band[:, :KV_W], band[:, KV_W:]
        kroll = pltpu.roll(kband, HEAD_DIM, 1)
        vroll = pltpu.roll(vband, HEAD_DIM, 1)
        kpad = [[jnp.where(lo, kband, 0.0).astype(BF16), jnp.where(lo, 0.0, kroll).astype(BF16)],
                [jnp.where(lo, kroll, 0.0).astype(BF16), jnp.where(lo, 0.0, kband).astype(BF16)]]
        vpad = [[jnp.where(lo, vband, 0.0).astype(BF16), jnp.where(lo, 0.0, vroll).astype(BF16)],
                [jnp.where(lo, vroll, 0.0).astype(BF16), jnp.where(lo, 0.0, vband).astype(BF16)]]
        variant = jnp.where(i * nsub + s > 0, 0, 1)
        for g in range(N_KV_HEADS):
            slabs = range(HEADS_PER_UNIT * g, HEADS_PER_UNIT * (g + 1))
            qs = jnp.concatenate([q_scr[WINDOW * s:WINDOW * (s + 1), LANES * j:LANES * (j + 1)] for j in slabs],
                                 axis=0)
            acc = None
            for p in range(2):
                u = 2 * g + p
                sc = lax.dot_general(qs, kpad[g][p], (((1,), (1,)), ((), ())), preferred_element_type=F32)
                sc = sc + bias_ref[variant, u]
                m = jnp.maximum(jnp.max(sc, axis=-1, keepdims=True), sinks[u])
                e = jnp.exp(sc - m)
                den = jnp.sum(e, axis=-1, keepdims=True) + jnp.exp(sinks[u] - m)
                pv = jnp.dot(e.astype(BF16), vpad[g][p], preferred_element_type=F32)
                term = pv * (1.0 / den)
                acc = term if acc is None else acc + term
            for jj, j in enumerate(slabs):
                o_scr[WINDOW * s:WINDOW * (s + 1), LANES * j:LANES * (j + 1)] = (
                    acc[WINDOW * jj:WINDOW * (jj + 1)].astype(BF16))

    kv_scr[0:WINDOW, :] = kv_scr[tq:tq + WINDOW, :]
    xo_ref[0] = x + jnp.dot(o_scr[...], wo_ref[...], preferred_element_type=F32)


def _attn_prompt_layer(x, g, wqkv, wo, bias, sinks):
    b, t, d = x.shape
    tq = min(512, t)
    kern = functools.partial(_attn_prompt_kernel, tq=tq)
    return pl.pallas_call(
        kern,
        grid=(b, t // tq),
        in_specs=[
            pl.BlockSpec(memory_space=pltpu.SMEM),
            pl.BlockSpec((1, tq, d), lambda bi, i: (bi, i, 0)),
            _full((1, d)),
            _full((d, QKV_W)),
            _full((d, d)),
            _full((2, N_UNITS, UNIT_ROWS, 2 * WINDOW)),
        ],
        out_specs=[
            pl.BlockSpec((1, tq, d), lambda bi, i: (bi, i, 0)),
            pl.BlockSpec((1, WINDOW, KV_W), lambda bi, i: (bi, 0, 0)),
            pl.BlockSpec((1, WINDOW, KV_W), lambda bi, i: (bi, 0, 0)),
        ],
        out_shape=[
            jax.ShapeDtypeStruct((b, t, d), F32),
            jax.ShapeDtypeStruct((b, WINDOW, KV_W), F32),
            jax.ShapeDtypeStruct((b, WINDOW, KV_W), F32),
        ],
        scratch_shapes=[
            pltpu.VMEM((WINDOW + tq, 2 * KV_W), F32),
            pltpu.VMEM((tq, d), BF16),
            pltpu.VMEM((tq, d), BF16),
        ],
        compiler_params=_cparams(("arbitrary", "arbitrary")),
        name="attn_prompt",
    )(sinks, x, g, wqkv, wo, bias)


def _norm_linear_kernel(x_ref, g_ref, w_ref, o_ref):
    h = _rms(x_ref[...], g_ref[...]).astype(BF16)
    o_ref[...] = jnp.dot(h, w_ref[...], preferred_element_type=F32)


def _norm_linear(x, g, w):
    n, d = x.shape
    return pl.pallas_call(
        _norm_linear_kernel,
        out_shape=jax.ShapeDtypeStruct((n, w.shape[1]), F32),
        compiler_params=_cparams(None),
        name="norm_linear",
    )(x, g, w)


def _linear_residual_kernel(x_ref, o_ref, w_ref, xo_ref):
    xo_ref[...] = x_ref[...] + jnp.dot(o_ref[...].astype(BF16), w_ref[...], preferred_element_type=F32)


def _linear_residual(x, o, w):
    return pl.pallas_call(
        _linear_residual_kernel,
        out_shape=jax.ShapeDtypeStruct(x.shape, F32),
        compiler_params=_cparams(None),
        name="linear_residual",
    )(x, o, w)


def _attn_sample_kernel(sink_ref, q_ref, ck_ref, cv_ref, kn_ref, vn_ref, bias_ref, o_ref, kw_ref, vw_ref):
    kw = jnp.concatenate([ck_ref[:, 1:, :], kn_ref[...]], axis=1)
    vw = jnp.concatenate([cv_ref[:, 1:, :], vn_ref[...]], axis=1)
    kw_ref[...] = kw
    vw_ref[...] = vw
    q = (q_ref[...] * (HEAD_DIM ** -0.5)).astype(BF16)
    sc = jnp.einsum('bhc,bjc->bhj', q, kw.astype(BF16), preferred_element_type=F32)
    sc = sc + bias_ref[...][None]
    sink = sink_ref[...][None]
    m = jnp.maximum(jnp.max(sc, axis=-1, keepdims=True), sink)
    e = jnp.exp(sc - m)
    den = jnp.sum(e, axis=-1, keepdims=True) + jnp.exp(sink - m)
    pr = (e / den).astype(BF16)
    o_ref[...] = jnp.einsum('bhj,bjc->bhc', pr, vw.astype(BF16), preferred_element_type=F32)


def _attn_sample(q3, ck, cv, kn, vn, bias_s, sinks_col):
    n = q3.shape[0]
    bb = 8
    blk = lambda s: pl.BlockSpec((bb,) + s, lambda i: (i, 0, 0))
    return pl.pallas_call(
        _attn_sample_kernel,
        grid=(n // bb,),
        in_specs=[_full((N_HEADS, 1)), blk((N_HEADS, KV_W)), blk((WINDOW, KV_W)), blk((WINDOW, KV_W)),
                  blk((1, KV_W)), blk((1, KV_W)), _full((N_HEADS, WINDOW))],
        out_specs=[blk((N_HEADS, KV_W)), blk((WINDOW, KV_W)), blk((WINDOW, KV_W))],
        out_shape=[jax.ShapeDtypeStruct((n, N_HEADS, KV_W), F32),
                   jax.ShapeDtypeStruct((n, WINDOW, KV_W), F32),
                   jax.ShapeDtypeStruct((n, WINDOW, KV_W), F32)],
        compiler_params=_cparams(("arbitrary",)),
        name="attn_sample",
    )(sinks_col, q3, ck, cv, kn, vn, bias_s)


def _attn_sample_layer(x, g, wqkv, wo, bias_s, sinks, cache_k, cache_v):
    n = x.shape[0]
    qkv = _norm_linear(x, g, wqkv)
    q = qkv[:, :D_MODEL].reshape(n, N_KV_HEADS, GQ, 1, HEAD_DIM)
    place = jnp.eye(N_KV_HEADS, dtype=F32).reshape(1, N_KV_HEADS, 1, N_KV_HEADS, 1)
    q3 = (q * place).reshape(n, N_HEADS, KV_W)
    kn = qkv[:, D_MODEL:D_MODEL + KV_W].reshape(n, 1, KV_W)
    vn = qkv[:, D_MODEL + KV_W:].reshape(n, 1, KV_W)
    o3, kw, vw = _attn_sample(q3, cache_k.reshape(n, WINDOW, KV_W), cache_v.reshape(n, WINDOW, KV_W),
                              kn, vn, bias_s, sinks.reshape(N_HEADS, 1))
    o5 = o3.reshape(n, N_KV_HEADS, GQ, N_KV_HEADS, HEAD_DIM)
    o = jnp.stack([o5[:, gi, :, gi, :] for gi in range(N_KV_HEADS)], axis=1).reshape(n, D_MODEL)
    return _linear_residual(x, o, wo), kw, vw


def _ssm_prep_kernel(ar_ref, ai_ref, ldt_ref, br_ref, bi_ref, c_ref, a_out, bc_out, wb_out, wc_out, wc2_out):
    ar, ai = ar_ref[...], ai_ref[...]
    dt = jnp.exp(ldt_ref[...])
    mag = jnp.exp(ar * dt)
    lr = mag * jnp.cos(ai * dt)
    li = mag * jnp.sin(ai * dt)
    den = ar * ar + ai * ai
    cr = ((lr - 1.0) * ar + li * ai) / den
    ci = (li * ar - (lr - 1.0) * ai) / den
    row = lax.broadcasted_iota(jnp.int32, (SUBLANES, N_STATES), 0)
    a_out[...] = jnp.broadcast_to(lr, (SUBLANES, N_STATES))
    bc_out[...] = jnp.where(row < SUBLANES // 2, -li, li)
    for j in range(N_SLABS):
        crj = cr[:, SLAB_STATES * j:SLAB_STATES * (j + 1)]
        cij = ci[:, SLAB_STATES * j:SLAB_STATES * (j + 1)]
        br, bi = br_ref[j], bi_ref[j]
        wb_out[j, :, :SLAB_STATES] = (crj * br - cij * bi).astype(BF16)
        wb_out[j, :, SLAB_STATES:] = (crj * bi + cij * br).astype(BF16)
        wc_out[j, :SLAB_STATES, :] = c_ref[j, :SLAB_STATES, :].astype(BF16)
        wc_out[j, SLAB_STATES:, :] = (-c_ref[j, SLAB_STATES:, :]).astype(BF16)
        wc2_out[j, :, :LANES] = c_ref[j, :SLAB_STATES, :].astype(BF16)
        wc2_out[j, :, LANES:] = (-c_ref[j, SLAB_STATES:, :]).astype(BF16)


def _ssm_prep(a_re, a_im, log_dt, b_re, b_im, c_re, c_im):
    eye = jnp.eye(SUBLANES, dtype=F32)

    def blk_b(b):
        t = b.reshape(N_SLABS, 8, SSM_STATE, SSM_GROUP).transpose(0, 1, 3, 2)
        return jnp.einsum('jghp,gk->jghkp', t, eye).reshape(N_SLABS, LANES, SLAB_STATES)

    def blk_c(c):
        t = c.reshape(N_SLABS, 8, SSM_GROUP, SSM_STATE).transpose(0, 1, 3, 2)
        return jnp.einsum('jgph,gk->jgpkh', t, eye).reshape(N_SLABS, SLAB_STATES, LANES)

    row = lambda a: a.reshape(1, N_STATES)
    ldt = jnp.broadcast_to(log_dt[:, None], (N_SSM_GROUPS, SSM_STATE))
    c_all = jnp.concatenate([blk_c(c_re), blk_c(c_im)], axis=1)
    return pl.pallas_call(
        _ssm_prep_kernel,
        out_shape=[jax.ShapeDtypeStruct((SUBLANES, N_STATES), F32),
                   jax.ShapeDtypeStruct((SUBLANES, N_STATES), F32),
                   jax.ShapeDtypeStruct((N_SLABS, LANES, 2 * SLAB_STATES), BF16),
                   jax.ShapeDtypeStruct((N_SLABS, 2 * SLAB_STATES, LANES), BF16),
                   jax.ShapeDtypeStruct((N_SLABS, SLAB_STATES, 2 * LANES), BF16)],
        compiler_params=_cparams(None),
        name="ssm_prep",
    )(row(a_re), row(a_im), row(ldt), blk_b(b_re), blk_b(b_im), c_all)


def _glu_residual(x, y, u, d, wa, wg):
    z = _gelu_tanh(y + d * u).astype(BF16)
    out = jnp.dot(z, wa, preferred_element_type=F32) * _sigmoid(jnp.dot(z, wg, preferred_element_type=F32))
    return x + out


def _ssm_prompt_kernel(x_ref, g_ref, wb_ref, wc_ref, a_ref, bc_ref, d_ref, wa_ref, wg_ref,
                       xo_ref, st_ref, z_scr, u_scr, o_scr, xs_scr, *, nb, lc):
    i = pl.program_id(0)
    blocks_per_slab = SLAB_STATES // LANES
    half = SUBLANES // 2
    pairs = lc // 2
    rows = half * lc
    roll_half = lambda v: pltpu.roll(v, half, 1)

    @pl.when(i == 0)
    def _():
        xs_scr[...] = jnp.zeros((SUBLANES, N_STATES), F32)
        if nb < half:
            u_scr[...] = jnp.zeros(u_scr.shape, F32)

    for b in range(nb):
        u = _rms(x_ref[b], g_ref[...])
        for c in range(N_SLABS):
            u_scr[c, pl.ds(b, lc, stride=half), :] = u[:, LANES * c:LANES * (c + 1)]

    lower = lax.broadcasted_iota(jnp.int32, (pairs, SUBLANES, LANES), 1) < half
    ys = [None] * N_SLABS

    def project_in(j):
        res = jnp.dot(u_scr[j].astype(BF16), wb_ref[j], preferred_element_type=F32)
        for c in range(blocks_per_slab):
            re = res[:, LANES * c:LANES * (c + 1)].reshape(pairs, SUBLANES, LANES)
            im = res[:, SLAB_STATES + LANES * c:SLAB_STATES + LANES * (c + 1)].reshape(pairs, SUBLANES, LANES)
            blk = blocks_per_slab * j + c
            z_scr[blk, :, 0] = jnp.where(lower, re, roll_half(im))
            z_scr[blk, :, 1] = jnp.where(lower, roll_half(re), im)

    def scan(blks):
        a = [a_ref[:, LANES * q:LANES * (q + 1)] for q in blks]
        bc = [bc_ref[:, LANES * q:LANES * (q + 1)] for q in blks]
        xs = [xs_scr[:, LANES * q:LANES * (q + 1)] for q in blks]
        for tp in range(pairs):
            for par in range(2):
                for k, q in enumerate(blks):
                    xs[k] = a[k] * xs[k] + bc[k] * pltpu.roll(xs[k], half, 0) + z_scr[q, tp, par]
                    z_scr[q, tp, par] = xs[k]
        for k, q in enumerate(blks):
            xs_scr[:, LANES * q:LANES * (q + 1)] = xs[k]

    def project_out(j):
        s = jnp.concatenate([z_scr[blocks_per_slab * j + c].reshape(SUBLANES * lc, LANES)
                             for c in range(blocks_per_slab)], axis=1).astype(BF16)
        p3 = jnp.dot(s, wc_ref[j], preferred_element_type=F32).reshape(lc, SUBLANES, 2 * LANES)
        y4 = (p3[:, :, :LANES] + roll_half(p3[:, :, LANES:])).reshape(pairs, 2, SUBLANES, LANES)
        ys[j] = jnp.where(lower, y4[:, 0], roll_half(y4[:, 1])).reshape(rows, LANES)

    slabs_per_group = 2
    n_groups = N_SLABS // slabs_per_group
    group_slabs = lambda gi: range(slabs_per_group * gi, slabs_per_group * (gi + 1))
    for gi in range(n_groups + 2):
        if gi < n_groups:
            for j in group_slabs(gi):
                project_in(j)
        if 1 <= gi <= n_groups:
            lo = slabs_per_group * blocks_per_slab * (gi - 1)
            scan(list(range(lo, lo + slabs_per_group * blocks_per_slab)))
        if gi >= 2:
            for j in group_slabs(gi - 2):
                project_out(j)
    st_ref[...] = xs_scr[...]
    y = jnp.concatenate(ys, axis=1)
    u_all = jnp.concatenate([u_scr[c] for c in range(N_SLABS)], axis=1)
    z = _gelu_tanh(y + d_ref[...] * u_all).astype(BF16)
    out = (jnp.dot(z, wa_ref[...], preferred_element_type=F32)
           * _sigmoid(jnp.dot(z, wg_ref[...], preferred_element_type=F32)))
    for c in range(N_SLABS):
        o_scr[c] = out[:, LANES * c:LANES * (c + 1)]
    for b in range(nb):
        xo_ref[b] = x_ref[b] + jnp.concatenate(
            [o_scr[c, pl.ds(b, lc, stride=half), :] for c in range(N_SLABS)], axis=1)


def _ssm_prompt_layer(x, g, prep, d, wa, wg):
    a_rows, bc_rows, wb, _, wc2 = prep
    nb, t, dm = x.shape
    half = SUBLANES // 2
    assert nb <= half
    lc = min(128, t)
    kern = functools.partial(_ssm_prompt_kernel, nb=nb, lc=lc)
    return pl.pallas_call(
        kern,
        grid=(t // lc,),
        in_specs=[
            pl.BlockSpec((nb, lc, dm), lambda i: (0, i, 0)),
            _full((1, dm)),
            _full(wb.shape), _full(wc2.shape),
            _full((SUBLANES, N_STATES)), _full((SUBLANES, N_STATES)),
            _full((1, dm)), _full((dm, dm)), _full((dm, dm)),
        ],
        out_specs=[pl.BlockSpec((nb, lc, dm), lambda i: (0, i, 0)), _full((SUBLANES, N_STATES))],
        out_shape=[jax.ShapeDtypeStruct((nb, t, dm), F32), jax.ShapeDtypeStruct((SUBLANES, N_STATES), F32)],
        scratch_shapes=[
            pltpu.VMEM((N_STATES // LANES, lc // 2, 2, SUBLANES, LANES), F32),
            pltpu.VMEM((N_SLABS, half * lc, LANES), F32),
            pltpu.VMEM((N_SLABS, half * lc, LANES), F32),
            pltpu.VMEM((SUBLANES, N_STATES), F32),
        ],
        compiler_params=_cparams(("arbitrary",)),
        name="ssm_prompt",
    )(x, g, wb, wc2, a_rows, bc_rows, d, wa, wg)


def _ssm_sample_kernel(x_ref, g_ref, wb_ref, wc_ref, a_ref, bc_ref, h0r_ref, h0i_ref, d_ref, wa_ref, wg_ref,
                       xo_ref, sr_ref, si_ref):
    x = x_ref[...]
    u = _rms(x, g_ref[...])
    ub = u.astype(BF16)
    ys = []
    for j in range(N_SLABS):
        sl = slice(SLAB_STATES * j, SLAB_STATES * (j + 1))
        res = jnp.dot(ub[:, LANES * j:LANES * (j + 1)], wb_ref[j], preferred_element_type=F32)
        lr = a_ref[0:1, sl]
        li = bc_ref[SUBLANES - 1:SUBLANES, sl]
        h0r, h0i = h0r_ref[:, sl], h0i_ref[:, sl]
        xr = res[:, :SLAB_STATES] + (lr * h0r - li * h0i)
        xi = res[:, SLAB_STATES:] + (lr * h0i + li * h0r)
        sr_ref[:, sl] = xr
        si_ref[:, sl] = xi
        s = jnp.concatenate([xr, xi], axis=1).astype(BF16)
        ys.append(jnp.dot(s, wc_ref[j], preferred_element_type=F32))
    y = jnp.concatenate(ys, axis=1)
    xo_ref[...] = _glu_residual(x, y, u, d_ref[...], wa_ref[...], wg_ref[...])


def _ssm_sample_layer(x, g, prep, h0r, h0i, d, wa, wg):
    a_rows, bc_rows, wb, wc, _ = prep
    n = x.shape[0]
    return pl.pallas_call(
        _ssm_sample_kernel,
        out_shape=[jax.ShapeDtypeStruct(x.shape, F32),
                   jax.ShapeDtypeStruct((n, N_STATES), F32),
                   jax.ShapeDtypeStruct((n, N_STATES), F32)],
        compiler_params=_cparams(None),
        name="ssm_sample",
    )(x, g, wb, wc, a_rows, bc_rows, h0r, h0i, d, wa, wg)


ROUTER_W = LANES
EXPERT_LANE0 = N_EXPERT_GROUPS


def _route_topk(h, wr, br):
    h_hi = h.astype(BF16)
    h_lo = (h - h_hi.astype(F32)).astype(BF16)
    w_hi = wr.astype(BF16)
    w_lo = (wr - w_hi.astype(F32)).astype(BF16)
    dot = lambda a, b: jnp.dot(a, b, preferred_element_type=F32)
    both = dot(h_hi, jnp.concatenate([w_hi, w_lo], axis=1))
    logits = both[:, :ROUTER_W] + (dot(h_lo, w_hi) + both[:, ROUTER_W:]) + br
    lane_i = lax.broadcasted_iota(jnp.int32, logits.shape, 1)
    lane = lane_i.astype(F32)
    neg = -jnp.inf
    past_end = float(ROUTER_W)
    gl = jnp.where(lane_i < N_EXPERT_GROUPS, logits, neg)
    gmax = jnp.max(gl, axis=-1, keepdims=True)
    g_sel = jnp.min(jnp.where(gl == gmax, lane, past_end), axis=-1, keepdims=True)
    p_grp = 1.0 / jnp.sum(jnp.exp(gl - gmax), axis=-1, keepdims=True)
    lane_grp = ((lane_i - EXPERT_LANE0) // EXPERTS_PER_GROUP).astype(F32)
    in_grp = (lane_i >= EXPERT_LANE0) & (lane_i < EXPERT_LANE0 + N_EXPERTS) & (lane_grp == g_sel)
    el = jnp.where(in_grp, logits, neg)
    v1 = jnp.max(el, axis=-1, keepdims=True)
    i1 = jnp.min(jnp.where(el == v1, lane, past_end), axis=-1, keepdims=True)
    el2 = jnp.where(lane == i1, neg, el)
    v2 = jnp.max(el2, axis=-1, keepdims=True)
    i2 = jnp.min(jnp.where(el2 == v2, lane, past_end), axis=-1, keepdims=True)
    e2 = jnp.exp(v2 - v1)
    w1 = p_grp / (1.0 + e2)
    w2 = p_grp * e2 / (1.0 + e2)
    return lane, i1, i2, w1, w2


def _route(h, wr, br):
    lane, i1, i2, w1, w2 = _route_topk(h, wr, br)
    return jnp.where(lane == i1, w1, 0.0) + jnp.where(lane == i2, w2, 0.0)


def _moe_kernel(x_ref, p_ref, gffn_ref, wr_ref, br_ref, wg_ref, wu_ref, wd_ref,
                gple_ref, wpg_ref, wpp_ref, gfin_ref, o_ref, hn_scr, gates_scr, acc_scr, *, final):
    e = pl.program_id(1)

    @pl.when(e == 0)
    def _():
        x = x_ref[...]
        h = _rms(x, gffn_ref[...])
        hn_scr[...] = h.astype(BF16)
        gates_scr[...] = _route(h, wr_ref[...], br_ref[...])
        acc_scr[...] = x

    hb = hn_scr[...]
    gt = jnp.dot(hb, wg_ref[0].astype(BF16), preferred_element_type=F32)
    up = jnp.dot(hb, wu_ref[0].astype(BF16), preferred_element_type=F32)
    a = (gt * _sigmoid(gt) * up).astype(BF16)
    gates = gates_scr[...]
    lane = lax.broadcasted_iota(jnp.int32, gates.shape, 1)
    ge = jnp.sum(jnp.where(lane == e + EXPERT_LANE0, gates, 0.0), axis=-1, keepdims=True)
    acc_scr[...] += ge * jnp.dot(a, wd_ref[0].astype(BF16), preferred_element_type=F32)

    @pl.when(e == pl.num_programs(1) - 1)
    def _():
        x1 = acc_scr[...]
        gate = _sigmoid(jnp.dot(_rms(x1, gple_ref[...]).astype(BF16), wpg_ref[...], preferred_element_type=F32))
        x2 = x1 + gate * jnp.dot(p_ref[...].astype(BF16), wpp_ref[...], preferred_element_type=F32)
        o_ref[...] = _rms(x2, gfin_ref[...]) if final else x2


def _moe_ple_layer(li, x, p, gffn, wr, br, wg, wu, wd, gple, wpg, wpp, gfin, final):
    n, d = x.shape
    tm = min(1024, n)
    kern = functools.partial(_moe_kernel, final=final)
    tok = lambda w: pl.BlockSpec((tm, w), lambda t, e: (t, 0))
    const = lambda s: pl.BlockSpec(s, lambda t, e: (0,) * len(s))
    return pl.pallas_call(
        kern,
        grid=(n // tm, N_EXPERTS),
        in_specs=[
            tok(d), pl.BlockSpec((None, tm, PLE_DIM), lambda t, e: (li, t, 0)),
            const((1, d)), const((d, ROUTER_W)), const((1, ROUTER_W)),
            pl.BlockSpec((None, 1, d, D_EXPERT), lambda t, e: (li, e, 0, 0)),
            pl.BlockSpec((None, 1, d, D_EXPERT), lambda t, e: (li, e, 0, 0)),
            pl.BlockSpec((None, 1, D_EXPERT, d), lambda t, e: (li, e, 0, 0)),
            const((1, d)), const((d, d)), const((PLE_DIM, d)), const((1, d)),
        ],
        out_specs=tok(d),
        out_shape=jax.ShapeDtypeStruct((n, d), F32),
        scratch_shapes=[pltpu.VMEM((tm, d), BF16), pltpu.VMEM((tm, ROUTER_W), F32), pltpu.VMEM((tm, d), F32)],
        compiler_params=_cparams(("arbitrary", "arbitrary")),
        name="moe_ple",
    )(x, p, gffn, wr, br, wg, wu, wd, gple, wpg, wpp, gfin)


TOKEN_DATA_ROWS = D_MODEL // LANES
TOKEN_PITCH = TOKEN_DATA_ROWS + 1
INFO_W1, INFO_W2, INFO_E1, INFO_E2, INFO_R1, INFO_R2 = range(6)
EXPERT_TILE = 256
ZERO_RUN = 256


def _to_token_tiles(ref_2d, val, n, first=0):
    for s in range(TOKEN_DATA_ROWS):
        ref_2d[pl.ds(first * TOKEN_PITCH + s, n, stride=TOKEN_PITCH), :] = val[:, LANES * s:LANES * (s + 1)]


def _zero_spare_rows(ref_2d, n):
    ref_2d[pl.ds(TOKEN_DATA_ROWS, n, stride=TOKEN_PITCH), :] = jnp.zeros((n, LANES), F32)


def _from_token_tiles(ref_2d, n, first=0):
    return jnp.concatenate([ref_2d[pl.ds(first * TOKEN_PITCH + s, n, stride=TOKEN_PITCH), :]
                            for s in range(TOKEN_DATA_ROWS)], axis=1)


def _token_slots(i, n=1):
    return pl.ds(i * TOKEN_PITCH, n * TOKEN_PITCH)


def _slot_at(first_row):
    return pl.ds(first_row, TOKEN_PITCH)


def _route_kernel(x_ref, g_ref, wr_ref, br_ref, before_ref, info_ref, cnt_ref, base_scr):
    i = pl.program_id(0)

    @pl.when(i == 0)
    def _():
        base_scr[...] = jnp.zeros(base_scr.shape, F32)

    h = _rms(x_ref[...], g_ref[...])
    lane, i1, i2, w1, w2 = _route_topk(h, wr_ref[...], br_ref[...])
    chosen = jnp.where((lane == i1) | (lane == i2), 1.0, 0.0)
    prefix = jnp.dot(before_ref[...], chosen.astype(BF16), preferred_element_type=F32) + base_scr[...]
    rank1 = jnp.sum(jnp.where(lane == i1, prefix, 0.0), axis=-1, keepdims=True)
    rank2 = jnp.sum(jnp.where(lane == i2, prefix, 0.0), axis=-1, keepdims=True)
    base_scr[...] += jnp.sum(chosen, axis=0, keepdims=True)
    cnt_ref[...] = base_scr[...]
    e1 = (i1 - EXPERT_LANE0).astype(F32)
    e2 = (i2 - EXPERT_LANE0).astype(F32)
    info = jnp.zeros(h.shape[:1] + (ROUTER_W,), F32)
    for ln, v in ((INFO_W1, w1), (INFO_W2, w2), (INFO_E1, e1), (INFO_E2, e2), (INFO_R1, rank1), (INFO_R2, rank2)):
        info = jnp.where(lane == ln, v, info)
    info_ref[...] = info


def _route_call(x, gffn, wr, br):
    n, d = x.shape
    tm = 512
    before = jnp.asarray(np.tril(np.ones((tm, tm), np.float32), -1), BF16)
    return pl.pallas_call(
        _route_kernel,
        grid=(n // tm,),
        in_specs=[pl.BlockSpec((tm, d), lambda i: (i, 0)), _full((1, d)), _full((d, ROUTER_W)), _full((1, ROUTER_W)),
                  _full((tm, tm))],
        out_specs=[pl.BlockSpec((tm, ROUTER_W), lambda i: (i, 0)), _full((1, ROUTER_W))],
        out_shape=[jax.ShapeDtypeStruct((n, ROUTER_W), F32), jax.ShapeDtypeStruct((1, ROUTER_W), F32)],
        scratch_shapes=[pltpu.VMEM((1, ROUTER_W), F32)],
        compiler_params=_cparams(("arbitrary",)),
        name="moe_route",
    )(x, gffn, wr, br, before)


def _dispatch_kernel(pos1_ref, pos2_ref, cnt_ref, start_ref, x_ref, g_ref, xs_hbm, hbuf, zbuf, sem, zsem,
                     *, td, n_steps, n_pad_rows):
    i = pl.program_id(0)
    slot = i % 2

    def wait_slot(sl):
        for _ in range(2):
            pltpu.make_async_copy(hbuf.at[sl], hbuf.at[sl], sem.at[sl]).wait()

    @pl.when(i >= 2)
    def _():
        wait_slot(slot)

    @pl.when(i < 2)
    def _():
        _zero_spare_rows(hbuf.at[slot], td)

    _to_token_tiles(hbuf.at[slot], _rms(x_ref[...], g_ref[...]), td)

    def issue(n8, carry):
        for u in range(SUBLANES):
            n = n8 * SUBLANES + u
            t = i * td + n
            src = hbuf.at[slot, _token_slots(n), :]
            pltpu.make_async_copy(src, xs_hbm.at[_slot_at(pos1_ref[t]), :], sem.at[slot]).start(priority=0)
            pltpu.make_async_copy(src, xs_hbm.at[_slot_at(pos2_ref[t]), :], sem.at[slot]).start(priority=1)
        return carry

    lax.fori_loop(0, td // SUBLANES, issue, 0)

    @pl.when(i == 0)
    def _():
        zbuf[...] = jnp.zeros(zbuf.shape, zbuf.dtype)

        def zero_run(first, n_slots):
            pltpu.make_async_copy(zbuf.at[_token_slots(0, n_slots), :], xs_hbm.at[_token_slots(first, n_slots), :],
                                  zsem).start()

        for e in range(N_EXPERTS + 1):
            lo = start_ref[e] + cnt_ref[e]
            n = start_ref[e + 1] - lo

            def full_run(k, carry, lo=lo):
                zero_run(lo + k * ZERO_RUN, ZERO_RUN)
                return carry

            lax.fori_loop(0, n // ZERO_RUN, full_run, 0)
            cur = lo + (n // ZERO_RUN) * ZERO_RUN
            p = ZERO_RUN // 2
            while p >= 1:
                @pl.when((n & p) != 0)
                def _(cur=cur, p=p):
                    zero_run(cur, p)

                cur = cur + (n & p)
                p //= 2

    @pl.when(i == n_steps - 1)
    def _():
        wait_slot(slot)
        if n_steps > 1:
            wait_slot(1 - slot)
        pad = xs_hbm.at[_token_slots(0, n_pad_rows), :]
        pltpu.make_async_copy(pad, pad, zsem).wait()


def _dispatch_call(pos1, pos2, cnt, start, x, gffn, a_pad, n_pad_rows):
    n, d = x.shape
    td = 512
    n_steps = n // td
    kern = functools.partial(_dispatch_kernel, td=td, n_steps=n_steps, n_pad_rows=n_pad_rows)
    return pl.pallas_call(
        kern,
        grid_spec=pltpu.PrefetchScalarGridSpec(
            num_scalar_prefetch=4,
            grid=(n_steps,),
            in_specs=[pl.BlockSpec((td, d), lambda i, *_: (i, 0)), pl.BlockSpec((1, d), lambda i, *_: (0, 0))],
            out_specs=pl.BlockSpec(memory_space=pl.ANY),
            scratch_shapes=[pltpu.VMEM((2, td * TOKEN_PITCH, LANES), F32),
                            pltpu.VMEM((ZERO_RUN * TOKEN_PITCH, LANES), F32),
                            pltpu.SemaphoreType.DMA((2,)), pltpu.SemaphoreType.DMA(())],
        ),
        out_shape=jax.ShapeDtypeStruct((a_pad * TOKEN_PITCH, LANES), F32),
        compiler_params=_cparams(("arbitrary",)),
        name="moe_dispatch",
    )(pos1, pos2, cnt, start, x, gffn)


EXPERT_IN_SLOTS = 3


TILES_PER_STEP = 2


def _experts_kernel(te_ref, xs_hbm, *refs, te, n_steps):
    nu = TILES_PER_STEP
    w_refs = [refs[3 * u:3 * u + 3] for u in range(nu)]
    y_ref, xbuf, xsem = refs[3 * nu:3 * nu + 3]
    w_scrs = [refs[3 * nu + 3 + 3 * u:3 * nu + 6 + 3 * u] for u in range(nu)]
    t = pl.program_id(0)
    rows = nu * te * TOKEN_PITCH

    def fetch(step):
        slot = step % EXPERT_IN_SLOTS
        src = xs_hbm.at[pl.ds(pl.multiple_of(step * rows, rows), rows), :]
        return pltpu.make_async_copy(src, xbuf.at[slot], xsem.at[slot])

    @pl.when(t == 0)
    def _():
        for ahead in range(min(EXPERT_IN_SLOTS - 1, n_steps)):
            fetch(ahead).start()

    @pl.when(t + EXPERT_IN_SLOTS - 1 < n_steps)
    def _():
        fetch(t + EXPERT_IN_SLOTS - 1).start()

    for u in range(nu):
        tile = nu * t + u

        @pl.when((t == 0) | (te_ref[tile] != te_ref[jnp.maximum(tile - nu, 0)]))
        def _(u=u):
            for scr, ref in zip(w_scrs[u], w_refs[u]):
                scr[...] = ref[0].astype(BF16)

    fetch(t).wait()
    for u in range(nu):
        wg_scr, wu_scr, wd_scr = w_scrs[u]
        x = _from_token_tiles(xbuf.at[t % EXPERT_IN_SLOTS], te, u * te).astype(BF16)
        gt = jnp.dot(x, wg_scr[...], preferred_element_type=F32)
        up = jnp.dot(x, wu_scr[...], preferred_element_type=F32)
        a = (gt * _sigmoid(gt) * up).astype(BF16)
        _to_token_tiles(y_ref, jnp.dot(a, wd_scr[...], preferred_element_type=F32), te, u * te)
    _zero_spare_rows(y_ref, nu * te)


def _experts_call(li, tile_expert, xs, wg, wu, wd, te):
    nu = TILES_PER_STEP
    rows = nu * te * TOKEN_PITCH
    assert xs.shape[0] % rows == 0
    n_steps = xs.shape[0] // rows
    d = wg.shape[2]
    kern = functools.partial(_experts_kernel, te=te, n_steps=n_steps)

    def weight_specs(u):
        pick = lambda t, ex: (li, ex[nu * t + u], 0, 0)
        return [pl.BlockSpec((None, 1, d, D_EXPERT), pick), pl.BlockSpec((None, 1, d, D_EXPERT), pick),
                pl.BlockSpec((None, 1, D_EXPERT, d), pick)]

    w_scratch = [pltpu.VMEM((d, D_EXPERT), BF16), pltpu.VMEM((d, D_EXPERT), BF16), pltpu.VMEM((D_EXPERT, d), BF16)]
    return pl.pallas_call(
        kern,
        grid_spec=pltpu.PrefetchScalarGridSpec(
            num_scalar_prefetch=1,
            grid=(n_steps,),
            in_specs=[pl.BlockSpec(memory_space=pl.ANY)] + [s for u in range(nu) for s in weight_specs(u)],
            out_specs=pl.BlockSpec((rows, LANES), lambda t, ex: (t, 0)),
            scratch_shapes=[pltpu.VMEM((EXPERT_IN_SLOTS, rows, LANES), F32),
                            pltpu.SemaphoreType.DMA((EXPERT_IN_SLOTS,))] + w_scratch * nu,
        ),
        out_shape=jax.ShapeDtypeStruct((n_steps * rows, LANES), F32),
        compiler_params=_cparams(("arbitrary",)),
        name="moe_experts",
    )(tile_expert, xs, *([wg, wu, wd] * nu))


def _combine_kernel(pos1_ref, pos2_ref, x_ref, info_ref, p_ref, gple_ref, wpg_ref, wpp_ref, gfin_ref, ys_hbm,
                    o_ref, ybuf, sem, *, tm, n_steps, final):
    i = pl.program_id(0)
    slot = i % 2

    def issue(step, sl):
        def body(n8, carry):
            for u in range(SUBLANES):
                n = n8 * SUBLANES + u
                t = step * tm + n
                for k, pos_ref in enumerate((pos1_ref, pos2_ref)):
                    pltpu.make_async_copy(ys_hbm.at[_slot_at(pos_ref[t]), :],
                                          ybuf.at[sl, k, _token_slots(n), :], sem.at[sl]).start(priority=k)
            return carry

        lax.fori_loop(0, tm // SUBLANES, body, 0)

    def wait_rows(sl):
        for k in range(2):
            pltpu.make_async_copy(ybuf.at[sl, k], ybuf.at[sl, k], sem.at[sl]).wait()

    @pl.when(i == 0)
    def _():
        issue(0, 0)

    @pl.when(i + 1 < n_steps)
    def _():
        issue(i + 1, 1 - slot)

    wait_rows(slot)
    part = tm // 2
    for first in (0, part):
        tok = slice(first, first + part)
        y1 = _from_token_tiles(ybuf.at[slot, 0], part, first)
        y2 = _from_token_tiles(ybuf.at[slot, 1], part, first)
        info = info_ref[tok, :]
        x1 = x_ref[tok, :] + (info[:, INFO_W1:INFO_W1 + 1] * y1 + info[:, INFO_W2:INFO_W2 + 1] * y2)
        gate = _sigmoid(jnp.dot(_rms(x1, gple_ref[...]).astype(BF16), wpg_ref[...], preferred_element_type=F32))
        x2 = x1 + gate * jnp.dot(p_ref[tok, :].astype(BF16), wpp_ref[...], preferred_element_type=F32)
        o_ref[tok, :] = _rms(x2, gfin_ref[...]) if final else x2


def _combine_call(li, pos1, pos2, x, info, p, gple, wpg, wpp, gfin, ys, final):
    n, d = x.shape
    tm = 512
    n_steps = n // tm
    kern = functools.partial(_combine_kernel, tm=tm, n_steps=n_steps, final=final)
    tok = lambda w: pl.BlockSpec((tm, w), lambda i, *_: (i, 0))
    const = lambda s: pl.BlockSpec(s, lambda i, *_: (0,) * len(s))
    return pl.pallas_call(
        kern,
        grid_spec=pltpu.PrefetchScalarGridSpec(
            num_scalar_prefetch=2,
            grid=(n_steps,),
            in_specs=[tok(d), tok(ROUTER_W), pl.BlockSpec((None, tm, PLE_DIM), lambda i, *_: (li, i, 0)),
                      const((1, d)), const((d, d)), const((PLE_DIM, d)),
                      const((1, d)), pl.BlockSpec(memory_space=pl.ANY)],
            out_specs=tok(d),
            scratch_shapes=[pltpu.VMEM((2, 2, tm * TOKEN_PITCH, LANES), F32), pltpu.SemaphoreType.DMA((2,))],
        ),
        out_shape=jax.ShapeDtypeStruct((n, d), F32),
        compiler_params=_cparams(("arbitrary",)),
        name="moe_combine",
    )(pos1, pos2, x, info, p, gple, wpg, wpp, gfin, ys)


def _positions_kernel(info_ref, start_ref, pos_ref):
    info = info_ref[...]
    lane = lax.broadcasted_iota(jnp.int32, info.shape, 1)
    lane_expert = (lane - EXPERT_LANE0).astype(F32)
    start = start_ref[...]
    cols = []
    for e_lane, r_lane in ((INFO_E1, INFO_R1), (INFO_E2, INFO_R2)):
        seg = jnp.sum(jnp.where(lane_expert == info[:, e_lane:e_lane + 1], start, 0.0), axis=-1, keepdims=True)
        cols.append((seg + info[:, r_lane:r_lane + 1]) * float(TOKEN_PITCH))
    both = jnp.where(lane == 0, cols[0], jnp.where(lane == 1, cols[1], 0.0))
    pos_ref[...] = both.T[:SUBLANES, :].astype(jnp.int32)


def _positions_call(info, start_row):
    n = info.shape[0]
    tm = 2048 if n % 2048 == 0 else 512
    return pl.pallas_call(
        _positions_kernel,
        grid=(n // tm,),
        in_specs=[pl.BlockSpec((tm, ROUTER_W), lambda i: (i, 0)), _full((1, ROUTER_W))],
        out_specs=pl.BlockSpec((SUBLANES, tm), lambda i: (0, i)),
        out_shape=jax.ShapeDtypeStruct((SUBLANES, n), jnp.int32),
        compiler_params=_cparams(("arbitrary",)),
        name="moe_positions",
    )(info, start_row)


def _moe_routed_layer(li, x, p, gffn, wr, br, wg, wu, wd, gple, wpg, wpp, gfin, final):
    n = x.shape[0]
    te = EXPERT_TILE
    n_assign = 2 * n
    assert n_assign % te == 0
    a_pad = n_assign + N_EXPERTS * te
    info, cnt = _route_call(x, gffn, wr, br)
    counts = cnt[0, EXPERT_LANE0:EXPERT_LANE0 + N_EXPERTS].astype(jnp.int32)
    padded = (counts + te - 1) // te * te
    seg_end = jnp.cumsum(padded)
    seg_start = seg_end - padded
    start_row = jnp.zeros((1, ROUTER_W), F32).at[0, EXPERT_LANE0:EXPERT_LANE0 + N_EXPERTS].set(seg_start.astype(F32))
    pos = _positions_call(info, start_row)
    pos1, pos2 = pos[0], pos[1]
    start = jnp.concatenate([seg_start, seg_end[-1:], jnp.full((1,), a_pad, jnp.int32)])
    cnt_pad = jnp.concatenate([counts, jnp.zeros((1,), jnp.int32)])
    tile_row0 = jnp.arange(a_pad // te, dtype=jnp.int32) * te
    tile_expert = jnp.minimum(jnp.sum(seg_end[None, :] <= tile_row0[:, None], axis=1), N_EXPERTS - 1).astype(jnp.int32)
    xs = _dispatch_call(pos1, pos2, cnt_pad, start, x, gffn, a_pad, N_EXPERTS * te)
    ys = _experts_call(li, tile_expert, xs, wg, wu, wd, te)
    return _combine_call(li, pos1, pos2, x, info, p, gple, wpg, wpp, gfin, ys, final)


def kernel(x_prompt, x_sample, cache_k_win, cache_v_win, state_ssm_re, state_ssm_im, p_prompt, p_sample,
           rel_bias_table, g_mix, w_qkv, w_o, sinks, ssm_a_re, ssm_a_im, ssm_log_dt, ssm_b_re, ssm_b_im,
           ssm_c_re, ssm_c_im, ssm_d, w_glu_a, w_glu_b, g_ffn, w_router_group, b_router_group,
           w_router_expert, b_router_expert, w_exp_gate, w_exp_up, w_exp_down, g_ple, w_ple_gate,
           w_ple_proj, g_final):
    bsz, seq, d = x_prompt.shape
    ns = x_sample.shape[0]
    depth = g_mix.shape[0]
    row = lambda v: v.reshape(1, -1).astype(F32)

    qi = np.arange(WINDOW)[:, None]
    sj = np.arange(2 * WINDOW)[None, :]
    dist = qi + WINDOW - sj
    idx_p = _t5_bucket_np(dist)
    ok_p = ((dist >= 0) & (dist < WINDOW)).astype(np.int32)
    idx_s = np.broadcast_to(_t5_bucket_np(WINDOW - 1 - np.arange(WINDOW))[None, :], (SUBLANES, WINDOW))
    bias_p = _band_bias(rel_bias_table, idx_p, ok_p)
    bias_s = _bias_from_table(rel_bias_table, idx_s)[:, 0, :]

    wqkv_b = w_qkv[0].astype(BF16)
    wo_b = w_o[0].astype(BF16)
    wpg_b, wpp_b = w_ple_gate.astype(BF16), w_ple_proj.astype(BF16)
    wa_b, wgl_b = w_glu_a[0].astype(BF16), w_glu_b[0].astype(BF16)
    pad = ROUTER_W - N_EXPERT_GROUPS - N_EXPERTS
    wr = jnp.concatenate([w_router_group, w_router_expert.reshape(depth, d, N_EXPERTS),
                          jnp.zeros((depth, d, pad), F32)], axis=-1)
    br = jnp.concatenate([b_router_group, b_router_expert.reshape(depth, N_EXPERTS),
                          jnp.zeros((depth, pad), F32)], axis=-1)

    prep = _ssm_prep(ssm_a_re[0], ssm_a_im[0], ssm_log_dt[0], ssm_b_re[0], ssm_b_im[0], ssm_c_re[0], ssm_c_im[0])
    d_row = row(ssm_d[0])

    def moe(x2d, p_all, i):
        layer = _moe_routed_layer if 2 * x2d.shape[0] >= N_EXPERTS * EXPERT_TILE else _moe_ple_layer
        return layer(i, x2d, p_all, row(g_ffn[i]), wr[i], br[i:i + 1], w_exp_gate, w_exp_up, w_exp_down,
                     row(g_ple[i]), wpg_b[i], wpp_b[i], row(g_final), final=(i == depth - 1))

    pp = p_prompt.reshape(depth, bsz * seq, PLE_DIM)
    xp, kp, vp = _attn_prompt_layer(x_prompt, row(g_mix[0]), wqkv_b, wo_b, bias_p, sinks[0])
    xp = moe(xp.reshape(bsz * seq, d), pp, 0).reshape(bsz, seq, d)
    xp, st_p = _ssm_prompt_layer(xp, row(g_mix[1]), prep, d_row, wa_b, wgl_b)
    y_prompt = moe(xp.reshape(bsz * seq, d), pp, 1).reshape(bsz, seq, d)

    ps = p_sample.reshape(depth, ns, PLE_DIM)
    xs = x_sample.reshape(ns, d)
    xs, kws, vws = _attn_sample_layer(xs, row(g_mix[0]), wqkv_b, wo_b, bias_s, sinks[0],
                                      cache_k_win[0], cache_v_win[0])
    xs = moe(xs, ps, 0)
    xs, sr_s, si_s = _ssm_sample_layer(xs, row(g_mix[1]), prep, state_ssm_re[0].reshape(ns, N_STATES),
                                       state_ssm_im[0].reshape(ns, N_STATES), d_row, wa_b, wgl_b)
    y_sample = moe(xs, ps, 1).reshape(ns, 1, d)

    half = SUBLANES // 2
    win = lambda a, n: a.reshape(1, n, WINDOW, N_KV_HEADS, HEAD_DIM)
    st = lambda a, n: a.reshape(1, n, N_SSM_GROUPS, SSM_STATE)
    return (y_prompt, y_sample, win(kp, bsz), win(vp, bsz), win(kws, ns), win(vws, ns),
            st(st_p[:bsz], bsz), st(st_p[half:half + bsz], bsz), st(sr_s, ns), st(si_s, ns))
```

```python
import functools
import math

import numpy as np
import jax
import jax.numpy as jnp
from jax import lax
from jax.experimental import pallas as pl
from jax.experimental.pallas import tpu as pltpu

F32 = jnp.float32
BF16 = jnp.bfloat16

D_MODEL = 1024
HEAD_DIM = 64
N_HEADS = 16
N_KV_HEADS = 2
GQ = N_HEADS // N_KV_HEADS
WINDOW = 128
N_BUCKETS = 32
MAX_EXACT = 16
MAX_DISTANCE = 128
SSM_GROUP = 16
N_SSM_GROUPS = 64
SSM_STATE = 64
N_STATES = N_SSM_GROUPS * SSM_STATE
N_EXPERT_GROUPS = 4
EXPERTS_PER_GROUP = 8
N_EXPERTS = 32
D_EXPERT = 256
PLE_DIM = 256
RMS_EPS = 1e-6
NEG_INF = -1e30

LANES = 128
SUBLANES = 8
KV_W = N_KV_HEADS * HEAD_DIM
QKV_W = D_MODEL + 2 * KV_W
SLAB_STATES = 512
N_SLABS = D_MODEL // LANES
VMEM_LIMIT = 56 * 1024 * 1024


def _cparams(sem):
    return pltpu.CompilerParams(dimension_semantics=sem, vmem_limit_bytes=VMEM_LIMIT)


def _rms(x, g):
    return x * lax.rsqrt(jnp.mean(x * x, axis=-1, keepdims=True) + RMS_EPS) * g


def _gelu_tanh(x):
    c = math.sqrt(2.0 / math.pi)
    return x * (0.5 * (1.0 + jnp.tanh(c * (x + 0.044715 * (x * x * x)))))


def _sigmoid(x):
    return 1.0 / (1.0 + jnp.exp(-x))


def _full(shape):
    n = len(shape)
    return pl.BlockSpec(shape, lambda *_: (0,) * n)


def _t5_bucket_np(dist):
    n = np.maximum(dist, 0)
    nf = np.maximum(n, 1).astype(np.float64)
    large = MAX_EXACT + (np.log(nf / MAX_EXACT) / math.log(MAX_DISTANCE / MAX_EXACT)
                         * (N_BUCKETS - MAX_EXACT)).astype(np.int32)
    large = np.minimum(large, N_BUCKETS - 1)
    return np.where(n < MAX_EXACT, n, large).astype(np.int32)


def _bias_kernel(table_ref, idx_ref, o_ref):
    h = pl.program_id(0)
    idx = idx_ref[...]
    acc = jnp.zeros(idx.shape, F32)
    for b in range(N_BUCKETS):
        acc = jnp.where(idx == b, table_ref[b, h], acc)
    o_ref[0] = acc


def _bias_from_table(table, idx_np):
    q, k = idx_np.shape
    return pl.pallas_call(
        _bias_kernel,
        grid=(N_HEADS,),
        in_specs=[pl.BlockSpec(memory_space=pltpu.SMEM), _full((q, k))],
        out_specs=pl.BlockSpec((1, q, k), lambda h: (h, 0, 0)),
        out_shape=jax.ShapeDtypeStruct((N_HEADS, q, k), F32),
        name="rel_bias",
    )(table, jnp.asarray(idx_np))


HEADS_PER_UNIT = GQ // 2
N_UNITS = N_HEADS // HEADS_PER_UNIT
UNIT_ROWS = HEADS_PER_UNIT * WINDOW


def _band_bias_kernel(table_ref, idx_ref, ok_ref, o_ref):
    h = pl.program_id(0)
    idx = idx_ref[...]
    acc = jnp.zeros(idx.shape, F32)
    for b in range(N_BUCKETS):
        acc = jnp.where(idx == b, table_ref[b, h], acc)
    base = jnp.where(ok_ref[...] != 0, acc, NEG_INF)
    sj = lax.broadcasted_iota(jnp.int32, idx.shape, 1)
    o_ref[0, 0] = base
    o_ref[1, 0] = jnp.where(sj >= WINDOW, base, NEG_INF)


def _band_bias(table, idx_np, ok_np):
    def out_map(h):
        r = h % GQ
        return (0, (h // GQ) * 2 + r % 2, r // 2, 0)

    return pl.pallas_call(
        _band_bias_kernel,
        grid=(N_HEADS,),
        in_specs=[pl.BlockSpec(memory_space=pltpu.SMEM), _full(idx_np.shape), _full(ok_np.shape)],
        out_specs=pl.BlockSpec((2, 1, WINDOW, 2 * WINDOW), out_map),
        out_shape=jax.ShapeDtypeStruct((2, N_UNITS, UNIT_ROWS, 2 * WINDOW), F32),
        name="band_bias",
    )(table, jnp.asarray(idx_np), jnp.asarray(ok_np))


def _attn_prompt_kernel(sink_ref, x_ref, g_ref, wqkv_ref, wo_ref, bias_ref,
                        xo_ref, k_ref, v_ref, kv_scr, q_scr, o_scr, *, tq):
    i = pl.program_id(1)
    nsub = tq // WINDOW

    @pl.when(i == 0)
    def _():
        kv_scr[0:WINDOW, :] = jnp.zeros((WINDOW, 2 * KV_W), F32)

    x = x_ref[0]
    h = _rms(x, g_ref[...]).astype(BF16)
    qkv = jnp.dot(h, wqkv_ref[...], preferred_element_type=F32)
    q_scr[...] = (qkv[:, :D_MODEL] * (HEAD_DIM ** -0.5)).astype(BF16)
    kv_scr[WINDOW:WINDOW + tq, :] = qkv[:, D_MODEL:]
    k_ref[0] = qkv[tq - WINDOW:, D_MODEL:D_MODEL + KV_W]
    v_ref[0] = qkv[tq - WINDOW:, D_MODEL + KV_W:]

    lo = lax.broadcasted_iota(jnp.int32, (2 * WINDOW, KV_W), 1) < HEAD_DIM
    row_head = lax.broadcasted_iota(jnp.int32, (UNIT_ROWS, 1), 0) // WINDOW
    sinks = []
    for u in range(N_UNITS):
        g, p = divmod(u, 2)
        col = jnp.zeros((UNIT_ROWS, 1), F32)
        for jj in range(HEADS_PER_UNIT):
            col = jnp.where(row_head == jj, sink_ref[GQ * g + 2 * jj + p], col)
        sinks.append(col)

    for s in range(nsub):
        band = kv_scr[WINDOW * s:WINDOW * s + 2 * WINDOW, :]
        kband, vband = band[:, :KV_W], band[:, KV_W:]
        kroll = pltpu.roll(kband, HEAD_DIM, 1)
        vroll = pltpu.roll(vband, HEAD_DIM, 1)
        kpad = [[jnp.where(lo, kband, 0.0).astype(BF16), jnp.where(lo, 0.0, kroll).astype(BF16)],
                [jnp.where(lo, kroll, 0.0).astype(BF16), jnp.where(lo, 0.0, kband).astype(BF16)]]
        vpad = [[jnp.where(lo, vband, 0.0).astype(BF16), jnp.where(lo, 0.0, vroll).astype(BF16)],
                [jnp.where(lo, vroll, 0.0).astype(BF16), jnp.where(lo, 0.0, vband).astype(BF16)]]
        variant = jnp.where(i * nsub + s > 0, 0, 1)
        for g in range(N_KV_HEADS):
            slabs = range(HEADS_PER_UNIT * g, HEADS_PER_UNIT * (g + 1))
            qs = jnp.concatenate([q_scr[WINDOW * s:WINDOW * (s + 1), LANES * j:LANES * (j + 1)] for j in slabs],
                                 axis=0)
            acc = None
            for p in range(2):
                u = 2 * g + p
                sc = lax.dot_general(qs, kpad[g][p], (((1,), (1,)), ((), ())), preferred_element_type=F32)
                sc = sc + bias_ref[variant, u]
                m = jnp.maximum(jnp.max(sc, axis=-1, keepdims=True), sinks[u])
                e = jnp.exp(sc - m)
                den = jnp.sum(e, axis=-1, keepdims=True) + jnp.exp(sinks[u] - m)
                pv = jnp.dot(e.astype(BF16), vpad[g][p], preferred_element_type=F32)
                term = pv * (1.0 / den)
                acc = term if acc is None else acc + term
            for jj, j in enumerate(slabs):
                o_scr[WINDOW * s:WINDOW * (s + 1), LANES * j:LANES * (j + 1)] = (
                    acc[WINDOW * jj:WINDOW * (jj + 1)].astype(BF16))

    kv_scr[0:WINDOW, :] = kv_scr[tq:tq + WINDOW, :]
    xo_ref[0] = x + jnp.dot(o_scr[...], wo_ref[...], preferred_element_type=F32)


def _attn_prompt_layer(x, g, wqkv, wo, bias, sinks):
    b, t, d = x.shape
    tq = min(512, t)
    kern = functools.partial(_attn_prompt_kernel, tq=tq)
    return pl.pallas_call(
        kern,
        grid=(b, t // tq),
        in_specs=[
            pl.BlockSpec(memory_space=pltpu.SMEM),
            pl.BlockSpec((1, tq, d), lambda bi, i: (bi, i, 0)),
            _full((1, d)),
            _full((d, QKV_W)),
            _full((d, d)),
            _full((2, N_UNITS, UNIT_ROWS, 2 * WINDOW)),
        ],
        out_specs=[
            pl.BlockSpec((1, tq, d), lambda bi, i: (bi, i, 0)),
            pl.BlockSpec((1, WINDOW, KV_W), lambda bi, i: (bi, 0, 0)),
            pl.BlockSpec((1, WINDOW, KV_W), lambda bi, i: (bi, 0, 0)),
        ],
        out_shape=[
            jax.ShapeDtypeStruct((b, t, d), F32),
            jax.ShapeDtypeStruct((b, WINDOW, KV_W), F32),
            jax.ShapeDtypeStruct((b, WINDOW, KV_W), F32),
        ],
        scratch_shapes=[
            pltpu.VMEM((WINDOW + tq, 2 * KV_W), F32),
            pltpu.VMEM((tq, d), BF16),
            pltpu.VMEM((tq, d), BF16),
        ],
        compiler_params=_cparams(("arbitrary", "arbitrary")),
        name="attn_prompt",
    )(sinks, x, g, wqkv, wo, bias)


def _norm_linear_kernel(x_ref, g_ref, w_ref, o_ref):
    h = _rms(x_ref[...], g_ref[...]).astype(BF16)
    o_ref[...] = jnp.dot(h, w_ref[...], preferred_element_type=F32)


def _norm_linear(x, g, w):
    n, d = x.shape
    return pl.pallas_call(
        _norm_linear_kernel,
        out_shape=jax.ShapeDtypeStruct((n, w.shape[1]), F32),
        compiler_params=_cparams(None),
        name="norm_linear",
    )(x, g, w)


def _linear_residual_kernel(x_ref, o_ref, w_ref, xo_ref):
    xo_ref[...] = x_ref[...] + jnp.dot(o_ref[...].astype(BF16), w_ref[...], preferred_element_type=F32)


def _linear_residual(x, o, w):
    return pl.pallas_call(
        _linear_residual_kernel,
        out_shape=jax.ShapeDtypeStruct(x.shape, F32),
        compiler_params=_cparams(None),
        name="linear_residual",
    )(x, o, w)


def _attn_sample_kernel(sink_ref, q_ref, ck_ref, cv_ref, kn_ref, vn_ref, bias_ref, o_ref, kw_ref, vw_ref):
    kw = jnp.concatenate([ck_ref[:, 1:, :], kn_ref[...]], axis=1)
    vw = jnp.concatenate([cv_ref[:, 1:, :], vn_ref[...]], axis=1)
    kw_ref[...] = kw
    vw_ref[...] = vw
    q = (q_ref[...] * (HEAD_DIM ** -0.5)).astype(BF16)
    sc = jnp.einsum('bhc,bjc->bhj', q, kw.astype(BF16), preferred_element_type=F32)
    sc = sc + bias_ref[...][None]
    sink = sink_ref[...][None]
    m = jnp.maximum(jnp.max(sc, axis=-1, keepdims=True), sink)
    e = jnp.exp(sc - m)
    den = jnp.sum(e, axis=-1, keepdims=True) + jnp.exp(sink - m)
    pr = (e / den).astype(BF16)
    o_ref[...] = jnp.einsum('bhj,bjc->bhc', pr, vw.astype(BF16), preferred_element_type=F32)


def _attn_sample(q3, ck, cv, kn, vn, bias_s, sinks_col):
    n = q3.shape[0]
    bb = 8
    blk = lambda s: pl.BlockSpec((bb,) + s, lambda i: (i, 0, 0))
    return pl.pallas_call(
        _attn_sample_kernel,
        grid=(n // bb,),
        in_specs=[_full((N_HEADS, 1)), blk((N_HEADS, KV_W)), blk((WINDOW, KV_W)), blk((WINDOW, KV_W)),
                  blk((1, KV_W)), blk((1, KV_W)), _full((N_HEADS, WINDOW))],
        out_specs=[blk((N_HEADS, KV_W)), blk((WINDOW, KV_W)), blk((WINDOW, KV_W))],
        out_shape=[jax.ShapeDtypeStruct((n, N_HEADS, KV_W), F32),
                   jax.ShapeDtypeStruct((n, WINDOW, KV_W), F32),
                   jax.ShapeDtypeStruct((n, WINDOW, KV_W), F32)],
        compiler_params=_cparams(("arbitrary",)),
        name="attn_sample",
    )(sinks_col, q3, ck, cv, kn, vn, bias_s)


def _attn_sample_layer(x, g, wqkv, wo, bias_s, sinks, cache_k, cache_v):
    n = x.shape[0]
    qkv = _norm_linear(x, g, wqkv)
    q = qkv[:, :D_MODEL].reshape(n, N_KV_HEADS, GQ, 1, HEAD_DIM)
    place = jnp.eye(N_KV_HEADS, dtype=F32).reshape(1, N_KV_HEADS, 1, N_KV_HEADS, 1)
    q3 = (q * place).reshape(n, N_HEADS, KV_W)
    kn = qkv[:, D_MODEL:D_MODEL + KV_W].reshape(n, 1, KV_W)
    vn = qkv[:, D_MODEL + KV_W:].reshape(n, 1, KV_W)
    o3, kw, vw = _attn_sample(q3, cache_k.reshape(n, WINDOW, KV_W), cache_v.reshape(n, WINDOW, KV_W),
                              kn, vn, bias_s, sinks.reshape(N_HEADS, 1))
    o5 = o3.reshape(n, N_KV_HEADS, GQ, N_KV_HEADS, HEAD_DIM)
    o = jnp.stack([o5[:, gi, :, gi, :] for gi in range(N_KV_HEADS)], axis=1).reshape(n, D_MODEL)
    return _linear_residual(x, o, wo), kw, vw


def _ssm_prep_kernel(ar_ref, ai_ref, ldt_ref, br_ref, bi_ref, c_ref, a_out, bc_out, wb_out, wc_out, wc2_out):
    ar, ai = ar_ref[...], ai_ref[...]
    dt = jnp.exp(ldt_ref[...])
    mag = jnp.exp(ar * dt)
    lr = mag * jnp.cos(ai * dt)
    li = mag * jnp.sin(ai * dt)
    den = ar * ar + ai * ai
    cr = ((lr - 1.0) * ar + li * ai) / den
    ci = (li * ar - (lr - 1.0) * ai) / den
    row = lax.broadcasted_iota(jnp.int32, (SUBLANES, N_STATES), 0)
    a_out[...] = jnp.broadcast_to(lr, (SUBLANES, N_STATES))
    bc_out[...] = jnp.where(row < SUBLANES // 2, -li, li)
    for j in range(N_SLABS):
        crj = cr[:, SLAB_STATES * j:SLAB_STATES * (j + 1)]
        cij = ci[:, SLAB_STATES * j:SLAB_STATES * (j + 1)]
        br, bi = br_ref[j], bi_ref[j]
        wb_out[j, :, :SLAB_STATES] = (crj * br - cij * bi).astype(BF16)
        wb_out[j, :, SLAB_STATES:] = (crj * bi + cij * br).astype(BF16)
        wc_out[j, :SLAB_STATES, :] = c_ref[j, :SLAB_STATES, :].astype(BF16)
        wc_out[j, SLAB_STATES:, :] = (-c_ref[j, SLAB_STATES:, :]).astype(BF16)
        wc2_out[j, :, :LANES] = c_ref[j, :SLAB_STATES, :].astype(BF16)
        wc2_out[j, :, LANES:] = (-c_ref[j, SLAB_STATES:, :]).astype(BF16)


def _ssm_prep(a_re, a_im, log_dt, b_re, b_im, c_re, c_im):
    eye = jnp.eye(SUBLANES, dtype=F32)

    def blk_b(b):
        t = b.reshape(N_SLABS, 8, SSM_STATE, SSM_GROUP).transpose(0, 1, 3, 2)
        return jnp.einsum('jghp,gk->jghkp', t, eye).reshape(N_SLABS, LANES, SLAB_STATES)

    def blk_c(c):
        t = c.reshape(N_SLABS, 8, SSM_GROUP, SSM_STATE).transpose(0, 1, 3, 2)
        return jnp.einsum('jgph,gk->jgpkh', t, eye).reshape(N_SLABS, SLAB_STATES, LANES)

    row = lambda a: a.reshape(1, N_STATES)
    ldt = jnp.broadcast_to(log_dt[:, None], (N_SSM_GROUPS, SSM_STATE))
    c_all = jnp.concatenate([blk_c(c_re), blk_c(c_im)], axis=1)
    return pl.pallas_call(
        _ssm_prep_kernel,
        out_shape=[jax.ShapeDtypeStruct((SUBLANES, N_STATES), F32),
                   jax.ShapeDtypeStruct((SUBLANES, N_STATES), F32),
                   jax.ShapeDtypeStruct((N_SLABS, LANES, 2 * SLAB_STATES), BF16),
                   jax.ShapeDtypeStruct((N_SLABS, 2 * SLAB_STATES, LANES), BF16),
                   jax.ShapeDtypeStruct((N_SLABS, SLAB_STATES, 2 * LANES), BF16)],
        compiler_params=_cparams(None),
        name="ssm_prep",
    )(row(a_re), row(a_im), row(ldt), blk_b(b_re), blk_b(b_im), c_all)


def _glu_residual(x, y, u, d, wa, wg):
    z = _gelu_tanh(y + d * u).astype(BF16)
    out = jnp.dot(z, wa, preferred_element_type=F32) * _sigmoid(jnp.dot(z, wg, preferred_element_type=F32))
    return x + out


def _ssm_prompt_kernel(x_ref, g_ref, wb_ref, wc_ref, a_ref, bc_ref, d_ref, wa_ref, wg_ref,
                       xo_ref, st_ref, z_scr, u_scr, o_scr, xs_scr, *, nb, lc):
    i = pl.program_id(0)
    blocks_per_slab = SLAB_STATES // LANES
    half = SUBLANES // 2
    pairs = lc // 2
    rows = half * lc
    roll_half = lambda v: pltpu.roll(v, half, 1)

    @pl.when(i == 0)
    def _():
        xs_scr[...] = jnp.zeros((SUBLANES, N_STATES), F32)
        if nb < half:
            u_scr[...] = jnp.zeros(u_scr.shape, F32)

    for b in range(nb):
        u = _rms(x_ref[b], g_ref[...])
        for c in range(N_SLABS):
            u_scr[c, pl.ds(b, lc, stride=half), :] = u[:, LANES * c:LANES * (c + 1)]

    lower = lax.broadcasted_iota(jnp.int32, (pairs, SUBLANES, LANES), 1) < half
    ys = [None] * N_SLABS

    def project_in(j):
        res = jnp.dot(u_scr[j].astype(BF16), wb_ref[j], preferred_element_type=F32)
        for c in range(blocks_per_slab):
            re = res[:, LANES * c:LANES * (c + 1)].reshape(pairs, SUBLANES, LANES)
            im = res[:, SLAB_STATES + LANES * c:SLAB_STATES + LANES * (c + 1)].reshape(pairs, SUBLANES, LANES)
            blk = blocks_per_slab * j + c
            z_scr[blk, :, 0] = jnp.where(lower, re, roll_half(im))
            z_scr[blk, :, 1] = jnp.where(lower, roll_half(re), im)

    def scan(blks):
        a = [a_ref[:, LANES * q:LANES * (q + 1)] for q in blks]
        bc = [bc_ref[:, LANES * q:LANES * (q + 1)] for q in blks]
        xs = [xs_scr[:, LANES * q:LANES * (q + 1)] for q in blks]
        for tp in range(pairs):
            for par in range(2):
                for k, q in enumerate(blks):
                    xs[k] = a[k] * xs[k] + bc[k] * pltpu.roll(xs[k], half, 0) + z_scr[q, tp, par]
                    z_scr[q, tp, par] = xs[k]
        for k, q in enumerate(blks):
            xs_scr[:, LANES * q:LANES * (q + 1)] = xs[k]

    def project_out(j):
        s = jnp.concatenate([z_scr[blocks_per_slab * j + c].reshape(SUBLANES * lc, LANES)
                             for c in range(blocks_per_slab)], axis=1).astype(BF16)
        p3 = jnp.dot(s, wc_ref[j], preferred_element_type=F32).reshape(lc, SUBLANES, 2 * LANES)
        y4 = (p3[:, :, :LANES] + roll_half(p3[:, :, LANES:])).reshape(pairs, 2, SUBLANES, LANES)
        ys[j] = jnp.where(lower, y4[:, 0], roll_half(y4[:, 1])).reshape(rows, LANES)

    slabs_per_group = 2
    n_groups = N_SLABS // slabs_per_group
    group_slabs = lambda gi: range(slabs_per_group * gi, slabs_per_group * (gi + 1))
    for gi in range(n_groups + 2):
        if gi < n_groups:
            for j in group_slabs(gi):
                project_in(j)
        if 1 <= gi <= n_groups:
            lo = slabs_per_group * blocks_per_slab * (gi - 1)
            scan(list(range(lo, lo + slabs_per_group * blocks_per_slab)))
        if gi >= 2:
            for j in group_slabs(gi - 2):
                project_out(j)
    st_ref[...] = xs_scr[...]
    y = jnp.concatenate(ys, axis=1)
    u_all = jnp.concatenate([u_scr[c] for c in range(N_SLABS)], axis=1)
    z = _gelu_tanh(y + d_ref[...] * u_all).astype(BF16)
    out = (jnp.dot(z, wa_ref[...], preferred_element_type=F32)
           * _sigmoid(jnp.dot(z, wg_ref[...], preferred_element_type=F32)))
    for c in range(N_SLABS):
        o_scr[c] = out[:, LANES * c:LANES * (c + 1)]
    for b in range(nb):
        xo_ref[b] = x_ref[b] + jnp.concatenate(
            [o_scr[c, pl.ds(b, lc, stride=half), :] for c in range(N_SLABS)], axis=1)


def _ssm_prompt_layer(x, g, prep, d, wa, wg):
    a_rows, bc_rows, wb, _, wc2 = prep
    nb, t, dm = x.shape
    half = SUBLANES // 2
    assert nb <= half
    lc = min(128, t)
    kern = functools.partial(_ssm_prompt_kernel, nb=nb, lc=lc)
    return pl.pallas_call(
        kern,
        grid=(t // lc,),
        in_specs=[
            pl.BlockSpec((nb, lc, dm), lambda i: (0, i, 0)),
            _full((1, dm)),
            _full(wb.shape), _full(wc2.shape),
            _full((SUBLANES, N_STATES)), _full((SUBLANES, N_STATES)),
            _full((1, dm)), _full((dm, dm)), _full((dm, dm)),
        ],
        out_specs=[pl.BlockSpec((nb, lc, dm), lambda i: (0, i, 0)), _full((SUBLANES, N_STATES))],
        out_shape=[jax.ShapeDtypeStruct((nb, t, dm), F32), jax.ShapeDtypeStruct((SUBLANES, N_STATES), F32)],
        scratch_shapes=[
            pltpu.VMEM((N_STATES // LANES, lc // 2, 2, SUBLANES, LANES), F32),
            pltpu.VMEM((N_SLABS, half * lc, LANES), F32),
            pltpu.VMEM((N_SLABS, half * lc, LANES), F32),
            pltpu.VMEM((SUBLANES, N_STATES), F32),
        ],
        compiler_params=_cparams(("arbitrary",)),
        name="ssm_prompt",
    )(x, g, wb, wc2, a_rows, bc_rows, d, wa, wg)


def _ssm_sample_kernel(x_ref, g_ref, wb_ref, wc_ref, a_ref, bc_ref, h0r_ref, h0i_ref, d_ref, wa_ref, wg_ref,
                       xo_ref, sr_ref, si_ref):
    x = x_ref[...]
    u = _rms(x, g_ref[...])
    ub = u.astype(BF16)
    ys = []
    for j in range(N_SLABS):
        sl = slice(SLAB_STATES * j, SLAB_STATES * (j + 1))
        res = jnp.dot(ub[:, LANES * j:LANES * (j + 1)], wb_ref[j], preferred_element_type=F32)
        lr = a_ref[0:1, sl]
        li = bc_ref[SUBLANES - 1:SUBLANES, sl]
        h0r, h0i = h0r_ref[:, sl], h0i_ref[:, sl]
        xr = res[:, :SLAB_STATES] + (lr * h0r - li * h0i)
        xi = res[:, SLAB_STATES:] + (lr * h0i + li * h0r)
        sr_ref[:, sl] = xr
        si_ref[:, sl] = xi
        s = jnp.concatenate([xr, xi], axis=1).astype(BF16)
        ys.append(jnp.dot(s, wc_ref[j], preferred_element_type=F32))
    y = jnp.concatenate(ys, axis=1)
    xo_ref[...] = _glu_residual(x, y, u, d_ref[...], wa_ref[...], wg_ref[...])


def _ssm_sample_layer(x, g, prep, h0r, h0i, d, wa, wg):
    a_rows, bc_rows, wb, wc, _ = prep
    n = x.shape[0]
    return pl.pallas_call(
        _ssm_sample_kernel,
        out_shape=[jax.ShapeDtypeStruct(x.shape, F32),
                   jax.ShapeDtypeStruct((n, N_STATES), F32),
                   jax.ShapeDtypeStruct((n, N_STATES), F32)],
        compiler_params=_cparams(None),
        name="ssm_sample",
    )(x, g, wb, wc, a_rows, bc_rows, h0r, h0i, d, wa, wg)


ROUTER_W = LANES
EXPERT_LANE0 = N_EXPERT_GROUPS


def _route_topk(h, wr, br):
    h_hi = h.astype(BF16)
    h_lo = (h - h_hi.astype(F32)).astype(BF16)
    w_hi = wr.astype(BF16)
    w_lo = (wr - w_hi.astype(F32)).astype(BF16)
    dot = lambda a, b: jnp.dot(a, b, preferred_element_type=F32)
    both = dot(h_hi, jnp.concatenate([w_hi, w_lo], axis=1))
    logits = both[:, :ROUTER_W] + (dot(h_lo, w_hi) + both[:, ROUTER_W:]) + br
    lane_i = lax.broadcasted_iota(jnp.int32, logits.shape, 1)
    lane = lane_i.astype(F32)
    neg = -jnp.inf
    past_end = float(ROUTER_W)
    gl = jnp.where(lane_i < N_EXPERT_GROUPS, logits, neg)
    gmax = jnp.max(gl, axis=-1, keepdims=True)
    g_sel = jnp.min(jnp.where(gl == gmax, lane, past_end), axis=-1, keepdims=True)
    p_grp = 1.0 / jnp.sum(jnp.exp(gl - gmax), axis=-1, keepdims=True)
    lane_grp = ((lane_i - EXPERT_LANE0) // EXPERTS_PER_GROUP).astype(F32)
    in_grp = (lane_i >= EXPERT_LANE0) & (lane_i < EXPERT_LANE0 + N_EXPERTS) & (lane_grp == g_sel)
    el = jnp.where(in_grp, logits, neg)
    v1 = jnp.max(el, axis=-1, keepdims=True)
    i1 = jnp.min(jnp.where(el == v1, lane, past_end), axis=-1, keepdims=True)
    el2 = jnp.where(lane == i1, neg, el)
    v2 = jnp.max(el2, axis=-1, keepdims=True)
    i2 = jnp.min(jnp.where(el2 == v2, lane, past_end), axis=-1, keepdims=True)
    e2 = jnp.exp(v2 - v1)
    w1 = p_grp / (1.0 + e2)
    w2 = p_grp * e2 / (1.0 + e2)
    return lane, i1, i2, w1, w2


def _route(h, wr, br):
    lane, i1, i2, w1, w2 = _route_topk(h, wr, br)
    return jnp.where(lane == i1, w1, 0.0) + jnp.where(lane == i2, w2, 0.0)


def _moe_kernel(x_ref, p_ref, gffn_ref, wr_ref, br_ref, wg_ref, wu_ref, wd_ref,
                gple_ref, wpg_ref, wpp_ref, gfin_ref, o_ref, hn_scr, gates_scr, acc_scr, *, final):
    e = pl.program_id(1)

    @pl.when(e == 0)
    def _():
        x = x_ref[...]
        h = _rms(x, gffn_ref[...])
        hn_scr[...] = h.astype(BF16)
        gates_scr[...] = _route(h, wr_ref[...], br_ref[...])
        acc_scr[...] = x

    hb = hn_scr[...]
    gt = jnp.dot(hb, wg_ref[0].astype(BF16), preferred_element_type=F32)
    up = jnp.dot(hb, wu_ref[0].astype(BF16), preferred_element_type=F32)
    a = (gt * _sigmoid(gt) * up).astype(BF16)
    gates = gates_scr[...]
    lane = lax.broadcasted_iota(jnp.int32, gates.shape, 1)
    ge = jnp.sum(jnp.where(lane == e + EXPERT_LANE0, gates, 0.0), axis=-1, keepdims=True)
    acc_scr[...] += ge * jnp.dot(a, wd_ref[0].astype(BF16), preferred_element_type=F32)

    @pl.when(e == pl.num_programs(1) - 1)
    def _():
        x1 = acc_scr[...]
        gate = _sigmoid(jnp.dot(_rms(x1, gple_ref[...]).astype(BF16), wpg_ref[...], preferred_element_type=F32))
        x2 = x1 + gate * jnp.dot(p_ref[...].astype(BF16), wpp_ref[...], preferred_element_type=F32)
        o_ref[...] = _rms(x2, gfin_ref[...]) if final else x2


def _moe_ple_layer(li, x, p, gffn, wr, br, wg, wu, wd, gple, wpg, wpp, gfin, final):
    n, d = x.shape
    tm = min(1024, n)
    kern = functools.partial(_moe_kernel, final=final)
    tok = lambda w: pl.BlockSpec((tm, w), lambda t, e: (t, 0))
    const = lambda s: pl.BlockSpec(s, lambda t, e: (0,) * len(s))
    return pl.pallas_call(
        kern,
        grid=(n // tm, N_EXPERTS),
        in_specs=[
            tok(d), pl.BlockSpec((None, tm, PLE_DIM), lambda t, e: (li, t, 0)),
            const((1, d)), const((d, ROUTER_W)), const((1, ROUTER_W)),
            pl.BlockSpec((None, 1, d, D_EXPERT), lambda t, e: (li, e, 0, 0)),
            pl.BlockSpec((None, 1, d, D_EXPERT), lambda t, e: (li, e, 0, 0)),
            pl.BlockSpec((None, 1, D_EXPERT, d), lambda t, e: (li, e, 0, 0)),
            const((1, d)), const((d, d)), const((PLE_DIM, d)), const((1, d)),
        ],
        out_specs=tok(d),
        out_shape=jax.ShapeDtypeStruct((n, d), F32),
        scratch_shapes=[pltpu.VMEM((tm, d), BF16), pltpu.VMEM((tm, ROUTER_W), F32), pltpu.VMEM((tm, d), F32)],
        compiler_params=_cparams(("arbitrary", "arbitrary")),
        name="moe_ple",
    )(x, p, gffn, wr, br, wg, wu, wd, gple, wpg, wpp, gfin)


TOKEN_DATA_ROWS = D_MODEL // LANES
TOKEN_PITCH = TOKEN_DATA_ROWS + 1
INFO_W1, INFO_W2, INFO_E1, INFO_E2, INFO_R1, INFO_R2 = range(6)
EXPERT_TILE = 256
ZERO_RUN = 256


def _to_token_tiles(ref_2d, val, n, first=0):
    for s in range(TOKEN_DATA_ROWS):
        ref_2d[pl.ds(first * TOKEN_PITCH + s, n, stride=TOKEN_PITCH), :] = val[:, LANES * s:LANES * (s + 1)]


def _zero_spare_rows(ref_2d, n):
    ref_2d[pl.ds(TOKEN_DATA_ROWS, n, stride=TOKEN_PITCH), :] = jnp.zeros((n, LANES), F32)


def _from_token_tiles(ref_2d, n, first=0):
    return jnp.concatenate([ref_2d[pl.ds(first * TOKEN_PITCH + s, n, stride=TOKEN_PITCH), :]
                            for s in range(TOKEN_DATA_ROWS)], axis=1)


def _token_slots(i, n=1):
    return pl.ds(i * TOKEN_PITCH, n * TOKEN_PITCH)


def _slot_at(first_row):
    return pl.ds(first_row, TOKEN_PITCH)


def _route_kernel(x_ref, g_ref, wr_ref, br_ref, before_ref, info_ref, cnt_ref, base_scr):
    i = pl.program_id(0)

    @pl.when(i == 0)
    def _():
        base_scr[...] = jnp.zeros(base_scr.shape, F32)

    h = _rms(x_ref[...], g_ref[...])
    lane, i1, i2, w1, w2 = _route_topk(h, wr_ref[...], br_ref[...])
    chosen = jnp.where((lane == i1) | (lane == i2), 1.0, 0.0)
    prefix = jnp.dot(before_ref[...], chosen.astype(BF16), preferred_element_type=F32) + base_scr[...]
    rank1 = jnp.sum(jnp.where(lane == i1, prefix, 0.0), axis=-1, keepdims=True)
    rank2 = jnp.sum(jnp.where(lane == i2, prefix, 0.0), axis=-1, keepdims=True)
    base_scr[...] += jnp.sum(chosen, axis=0, keepdims=True)
    cnt_ref[...] = base_scr[...]
    e1 = (i1 - EXPERT_LANE0).astype(F32)
    e2 = (i2 - EXPERT_LANE0).astype(F32)
    info = jnp.zeros(h.shape[:1] + (ROUTER_W,), F32)
    for ln, v in ((INFO_W1, w1), (INFO_W2, w2), (INFO_E1, e1), (INFO_E2, e2), (INFO_R1, rank1), (INFO_R2, rank2)):
        info = jnp.where(lane == ln, v, info)
    info_ref[...] = info


def _route_call(x, gffn, wr, br):
    n, d = x.shape
    tm = 512
    before = jnp.asarray(np.tril(np.ones((tm, tm), np.float32), -1), BF16)
    return pl.pallas_call(
        _route_kernel,
        grid=(n // tm,),
        in_specs=[pl.BlockSpec((tm, d), lambda i: (i, 0)), _full((1, d)), _full((d, ROUTER_W)), _full((1, ROUTER_W)),
                  _full((tm, tm))],
        out_specs=[pl.BlockSpec((tm, ROUTER_W), lambda i: (i, 0)), _full((1, ROUTER_W))],
        out_shape=[jax.ShapeDtypeStruct((n, ROUTER_W), F32), jax.ShapeDtypeStruct((1, ROUTER_W), F32)],
        scratch_shapes=[pltpu.VMEM((1, ROUTER_W), F32)],
        compiler_params=_cparams(("arbitrary",)),
        name="moe_route",
    )(x, gffn, wr, br, before)


def _dispatch_kernel(pos1_ref, pos2_ref, cnt_ref, start_ref, x_ref, g_ref, xs_hbm, hbuf, zbuf, sem, zsem,
                     *, td, n_steps, n_pad_rows):
    i = pl.program_id(0)
    slot = i % 2

    def wait_slot(sl):
        for _ in range(2):
            pltpu.make_async_copy(hbuf.at[sl], hbuf.at[sl], sem.at[sl]).wait()

    @pl.when(i >= 2)
    def _():
        wait_slot(slot)

    @pl.when(i < 2)
    def _():
        _zero_spare_rows(hbuf.at[slot], td)

    _to_token_tiles(hbuf.at[slot], _rms(x_ref[...], g_ref[...]), td)

    def issue(n8, carry):
        for u in range(SUBLANES):
            n = n8 * SUBLANES + u
            t = i * td + n
            src = hbuf.at[slot, _token_slots(n), :]
            pltpu.make_async_copy(src, xs_hbm.at[_slot_at(pos1_ref[t]), :], sem.at[slot]).start(priority=0)
            pltpu.make_async_copy(src, xs_hbm.at[_slot_at(pos2_ref[t]), :], sem.at[slot]).start(priority=1)
        return carry

    lax.fori_loop(0, td // SUBLANES, issue, 0)

    @pl.when(i == 0)
    def _():
        zbuf[...] = jnp.zeros(zbuf.shape, zbuf.dtype)

        def zero_run(first, n_slots):
            pltpu.make_async_copy(zbuf.at[_token_slots(0, n_slots), :], xs_hbm.at[_token_slots(first, n_slots), :],
                                  zsem).start()

        for e in range(N_EXPERTS + 1):
            lo = start_ref[e] + cnt_ref[e]
            n = start_ref[e + 1] - lo

            def full_run(k, carry, lo=lo):
                zero_run(lo + k * ZERO_RUN, ZERO_RUN)
                return carry

            lax.fori_loop(0, n // ZERO_RUN, full_run, 0)
            cur = lo + (n // ZERO_RUN) * ZERO_RUN
            p = ZERO_RUN // 2
            while p >= 1:
                @pl.when((n & p) != 0)
                def _(cur=cur, p=p):
                    zero_run(cur, p)

                cur = cur + (n & p)
                p //= 2

    @pl.when(i == n_steps - 1)
    def _():
        wait_slot(slot)
        if n_steps > 1:
            wait_slot(1 - slot)
        pad = xs_hbm.at[_token_slots(0, n_pad_rows), :]
        pltpu.make_async_copy(pad, pad, zsem).wait()


def _dispatch_call(pos1, pos2, cnt, start, x, gffn, a_pad, n_pad_rows):
    n, d = x.shape
    td = 512
    n_steps = n // td
    kern = functools.partial(_dispatch_kernel, td=td, n_steps=n_steps, n_pad_rows=n_pad_rows)
    return pl.pallas_call(
        kern,
        grid_spec=pltpu.PrefetchScalarGridSpec(
            num_scalar_prefetch=4,
            grid=(n_steps,),
            in_specs=[pl.BlockSpec((td, d), lambda i, *_: (i, 0)), pl.BlockSpec((1, d), lambda i, *_: (0, 0))],
            out_specs=pl.BlockSpec(memory_space=pl.ANY),
            scratch_shapes=[pltpu.VMEM((2, td * TOKEN_PITCH, LANES), F32),
                            pltpu.VMEM((ZERO_RUN * TOKEN_PITCH, LANES), F32),
                            pltpu.SemaphoreType.DMA((2,)), pltpu.SemaphoreType.DMA(())],
        ),
        out_shape=jax.ShapeDtypeStruct((a_pad * TOKEN_PITCH, LANES), F32),
        compiler_params=_cparams(("arbitrary",)),
        name="moe_dispatch",
    )(pos1, pos2, cnt, start, x, gffn)


EXPERT_IN_SLOTS = 3


TILES_PER_STEP = 2


def _experts_kernel(te_ref, xs_hbm, *refs, te, n_steps):
    nu = TILES_PER_STEP
    w_refs = [refs[3 * u:3 * u + 3] for u in range(nu)]
    y_ref, xbuf, xsem = refs[3 * nu:3 * nu + 3]
    w_scrs = [refs[3 * nu + 3 + 3 * u:3 * nu + 6 + 3 * u] for u in range(nu)]
    t = pl.program_id(0)
    rows = nu * te * TOKEN_PITCH

    def fetch(step):
        slot = step % EXPERT_IN_SLOTS
        src = xs_hbm.at[pl.ds(pl.multiple_of(step * rows, rows), rows), :]
        return pltpu.make_async_copy(src, xbuf.at[slot], xsem.at[slot])

    @pl.when(t == 0)
    def _():
        for ahead in range(min(EXPERT_IN_SLOTS - 1, n_steps)):
            fetch(ahead).start()

    @pl.when(t + EXPERT_IN_SLOTS - 1 < n_steps)
    def _():
        fetch(t + EXPERT_IN_SLOTS - 1).start()

    for u in range(nu):
        tile = nu * t + u

        @pl.when((t == 0) | (te_ref[tile] != te_ref[jnp.maximum(tile - nu, 0)]))
        def _(u=u):
            for scr, ref in zip(w_scrs[u], w_refs[u]):
                scr[...] = ref[0].astype(BF16)

    fetch(t).wait()
    for u in range(nu):
        wg_scr, wu_scr, wd_scr = w_scrs[u]
        x = _from_token_tiles(xbuf.at[t % EXPERT_IN_SLOTS], te, u * te).astype(BF16)
        gt = jnp.dot(x, wg_scr[...], preferred_element_type=F32)
        up = jnp.dot(x, wu_scr[...], preferred_element_type=F32)
        a = (gt * _sigmoid(gt) * up).astype(BF16)
        _to_token_tiles(y_ref, jnp.dot(a, wd_scr[...], preferred_element_type=F32), te, u * te)
    _zero_spare_rows(y_ref, nu * te)


def _experts_call(li, tile_expert, xs, wg, wu, wd, te):
    nu = TILES_PER_STEP
    rows = nu * te * TOKEN_PITCH
    assert xs.shape[0] % rows == 0
    n_steps = xs.shape[0] // rows
    d = wg.shape[2]
    kern = functools.partial(_experts_kernel, te=te, n_steps=n_steps)

    def weight_specs(u):
        pick = lambda t, ex: (li, ex[nu * t + u], 0, 0)
        return [pl.BlockSpec((None, 1, d, D_EXPERT), pick), pl.BlockSpec((None, 1, d, D_EXPERT), pick),
                pl.BlockSpec((None, 1, D_EXPERT, d), pick)]

    w_scratch = [pltpu.VMEM((d, D_EXPERT), BF16), pltpu.VMEM((d, D_EXPERT), BF16), pltpu.VMEM((D_EXPERT, d), BF16)]
    return pl.pallas_call(
        kern,
        grid_spec=pltpu.PrefetchScalarGridSpec(
            num_scalar_prefetch=1,
            grid=(n_steps,),
            in_specs=[pl.BlockSpec(memory_space=pl.ANY)] + [s for u in range(nu) for s in weight_specs(u)],
            out_specs=pl.BlockSpec((rows, LANES), lambda t, ex: (t, 0)),
            scratch_shapes=[pltpu.VMEM((EXPERT_IN_SLOTS, rows, LANES), F32),
                            pltpu.SemaphoreType.DMA((EXPERT_IN_SLOTS,))] + w_scratch * nu,
        ),
        out_shape=jax.ShapeDtypeStruct((n_steps * rows, LANES), F32),
        compiler_params=_cparams(("arbitrary",)),
        name="moe_experts",
    )(tile_expert, xs, *([wg, wu, wd] * nu))


def _combine_kernel(pos1_ref, pos2_ref, x_ref, info_ref, p_ref, gple_ref, wpg_ref, wpp_ref, gfin_ref, ys_hbm,
                    o_ref, ybuf, sem, *, tm, n_steps, final):
    i = pl.program_id(0)
    slot = i % 2

    def issue(step, sl):
        def body(n8, carry):
            for u in range(SUBLANES):
                n = n8 * SUBLANES + u
                t = step * tm + n
                for k, pos_ref in enumerate((pos1_ref, pos2_ref)):
                    pltpu.make_async_copy(ys_hbm.at[_slot_at(pos_ref[t]), :],
                                          ybuf.at[sl, k, _token_slots(n), :], sem.at[sl]).start(priority=k)
            return carry

        lax.fori_loop(0, tm // SUBLANES, body, 0)

    def wait_rows(sl):
        for k in range(2):
            pltpu.make_async_copy(ybuf.at[sl, k], ybuf.at[sl, k], sem.at[sl]).wait()

    @pl.when(i == 0)
    def _():
        issue(0, 0)

    @pl.when(i + 1 < n_steps)
    def _():
        issue(i + 1, 1 - slot)

    wait_rows(slot)
    part = tm // 2
    for first in (0, part):
        tok = slice(first, first + part)
        y1 = _from_token_tiles(ybuf.at[slot, 0], part, first)
        y2 = _from_token_tiles(ybuf.at[slot, 1], part, first)
        info = info_ref[tok, :]
        x1 = x_ref[tok, :] + (info[:, INFO_W1:INFO_W1 + 1] * y1 + info[:, INFO_W2:INFO_W2 + 1] * y2)
        gate = _sigmoid(jnp.dot(_rms(x1, gple_ref[...]).astype(BF16), wpg_ref[...], preferred_element_type=F32))
        x2 = x1 + gate * jnp.dot(p_ref[tok, :].astype(BF16), wpp_ref[...], preferred_element_type=F32)
        o_ref[tok, :] = _rms(x2, gfin_ref[...]) if final else x2


def _combine_call(li, pos1, pos2, x, info, p, gple, wpg, wpp, gfin, ys, final):
    n, d = x.shape
    tm = 1024 if n % 1024 == 0 else 512
    n_steps = n // tm
    kern = functools.partial(_combine_kernel, tm=tm, n_steps=n_steps, final=final)
    tok = lambda w: pl.BlockSpec((tm, w), lambda i, *_: (i, 0))
    const = lambda s: pl.BlockSpec(s, lambda i, *_: (0,) * len(s))
    return pl.pallas_call(
        kern,
        grid_spec=pltpu.PrefetchScalarGridSpec(
            num_scalar_prefetch=2,
            grid=(n_steps,),
            in_specs=[tok(d), tok(ROUTER_W), pl.BlockSpec((None, tm, PLE_DIM), lambda i, *_: (li, i, 0)),
                      const((1, d)), const((d, d)), const((PLE_DIM, d)),
                      const((1, d)), pl.BlockSpec(memory_space=pl.ANY)],
            out_specs=tok(d),
            scratch_shapes=[pltpu.VMEM((2, 2, tm * TOKEN_PITCH, LANES), F32), pltpu.SemaphoreType.DMA((2,))],
        ),
        out_shape=jax.ShapeDtypeStruct((n, d), F32),
        compiler_params=_cparams(("arbitrary",)),
        name="moe_combine",
    )(pos1, pos2, x, info, p, gple, wpg, wpp, gfin, ys)


def _positions_kernel(info_ref, start_ref, pos_ref):
    info = info_ref[...]
    lane = lax.broadcasted_iota(jnp.int32, info.shape, 1)
    lane_expert = (lane - EXPERT_LANE0).astype(F32)
    start = start_ref[...]
    cols = []
    for e_lane, r_lane in ((INFO_E1, INFO_R1), (INFO_E2, INFO_R2)):
        seg = jnp.sum(jnp.where(lane_expert == info[:, e_lane:e_lane + 1], start, 0.0), axis=-1, keepdims=True)
        cols.append((seg + info[:, r_lane:r_lane + 1]) * float(TOKEN_PITCH))
    both = jnp.where(lane == 0, cols[0], jnp.where(lane == 1, cols[1], 0.0))
    pos_ref[...] = both.T[:SUBLANES, :].astype(jnp.int32)


def _positions_call(info, start_row):
    n = info.shape[0]
    tm = 2048 if n % 2048 == 0 else 512
    return pl.pallas_call(
        _positions_kernel,
        grid=(n // tm,),
        in_specs=[pl.BlockSpec((tm, ROUTER_W), lambda i: (i, 0)), _full((1, ROUTER_W))],
        out_specs=pl.BlockSpec((SUBLANES, tm), lambda i: (0, i)),
        out_shape=jax.ShapeDtypeStruct((SUBLANES, n), jnp.int32),
        compiler_params=_cparams(("arbitrary",)),
        name="moe_positions",
    )(info, start_row)


def _moe_routed_layer(li, x, p, gffn, wr, br, wg, wu, wd, gple, wpg, wpp, gfin, final):
    n = x.shape[0]
    te = EXPERT_TILE
    n_assign = 2 * n
    assert n_assign % te == 0
    a_pad = n_assign + N_EXPERTS * te
    info, cnt = _route_call(x, gffn, wr, br)
    counts = cnt[0, EXPERT_LANE0:EXPERT_LANE0 + N_EXPERTS].astype(jnp.int32)
    padded = (counts + te - 1) // te * te
    seg_end = jnp.cumsum(padded)
    seg_start = seg_end - padded
    start_row = jnp.zeros((1, ROUTER_W), F32).at[0, EXPERT_LANE0:EXPERT_LANE0 + N_EXPERTS].set(seg_start.astype(F32))
    pos = _positions_call(info, start_row)
    pos1, pos2 = pos[0], pos[1]
    start = jnp.concatenate([seg_start, seg_end[-1:], jnp.full((1,), a_pad, jnp.int32)])
    cnt_pad = jnp.concatenate([counts, jnp.zeros((1,), jnp.int32)])
    tile_row0 = jnp.arange(a_pad // te, dtype=jnp.int32) * te
    tile_expert = jnp.minimum(jnp.sum(seg_end[None, :] <= tile_row0[:, None], axis=1), N_EXPERTS - 1).astype(jnp.int32)
    xs = _dispatch_call(pos1, pos2, cnt_pad, start, x, gffn, a_pad, N_EXPERTS * te)
    ys = _experts_call(li, tile_expert, xs, wg, wu, wd, te)
    return _combine_call(li, pos1, pos2, x, info, p, gple, wpg, wpp, gfin, ys, final)


def kernel(x_prompt, x_sample, cache_k_win, cache_v_win, state_ssm_re, state_ssm_im, p_prompt, p_sample,
           rel_bias_table, g_mix, w_qkv, w_o, sinks, ssm_a_re, ssm_a_im, ssm_log_dt, ssm_b_re, ssm_b_im,
           ssm_c_re, ssm_c_im, ssm_d, w_glu_a, w_glu_b, g_ffn, w_router_group, b_router_group,
           w_router_expert, b_router_expert, w_exp_gate, w_exp_up, w_exp_down, g_ple, w_ple_gate,
           w_ple_proj, g_final):
    bsz, seq, d = x_prompt.shape
    ns = x_sample.shape[0]
    depth = g_mix.shape[0]
    row = lambda v: v.reshape(1, -1).astype(F32)

    qi = np.arange(WINDOW)[:, None]
    sj = np.arange(2 * WINDOW)[None, :]
    dist = qi + WINDOW - sj
    idx_p = _t5_bucket_np(dist)
    ok_p = ((dist >= 0) & (dist < WINDOW)).astype(np.int32)
    idx_s = np.broadcast_to(_t5_bucket_np(WINDOW - 1 - np.arange(WINDOW))[None, :], (SUBLANES, WINDOW))
    bias_p = _band_bias(rel_bias_table, idx_p, ok_p)
    bias_s = _bias_from_table(rel_bias_table, idx_s)[:, 0, :]

    wqkv_b = w_qkv[0].astype(BF16)
    wo_b = w_o[0].astype(BF16)
    wpg_b, wpp_b = w_ple_gate.astype(BF16), w_ple_proj.astype(BF16)
    wa_b, wgl_b = w_glu_a[0].astype(BF16), w_glu_b[0].astype(BF16)
    pad = ROUTER_W - N_EXPERT_GROUPS - N_EXPERTS
    wr = jnp.concatenate([w_router_group, w_router_expert.reshape(depth, d, N_EXPERTS),
                          jnp.zeros((depth, d, pad), F32)], axis=-1)
    br = jnp.concatenate([b_router_group, b_router_expert.reshape(depth, N_EXPERTS),
                          jnp.zeros((depth, pad), F32)], axis=-1)

    prep = _ssm_prep(ssm_a_re[0], ssm_a_im[0], ssm_log_dt[0], ssm_b_re[0], ssm_b_im[0], ssm_c_re[0], ssm_c_im[0])
    d_row = row(ssm_d[0])

    def moe(x2d, p_all, i):
        layer = _moe_routed_layer if 2 * x2d.shape[0] >= N_EXPERTS * EXPERT_TILE else _moe_ple_layer
        return layer(i, x2d, p_all, row(g_ffn[i]), wr[i], br[i:i + 1], w_exp_gate, w_exp_up, w_exp_down,
                     row(g_ple[i]), wpg_b[i], wpp_b[i], row(g_final), final=(i == depth - 1))

    pp = p_prompt.reshape(depth, bsz * seq, PLE_DIM)
    xp, kp, vp = _attn_prompt_layer(x_prompt, row(g_mix[0]), wqkv_b, wo_b, bias_p, sinks[0])
    xp = moe(xp.reshape(bsz * seq, d), pp, 0).reshape(bsz, seq, d)
    xp, st_p = _ssm_prompt_layer(xp, row(g_mix[1]), prep, d_row, wa_b, wgl_b)
    y_prompt = moe(xp.reshape(bsz * seq, d), pp, 1).reshape(bsz, seq, d)

    ps = p_sample.reshape(depth, ns, PLE_DIM)
    xs = x_sample.reshape(ns, d)
    xs, kws, vws = _attn_sample_layer(xs, row(g_mix[0]), wqkv_b, wo_b, bias_s, sinks[0],
                                      cache_k_win[0], cache_v_win[0])
    xs = moe(xs, ps, 0)
    xs, sr_s, si_s = _ssm_sample_layer(xs, row(g_mix[1]), prep, state_ssm_re[0].reshape(ns, N_STATES),
                                       state_ssm_im[0].reshape(ns, N_STATES), d_row, wa_b, wgl_b)
    y_sample = moe(xs, ps, 1).reshape(ns, 1, d)

    half = SUBLANES // 2
    win = lambda a, n: a.reshape(1, n, WINDOW, N_KV_HEADS, HEAD_DIM)
    st = lambda a, n: a.reshape(1, n, N_SSM_GROUPS, SSM_STATE)
    return (y_prompt, y_sample, win(kp, bsz), win(vp, bsz), win(kws, ns), win(vws, ns),
            st(st_p[:bsz], bsz), st(st_p[half:half + bsz], bsz), st(sr_s, ns), st(si_s, ns))
```

```python
import functools
import math

import numpy as np
import jax
import jax.numpy as jnp
from jax import lax
from jax.experimental import pallas as pl
from jax.experimental.pallas import tpu as pltpu

F32 = jnp.float32
BF16 = jnp.bfloat16

D_MODEL = 1024
HEAD_DIM = 64
N_HEADS = 16
N_KV_HEADS = 2
GQ = N_HEADS // N_KV_HEADS
WINDOW = 128
N_BUCKETS = 32
MAX_EXACT = 16
MAX_DISTANCE = 128
SSM_GROUP = 16
N_SSM_GROUPS = 64
SSM_STATE = 64
N_STATES = N_SSM_GROUPS * SSM_STATE
N_EXPERT_GROUPS = 4
EXPERTS_PER_GROUP = 8
N_EXPERTS = 32
D_EXPERT = 256
PLE_DIM = 256
RMS_EPS = 1e-6
NEG_INF = -1e30

LANES = 128
SUBLANES = 8
KV_W = N_KV_HEADS * HEAD_DIM
QKV_W = D_MODEL + 2 * KV_W
SLAB_STATES = 512
N_SLABS = D_MODEL // LANES
VMEM_LIMIT = 56 * 1024 * 1024


def _cparams(sem):
    return pltpu.CompilerParams(dimension_semantics=sem, vmem_limit_bytes=VMEM_LIMIT)


def _rms(x, g):
    return x * lax.rsqrt(jnp.mean(x * x, axis=-1, keepdims=True) + RMS_EPS) * g


def _gelu_tanh(x):
    c = math.sqrt(2.0 / math.pi)
    return x * (0.5 * (1.0 + jnp.tanh(c * (x + 0.044715 * (x * x * x)))))


def _sigmoid(x):
    return 1.0 / (1.0 + jnp.exp(-x))


def _full(shape):
    n = len(shape)
    return pl.BlockSpec(shape, lambda *_: (0,) * n)


def _t5_bucket_np(dist):
    n = np.maximum(dist, 0)
    nf = np.maximum(n, 1).astype(np.float64)
    large = MAX_EXACT + (np.log(nf / MAX_EXACT) / math.log(MAX_DISTANCE / MAX_EXACT)
                         * (N_BUCKETS - MAX_EXACT)).astype(np.int32)
    large = np.minimum(large, N_BUCKETS - 1)
    return np.where(n < MAX_EXACT, n, large).astype(np.int32)


def _bias_kernel(table_ref, idx_ref, o_ref):
    h = pl.program_id(0)
    idx = idx_ref[...]
    acc = jnp.zeros(idx.shape, F32)
    for b in range(N_BUCKETS):
        acc = jnp.where(idx == b, table_ref[b, h], acc)
    o_ref[0] = acc


def _bias_from_table(table, idx_np):
    q, k = idx_np.shape
    return pl.pallas_call(
        _bias_kernel,
        grid=(N_HEADS,),
        in_specs=[pl.BlockSpec(memory_space=pltpu.SMEM), _full((q, k))],
        out_specs=pl.BlockSpec((1, q, k), lambda h: (h, 0, 0)),
        out_shape=jax.ShapeDtypeStruct((N_HEADS, q, k), F32),
        name="rel_bias",
    )(table, jnp.asarray(idx_np))


HEADS_PER_UNIT = GQ // 2
N_UNITS = N_HEADS // HEADS_PER_UNIT
UNIT_ROWS = HEADS_PER_UNIT * WINDOW


def _band_bias_kernel(table_ref, idx_ref, ok_ref, o_ref):
    h = pl.program_id(0)
    idx = idx_ref[...]
    acc = jnp.zeros(idx.shape, F32)
    for b in range(N_BUCKETS):
        acc = jnp.where(idx == b, table_ref[b, h], acc)
    base = jnp.where(ok_ref[...] != 0, acc, NEG_INF)
    sj = lax.broadcasted_iota(jnp.int32, idx.shape, 1)
    o_ref[0, 0] = base
    o_ref[1, 0] = jnp.where(sj >= WINDOW, base, NEG_INF)


def _band_bias(table, idx_np, ok_np):
    def out_map(h):
        r = h % GQ
        return (0, (h // GQ) * 2 + r % 2, r // 2, 0)

    return pl.pallas_call(
        _band_bias_kernel,
        grid=(N_HEADS,),
        in_specs=[pl.BlockSpec(memory_space=pltpu.SMEM), _full(idx_np.shape), _full(ok_np.shape)],
        out_specs=pl.BlockSpec((2, 1, WINDOW, 2 * WINDOW), out_map),
        out_shape=jax.ShapeDtypeStruct((2, N_UNITS, UNIT_ROWS, 2 * WINDOW), F32),
        name="band_bias",
    )(table, jnp.asarray(idx_np), jnp.asarray(ok_np))


def _attn_prompt_kernel(sink_ref, x_ref, g_ref, wqkv_ref, wo_ref, bias_ref,
                        xo_ref, k_ref, v_ref, kv_scr, q_scr, o_scr, *, tq):
    i = pl.program_id(1)
    nsub = tq // WINDOW

    @pl.when(i == 0)
    def _():
        kv_scr[0:WINDOW, :] = jnp.zeros((WINDOW, 2 * KV_W), F32)

    x = x_ref[0]
    h = _rms(x, g_ref[...]).astype(BF16)
    qkv = jnp.dot(h, wqkv_ref[...], preferred_element_type=F32)
    q_scr[...] = (qkv[:, :D_MODEL] * (HEAD_DIM ** -0.5)).astype(BF16)
    kv_scr[WINDOW:WINDOW + tq, :] = qkv[:, D_MODEL:]
    k_ref[0] = qkv[tq - WINDOW:, D_MODEL:D_MODEL + KV_W]
    v_ref[0] = qkv[tq - WINDOW:, D_MODEL + KV_W:]

    lo = lax.broadcasted_iota(jnp.int32, (2 * WINDOW, KV_W), 1) < HEAD_DIM
    row_head = lax.broadcasted_iota(jnp.int32, (UNIT_ROWS, 1), 0) // WINDOW
    sinks = []
    for u in range(N_UNITS):
        g, p = divmod(u, 2)
        col = jnp.zeros((UNIT_ROWS, 1), F32)
        for jj in range(HEADS_PER_UNIT):
            col = jnp.where(row_head == jj, sink_ref[GQ * g + 2 * jj + p], col)
        sinks.append(col)

    for s in range(nsub):
        band = kv_scr[WINDOW * s:WINDOW * s + 2 * WINDOW, :]
        kband, vband = band[:, :KV_W], band[:, KV_W:]
        kroll = pltpu.roll(kband, HEAD_DIM, 1)
        vroll = pltpu.roll(vband, HEAD_DIM, 1)
        kpad = [[jnp.where(lo, kband, 0.0).astype(BF16), jnp.where(lo, 0.0, kroll).astype(BF16)],
                [jnp.where(lo, kroll, 0.0).astype(BF16), jnp.where(lo, 0.0, kband).astype(BF16)]]
        vpad = [[jnp.where(lo, vband, 0.0).astype(BF16), jnp.where(lo, 0.0, vroll).astype(BF16)],
                [jnp.where(lo, vroll, 0.0).astype(BF16), jnp.where(lo, 0.0, vband).astype(BF16)]]
        variant = jnp.where(i * nsub + s > 0, 0, 1)
        for g in range(N_KV_HEADS):
            slabs = range(HEADS_PER_UNIT * g, HEADS_PER_UNIT * (g + 1))
            qs = jnp.concatenate([q_scr[WINDOW * s:WINDOW * (s + 1), LANES * j:LANES * (j + 1)] for j in slabs],
                                 axis=0)
            acc = None
            for p in range(2):
                u = 2 * g + p
                sc = lax.dot_general(qs, kpad[g][p], (((1,), (1,)), ((), ())), preferred_element_type=F32)
                sc = sc + bias_ref[variant, u]
                m = jnp.maximum(jnp.max(sc, axis=-1, keepdims=True), sinks[u])
                e = jnp.exp(sc - m)
                den = jnp.sum(e, axis=-1, keepdims=True) + jnp.exp(sinks[u] - m)
                pv = jnp.dot(e.astype(BF16), vpad[g][p], preferred_element_type=F32)
                term = pv * (1.0 / den)
                acc = term if acc is None else acc + term
            for jj, j in enumerate(slabs):
                o_scr[WINDOW * s:WINDOW * (s + 1), LANES * j:LANES * (j + 1)] = (
                    acc[WINDOW * jj:WINDOW * (jj + 1)].astype(BF16))

    kv_scr[0:WINDOW, :] = kv_scr[tq:tq + WINDOW, :]
    xo_ref[0] = x + jnp.dot(o_scr[...], wo_ref[...], preferred_element_type=F32)


def _attn_prompt_layer(x, g, wqkv, wo, bias, sinks):
    b, t, d = x.shape
    tq = min(512, t)
    kern = functools.partial(_attn_prompt_kernel, tq=tq)
    return pl.pallas_call(
        kern,
        grid=(b, t // tq),
        in_specs=[
            pl.BlockSpec(memory_space=pltpu.SMEM),
            pl.BlockSpec((1, tq, d), lambda bi, i: (bi, i, 0)),
            _full((1, d)),
            _full((d, QKV_W)),
            _full((d, d)),
            _full((2, N_UNITS, UNIT_ROWS, 2 * WINDOW)),
        ],
        out_specs=[
            pl.BlockSpec((1, tq, d), lambda bi, i: (bi, i, 0)),
            pl.BlockSpec((1, WINDOW, KV_W), lambda bi, i: (bi, 0, 0)),
            pl.BlockSpec((1, WINDOW, KV_W), lambda bi, i: (bi, 0, 0)),
        ],
        out_shape=[
            jax.ShapeDtypeStruct((b, t, d), F32),
            jax.ShapeDtypeStruct((b, WINDOW, KV_W), F32),
            jax.ShapeDtypeStruct((b, WINDOW, KV_W), F32),
        ],
        scratch_shapes=[
            pltpu.VMEM((WINDOW + tq, 2 * KV_W), F32),
            pltpu.VMEM((tq, d), BF16),
            pltpu.VMEM((tq, d), BF16),
        ],
        compiler_params=_cparams(("arbitrary", "arbitrary")),
        name="attn_prompt",
    )(sinks, x, g, wqkv, wo, bias)


def _norm_linear_kernel(x_ref, g_ref, w_ref, o_ref):
    h = _rms(x_ref[...], g_ref[...]).astype(BF16)
    o_ref[...] = jnp.dot(h, w_ref[...], preferred_element_type=F32)


def _norm_linear(x, g, w):
    n, d = x.shape
    return pl.pallas_call(
        _norm_linear_kernel,
        out_shape=jax.ShapeDtypeStruct((n, w.shape[1]), F32),
        compiler_params=_cparams(None),
        name="norm_linear",
    )(x, g, w)


def _linear_residual_kernel(x_ref, o_ref, w_ref, xo_ref):
    xo_ref[...] = x_ref[...] + jnp.dot(o_ref[...].astype(BF16), w_ref[...], preferred_element_type=F32)


def _linear_residual(x, o, w):
    return pl.pallas_call(
        _linear_residual_kernel,
        out_shape=jax.ShapeDtypeStruct(x.shape, F32),
        compiler_params=_cparams(None),
        name="linear_residual",
    )(x, o, w)


def _attn_sample_kernel(sink_ref, q_ref, ck_ref, cv_ref, kn_ref, vn_ref, bias_ref, o_ref, kw_ref, vw_ref):
    kw = jnp.concatenate([ck_ref[:, 1:, :], kn_ref[...]], axis=1)
    vw = jnp.concatenate([cv_ref[:, 1:, :], vn_ref[...]], axis=1)
    kw_ref[...] = kw
    vw_ref[...] = vw
    q = (q_ref[...] * (HEAD_DIM ** -0.5)).astype(BF16)
    sc = jnp.einsum('bhc,bjc->bhj', q, kw.astype(BF16), preferred_element_type=F32)
    sc = sc + bias_ref[...][None]
    sink = sink_ref[...][None]
    m = jnp.maximum(jnp.max(sc, axis=-1, keepdims=True), sink)
    e = jnp.exp(sc - m)
    den = jnp.sum(e, axis=-1, keepdims=True) + jnp.exp(sink - m)
    pr = (e / den).astype(BF16)
    o_ref[...] = jnp.einsum('bhj,bjc->bhc', pr, vw.astype(BF16), preferred_element_type=F32)


def _attn_sample(q3, ck, cv, kn, vn, bias_s, sinks_col):
    n = q3.shape[0]
    bb = 8
    blk = lambda s: pl.BlockSpec((bb,) + s, lambda i: (i, 0, 0))
    return pl.pallas_call(
        _attn_sample_kernel,
        grid=(n // bb,),
        in_specs=[_full((N_HEADS, 1)), blk((N_HEADS, KV_W)), blk((WINDOW, KV_W)), blk((WINDOW, KV_W)),
                  blk((1, KV_W)), blk((1, KV_W)), _full((N_HEADS, WINDOW))],
        out_specs=[blk((N_HEADS, KV_W)), blk((WINDOW, KV_W)), blk((WINDOW, KV_W))],
        out_shape=[jax.ShapeDtypeStruct((n, N_HEADS, KV_W), F32),
                   jax.ShapeDtypeStruct((n, WINDOW, KV_W), F32),
                   jax.ShapeDtypeStruct((n, WINDOW, KV_W), F32)],
        compiler_params=_cparams(("arbitrary",)),
        name="attn_sample",
    )(sinks_col, q3, ck, cv, kn, vn, bias_s)


def _attn_sample_layer(x, g, wqkv, wo, bias_s, sinks, cache_k, cache_v):
    n = x.shape[0]
    qkv = _norm_linear(x, g, wqkv)
    q = qkv[:, :D_MODEL].reshape(n, N_KV_HEADS, GQ, 1, HEAD_DIM)
    place = jnp.eye(N_KV_HEADS, dtype=F32).reshape(1, N_KV_HEADS, 1, N_KV_HEADS, 1)
    q3 = (q * place).reshape(n, N_HEADS, KV_W)
    kn = qkv[:, D_MODEL:D_MODEL + KV_W].reshape(n, 1, KV_W)
    vn = qkv[:, D_MODEL + KV_W:].reshape(n, 1, KV_W)
    o3, kw, vw = _attn_sample(q3, cache_k.reshape(n, WINDOW, KV_W), cache_v.reshape(n, WINDOW, KV_W),
                              kn, vn, bias_s, sinks.reshape(N_HEADS, 1))
    o5 = o3.reshape(n, N_KV_HEADS, GQ, N_KV_HEADS, HEAD_DIM)
    o = jnp.stack([o5[:, gi, :, gi, :] for gi in range(N_KV_HEADS)], axis=1).reshape(n, D_MODEL)
    return _linear_residual(x, o, wo), kw, vw


def _ssm_prep_kernel(ar_ref, ai_ref, ldt_ref, br_ref, bi_ref, c_ref, a_out, bc_out, wb_out, wc_out, wc2_out):
    ar, ai = ar_ref[...], ai_ref[...]
    dt = jnp.exp(ldt_ref[...])
    mag = jnp.exp(ar * dt)
    lr = mag * jnp.cos(ai * dt)
    li = mag * jnp.sin(ai * dt)
    den = ar * ar + ai * ai
    cr = ((lr - 1.0) * ar + li * ai) / den
    ci = (li * ar - (lr - 1.0) * ai) / den
    row = lax.broadcasted_iota(jnp.int32, (SUBLANES, N_STATES), 0)
    a_out[...] = jnp.broadcast_to(lr, (SUBLANES, N_STATES))
    bc_out[...] = jnp.where(row < SUBLANES // 2, -li, li)
    for j in range(N_SLABS):
        crj = cr[:, SLAB_STATES * j:SLAB_STATES * (j + 1)]
        cij = ci[:, SLAB_STATES * j:SLAB_STATES * (j + 1)]
        br, bi = br_ref[j], bi_ref[j]
        wb_out[j, :, :SLAB_STATES] = (crj * br - cij * bi).astype(BF16)
        wb_out[j, :, SLAB_STATES:] = (crj * bi + cij * br).astype(BF16)
        wc_out[j, :SLAB_STATES, :] = c_ref[j, :SLAB_STATES, :].astype(BF16)
        wc_out[j, SLAB_STATES:, :] = (-c_ref[j, SLAB_STATES:, :]).astype(BF16)
        wc2_out[j, :, :LANES] = c_ref[j, :SLAB_STATES, :].astype(BF16)
        wc2_out[j, :, LANES:] = (-c_ref[j, SLAB_STATES:, :]).astype(BF16)


def _ssm_prep(a_re, a_im, log_dt, b_re, b_im, c_re, c_im):
    eye = jnp.eye(SUBLANES, dtype=F32)

    def blk_b(b):
        t = b.reshape(N_SLABS, 8, SSM_STATE, SSM_GROUP).transpose(0, 1, 3, 2)
        return jnp.einsum('jghp,gk->jghkp', t, eye).reshape(N_SLABS, LANES, SLAB_STATES)

    def blk_c(c):
        t = c.reshape(N_SLABS, 8, SSM_GROUP, SSM_STATE).transpose(0, 1, 3, 2)
        return jnp.einsum('jgph,gk->jgpkh', t, eye).reshape(N_SLABS, SLAB_STATES, LANES)

    row = lambda a: a.reshape(1, N_STATES)
    ldt = jnp.broadcast_to(log_dt[:, None], (N_SSM_GROUPS, SSM_STATE))
    c_all = jnp.concatenate([blk_c(c_re), blk_c(c_im)], axis=1)
    return pl.pallas_call(
        _ssm_prep_kernel,
        out_shape=[jax.ShapeDtypeStruct((SUBLANES, N_STATES), F32),
                   jax.ShapeDtypeStruct((SUBLANES, N_STATES), F32),
                   jax.ShapeDtypeStruct((N_SLABS, LANES, 2 * SLAB_STATES), BF16),
                   jax.ShapeDtypeStruct((N_SLABS, 2 * SLAB_STATES, LANES), BF16),
                   jax.ShapeDtypeStruct((N_SLABS, SLAB_STATES, 2 * LANES), BF16)],
        compiler_params=_cparams(None),
        name="ssm_prep",
    )(row(a_re), row(a_im), row(ldt), blk_b(b_re), blk_b(b_im), c_all)


def _glu_residual(x, y, u, d, wa, wg):
    z = _gelu_tanh(y + d * u).astype(BF16)
    out = jnp.dot(z, wa, preferred_element_type=F32) * _sigmoid(jnp.dot(z, wg, preferred_element_type=F32))
    return x + out


def _ssm_prompt_kernel(x_ref, g_ref, wb_ref, wc_ref, a_ref, bc_ref, d_ref, wa_ref, wg_ref,
                       xo_ref, st_ref, z_scr, u_scr, o_scr, xs_scr, *, nb, lc):
    i = pl.program_id(0)
    blocks_per_slab = SLAB_STATES // LANES
    half = SUBLANES // 2
    pairs = lc // 2
    rows = half * lc
    roll_half = lambda v: pltpu.roll(v, half, 1)

    @pl.when(i == 0)
    def _():
        xs_scr[...] = jnp.zeros((SUBLANES, N_STATES), F32)
        if nb < half:
            u_scr[...] = jnp.zeros(u_scr.shape, F32)

    for b in range(nb):
        u = _rms(x_ref[b], g_ref[...])
        for c in range(N_SLABS):
            u_scr[c, pl.ds(b, lc, stride=half), :] = u[:, LANES * c:LANES * (c + 1)]

    lower = lax.broadcasted_iota(jnp.int32, (pairs, SUBLANES, LANES), 1) < half
    ys = [None] * N_SLABS

    def project_in(j):
        res = jnp.dot(u_scr[j].astype(BF16), wb_ref[j], preferred_element_type=F32)
        for c in range(blocks_per_slab):
            re = res[:, LANES * c:LANES * (c + 1)].reshape(pairs, SUBLANES, LANES)
            im = res[:, SLAB_STATES + LANES * c:SLAB_STATES + LANES * (c + 1)].reshape(pairs, SUBLANES, LANES)
            blk = blocks_per_slab * j + c
            z_scr[blk, :, 0] = jnp.where(lower, re, roll_half(im))
            z_scr[blk, :, 1] = jnp.where(lower, roll_half(re), im)

    def scan(blks):
        a = [a_ref[:, LANES * q:LANES * (q + 1)] for q in blks]
        bc = [bc_ref[:, LANES * q:LANES * (q + 1)] for q in blks]
        xs = [xs_scr[:, LANES * q:LANES * (q + 1)] for q in blks]
        for tp in range(pairs):
            for par in range(2):
                for k, q in enumerate(blks):
                    xs[k] = a[k] * xs[k] + bc[k] * pltpu.roll(xs[k], half, 0) + z_scr[q, tp, par]
                    z_scr[q, tp, par] = xs[k]
        for k, q in enumerate(blks):
            xs_scr[:, LANES * q:LANES * (q + 1)] = xs[k]

    def project_out(j):
        s = jnp.concatenate([z_scr[blocks_per_slab * j + c].reshape(SUBLANES * lc, LANES)
                             for c in range(blocks_per_slab)], axis=1).astype(BF16)
        p3 = jnp.dot(s, wc_ref[j], preferred_element_type=F32).reshape(lc, SUBLANES, 2 * LANES)
        y4 = (p3[:, :, :LANES] + roll_half(p3[:, :, LANES:])).reshape(pairs, 2, SUBLANES, LANES)
        ys[j] = jnp.where(lower, y4[:, 0], roll_half(y4[:, 1])).reshape(rows, LANES)

    slabs_per_group = 2
    n_groups = N_SLABS // slabs_per_group
    group_slabs = lambda gi: range(slabs_per_group * gi, slabs_per_group * (gi + 1))
    for gi in range(n_groups + 2):
        if gi < n_groups:
            for j in group_slabs(gi):
                project_in(j)
        if 1 <= gi <= n_groups:
            lo = slabs_per_group * blocks_per_slab * (gi - 1)
            scan(list(range(lo, lo + slabs_per_group * blocks_per_slab)))
        if gi >= 2:
            for j in group_slabs(gi - 2):
                project_out(j)
    st_ref[...] = xs_scr[...]
    y = jnp.concatenate(ys, axis=1)
    u_all = jnp.concatenate([u_scr[c] for c in range(N_SLABS)], axis=1)
    z = _gelu_tanh(y + d_ref[...] * u_all).astype(BF16)
    out = (jnp.dot(z, wa_ref[...], preferred_element_type=F32)
           * _sigmoid(jnp.dot(z, wg_ref[...], preferred_element_type=F32)))
    for c in range(N_SLABS):
        o_scr[c] = out[:, LANES * c:LANES * (c + 1)]
    for b in range(nb):
        xo_ref[b] = x_ref[b] + jnp.concatenate(
            [o_scr[c, pl.ds(b, lc, stride=half), :] for c in range(N_SLABS)], axis=1)


def _ssm_prompt_layer(x, g, prep, d, wa, wg):
    a_rows, bc_rows, wb, _, wc2 = prep
    nb, t, dm = x.shape
    half = SUBLANES // 2
    assert nb <= half
    lc = min(128, t)
    kern = functools.partial(_ssm_prompt_kernel, nb=nb, lc=lc)
    return pl.pallas_call(
        kern,
        grid=(t // lc,),
        in_specs=[
            pl.BlockSpec((nb, lc, dm), lambda i: (0, i, 0)),
            _full((1, dm)),
            _full(wb.shape), _full(wc2.shape),
            _full((SUBLANES, N_STATES)), _full((SUBLANES, N_STATES)),
            _full((1, dm)), _full((dm, dm)), _full((dm, dm)),
        ],
        out_specs=[pl.BlockSpec((nb, lc, dm), lambda i: (0, i, 0)), _full((SUBLANES, N_STATES))],
        out_shape=[jax.ShapeDtypeStruct((nb, t, dm), F32), jax.ShapeDtypeStruct((SUBLANES, N_STATES), F32)],
        scratch_shapes=[
            pltpu.VMEM((N_STATES // LANES, lc // 2, 2, SUBLANES, LANES), F32),
            pltpu.VMEM((N_SLABS, half * lc, LANES), F32),
            pltpu.VMEM((N_SLABS, half * lc, LANES), F32),
            pltpu.VMEM((SUBLANES, N_STATES), F32),
        ],
        compiler_params=_cparams(("arbitrary",)),
        name="ssm_prompt",
    )(x, g, wb, wc2, a_rows, bc_rows, d, wa, wg)


def _ssm_sample_kernel(x_ref, g_ref, wb_ref, wc_ref, a_ref, bc_ref, h0r_ref, h0i_ref, d_ref, wa_ref, wg_ref,
                       xo_ref, sr_ref, si_ref):
    x = x_ref[...]
    u = _rms(x, g_ref[...])
    ub = u.astype(BF16)
    ys = []
    for j in range(N_SLABS):
        sl = slice(SLAB_STATES * j, SLAB_STATES * (j + 1))
        res = jnp.dot(ub[:, LANES * j:LANES * (j + 1)], wb_ref[j], preferred_element_type=F32)
        lr = a_ref[0:1, sl]
        li = bc_ref[SUBLANES - 1:SUBLANES, sl]
        h0r, h0i = h0r_ref[:, sl], h0i_ref[:, sl]
        xr = res[:, :SLAB_STATES] + (lr * h0r - li * h0i)
        xi = res[:, SLAB_STATES:] + (lr * h0i + li * h0r)
        sr_ref[:, sl] = xr
        si_ref[:, sl] = xi
        s = jnp.concatenate([xr, xi], axis=1).astype(BF16)
        ys.append(jnp.dot(s, wc_ref[j], preferred_element_type=F32))
    y = jnp.concatenate(ys, axis=1)
    xo_ref[...] = _glu_residual(x, y, u, d_ref[...], wa_ref[...], wg_ref[...])


def _ssm_sample_layer(x, g, prep, h0r, h0i, d, wa, wg):
    a_rows, bc_rows, wb, wc, _ = prep
    n = x.shape[0]
    return pl.pallas_call(
        _ssm_sample_kernel,
        out_shape=[jax.ShapeDtypeStruct(x.shape, F32),
                   jax.ShapeDtypeStruct((n, N_STATES), F32),
                   jax.ShapeDtypeStruct((n, N_STATES), F32)],
        compiler_params=_cparams(None),
        name="ssm_sample",
    )(x, g, wb, wc, a_rows, bc_rows, h0r, h0i, d, wa, wg)


ROUTER_W = LANES
EXPERT_LANE0 = N_EXPERT_GROUPS


def _route_topk(h, wr, br):
    h_hi = h.astype(BF16)
    h_lo = (h - h_hi.astype(F32)).astype(BF16)
    w_hi = wr.astype(BF16)
    w_lo = (wr - w_hi.astype(F32)).astype(BF16)
    dot = lambda a, b: jnp.dot(a, b, preferred_element_type=F32)
    both = dot(h_hi, jnp.concatenate([w_hi, w_lo], axis=1))
    logits = both[:, :ROUTER_W] + (dot(h_lo, w_hi) + both[:, ROUTER_W:]) + br
    lane_i = lax.broadcasted_iota(jnp.int32, logits.shape, 1)
    lane = lane_i.astype(F32)
    neg = -jnp.inf
    past_end = float(ROUTER_W)
    gl = jnp.where(lane_i < N_EXPERT_GROUPS, logits, neg)
    gmax = jnp.max(gl, axis=-1, keepdims=True)
    g_sel = jnp.min(jnp.where(gl == gmax, lane, past_end), axis=-1, keepdims=True)
    p_grp = 1.0 / jnp.sum(jnp.exp(gl - gmax), axis=-1, keepdims=True)
    lane_grp = ((lane_i - EXPERT_LANE0) // EXPERTS_PER_GROUP).astype(F32)
    in_grp = (lane_i >= EXPERT_LANE0) & (lane_i < EXPERT_LANE0 + N_EXPERTS) & (lane_grp == g_sel)
    el = jnp.where(in_grp, logits, neg)
    v1 = jnp.max(el, axis=-1, keepdims=True)
    i1 = jnp.min(jnp.where(el == v1, lane, past_end), axis=-1, keepdims=True)
    el2 = jnp.where(lane == i1, neg, el)
    v2 = jnp.max(el2, axis=-1, keepdims=True)
    i2 = jnp.min(jnp.where(el2 == v2, lane, past_end), axis=-1, keepdims=True)
    e2 = jnp.exp(v2 - v1)
    w1 = p_grp / (1.0 + e2)
    w2 = p_grp * e2 / (1.0 + e2)
    return lane, i1, i2, w1, w2


def _route(h, wr, br):
    lane, i1, i2, w1, w2 = _route_topk(h, wr, br)
    return jnp.where(lane == i1, w1, 0.0) + jnp.where(lane == i2, w2, 0.0)


def _moe_kernel(x_ref, p_ref, gffn_ref, wr_ref, br_ref, wg_ref, wu_ref, wd_ref,
                gple_ref, wpg_ref, wpp_ref, gfin_ref, o_ref, hn_scr, gates_scr, acc_scr, *, final):
    e = pl.program_id(1)

    @pl.when(e == 0)
    def _():
        x = x_ref[...]
        h = _rms(x, gffn_ref[...])
        hn_scr[...] = h.astype(BF16)
        gates_scr[...] = _route(h, wr_ref[...], br_ref[...])
        acc_scr[...] = x

    hb = hn_scr[...]
    gt = jnp.dot(hb, wg_ref[0].astype(BF16), preferred_element_type=F32)
    up = jnp.dot(hb, wu_ref[0].astype(BF16), preferred_element_type=F32)
    a = (gt * _sigmoid(gt) * up).astype(BF16)
    gates = gates_scr[...]
    lane = lax.broadcasted_iota(jnp.int32, gates.shape, 1)
    ge = jnp.sum(jnp.where(lane == e + EXPERT_LANE0, gates, 0.0), axis=-1, keepdims=True)
    acc_scr[...] += ge * jnp.dot(a, wd_ref[0].astype(BF16), preferred_element_type=F32)

    @pl.when(e == pl.num_programs(1) - 1)
    def _():
        x1 = acc_scr[...]
        gate = _sigmoid(jnp.dot(_rms(x1, gple_ref[...]).astype(BF16), wpg_ref[...], preferred_element_type=F32))
        x2 = x1 + gate * jnp.dot(p_ref[...].astype(BF16), wpp_ref[...], preferred_element_type=F32)
        o_ref[...] = _rms(x2, gfin_ref[...]) if final else x2


def _moe_ple_layer(li, x, p, gffn, wr, br, wg, wu, wd, gple, wpg, wpp, gfin, final):
    n, d = x.shape
    tm = min(1024, n)
    kern = functools.partial(_moe_kernel, final=final)
    tok = lambda w: pl.BlockSpec((tm, w), lambda t, e: (t, 0))
    const = lambda s: pl.BlockSpec(s, lambda t, e: (0,) * len(s))
    return pl.pallas_call(
        kern,
        grid=(n // tm, N_EXPERTS),
        in_specs=[
            tok(d), pl.BlockSpec((None, tm, PLE_DIM), lambda t, e: (li, t, 0)),
            const((1, d)), const((d, ROUTER_W)), const((1, ROUTER_W)),
            pl.BlockSpec((None, 1, d, D_EXPERT), lambda t, e: (li, e, 0, 0)),
            pl.BlockSpec((None, 1, d, D_EXPERT), lambda t, e: (li, e, 0, 0)),
            pl.BlockSpec((None, 1, D_EXPERT, d), lambda t, e: (li, e, 0, 0)),
            const((1, d)), const((d, d)), const((PLE_DIM, d)), const((1, d)),
        ],
        out_specs=tok(d),
        out_shape=jax.ShapeDtypeStruct((n, d), F32),
        scratch_shapes=[pltpu.VMEM((tm, d), BF16), pltpu.VMEM((tm, ROUTER_W), F32), pltpu.VMEM((tm, d), F32)],
        compiler_params=_cparams(("arbitrary", "arbitrary")),
        name="moe_ple",
    )(x, p, gffn, wr, br, wg, wu, wd, gple, wpg, wpp, gfin)


TOKEN_DATA_ROWS = D_MODEL // LANES
TOKEN_PITCH = TOKEN_DATA_ROWS + 1
INFO_W1, INFO_W2, INFO_E1, INFO_E2, INFO_R1, INFO_R2 = range(6)
EXPERT_TILE = 256
ZERO_RUN = 256


def _to_token_tiles(ref_2d, val, n, first=0):
    for s in range(TOKEN_DATA_ROWS):
        ref_2d[pl.ds(first * TOKEN_PITCH + s, n, stride=TOKEN_PITCH), :] = val[:, LANES * s:LANES * (s + 1)]


def _zero_spare_rows(ref_2d, n):
    ref_2d[pl.ds(TOKEN_DATA_ROWS, n, stride=TOKEN_PITCH), :] = jnp.zeros((n, LANES), F32)


def _from_token_tiles(ref_2d, n, first=0):
    return jnp.concatenate([ref_2d[pl.ds(first * TOKEN_PITCH + s, n, stride=TOKEN_PITCH), :]
                            for s in range(TOKEN_DATA_ROWS)], axis=1)


def _token_slots(i, n=1):
    return pl.ds(i * TOKEN_PITCH, n * TOKEN_PITCH)


def _slot_at(first_row):
    return pl.ds(first_row, TOKEN_PITCH)


def _route_kernel(x_ref, g_ref, wr_ref, br_ref, before_ref, info_ref, cnt_ref, base_scr):
    i = pl.program_id(0)

    @pl.when(i == 0)
    def _():
        base_scr[...] = jnp.zeros(base_scr.shape, F32)

    h = _rms(x_ref[...], g_ref[...])
    lane, i1, i2, w1, w2 = _route_topk(h, wr_ref[...], br_ref[...])
    chosen = jnp.where((lane == i1) | (lane == i2), 1.0, 0.0)
    prefix = jnp.dot(before_ref[...], chosen.astype(BF16), preferred_element_type=F32) + base_scr[...]
    rank1 = jnp.sum(jnp.where(lane == i1, prefix, 0.0), axis=-1, keepdims=True)
    rank2 = jnp.sum(jnp.where(lane == i2, prefix, 0.0), axis=-1, keepdims=True)
    base_scr[...] += jnp.sum(chosen, axis=0, keepdims=True)
    cnt_ref[...] = base_scr[...]
    e1 = (i1 - EXPERT_LANE0).astype(F32)
    e2 = (i2 - EXPERT_LANE0).astype(F32)
    info = jnp.zeros(h.shape[:1] + (ROUTER_W,), F32)
    for ln, v in ((INFO_W1, w1), (INFO_W2, w2), (INFO_E1, e1), (INFO_E2, e2), (INFO_R1, rank1), (INFO_R2, rank2)):
        info = jnp.where(lane == ln, v, info)
    info_ref[...] = info


def _route_call(x, gffn, wr, br):
    n, d = x.shape
    tm = 512
    before = jnp.asarray(np.tril(np.ones((tm, tm), np.float32), -1), BF16)
    return pl.pallas_call(
        _route_kernel,
        grid=(n // tm,),
        in_specs=[pl.BlockSpec((tm, d), lambda i: (i, 0)), _full((1, d)), _full((d, ROUTER_W)), _full((1, ROUTER_W)),
                  _full((tm, tm))],
        out_specs=[pl.BlockSpec((tm, ROUTER_W), lambda i: (i, 0)), _full((1, ROUTER_W))],
        out_shape=[jax.ShapeDtypeStruct((n, ROUTER_W), F32), jax.ShapeDtypeStruct((1, ROUTER_W), F32)],
        scratch_shapes=[pltpu.VMEM((1, ROUTER_W), F32)],
        compiler_params=_cparams(("arbitrary",)),
        name="moe_route",
    )(x, gffn, wr, br, before)


def _dispatch_kernel(pos1_ref, pos2_ref, cnt_ref, start_ref, x_ref, g_ref, xs_hbm, hbuf, zbuf, sem, zsem,
                     *, td, n_steps, n_pad_rows):
    i = pl.program_id(0)
    slot = i % 2

    def wait_slot(sl):
        for _ in range(2):
            pltpu.make_async_copy(hbuf.at[sl], hbuf.at[sl], sem.at[sl]).wait()

    @pl.when(i >= 2)
    def _():
        wait_slot(slot)

    @pl.when(i < 2)
    def _():
        _zero_spare_rows(hbuf.at[slot], td)

    _to_token_tiles(hbuf.at[slot], _rms(x_ref[...], g_ref[...]), td)

    def issue(n8, carry):
        for u in range(SUBLANES):
            n = n8 * SUBLANES + u
            t = i * td + n
            src = hbuf.at[slot, _token_slots(n), :]
            pltpu.make_async_copy(src, xs_hbm.at[_slot_at(pos1_ref[t]), :], sem.at[slot]).start(priority=0)
            pltpu.make_async_copy(src, xs_hbm.at[_slot_at(pos2_ref[t]), :], sem.at[slot]).start(priority=1)
        return carry

    lax.fori_loop(0, td // SUBLANES, issue, 0)

    @pl.when(i == 0)
    def _():
        zbuf[...] = jnp.zeros(zbuf.shape, zbuf.dtype)

        def zero_run(first, n_slots):
            pltpu.make_async_copy(zbuf.at[_token_slots(0, n_slots), :], xs_hbm.at[_token_slots(first, n_slots), :],
                                  zsem).start()

        for e in range(N_EXPERTS + 1):
            lo = start_ref[e] + cnt_ref[e]
            n = start_ref[e + 1] - lo

            def full_run(k, carry, lo=lo):
                zero_run(lo + k * ZERO_RUN, ZERO_RUN)
                return carry

            lax.fori_loop(0, n // ZERO_RUN, full_run, 0)
            cur = lo + (n // ZERO_RUN) * ZERO_RUN
            p = ZERO_RUN // 2
            while p >= 1:
                @pl.when((n & p) != 0)
                def _(cur=cur, p=p):
                    zero_run(cur, p)

                cur = cur + (n & p)
                p //= 2

    @pl.when(i == n_steps - 1)
    def _():
        wait_slot(slot)
        if n_steps > 1:
            wait_slot(1 - slot)
        pad = xs_hbm.at[_token_slots(0, n_pad_rows), :]
        pltpu.make_async_copy(pad, pad, zsem).wait()


def _dispatch_call(pos1, pos2, cnt, start, x, gffn, a_pad, n_pad_rows):
    n, d = x.shape
    td = 512
    n_steps = n // td
    kern = functools.partial(_dispatch_kernel, td=td, n_steps=n_steps, n_pad_rows=n_pad_rows)
    return pl.pallas_call(
        kern,
        grid_spec=pltpu.PrefetchScalarGridSpec(
            num_scalar_prefetch=4,
            grid=(n_steps,),
            in_specs=[pl.BlockSpec((td, d), lambda i, *_: (i, 0)), pl.BlockSpec((1, d), lambda i, *_: (0, 0))],
            out_specs=pl.BlockSpec(memory_space=pl.ANY),
            scratch_shapes=[pltpu.VMEM((2, td * TOKEN_PITCH, LANES), F32),
                            pltpu.VMEM((ZERO_RUN * TOKEN_PITCH, LANES), F32),
                            pltpu.SemaphoreType.DMA((2,)), pltpu.SemaphoreType.DMA(())],
        ),
        out_shape=jax.ShapeDtypeStruct((a_pad * TOKEN_PITCH, LANES), F32),
        compiler_params=_cparams(("arbitrary",)),
        name="moe_dispatch",
    )(pos1, pos2, cnt, start, x, gffn)


EXPERT_IN_SLOTS = 3


TILES_PER_STEP = 2


def _experts_kernel(te_ref, xs_hbm, *refs, te, n_steps):
    nu = TILES_PER_STEP
    w_refs = [refs[3 * u:3 * u + 3] for u in range(nu)]
    y_ref, xbuf, xsem = refs[3 * nu:3 * nu + 3]
    w_scrs = [refs[3 * nu + 3 + 3 * u:3 * nu + 6 + 3 * u] for u in range(nu)]
    t = pl.program_id(0)
    rows = nu * te * TOKEN_PITCH

    def fetch(step):
        slot = step % EXPERT_IN_SLOTS
        src = xs_hbm.at[pl.ds(pl.multiple_of(step * rows, rows), rows), :]
        return pltpu.make_async_copy(src, xbuf.at[slot], xsem.at[slot])

    @pl.when(t == 0)
    def _():
        for ahead in range(min(EXPERT_IN_SLOTS - 1, n_steps)):
            fetch(ahead).start()

    @pl.when(t + EXPERT_IN_SLOTS - 1 < n_steps)
    def _():
        fetch(t + EXPERT_IN_SLOTS - 1).start()

    for u in range(nu):
        tile = nu * t + u

        @pl.when((t == 0) | (te_ref[tile] != te_ref[jnp.maximum(tile - nu, 0)]))
        def _(u=u):
            for scr, ref in zip(w_scrs[u], w_refs[u]):
                scr[...] = ref[0].astype(BF16)

    fetch(t).wait()
    for u in range(nu):
        wg_scr, wu_scr, wd_scr = w_scrs[u]
        x = _from_token_tiles(xbuf.at[t % EXPERT_IN_SLOTS], te, u * te).astype(BF16)
        gt = jnp.dot(x, wg_scr[...], preferred_element_type=F32)
        up = jnp.dot(x, wu_scr[...], preferred_element_type=F32)
        a = (gt * _sigmoid(gt) * up).astype(BF16)
        _to_token_tiles(y_ref, jnp.dot(a, wd_scr[...], preferred_element_type=F32), te, u * te)
    _zero_spare_rows(y_ref, nu * te)


def _experts_call(li, tile_expert, xs, wg, wu, wd, te):
    nu = TILES_PER_STEP
    rows = nu * te * TOKEN_PITCH
    assert xs.shape[0] % rows == 0
    n_steps = xs.shape[0] // rows
    d = wg.shape[2]
    kern = functools.partial(_experts_kernel, te=te, n_steps=n_steps)

    def weight_specs(u):
        pick = lambda t, ex: (li, ex[nu * t + u], 0, 0)
        return [pl.BlockSpec((None, 1, d, D_EXPERT), pick), pl.BlockSpec((None, 1, d, D_EXPERT), pick),
                pl.BlockSpec((None, 1, D_EXPERT, d), pick)]

    w_scratch = [pltpu.VMEM((d, D_EXPERT), BF16), pltpu.VMEM((d, D_EXPERT), BF16), pltpu.VMEM((D_EXPERT, d), BF16)]
    return pl.pallas_call(
        kern,
        grid_spec=pltpu.PrefetchScalarGridSpec(
            num_scalar_prefetch=1,
            grid=(n_steps,),
            in_specs=[pl.BlockSpec(memory_space=pl.ANY)] + [s for u in range(nu) for s in weight_specs(u)],
            out_specs=pl.BlockSpec((rows, LANES), lambda t, ex: (t, 0)),
            scratch_shapes=[pltpu.VMEM((EXPERT_IN_SLOTS, rows, LANES), F32),
                            pltpu.SemaphoreType.DMA((EXPERT_IN_SLOTS,))] + w_scratch * nu,
        ),
        out_shape=jax.ShapeDtypeStruct((n_steps * rows, LANES), F32),
        compiler_params=_cparams(("arbitrary",)),
        name="moe_experts",
    )(tile_expert, xs, *([wg, wu, wd] * nu))


def _combine_kernel(pos1_ref, pos2_ref, x_ref, info_ref, p_ref, gple_ref, wpg_ref, wpp_ref, gfin_ref, ys_hbm,
                    o_ref, ybuf, sem, *, tm, n_steps, final):
    i = pl.program_id(0)
    slot = i % 2

    def issue(step, sl):
        def body(n8, carry):
            for u in range(SUBLANES):
                n = n8 * SUBLANES + u
                t = step * tm + n
                for k, pos_ref in enumerate((pos1_ref, pos2_ref)):
                    pltpu.make_async_copy(ys_hbm.at[_slot_at(pos_ref[t]), :],
                                          ybuf.at[sl, k, _token_slots(n), :], sem.at[sl]).start(priority=k)
            return carry

        lax.fori_loop(0, tm // SUBLANES, body, 0)

    def wait_rows(sl):
        for k in range(2):
            pltpu.make_async_copy(ybuf.at[sl, k], ybuf.at[sl, k], sem.at[sl]).wait()

    @pl.when(i == 0)
    def _():
        issue(0, 0)

    @pl.when(i + 1 < n_steps)
    def _():
        issue(i + 1, 1 - slot)

    wait_rows(slot)
    part = min(tm, 256)
    for first in range(0, tm, part):
        tok = slice(first, first + part)
        y1 = _from_token_tiles(ybuf.at[slot, 0], part, first)
        y2 = _from_token_tiles(ybuf.at[slot, 1], part, first)
        info = info_ref[tok, :]
        x1 = x_ref[tok, :] + (info[:, INFO_W1:INFO_W1 + 1] * y1 + info[:, INFO_W2:INFO_W2 + 1] * y2)
        gate = _sigmoid(jnp.dot(_rms(x1, gple_ref[...]).astype(BF16), wpg_ref[...], preferred_element_type=F32))
        x2 = x1 + gate * jnp.dot(p_ref[tok, :].astype(BF16), wpp_ref[...], preferred_element_type=F32)
        o_ref[tok, :] = _rms(x2, gfin_ref[...]) if final else x2


def _combine_call(li, pos1, pos2, x, info, p, gple, wpg, wpp, gfin, ys, final):
    n, d = x.shape
    tm = 256
    n_steps = n // tm
    kern = functools.partial(_combine_kernel, tm=tm, n_steps=n_steps, final=final)
    tok = lambda w: pl.BlockSpec((tm, w), lambda i, *_: (i, 0))
    const = lambda s: pl.BlockSpec(s, lambda i, *_: (0,) * len(s))
    return pl.pallas_call(
        kern,
        grid_spec=pltpu.PrefetchScalarGridSpec(
            num_scalar_prefetch=2,
            grid=(n_steps,),
            in_specs=[tok(d), tok(ROUTER_W), pl.BlockSpec((None, tm, PLE_DIM), lambda i, *_: (li, i, 0)),
                      const((1, d)), const((d, d)), const((PLE_DIM, d)),
                      const((1, d)), pl.BlockSpec(memory_space=pl.ANY)],
            out_specs=tok(d),
            scratch_shapes=[pltpu.VMEM((2, 2, tm * TOKEN_PITCH, LANES), F32), pltpu.SemaphoreType.DMA((2,))],
        ),
        out_shape=jax.ShapeDtypeStruct((n, d), F32),
        compiler_params=_cparams(("arbitrary",)),
        name="moe_combine",
    )(pos1, pos2, x, info, p, gple, wpg, wpp, gfin, ys)


def _positions_kernel(info_ref, start_ref, pos_ref):
    info = info_ref[...]
    lane = lax.broadcasted_iota(jnp.int32, info.shape, 1)
    lane_expert = (lane - EXPERT_LANE0).astype(F32)
    start = start_ref[...]
    cols = []
    for e_lane, r_lane in ((INFO_E1, INFO_R1), (INFO_E2, INFO_R2)):
        seg = jnp.sum(jnp.where(lane_expert == info[:, e_lane:e_lane + 1], start, 0.0), axis=-1, keepdims=True)
        cols.append((seg + info[:, r_lane:r_lane + 1]) * float(TOKEN_PITCH))
    both = jnp.where(lane == 0, cols[0], jnp.where(lane == 1, cols[1], 0.0))
    pos_ref[...] = both.T[:SUBLANES, :].astype(jnp.int32)


def _positions_call(info, start_row):
    n = info.shape[0]
    tm = 2048 if n % 2048 == 0 else 512
    return pl.pallas_call(
        _positions_kernel,
        grid=(n // tm,),
        in_specs=[pl.BlockSpec((tm, ROUTER_W), lambda i: (i, 0)), _full((1, ROUTER_W))],
        out_specs=pl.BlockSpec((SUBLANES, tm), lambda i: (0, i)),
        out_shape=jax.ShapeDtypeStruct((SUBLANES, n), jnp.int32),
        compiler_params=_cparams(("arbitrary",)),
        name="moe_positions",
    )(info, start_row)


def _moe_routed_layer(li, x, p, gffn, wr, br, wg, wu, wd, gple, wpg, wpp, gfin, final):
    n = x.shape[0]
    te = EXPERT_TILE
    n_assign = 2 * n
    assert n_assign % te == 0
    a_pad = n_assign + N_EXPERTS * te
    info, cnt = _route_call(x, gffn, wr, br)
    counts = cnt[0, EXPERT_LANE0:EXPERT_LANE0 + N_EXPERTS].astype(jnp.int32)
    padded = (counts + te - 1) // te * te
    seg_end = jnp.cumsum(padded)
    seg_start = seg_end - padded
    start_row = jnp.zeros((1, ROUTER_W), F32).at[0, EXPERT_LANE0:EXPERT_LANE0 + N_EXPERTS].set(seg_start.astype(F32))
    pos = _positions_call(info, start_row)
    pos1, pos2 = pos[0], pos[1]
    start = jnp.concatenate([seg_start, seg_end[-1:], jnp.full((1,), a_pad, jnp.int32)])
    cnt_pad = jnp.concatenate([counts, jnp.zeros((1,), jnp.int32)])
    tile_row0 = jnp.arange(a_pad // te, dtype=jnp.int32) * te
    tile_expert = jnp.minimum(jnp.sum(seg_end[None, :] <= tile_row0[:, None], axis=1), N_EXPERTS - 1).astype(jnp.int32)
    xs = _dispatch_call(pos1, pos2, cnt_pad, start, x, gffn, a_pad, N_EXPERTS * te)
    ys = _experts_call(li, tile_expert, xs, wg, wu, wd, te)
    return _combine_call(li, pos1, pos2, x, info, p, gple, wpg, wpp, gfin, ys, final)


def kernel(x_prompt, x_sample, cache_k_win, cache_v_win, state_ssm_re, state_ssm_im, p_prompt, p_sample,
           rel_bias_table, g_mix, w_qkv, w_o, sinks, ssm_a_re, ssm_a_im, ssm_log_dt, ssm_b_re, ssm_b_im,
           ssm_c_re, ssm_c_im, ssm_d, w_glu_a, w_glu_b, g_ffn, w_router_group, b_router_group,
           w_router_expert, b_router_expert, w_exp_gate, w_exp_up, w_exp_down, g_ple, w_ple_gate,
           w_ple_proj, g_final):
    bsz, seq, d = x_prompt.shape
    ns = x_sample.shape[0]
    depth = g_mix.shape[0]
    row = lambda v: v.reshape(1, -1).astype(F32)

    qi = np.arange(WINDOW)[:, None]
    sj = np.arange(2 * WINDOW)[None, :]
    dist = qi + WINDOW - sj
    idx_p = _t5_bucket_np(dist)
    ok_p = ((dist >= 0) & (dist < WINDOW)).astype(np.int32)
    idx_s = np.broadcast_to(_t5_bucket_np(WINDOW - 1 - np.arange(WINDOW))[None, :], (SUBLANES, WINDOW))
    bias_p = _band_bias(rel_bias_table, idx_p, ok_p)
    bias_s = _bias_from_table(rel_bias_table, idx_s)[:, 0, :]

    wqkv_b = w_qkv[0].astype(BF16)
    wo_b = w_o[0].astype(BF16)
    wpg_b, wpp_b = w_ple_gate.astype(BF16), w_ple_proj.astype(BF16)
    wa_b, wgl_b = w_glu_a[0].astype(BF16), w_glu_b[0].astype(BF16)
    pad = ROUTER_W - N_EXPERT_GROUPS - N_EXPERTS
    wr = jnp.concatenate([w_router_group, w_router_expert.reshape(depth, d, N_EXPERTS),
                          jnp.zeros((depth, d, pad), F32)], axis=-1)
    br = jnp.concatenate([b_router_group, b_router_expert.reshape(depth, N_EXPERTS),
                          jnp.zeros((depth, pad), F32)], axis=-1)

    prep = _ssm_prep(ssm_a_re[0], ssm_a_im[0], ssm_log_dt[0], ssm_b_re[0], ssm_b_im[0], ssm_c_re[0], ssm_c_im[0])
    d_row = row(ssm_d[0])

    def moe(x2d, p_all, i):
        layer = _moe_routed_layer if 2 * x2d.shape[0] >= N_EXPERTS * EXPERT_TILE else _moe_ple_layer
        return layer(i, x2d, p_all, row(g_ffn[i]), wr[i], br[i:i + 1], w_exp_gate, w_exp_up, w_exp_down,
                     row(g_ple[i]), wpg_b[i], wpp_b[i], row(g_final), final=(i == depth - 1))

    pp = p_prompt.reshape(depth, bsz * seq, PLE_DIM)
    xp, kp, vp = _attn_prompt_layer(x_prompt, row(g_mix[0]), wqkv_b, wo_b, bias_p, sinks[0])
    xp = moe(xp.reshape(bsz * seq, d), pp, 0).reshape(bsz, seq, d)
    xp, st_p = _ssm_prompt_layer(xp, row(g_mix[1]), prep, d_row, wa_b, wgl_b)
    y_prompt = moe(xp.reshape(bsz * seq, d), pp, 1).reshape(bsz, seq, d)

    ps = p_sample.reshape(depth, ns, PLE_DIM)
    xs = x_sample.reshape(ns, d)
    xs, kws, vws = _attn_sample_layer(xs, row(g_mix[0]), wqkv_b, wo_b, bias_s, sinks[0],
                                      cache_k_win[0], cache_v_win[0])
    xs = moe(xs, ps, 0)
    xs, sr_s, si_s = _ssm_sample_layer(xs, row(g_mix[1]), prep, state_ssm_re[0].reshape(ns, N_STATES),
                                       state_ssm_im[0].reshape(ns, N_STATES), d_row, wa_b, wgl_b)
    y_sample = moe(xs, ps, 1).reshape(ns, 1, d)

    half = SUBLANES // 2
    win = lambda a, n: a.reshape(1, n, WINDOW, N_KV_HEADS, HEAD_DIM)
    st = lambda a, n: a.reshape(1, n, N_SSM_GROUPS, SSM_STATE)
    return (y_prompt, y_sample, win(kp, bsz), win(vp, bsz), win(kws, ns), win(vws, ns),
            st(st_p[:bsz], bsz), st(st_p[half:half + bsz], bsz), st(sr_s, ns), st(si_s, ns))
```

```python
import functools
import math

import numpy as np
import jax
import jax.numpy as jnp
from jax import lax
from jax.experimental import pallas as pl
from jax.experimental.pallas import tpu as pltpu

F32 = jnp.float32
BF16 = jnp.bfloat16

D_MODEL = 1024
HEAD_DIM = 64
N_HEADS = 16
N_KV_HEADS = 2
GQ = N_HEADS // N_KV_HEADS
WINDOW = 128
N_BUCKETS = 32
MAX_EXACT = 16
MAX_DISTANCE = 128
SSM_GROUP = 16
N_SSM_GROUPS = 64
SSM_STATE = 64
N_STATES = N_SSM_GROUPS * SSM_STATE
N_EXPERT_GROUPS = 4
EXPERTS_PER_GROUP = 8
N_EXPERTS = 32
D_EXPERT = 256
PLE_DIM = 256
RMS_EPS = 1e-6
NEG_INF = -1e30

LANES = 128
SUBLANES = 8
KV_W = N_KV_HEADS * HEAD_DIM
QKV_W = D_MODEL + 2 * KV_W
SLAB_STATES = 512
N_SLABS = D_MODEL // LANES
VMEM_LIMIT = 56 * 1024 * 1024


def _cparams(sem):
    return pltpu.CompilerParams(dimension_semantics=sem, vmem_limit_bytes=VMEM_LIMIT)


def _rms(x, g):
    return x * lax.rsqrt(jnp.mean(x * x, axis=-1, keepdims=True) + RMS_EPS) * g


def _gelu_tanh(x):
    c = math.sqrt(2.0 / math.pi)
    return x * (0.5 * (1.0 + jnp.tanh(c * (x + 0.044715 * (x * x * x)))))


def _sigmoid(x):
    return 1.0 / (1.0 + jnp.exp(-x))


def _full(shape):
    n = len(shape)
    return pl.BlockSpec(shape, lambda *_: (0,) * n)


def _t5_bucket_np(dist):
    n = np.maximum(dist, 0)
    nf = np.maximum(n, 1).astype(np.float64)
    large = MAX_EXACT + (np.log(nf / MAX_EXACT) / math.log(MAX_DISTANCE / MAX_EXACT)
                         * (N_BUCKETS - MAX_EXACT)).astype(np.int32)
    large = np.minimum(large, N_BUCKETS - 1)
    return np.where(n < MAX_EXACT, n, large).astype(np.int32)


def _bias_kernel(table_ref, idx_ref, o_ref):
    h = pl.program_id(0)
    idx = idx_ref[...]
    acc = jnp.zeros(idx.shape, F32)
    for b in range(N_BUCKETS):
        acc = jnp.where(idx == b, table_ref[b, h], acc)
    o_ref[0] = acc


def _bias_from_table(table, idx_np):
    q, k = idx_np.shape
    return pl.pallas_call(
        _bias_kernel,
        grid=(N_HEADS,),
        in_specs=[pl.BlockSpec(memory_space=pltpu.SMEM), _full((q, k))],
        out_specs=pl.BlockSpec((1, q, k), lambda h: (h, 0, 0)),
        out_shape=jax.ShapeDtypeStruct((N_HEADS, q, k), F32),
        name="rel_bias",
    )(table, jnp.asarray(idx_np))


HEADS_PER_UNIT = GQ // 2
N_UNITS = N_HEADS // HEADS_PER_UNIT
UNIT_ROWS = HEADS_PER_UNIT * WINDOW


def _band_bias_kernel(table_ref, idx_ref, ok_ref, o_ref):
    h = pl.program_id(0)
    idx = idx_ref[...]
    acc = jnp.zeros(idx.shape, F32)
    for b in range(N_BUCKETS):
        acc = jnp.where(idx == b, table_ref[b, h], acc)
    base = jnp.where(ok_ref[...] != 0, acc, NEG_INF)
    sj = lax.broadcasted_iota(jnp.int32, idx.shape, 1)
    o_ref[0, 0] = base
    o_ref[1, 0] = jnp.where(sj >= WINDOW, base, NEG_INF)


def _band_bias(table, idx_np, ok_np):
    def out_map(h):
        r = h % GQ
        return (0, (h // GQ) * 2 + r % 2, r // 2, 0)

    return pl.pallas_call(
        _band_bias_kernel,
        grid=(N_HEADS,),
        in_specs=[pl.BlockSpec(memory_space=pltpu.SMEM), _full(idx_np.shape), _full(ok_np.shape)],
        out_specs=pl.BlockSpec((2, 1, WINDOW, 2 * WINDOW), out_map),
        out_shape=jax.ShapeDtypeStruct((2, N_UNITS, UNIT_ROWS, 2 * WINDOW), F32),
        name="band_bias",
    )(table, jnp.asarray(idx_np), jnp.asarray(ok_np))


def _attn_prompt_kernel(sink_ref, x_ref, g_ref, wqkv_ref, wo_ref, bias_ref,
                        xo_ref, k_ref, v_ref, kv_scr, q_scr, o_scr, *, tq):
    i = pl.program_id(1)
    nsub = tq // WINDOW

    @pl.when(i == 0)
    def _():
        kv_scr[0:WINDOW, :] = jnp.zeros((WINDOW, 2 * KV_W), F32)

    x = x_ref[0]
    h = _rms(x, g_ref[...]).astype(BF16)
    qkv = jnp.dot(h, wqkv_ref[...], preferred_element_type=F32)
    q_scr[...] = (qkv[:, :D_MODEL] * (HEAD_DIM ** -0.5)).astype(BF16)
    kv_scr[WINDOW:WINDOW + tq, :] = qkv[:, D_MODEL:]
    k_ref[0] = qkv[tq - WINDOW:, D_MODEL:D_MODEL + KV_W]
    v_ref[0] = qkv[tq - WINDOW:, D_MODEL + KV_W:]

    lo = lax.broadcasted_iota(jnp.int32, (2 * WINDOW, KV_W), 1) < HEAD_DIM
    row_head = lax.broadcasted_iota(jnp.int32, (UNIT_ROWS, 1), 0) // WINDOW
    sinks = []
    for u in range(N_UNITS):
        g, p = divmod(u, 2)
        col = jnp.zeros((UNIT_ROWS, 1), F32)
        for jj in range(HEADS_PER_UNIT):
            col = jnp.where(row_head == jj, sink_ref[GQ * g + 2 * jj + p], col)
        sinks.append(col)

    for s in range(nsub):
        band = kv_scr[WINDOW * s:WINDOW * s + 2 * WINDOW, :]
        kband, vband = band[:, :KV_W], band[:, KV_W:]
        kroll = pltpu.roll(kband, HEAD_DIM, 1)
        vroll = pltpu.roll(vband, HEAD_DIM, 1)
        kpad = [[jnp.where(lo, kband, 0.0).astype(BF16), jnp.where(lo, 0.0, kroll).astype(BF16)],
                [jnp.where(lo, kroll, 0.0).astype(BF16), jnp.where(lo, 0.0, kband).astype(BF16)]]
        vpad = [[jnp.where(lo, vband, 0.0).astype(BF16), jnp.where(lo, 0.0, vroll).astype(BF16)],
                [jnp.where(lo, vroll, 0.0).astype(BF16), jnp.where(lo, 0.0, vband).astype(BF16)]]
        variant = jnp.where(i * nsub + s > 0, 0, 1)
        for g in range(N_KV_HEADS):
            slabs = range(HEADS_PER_UNIT * g, HEADS_PER_UNIT * (g + 1))
            qs = jnp.concatenate([q_scr[WINDOW * s:WINDOW * (s + 1), LANES * j:LANES * (j + 1)] for j in slabs],
                                 axis=0)
            acc = None
            for p in range(2):
                u = 2 * g + p
                sc = lax.dot_general(qs, kpad[g][p], (((1,), (1,)), ((), ())), preferred_element_type=F32)
                sc = sc + bias_ref[variant, u]
                m = jnp.maximum(jnp.max(sc, axis=-1, keepdims=True), sinks[u])
                e = jnp.exp(sc - m)
                den = jnp.sum(e, axis=-1, keepdims=True) + jnp.exp(sinks[u] - m)
                pv = jnp.dot(e.astype(BF16), vpad[g][p], preferred_element_type=F32)
                term = pv * (1.0 / den)
                acc = term if acc is None else acc + term
            for jj, j in enumerate(slabs):
                o_scr[WINDOW * s:WINDOW * (s + 1), LANES * j:LANES * (j + 1)] = (
                    acc[WINDOW * jj:WINDOW * (jj + 1)].astype(BF16))

    kv_scr[0:WINDOW, :] = kv_scr[tq:tq + WINDOW, :]
    xo_ref[0] = x + jnp.dot(o_scr[...], wo_ref[...], preferred_element_type=F32)


def _attn_prompt_layer(x, g, wqkv, wo, bias, sinks):
    b, t, d = x.shape
    tq = min(512, t)
    kern = functools.partial(_attn_prompt_kernel, tq=tq)
    return pl.pallas_call(
        kern,
        grid=(b, t // tq),
        in_specs=[
            pl.BlockSpec(memory_space=pltpu.SMEM),
            pl.BlockSpec((1, tq, d), lambda bi, i: (bi, i, 0)),
            _full((1, d)),
            _full((d, QKV_W)),
            _full((d, d)),
            _full((2, N_UNITS, UNIT_ROWS, 2 * WINDOW)),
        ],
        out_specs=[
            pl.BlockSpec((1, tq, d), lambda bi, i: (bi, i, 0)),
            pl.BlockSpec((1, WINDOW, KV_W), lambda bi, i: (bi, 0, 0)),
            pl.BlockSpec((1, WINDOW, KV_W), lambda bi, i: (bi, 0, 0)),
        ],
        out_shape=[
            jax.ShapeDtypeStruct((b, t, d), F32),
            jax.ShapeDtypeStruct((b, WINDOW, KV_W), F32),
            jax.ShapeDtypeStruct((b, WINDOW, KV_W), F32),
        ],
        scratch_shapes=[
            pltpu.VMEM((WINDOW + tq, 2 * KV_W), F32),
            pltpu.VMEM((tq, d), BF16),
            pltpu.VMEM((tq, d), BF16),
        ],
        compiler_params=_cparams(("arbitrary", "arbitrary")),
        name="attn_prompt",
    )(sinks, x, g, wqkv, wo, bias)


def _norm_linear_kernel(x_ref, g_ref, w_ref, o_ref):
    h = _rms(x_ref[...], g_ref[...]).astype(BF16)
    o_ref[...] = jnp.dot(h, w_ref[...], preferred_element_type=F32)


def _norm_linear(x, g, w):
    n, d = x.shape
    return pl.pallas_call(
        _norm_linear_kernel,
        out_shape=jax.ShapeDtypeStruct((n, w.shape[1]), F32),
        compiler_params=_cparams(None),
        name="norm_linear",
    )(x, g, w)


def _linear_residual_kernel(x_ref, o_ref, w_ref, xo_ref):
    xo_ref[...] = x_ref[...] + jnp.dot(o_ref[...].astype(BF16), w_ref[...], preferred_element_type=F32)


def _linear_residual(x, o, w):
    return pl.pallas_call(
        _linear_residual_kernel,
        out_shape=jax.ShapeDtypeStruct(x.shape, F32),
        compiler_params=_cparams(None),
        name="linear_residual",
    )(x, o, w)


def _attn_sample_kernel(sink_ref, q_ref, ck_ref, cv_ref, kn_ref, vn_ref, bias_ref, o_ref, kw_ref, vw_ref):
    kw = jnp.concatenate([ck_ref[:, 1:, :], kn_ref[...]], axis=1)
    vw = jnp.concatenate([cv_ref[:, 1:, :], vn_ref[...]], axis=1)
    kw_ref[...] = kw
    vw_ref[...] = vw
    q = (q_ref[...] * (HEAD_DIM ** -0.5)).astype(BF16)
    sc = jnp.einsum('bhc,bjc->bhj', q, kw.astype(BF16), preferred_element_type=F32)
    sc = sc + bias_ref[...][None]
    sink = sink_ref[...][None]
    m = jnp.maximum(jnp.max(sc, axis=-1, keepdims=True), sink)
    e = jnp.exp(sc - m)
    den = jnp.sum(e, axis=-1, keepdims=True) + jnp.exp(sink - m)
    pr = (e / den).astype(BF16)
    o_ref[...] = jnp.einsum('bhj,bjc->bhc', pr, vw.astype(BF16), preferred_element_type=F32)


def _attn_sample(q3, ck, cv, kn, vn, bias_s, sinks_col):
    n = q3.shape[0]
    bb = 8
    blk = lambda s: pl.BlockSpec((bb,) + s, lambda i: (i, 0, 0))
    return pl.pallas_call(
        _attn_sample_kernel,
        grid=(n // bb,),
        in_specs=[_full((N_HEADS, 1)), blk((N_HEADS, KV_W)), blk((WINDOW, KV_W)), blk((WINDOW, KV_W)),
                  blk((1, KV_W)), blk((1, KV_W)), _full((N_HEADS, WINDOW))],
        out_specs=[blk((N_HEADS, KV_W)), blk((WINDOW, KV_W)), blk((WINDOW, KV_W))],
        out_shape=[jax.ShapeDtypeStruct((n, N_HEADS, KV_W), F32),
                   jax.ShapeDtypeStruct((n, WINDOW, KV_W), F32),
                   jax.ShapeDtypeStruct((n, WINDOW, KV_W), F32)],
        compiler_params=_cparams(("arbitrary",)),
        name="attn_sample",
    )(sinks_col, q3, ck, cv, kn, vn, bias_s)


def _attn_sample_layer(x, g, wqkv, wo, bias_s, sinks, cache_k, cache_v):
    n = x.shape[0]
    qkv = _norm_linear(x, g, wqkv)
    q = qkv[:, :D_MODEL].reshape(n, N_KV_HEADS, GQ, 1, HEAD_DIM)
    place = jnp.eye(N_KV_HEADS, dtype=F32).reshape(1, N_KV_HEADS, 1, N_KV_HEADS, 1)
    q3 = (q * place).reshape(n, N_HEADS, KV_W)
    kn = qkv[:, D_MODEL:D_MODEL + KV_W].reshape(n, 1, KV_W)
    vn = qkv[:, D_MODEL + KV_W:].reshape(n, 1, KV_W)
    o3, kw, vw = _attn_sample(q3, cache_k.reshape(n, WINDOW, KV_W), cache_v.reshape(n, WINDOW, KV_W),
                              kn, vn, bias_s, sinks.reshape(N_HEADS, 1))
    o5 = o3.reshape(n, N_KV_HEADS, GQ, N_KV_HEADS, HEAD_DIM)
    o = jnp.stack([o5[:, gi, :, gi, :] for gi in range(N_KV_HEADS)], axis=1).reshape(n, D_MODEL)
    return _linear_residual(x, o, wo), kw, vw


def _ssm_prep_kernel(ar_ref, ai_ref, ldt_ref, br_ref, bi_ref, c_ref, a_out, bc_out, wb_out, wc_out, wc2_out):
    ar, ai = ar_ref[...], ai_ref[...]
    dt = jnp.exp(ldt_ref[...])
    mag = jnp.exp(ar * dt)
    lr = mag * jnp.cos(ai * dt)
    li = mag * jnp.sin(ai * dt)
    den = ar * ar + ai * ai
    cr = ((lr - 1.0) * ar + li * ai) / den
    ci = (li * ar - (lr - 1.0) * ai) / den
    row = lax.broadcasted_iota(jnp.int32, (SUBLANES, N_STATES), 0)
    a_out[...] = jnp.broadcast_to(lr, (SUBLANES, N_STATES))
    bc_out[...] = jnp.where(row < SUBLANES // 2, -li, li)
    for j in range(N_SLABS):
        crj = cr[:, SLAB_STATES * j:SLAB_STATES * (j + 1)]
        cij = ci[:, SLAB_STATES * j:SLAB_STATES * (j + 1)]
        br, bi = br_ref[j], bi_ref[j]
        w_re = (crj * br - cij * bi).astype(BF16)
        w_im = (crj * bi + cij * br).astype(BF16)
        for c in range(SLAB_STATES // LANES):
            wb_out[j, :, 2 * LANES * c:2 * LANES * c + LANES] = w_re[:, LANES * c:LANES * (c + 1)]
            wb_out[j, :, 2 * LANES * c + LANES:2 * LANES * (c + 1)] = w_im[:, LANES * c:LANES * (c + 1)]
        wc_out[j, :SLAB_STATES, :] = c_ref[j, :SLAB_STATES, :].astype(BF16)
        wc_out[j, SLAB_STATES:, :] = (-c_ref[j, SLAB_STATES:, :]).astype(BF16)
        wc2_out[j, :, :LANES] = c_ref[j, :SLAB_STATES, :].astype(BF16)
        wc2_out[j, :, LANES:] = (-c_ref[j, SLAB_STATES:, :]).astype(BF16)


def _ssm_prep(a_re, a_im, log_dt, b_re, b_im, c_re, c_im):
    eye = jnp.eye(SUBLANES, dtype=F32)

    def blk_b(b):
        t = b.reshape(N_SLABS, 8, SSM_STATE, SSM_GROUP).transpose(0, 1, 3, 2)
        return jnp.einsum('jghp,gk->jghkp', t, eye).reshape(N_SLABS, LANES, SLAB_STATES)

    def blk_c(c):
        t = c.reshape(N_SLABS, 8, SSM_GROUP, SSM_STATE).transpose(0, 1, 3, 2)
        return jnp.einsum('jgph,gk->jgpkh', t, eye).reshape(N_SLABS, SLAB_STATES, LANES)

    row = lambda a: a.reshape(1, N_STATES)
    ldt = jnp.broadcast_to(log_dt[:, None], (N_SSM_GROUPS, SSM_STATE))
    c_all = jnp.concatenate([blk_c(c_re), blk_c(c_im)], axis=1)
    return pl.pallas_call(
        _ssm_prep_kernel,
        out_shape=[jax.ShapeDtypeStruct((SUBLANES, N_STATES), F32),
                   jax.ShapeDtypeStruct((SUBLANES, N_STATES), F32),
                   jax.ShapeDtypeStruct((N_SLABS, LANES, 2 * SLAB_STATES), BF16),
                   jax.ShapeDtypeStruct((N_SLABS, 2 * SLAB_STATES, LANES), BF16),
                   jax.ShapeDtypeStruct((N_SLABS, SLAB_STATES, 2 * LANES), BF16)],
        compiler_params=_cparams(None),
        name="ssm_prep",
    )(row(a_re), row(a_im), row(ldt), blk_b(b_re), blk_b(b_im), c_all)


def _glu_residual(x, y, u, d, wa, wg):
    z = _gelu_tanh(y + d * u).astype(BF16)
    out = jnp.dot(z, wa, preferred_element_type=F32) * _sigmoid(jnp.dot(z, wg, preferred_element_type=F32))
    return x + out


def _ssm_prompt_kernel(x_ref, g_ref, wb_ref, wc_ref, a_ref, bc_ref, d_ref, wa_ref, wg_ref,
                       xo_ref, st_ref, z_scr, u_scr, o_scr, xs_scr, *, nb, lc):
    i = pl.program_id(0)
    blocks_per_slab = SLAB_STATES // LANES
    half = SUBLANES // 2
    pairs = lc // 2
    rows = half * lc
    roll_half = lambda v: pltpu.roll(v, half, 1)

    @pl.when(i == 0)
    def _():
        xs_scr[...] = jnp.zeros((SUBLANES, N_STATES), F32)
        if nb < half:
            u_scr[...] = jnp.zeros(u_scr.shape, F32)

    for b in range(nb):
        u = _rms(x_ref[b], g_ref[...])
        for c in range(N_SLABS):
            u_scr[c, pl.ds(b, lc, stride=half), :] = u[:, LANES * c:LANES * (c + 1)]

    lower = lax.broadcasted_iota(jnp.int32, (pairs, SUBLANES, LANES), 1) < half
    ys = [None] * N_SLABS

    def project_in(j):
        ub = u_scr[j].astype(BF16)
        for c in range(blocks_per_slab):
            res = jnp.dot(ub, wb_ref[j, :, 2 * LANES * c:2 * LANES * (c + 1)], preferred_element_type=F32)
            re = res[:, :LANES].reshape(pairs, SUBLANES, LANES)
            im = res[:, LANES:].reshape(pairs, SUBLANES, LANES)
            blk = blocks_per_slab * j + c
            z_scr[blk, :, 0] = jnp.where(lower, re, roll_half(im))
            z_scr[blk, :, 1] = jnp.where(lower, roll_half(re), im)

    def scan(blks):
        a = [a_ref[:, LANES * q:LANES * (q + 1)] for q in blks]
        bc = [bc_ref[:, LANES * q:LANES * (q + 1)] for q in blks]
        xs = [xs_scr[:, LANES * q:LANES * (q + 1)] for q in blks]
        for tp in range(pairs):
            for par in range(2):
                for k, q in enumerate(blks):
                    xs[k] = a[k] * xs[k] + bc[k] * pltpu.roll(xs[k], half, 0) + z_scr[q, tp, par]
                    z_scr[q, tp, par] = xs[k]
        for k, q in enumerate(blks):
            xs_scr[:, LANES * q:LANES * (q + 1)] = xs[k]

    def project_out(j):
        s = jnp.concatenate([z_scr[blocks_per_slab * j + c].reshape(SUBLANES * lc, LANES)
                             for c in range(blocks_per_slab)], axis=1).astype(BF16)
        p3 = jnp.dot(s, wc_ref[j], preferred_element_type=F32).reshape(lc, SUBLANES, 2 * LANES)
        y4 = (p3[:, :, :LANES] + roll_half(p3[:, :, LANES:])).reshape(pairs, 2, SUBLANES, LANES)
        ys[j] = jnp.where(lower, y4[:, 0], roll_half(y4[:, 1])).reshape(rows, LANES)

    slabs_per_group = 2
    n_groups = N_SLABS // slabs_per_group
    group_slabs = lambda gi: range(slabs_per_group * gi, slabs_per_group * (gi + 1))
    for gi in range(n_groups + 2):
        if gi < n_groups:
            for j in group_slabs(gi):
                project_in(j)
        if 1 <= gi <= n_groups:
            lo = slabs_per_group * blocks_per_slab * (gi - 1)
            scan(list(range(lo, lo + slabs_per_group * blocks_per_slab)))
        if gi >= 2:
            for j in group_slabs(gi - 2):
                project_out(j)
    st_ref[...] = xs_scr[...]
    y = jnp.concatenate(ys, axis=1)
    u_all = jnp.concatenate([u_scr[c] for c in range(N_SLABS)], axis=1)
    z = _gelu_tanh(y + d_ref[...] * u_all).astype(BF16)
    out = (jnp.dot(z, wa_ref[...], preferred_element_type=F32)
           * _sigmoid(jnp.dot(z, wg_ref[...], preferred_element_type=F32)))
    for c in range(N_SLABS):
        o_scr[c] = out[:, LANES * c:LANES * (c + 1)]
    for b in range(nb):
        xo_ref[b] = x_ref[b] + jnp.concatenate(
            [o_scr[c, pl.ds(b, lc, stride=half), :] for c in range(N_SLABS)], axis=1)


def _ssm_prompt_layer(x, g, prep, d, wa, wg):
    a_rows, bc_rows, wb, _, wc2 = prep
    nb, t, dm = x.shape
    half = SUBLANES // 2
    assert nb <= half
    lc = min(128, t)
    kern = functools.partial(_ssm_prompt_kernel, nb=nb, lc=lc)
    return pl.pallas_call(
        kern,
        grid=(t // lc,),
        in_specs=[
            pl.BlockSpec((nb, lc, dm), lambda i: (0, i, 0)),
            _full((1, dm)),
            _full(wb.shape), _full(wc2.shape),
            _full((SUBLANES, N_STATES)), _full((SUBLANES, N_STATES)),
            _full((1, dm)), _full((dm, dm)), _full((dm, dm)),
        ],
        out_specs=[pl.BlockSpec((nb, lc, dm), lambda i: (0, i, 0)), _full((SUBLANES, N_STATES))],
        out_shape=[jax.ShapeDtypeStruct((nb, t, dm), F32), jax.ShapeDtypeStruct((SUBLANES, N_STATES), F32)],
        scratch_shapes=[
            pltpu.VMEM((N_STATES // LANES, lc // 2, 2, SUBLANES, LANES), F32),
            pltpu.VMEM((N_SLABS, half * lc, LANES), F32),
            pltpu.VMEM((N_SLABS, half * lc, LANES), F32),
            pltpu.VMEM((SUBLANES, N_STATES), F32),
        ],
        compiler_params=_cparams(("arbitrary",)),
        name="ssm_prompt",
    )(x, g, wb, wc2, a_rows, bc_rows, d, wa, wg)


def _ssm_sample_kernel(x_ref, g_ref, wb_ref, wc_ref, a_ref, bc_ref, h0r_ref, h0i_ref, d_ref, wa_ref, wg_ref,
                       xo_ref, sr_ref, si_ref):
    x = x_ref[...]
    u = _rms(x, g_ref[...])
    ub = u.astype(BF16)
    ys = []
    for j in range(N_SLABS):
        sl = slice(SLAB_STATES * j, SLAB_STATES * (j + 1))
        res = jnp.dot(ub[:, LANES * j:LANES * (j + 1)], wb_ref[j], preferred_element_type=F32)
        lr = a_ref[0:1, sl]
        li = bc_ref[SUBLANES - 1:SUBLANES, sl]
        h0r, h0i = h0r_ref[:, sl], h0i_ref[:, sl]
        blocks = range(SLAB_STATES // LANES)
        res_re = jnp.concatenate([res[:, 2 * LANES * c:2 * LANES * c + LANES] for c in blocks], axis=1)
        res_im = jnp.concatenate([res[:, 2 * LANES * c + LANES:2 * LANES * (c + 1)] for c in blocks], axis=1)
        xr = res_re + (lr * h0r - li * h0i)
        xi = res_im + (lr * h0i + li * h0r)
        sr_ref[:, sl] = xr
        si_ref[:, sl] = xi
        s = jnp.concatenate([xr, xi], axis=1).astype(BF16)
        ys.append(jnp.dot(s, wc_ref[j], preferred_element_type=F32))
    y = jnp.concatenate(ys, axis=1)
    xo_ref[...] = _glu_residual(x, y, u, d_ref[...], wa_ref[...], wg_ref[...])


def _ssm_sample_layer(x, g, prep, h0r, h0i, d, wa, wg):
    a_rows, bc_rows, wb, wc, _ = prep
    n = x.shape[0]
    return pl.pallas_call(
        _ssm_sample_kernel,
        out_shape=[jax.ShapeDtypeStruct(x.shape, F32),
                   jax.ShapeDtypeStruct((n, N_STATES), F32),
                   jax.ShapeDtypeStruct((n, N_STATES), F32)],
        compiler_params=_cparams(None),
        name="ssm_sample",
    )(x, g, wb, wc, a_rows, bc_rows, h0r, h0i, d, wa, wg)


ROUTER_W = LANES
EXPERT_LANE0 = N_EXPERT_GROUPS


def _route_topk(h, wr, br):
    h_hi = h.astype(BF16)
    h_lo = (h - h_hi.astype(F32)).astype(BF16)
    w_hi = wr.astype(BF16)
    w_lo = (wr - w_hi.astype(F32)).astype(BF16)
    dot = lambda a, b: jnp.dot(a, b, preferred_element_type=F32)
    both = dot(h_hi, jnp.concatenate([w_hi, w_lo], axis=1))
    logits = both[:, :ROUTER_W] + (dot(h_lo, w_hi) + both[:, ROUTER_W:]) + br
    lane_i = lax.broadcasted_iota(jnp.int32, logits.shape, 1)
    lane = lane_i.astype(F32)
    neg = -jnp.inf
    past_end = float(ROUTER_W)
    gl = jnp.where(lane_i < N_EXPERT_GROUPS, logits, neg)
    gmax = jnp.max(gl, axis=-1, keepdims=True)
    g_sel = jnp.min(jnp.where(gl == gmax, lane, past_end), axis=-1, keepdims=True)
    p_grp = 1.0 / jnp.sum(jnp.exp(gl - gmax), axis=-1, keepdims=True)
    lane_grp = ((lane_i - EXPERT_LANE0) // EXPERTS_PER_GROUP).astype(F32)
    in_grp = (lane_i >= EXPERT_LANE0) & (lane_i < EXPERT_LANE0 + N_EXPERTS) & (lane_grp == g_sel)
    el = jnp.where(in_grp, logits, neg)
    v1 = jnp.max(el, axis=-1, keepdims=True)
    i1 = jnp.min(jnp.where(el == v1, lane, past_end), axis=-1, keepdims=True)
    el2 = jnp.where(lane == i1, neg, el)
    v2 = jnp.max(el2, axis=-1, keepdims=True)
    i2 = jnp.min(jnp.where(el2 == v2, lane, past_end), axis=-1, keepdims=True)
    e2 = jnp.exp(v2 - v1)
    w1 = p_grp / (1.0 + e2)
    w2 = p_grp * e2 / (1.0 + e2)
    return lane, i1, i2, w1, w2


def _route(h, wr, br):
    lane, i1, i2, w1, w2 = _route_topk(h, wr, br)
    return jnp.where(lane == i1, w1, 0.0) + jnp.where(lane == i2, w2, 0.0)


def _moe_kernel(x_ref, p_ref, gffn_ref, wr_ref, br_ref, wg_ref, wu_ref, wd_ref,
                gple_ref, wpg_ref, wpp_ref, gfin_ref, o_ref, hn_scr, gates_scr, acc_scr, *, final):
    e = pl.program_id(1)

    @pl.when(e == 0)
    def _():
        x = x_ref[...]
        h = _rms(x, gffn_ref[...])
        hn_scr[...] = h.astype(BF16)
        gates_scr[...] = _route(h, wr_ref[...], br_ref[...])
        acc_scr[...] = x

    hb = hn_scr[...]
    gt = jnp.dot(hb, wg_ref[0].astype(BF16), preferred_element_type=F32)
    up = jnp.dot(hb, wu_ref[0].astype(BF16), preferred_element_type=F32)
    a = (gt * _sigmoid(gt) * up).astype(BF16)
    gates = gates_scr[...]
    lane = lax.broadcasted_iota(jnp.int32, gates.shape, 1)
    ge = jnp.sum(jnp.where(lane == e + EXPERT_LANE0, gates, 0.0), axis=-1, keepdims=True)
    acc_scr[...] += ge * jnp.dot(a, wd_ref[0].astype(BF16), preferred_element_type=F32)

    @pl.when(e == pl.num_programs(1) - 1)
    def _():
        x1 = acc_scr[...]
        gate = _sigmoid(jnp.dot(_rms(x1, gple_ref[...]).astype(BF16), wpg_ref[...], preferred_element_type=F32))
        x2 = x1 + gate * jnp.dot(p_ref[...].astype(BF16), wpp_ref[...], preferred_element_type=F32)
        o_ref[...] = _rms(x2, gfin_ref[...]) if final else x2


def _moe_ple_layer(li, x, p, gffn, wr, br, wg, wu, wd, gple, wpg, wpp, gfin, final):
    n, d = x.shape
    tm = min(1024, n)
    kern = functools.partial(_moe_kernel, final=final)
    tok = lambda w: pl.BlockSpec((tm, w), lambda t, e: (t, 0))
    const = lambda s: pl.BlockSpec(s, lambda t, e: (0,) * len(s))
    return pl.pallas_call(
        kern,
        grid=(n // tm, N_EXPERTS),
        in_specs=[
            tok(d), pl.BlockSpec((None, tm, PLE_DIM), lambda t, e: (li, t, 0)),
            const((1, d)), const((d, ROUTER_W)), const((1, ROUTER_W)),
            pl.BlockSpec((None, 1, d, D_EXPERT), lambda t, e: (li, e, 0, 0)),
            pl.BlockSpec((None, 1, d, D_EXPERT), lambda t, e: (li, e, 0, 0)),
            pl.BlockSpec((None, 1, D_EXPERT, d), lambda t, e: (li, e, 0, 0)),
            const((1, d)), const((d, d)), const((PLE_DIM, d)), const((1, d)),
        ],
        out_specs=tok(d),
        out_shape=jax.ShapeDtypeStruct((n, d), F32),
        scratch_shapes=[pltpu.VMEM((tm, d), BF16), pltpu.VMEM((tm, ROUTER_W), F32), pltpu.VMEM((tm, d), F32)],
        compiler_params=_cparams(("arbitrary", "arbitrary")),
        name="moe_ple",
    )(x, p, gffn, wr, br, wg, wu, wd, gple, wpg, wpp, gfin)


TOKEN_DATA_ROWS = D_MODEL // LANES
TOKEN_PITCH = TOKEN_DATA_ROWS + 1
INFO_W1, INFO_W2, INFO_E1, INFO_E2, INFO_R1, INFO_R2 = range(6)
EXPERT_TILE = 256
ZERO_RUN = 256


def _to_token_tiles(ref_2d, val, n, first=0):
    for s in range(TOKEN_DATA_ROWS):
        ref_2d[pl.ds(first * TOKEN_PITCH + s, n, stride=TOKEN_PITCH), :] = val[:, LANES * s:LANES * (s + 1)]


def _zero_spare_rows(ref_2d, n):
    ref_2d[pl.ds(TOKEN_DATA_ROWS, n, stride=TOKEN_PITCH), :] = jnp.zeros((n, LANES), F32)


def _from_token_tiles(ref_2d, n, first=0):
    return jnp.concatenate([ref_2d[pl.ds(first * TOKEN_PITCH + s, n, stride=TOKEN_PITCH), :]
                            for s in range(TOKEN_DATA_ROWS)], axis=1)


def _token_slots(i, n=1):
    return pl.ds(i * TOKEN_PITCH, n * TOKEN_PITCH)


def _slot_at(first_row):
    return pl.ds(first_row, TOKEN_PITCH)


def _route_kernel(x_ref, g_ref, wr_ref, br_ref, before_ref, info_ref, cnt_ref, base_scr):
    i = pl.program_id(0)

    @pl.when(i == 0)
    def _():
        base_scr[...] = jnp.zeros(base_scr.shape, F32)

    h = _rms(x_ref[...], g_ref[...])
    lane, i1, i2, w1, w2 = _route_topk(h, wr_ref[...], br_ref[...])
    chosen = jnp.where((lane == i1) | (lane == i2), 1.0, 0.0)
    prefix = jnp.dot(before_ref[...], chosen.astype(BF16), preferred_element_type=F32) + base_scr[...]
    rank1 = jnp.sum(jnp.where(lane == i1, prefix, 0.0), axis=-1, keepdims=True)
    rank2 = jnp.sum(jnp.where(lane == i2, prefix, 0.0), axis=-1, keepdims=True)
    base_scr[...] += jnp.sum(chosen, axis=0, keepdims=True)
    cnt_ref[...] = base_scr[...]
    e1 = (i1 - EXPERT_LANE0).astype(F32)
    e2 = (i2 - EXPERT_LANE0).astype(F32)
    info = jnp.zeros(h.shape[:1] + (ROUTER_W,), F32)
    for ln, v in ((INFO_W1, w1), (INFO_W2, w2), (INFO_E1, e1), (INFO_E2, e2), (INFO_R1, rank1), (INFO_R2, rank2)):
        info = jnp.where(lane == ln, v, info)
    info_ref[...] = info


def _route_call(x, gffn, wr, br):
    n, d = x.shape
    tm = 512
    before = jnp.asarray(np.tril(np.ones((tm, tm), np.float32), -1), BF16)
    return pl.pallas_call(
        _route_kernel,
        grid=(n // tm,),
        in_specs=[pl.BlockSpec((tm, d), lambda i: (i, 0)), _full((1, d)), _full((d, ROUTER_W)), _full((1, ROUTER_W)),
                  _full((tm, tm))],
        out_specs=[pl.BlockSpec((tm, ROUTER_W), lambda i: (i, 0)), _full((1, ROUTER_W))],
        out_shape=[jax.ShapeDtypeStruct((n, ROUTER_W), F32), jax.ShapeDtypeStruct((1, ROUTER_W), F32)],
        scratch_shapes=[pltpu.VMEM((1, ROUTER_W), F32)],
        compiler_params=_cparams(("arbitrary",)),
        name="moe_route",
    )(x, gffn, wr, br, before)


def _dispatch_kernel(pos1_ref, pos2_ref, cnt_ref, start_ref, x_ref, g_ref, xs_hbm, hbuf, zbuf, sem, zsem,
                     *, td, n_steps, n_pad_rows):
    i = pl.program_id(0)
    slot = i % 2

    def wait_slot(sl):
        for _ in range(2):
            pltpu.make_async_copy(hbuf.at[sl], hbuf.at[sl], sem.at[sl]).wait()

    @pl.when(i >= 2)
    def _():
        wait_slot(slot)

    @pl.when(i < 2)
    def _():
        _zero_spare_rows(hbuf.at[slot], td)

    _to_token_tiles(hbuf.at[slot], _rms(x_ref[...], g_ref[...]), td)

    def issue(n8, carry):
        for u in range(SUBLANES):
            n = n8 * SUBLANES + u
            t = i * td + n
            src = hbuf.at[slot, _token_slots(n), :]
            pltpu.make_async_copy(src, xs_hbm.at[_slot_at(pos1_ref[t]), :], sem.at[slot]).start(priority=0)
            pltpu.make_async_copy(src, xs_hbm.at[_slot_at(pos2_ref[t]), :], sem.at[slot]).start(priority=1)
        return carry

    lax.fori_loop(0, td // SUBLANES, issue, 0)

    @pl.when(i == 0)
    def _():
        zbuf[...] = jnp.zeros(zbuf.shape, zbuf.dtype)

        def zero_run(first, n_slots):
            pltpu.make_async_copy(zbuf.at[_token_slots(0, n_slots), :], xs_hbm.at[_token_slots(first, n_slots), :],
                                  zsem).start()

        for e in range(N_EXPERTS + 1):
            lo = start_ref[e] + cnt_ref[e]
            n = start_ref[e + 1] - lo

            def full_run(k, carry, lo=lo):
                zero_run(lo + k * ZERO_RUN, ZERO_RUN)
                return carry

            lax.fori_loop(0, n // ZERO_RUN, full_run, 0)
            cur = lo + (n // ZERO_RUN) * ZERO_RUN
            p = ZERO_RUN // 2
            while p >= 1:
                @pl.when((n & p) != 0)
                def _(cur=cur, p=p):
                    zero_run(cur, p)

                cur = cur + (n & p)
                p //= 2

    @pl.when(i == n_steps - 1)
    def _():
        wait_slot(slot)
        if n_steps > 1:
            wait_slot(1 - slot)
        pad = xs_hbm.at[_token_slots(0, n_pad_rows), :]
        pltpu.make_async_copy(pad, pad, zsem).wait()


def _dispatch_call(pos1, pos2, cnt, start, x, gffn, a_pad, n_pad_rows):
    n, d = x.shape
    td = 512
    n_steps = n // td
    kern = functools.partial(_dispatch_kernel, td=td, n_steps=n_steps, n_pad_rows=n_pad_rows)
    return pl.pallas_call(
        kern,
        grid_spec=pltpu.PrefetchScalarGridSpec(
            num_scalar_prefetch=4,
            grid=(n_steps,),
            in_specs=[pl.BlockSpec((td, d), lambda i, *_: (i, 0)), pl.BlockSpec((1, d), lambda i, *_: (0, 0))],
            out_specs=pl.BlockSpec(memory_space=pl.ANY),
            scratch_shapes=[pltpu.VMEM((2, td * TOKEN_PITCH, LANES), F32),
                            pltpu.VMEM((ZERO_RUN * TOKEN_PITCH, LANES), F32),
                            pltpu.SemaphoreType.DMA((2,)), pltpu.SemaphoreType.DMA(())],
        ),
        out_shape=jax.ShapeDtypeStruct((a_pad * TOKEN_PITCH, LANES), F32),
        compiler_params=_cparams(("arbitrary",)),
        name="moe_dispatch",
    )(pos1, pos2, cnt, start, x, gffn)


EXPERT_IN_SLOTS = 3


TILES_PER_STEP = 2


def _experts_kernel(te_ref, xs_hbm, *refs, te, n_steps):
    nu = TILES_PER_STEP
    w_refs = [refs[3 * u:3 * u + 3] for u in range(nu)]
    y_ref, xbuf, xsem = refs[3 * nu:3 * nu + 3]
    w_scrs = [refs[3 * nu + 3 + 3 * u:3 * nu + 6 + 3 * u] for u in range(nu)]
    t = pl.program_id(0)
    rows = nu * te * TOKEN_PITCH

    def fetch(step):
        slot = step % EXPERT_IN_SLOTS
        src = xs_hbm.at[pl.ds(pl.multiple_of(step * rows, rows), rows), :]
        return pltpu.make_async_copy(src, xbuf.at[slot], xsem.at[slot])

    @pl.when(t == 0)
    def _():
        for ahead in range(min(EXPERT_IN_SLOTS - 1, n_steps)):
            fetch(ahead).start()

    @pl.when(t + EXPERT_IN_SLOTS - 1 < n_steps)
    def _():
        fetch(t + EXPERT_IN_SLOTS - 1).start()

    for u in range(nu):
        tile = nu * t + u

        @pl.when((t == 0) | (te_ref[tile] != te_ref[jnp.maximum(tile - nu, 0)]))
        def _(u=u):
            for scr, ref in zip(w_scrs[u], w_refs[u]):
                scr[...] = ref[0].astype(BF16)

    fetch(t).wait()
    for u in range(nu):
        wg_scr, wu_scr, wd_scr = w_scrs[u]
        x = _from_token_tiles(xbuf.at[t % EXPERT_IN_SLOTS], te, u * te).astype(BF16)
        gt = jnp.dot(x, wg_scr[...], preferred_element_type=F32)
        up = jnp.dot(x, wu_scr[...], preferred_element_type=F32)
        a = (gt * _sigmoid(gt) * up).astype(BF16)
        _to_token_tiles(y_ref, jnp.dot(a, wd_scr[...], preferred_element_type=F32), te, u * te)
    _zero_spare_rows(y_ref, nu * te)


def _experts_call(li, tile_expert, xs, wg, wu, wd, te):
    nu = TILES_PER_STEP
    rows = nu * te * TOKEN_PITCH
    assert xs.shape[0] % rows == 0
    n_steps = xs.shape[0] // rows
    d = wg.shape[2]
    kern = functools.partial(_experts_kernel, te=te, n_steps=n_steps)

    def weight_specs(u):
        pick = lambda t, ex: (li, ex[nu * t + u], 0, 0)
        return [pl.BlockSpec((None, 1, d, D_EXPERT), pick), pl.BlockSpec((None, 1, d, D_EXPERT), pick),
                pl.BlockSpec((None, 1, D_EXPERT, d), pick)]

    w_scratch = [pltpu.VMEM((d, D_EXPERT), BF16), pltpu.VMEM((d, D_EXPERT), BF16), pltpu.VMEM((D_EXPERT, d), BF16)]
    return pl.pallas_call(
        kern,
        grid_spec=pltpu.PrefetchScalarGridSpec(
            num_scalar_prefetch=1,
            grid=(n_steps,),
            in_specs=[pl.BlockSpec(memory_space=pl.ANY)] + [s for u in range(nu) for s in weight_specs(u)],
            out_specs=pl.BlockSpec((rows, LANES), lambda t, ex: (t, 0)),
            scratch_shapes=[pltpu.VMEM((EXPERT_IN_SLOTS, rows, LANES), F32),
                            pltpu.SemaphoreType.DMA((EXPERT_IN_SLOTS,))] + w_scratch * nu,
        ),
        out_shape=jax.ShapeDtypeStruct((n_steps * rows, LANES), F32),
        compiler_params=_cparams(("arbitrary",)),
        name="moe_experts",
    )(tile_expert, xs, *([wg, wu, wd] * nu))


def _combine_kernel(pos1_ref, pos2_ref, x_ref, info_ref, p_ref, gple_ref, wpg_ref, wpp_ref, gfin_ref, ys_hbm,
                    o_ref, ybuf, sem, *, tm, n_steps, final):
    i = pl.program_id(0)
    slot = i % 2

    def issue(step, sl):
        def body(n8, carry):
            for u in range(SUBLANES):
                n = n8 * SUBLANES + u
                t = step * tm + n
                for k, pos_ref in enumerate((pos1_ref, pos2_ref)):
                    pltpu.make_async_copy(ys_hbm.at[_slot_at(pos_ref[t]), :],
                                          ybuf.at[sl, k, _token_slots(n), :], sem.at[sl]).start(priority=k)
            return carry

        lax.fori_loop(0, tm // SUBLANES, body, 0)

    def wait_rows(sl):
        for k in range(2):
            pltpu.make_async_copy(ybuf.at[sl, k], ybuf.at[sl, k], sem.at[sl]).wait()

    @pl.when(i == 0)
    def _():
        issue(0, 0)

    @pl.when(i + 1 < n_steps)
    def _():
        issue(i + 1, 1 - slot)

    wait_rows(slot)
    part = tm // 2
    for first in (0, part):
        tok = slice(first, first + part)
        y1 = _from_token_tiles(ybuf.at[slot, 0], part, first)
        y2 = _from_token_tiles(ybuf.at[slot, 1], part, first)
        info = info_ref[tok, :]
        x1 = x_ref[tok, :] + (info[:, INFO_W1:INFO_W1 + 1] * y1 + info[:, INFO_W2:INFO_W2 + 1] * y2)
        gate = _sigmoid(jnp.dot(_rms(x1, gple_ref[...]).astype(BF16), wpg_ref[...], preferred_element_type=F32))
        x2 = x1 + gate * jnp.dot(p_ref[tok, :].astype(BF16), wpp_ref[...], preferred_element_type=F32)
        o_ref[tok, :] = _rms(x2, gfin_ref[...]) if final else x2


def _combine_call(li, pos1, pos2, x, info, p, gple, wpg, wpp, gfin, ys, final):
    n, d = x.shape
    tm = 512
    n_steps = n // tm
    kern = functools.partial(_combine_kernel, tm=tm, n_steps=n_steps, final=final)
    tok = lambda w: pl.BlockSpec((tm, w), lambda i, *_: (i, 0))
    const = lambda s: pl.BlockSpec(s, lambda i, *_: (0,) * len(s))
    return pl.pallas_call(
        kern,
        grid_spec=pltpu.PrefetchScalarGridSpec(
            num_scalar_prefetch=2,
            grid=(n_steps,),
            in_specs=[tok(d), tok(ROUTER_W), pl.BlockSpec((None, tm, PLE_DIM), lambda i, *_: (li, i, 0)),
                      const((1, d)), const((d, d)), const((PLE_DIM, d)),
                      const((1, d)), pl.BlockSpec(memory_space=pl.ANY)],
            out_specs=tok(d),
            scratch_shapes=[pltpu.VMEM((2, 2, tm * TOKEN_PITCH, LANES), F32), pltpu.SemaphoreType.DMA((2,))],
        ),
        out_shape=jax.ShapeDtypeStruct((n, d), F32),
        compiler_params=_cparams(("arbitrary",)),
        name="moe_combine",
    )(pos1, pos2, x, info, p, gple, wpg, wpp, gfin, ys)


def _positions_kernel(info_ref, start_ref, pos_ref):
    info = info_ref[...]
    lane = lax.broadcasted_iota(jnp.int32, info.shape, 1)
    lane_expert = (lane - EXPERT_LANE0).astype(F32)
    start = start_ref[...]
    cols = []
    for e_lane, r_lane in ((INFO_E1, INFO_R1), (INFO_E2, INFO_R2)):
        seg = jnp.sum(jnp.where(lane_expert == info[:, e_lane:e_lane + 1], start, 0.0), axis=-1, keepdims=True)
        cols.append((seg + info[:, r_lane:r_lane + 1]) * float(TOKEN_PITCH))
    both = jnp.where(lane == 0, cols[0], jnp.where(lane == 1, cols[1], 0.0))
    pos_ref[...] = both.T[:SUBLANES, :].astype(jnp.int32)


def _positions_call(info, start_row):
    n = info.shape[0]
    tm = 2048 if n % 2048 == 0 else 512
    return pl.pallas_call(
        _positions_kernel,
        grid=(n // tm,),
        in_specs=[pl.BlockSpec((tm, ROUTER_W), lambda i: (i, 0)), _full((1, ROUTER_W))],
        out_specs=pl.BlockSpec((SUBLANES, tm), lambda i: (0, i)),
        out_shape=jax.ShapeDtypeStruct((SUBLANES, n), jnp.int32),
        compiler_params=_cparams(("arbitrary",)),
        name="moe_positions",
    )(info, start_row)


def _moe_routed_layer(li, x, p, gffn, wr, br, wg, wu, wd, gple, wpg, wpp, gfin, final):
    n = x.shape[0]
    te = EXPERT_TILE
    n_assign = 2 * n
    assert n_assign % te == 0
    a_pad = n_assign + N_EXPERTS * te
    info, cnt = _route_call(x, gffn, wr, br)
    counts = cnt[0, EXPERT_LANE0:EXPERT_LANE0 + N_EXPERTS].astype(jnp.int32)
    padded = (counts + te - 1) // te * te
    seg_end = jnp.cumsum(padded)
    seg_start = seg_end - padded
    start_row = jnp.zeros((1, ROUTER_W), F32).at[0, EXPERT_LANE0:EXPERT_LANE0 + N_EXPERTS].set(seg_start.astype(F32))
    pos = _positions_call(info, start_row)
    pos1, pos2 = pos[0], pos[1]
    start = jnp.concatenate([seg_start, seg_end[-1:], jnp.full((1,), a_pad, jnp.int32)])
    cnt_pad = jnp.concatenate([counts, jnp.zeros((1,), jnp.int32)])
    tile_row0 = jnp.arange(a_pad // te, dtype=jnp.int32) * te
    tile_expert = jnp.minimum(jnp.sum(seg_end[None, :] <= tile_row0[:, None], axis=1), N_EXPERTS - 1).astype(jnp.int32)
    xs = _dispatch_call(pos1, pos2, cnt_pad, start, x, gffn, a_pad, N_EXPERTS * te)
    ys = _experts_call(li, tile_expert, xs, wg, wu, wd, te)
    return _combine_call(li, pos1, pos2, x, info, p, gple, wpg, wpp, gfin, ys, final)


def kernel(x_prompt, x_sample, cache_k_win, cache_v_win, state_ssm_re, state_ssm_im, p_prompt, p_sample,
           rel_bias_table, g_mix, w_qkv, w_o, sinks, ssm_a_re, ssm_a_im, ssm_log_dt, ssm_b_re, ssm_b_im,
           ssm_c_re, ssm_c_im, ssm_d, w_glu_a, w_glu_b, g_ffn, w_router_group, b_router_group,
           w_router_expert, b_router_expert, w_exp_gate, w_exp_up, w_exp_down, g_ple, w_ple_gate,
           w_ple_proj, g_final):
    bsz, seq, d = x_prompt.shape
    ns = x_sample.shape[0]
    depth = g_mix.shape[0]
    row = lambda v: v.reshape(1, -1).astype(F32)

    qi = np.arange(WINDOW)[:, None]
    sj = np.arange(2 * WINDOW)[None, :]
    dist = qi + WINDOW - sj
    idx_p = _t5_bucket_np(dist)
    ok_p = ((dist >= 0) & (dist < WINDOW)).astype(np.int32)
    idx_s = np.broadcast_to(_t5_bucket_np(WINDOW - 1 - np.arange(WINDOW))[None, :], (SUBLANES, WINDOW))
    bias_p = _band_bias(rel_bias_table, idx_p, ok_p)
    bias_s = _bias_from_table(rel_bias_table, idx_s)[:, 0, :]

    wqkv_b = w_qkv[0].astype(BF16)
    wo_b = w_o[0].astype(BF16)
    wpg_b, wpp_b = w_ple_gate.astype(BF16), w_ple_proj.astype(BF16)
    wa_b, wgl_b = w_glu_a[0].astype(BF16), w_glu_b[0].astype(BF16)
    pad = ROUTER_W - N_EXPERT_GROUPS - N_EXPERTS
    wr = jnp.concatenate([w_router_group, w_router_expert.reshape(depth, d, N_EXPERTS),
                          jnp.zeros((depth, d, pad), F32)], axis=-1)
    br = jnp.concatenate([b_router_group, b_router_expert.reshape(depth, N_EXPERTS),
                          jnp.zeros((depth, pad), F32)], axis=-1)

    prep = _ssm_prep(ssm_a_re[0], ssm_a_im[0], ssm_log_dt[0], ssm_b_re[0], ssm_b_im[0], ssm_c_re[0], ssm_c_im[0])
    d_row = row(ssm_d[0])

    def moe(x2d, p_all, i):
        layer = _moe_routed_layer if 2 * x2d.shape[0] >= N_EXPERTS * EXPERT_TILE else _moe_ple_layer
        return layer(i, x2d, p_all, row(g_ffn[i]), wr[i], br[i:i + 1], w_exp_gate, w_exp_up, w_exp_down,
                     row(g_ple[i]), wpg_b[i], wpp_b[i], row(g_final), final=(i == depth - 1))

    pp = p_prompt.reshape(depth, bsz * seq, PLE_DIM)
    xp, kp, vp = _attn_prompt_layer(x_prompt, row(g_mix[0]), wqkv_b, wo_b, bias_p, sinks[0])
    xp = moe(xp.reshape(bsz * seq, d), pp, 0).reshape(bsz, seq, d)
    xp, st_p = _ssm_prompt_layer(xp, row(g_mix[1]), prep, d_row, wa_b, wgl_b)
    y_prompt = moe(xp.reshape(bsz * seq, d), pp, 1).reshape(bsz, seq, d)

    ps = p_sample.reshape(depth, ns, PLE_DIM)
    xs = x_sample.reshape(ns, d)
    xs, kws, vws = _attn_sample_layer(xs, row(g_mix[0]), wqkv_b, wo_b, bias_s, sinks[0],
                                      cache_k_win[0], cache_v_win[0])
    xs = moe(xs, ps, 0)
    xs, sr_s, si_s = _ssm_sample_layer(xs, row(g_mix[1]), prep, state_ssm_re[0].reshape(ns, N_STATES),
                                       state_ssm_im[0].reshape(ns, N_STATES), d_row, wa_b, wgl_b)
    y_sample = moe(xs, ps, 1).reshape(ns, 1, d)

    half = SUBLANES // 2
    win = lambda a, n: a.reshape(1, n, WINDOW, N_KV_HEADS, HEAD_DIM)
    st = lambda a, n: a.reshape(1, n, N_SSM_GROUPS, SSM_STATE)
    return (y_prompt, y_sample, win(kp, bsz), win(vp, bsz), win(kws, ns), win(vws, ns),
            st(st_p[:bsz], bsz), st(st_p[half:half + bsz], bsz), st(sr_s, ns), st(si_s, ns))
```
